```python
import math
import jax, jax.numpy as jnp
from jax import lax
import numpy as np

D_MODEL = 2048
BATCH = 2
SEQ = 4096
DEPTH = 2
DEC_BATCH = 8
DEC_SEQ = 1
PAST_LEN = 16384
PAGE_SIZE = 128

N_A_LAYERS = DEPTH // 2
N_B_LAYERS = DEPTH - N_A_LAYERS
EPS = 1e-6

SSM_EXPAND = 2
D_INNER = SSM_EXPAND * D_MODEL
SSM_HEAD_DIM = 64
SSM_HEADS = D_INNER // SSM_HEAD_DIM
SSM_GROUPS = 8
HEADS_PER_GROUP = SSM_HEADS // SSM_GROUPS
SSM_STATE = 128
CONV_W = 4
CONV_DIM = D_INNER + 2 * SSM_GROUPS * SSM_STATE
IN_PROJ_DIM = D_INNER + CONV_DIM + SSM_HEADS
SSD_CHUNK = 128

HEAD_DIM = 128
KV_HEADS = 8
DIL_WINDOWS = (128, 512, 2048)
DIL_RATES = (1, 4, 16)
N_DIL = len(DIL_RATES)
Q_HEADS = N_DIL * KV_HEADS
N_SLOTS = DIL_WINDOWS[0] // DIL_RATES[0] + 1
W_MAX = max(DIL_WINDOWS)
ATT_BLOCK = 128
ROPE_THETA = 10000.0

FFN_HIDDEN = -(-8 * D_MODEL // (3 * 256)) * 256

kernel_name = "yoco_mamba2_dilated_swa_step"


def rmsnorm(x, g):
    xf = x.astype(jnp.float32)
    y = xf * lax.rsqrt(jnp.mean(xf * xf, axis=-1, keepdims=True) + EPS)
    return (y * g.astype(jnp.float32)).astype(x.dtype)


def rope(x, pos):
    half = HEAD_DIM // 2
    inv = jnp.power(jnp.float32(ROPE_THETA), -jnp.arange(half, dtype=jnp.float32) / half)
    ang = pos.astype(jnp.float32)[:, None] * inv[None, :]
    cos = jnp.cos(ang)[None, :, None, :]
    sin = jnp.sin(ang)[None, :, None, :]
    xf = x.astype(jnp.float32)
    x1, x2 = xf[..., :half], xf[..., half:]
    return jnp.concatenate([x1 * cos - x2 * sin, x2 * cos + x1 * sin], axis=-1).astype(x.dtype)


def swiglu(h, w_gu, w_down):
    g, u = jnp.split(h @ w_gu, 2, axis=-1)
    return (jax.nn.silu(g) * u) @ w_down


def causal_dwconv(xe, w, b):
    L = xe.shape[1] - (CONV_W - 1)
    acc = b
    for k in range(CONV_W):
        acc = acc + xe[:, k:k + L] * w[k]
    return jax.nn.silu(acc)


def ssd_chunked(xdt, dA, Bm, Cm, h0):
    b, L = xdt.shape[:2]
    Q = SSD_CHUNK if L % SSD_CHUNK == 0 else L
    nc = L // Q
    X = xdt.reshape(b, nc, Q, SSM_GROUPS, HEADS_PER_GROUP, SSM_HEAD_DIM)
    A = dA.reshape(b, nc, Q, SSM_GROUPS, HEADS_PER_GROUP)
    Bc = Bm.reshape(b, nc, Q, SSM_GROUPS, SSM_STATE)
    Cc = Cm.reshape(b, nc, Q, SSM_GROUPS, SSM_STATE)
    A_cs = jnp.cumsum(A, axis=2)
    seg = A_cs[:, :, :, None] - A_cs[:, :, None, :]
    mask = jnp.tril(jnp.ones((Q, Q), dtype=bool))[None, None, :, :, None, None]
    decay = jnp.exp(jnp.where(mask, seg, -jnp.inf))
    CB = jnp.einsum('bclgn,bcsgn->bclsg', Cc, Bc)
    att = CB[..., None] * decay
    y_diag = jnp.einsum('bclsgr,bcsgrp->bclgrp', att, X)
    decay_end = jnp.exp(A_cs[:, :, -1:] - A_cs)
    chunk_states = jnp.einsum('bclgn,bclgrp->bcgrpn', Bc, X * decay_end[..., None])
    chunk_decay = jnp.exp(A_cs[:, :, -1])

    def step(h, inp):
        s_c, d_c = inp
        return h * d_c[..., None, None] + s_c, h

    h_final, h_prev = lax.scan(step, h0, (jnp.moveaxis(chunk_states, 1, 0), jnp.moveaxis(chunk_decay, 1, 0)))
    h_prev = jnp.moveaxis(h_prev, 0, 1)
    y_off = jnp.einsum('bclgn,bcgrpn->bclgrp', Cc, h_prev) * jnp.exp(A_cs)[..., None]
    y = (y_diag + y_off).reshape(b, L, SSM_GROUPS, HEADS_PER_GROUP, SSM_HEAD_DIM)
    return y, h_final


def ssd_recurrent(xdt, dA, Bm, Cm, h0):
    def step(h, inp):
        x_t, a_t, b_t, c_t = inp
        h = h * jnp.exp(a_t)[..., None, None] + x_t[..., None] * b_t[:, :, None, None, :]
        return h, jnp.einsum('bgrpn,bgn->bgrp', h, c_t)

    mv = lambda a: jnp.moveaxis(a, 1, 0)
    h, ys = lax.scan(step, h0, (mv(xdt), mv(dA), mv(Bm), mv(Cm)))
    return jnp.moveaxis(ys, 0, 1), h


def mamba2_mixer(h, conv_prev, ssm_prev, chunked, w_in, conv_w, conv_b, dt_bias, a_log, d_skip, gate_norm, w_out):
    b, L, _ = h.shape
    zxbcdt = h @ w_in
    z = zxbcdt[..., :D_INNER]
    xbc = zxbcdt[..., D_INNER:D_INNER + CONV_DIM]
    dt = zxbcdt[..., D_INNER + CONV_DIM:]
    xe = jnp.concatenate([conv_prev.astype(xbc.dtype), xbc], axis=1)
    new_conv = xe[:, -(CONV_W - 1):]
    xbc = causal_dwconv(xe, conv_w, conv_b).astype(jnp.float32)
    xs = xbc[..., :D_INNER].reshape(b, L, SSM_GROUPS, HEADS_PER_GROUP, SSM_HEAD_DIM)
    Bm = xbc[..., D_INNER:D_INNER + SSM_GROUPS * SSM_STATE].reshape(b, L, SSM_GROUPS, SSM_STATE)
    Cm = xbc[..., D_INNER + SSM_GROUPS * SSM_STATE:].reshape(b, L, SSM_GROUPS, SSM_STATE)
    dt = jax.nn.softplus(dt.astype(jnp.float32) + dt_bias.astype(jnp.float32))
    dt = dt.reshape(b, L, SSM_GROUPS, HEADS_PER_GROUP)
    A = -jnp.exp(a_log.astype(jnp.float32)).reshape(SSM_GROUPS, HEADS_PER_GROUP)
    dA = dt * A
    xdt = xs * dt[..., None]
    h0 = ssm_prev.astype(jnp.float32).reshape(b, SSM_GROUPS, HEADS_PER_GROUP, SSM_HEAD_DIM, SSM_STATE)
    if chunked:
        y, h_new = ssd_chunked(xdt, dA, Bm, Cm, h0)
    else:
        y, h_new = ssd_recurrent(xdt, dA, Bm, Cm, h0)
    y = y + xs * d_skip.astype(jnp.float32).reshape(SSM_GROUPS, HEADS_PER_GROUP)[..., None]
    y = y.reshape(b, L, D_INNER) * jax.nn.silu(z.astype(jnp.float32))
    yg = y.reshape(b, L, SSM_GROUPS, D_INNER // SSM_GROUPS)
    yg = yg * lax.rsqrt(jnp.mean(yg * yg, axis=-1, keepdims=True) + EPS)
    y = (yg.reshape(b, L, D_INNER) * gate_norm.astype(jnp.float32)).astype(h.dtype)
    out = y @ w_out
    return out, new_conv, h_new.reshape(b, SSM_HEADS, SSM_HEAD_DIM, SSM_STATE)


def dilated_attention(q, k_all, v_all, q_row0):
    b, Lq = q.shape[:2]
    QB = ATT_BLOCK if Lq % ATT_BLOCK == 0 else Lq
    nb = Lq // QB
    offs = jnp.arange(N_SLOTS, dtype=jnp.int32)[None, :] * jnp.array(DIL_RATES, dtype=jnp.int32)[:, None]
    scale = HEAD_DIM ** -0.5

    def block(i):
        q_blk = lax.dynamic_slice_in_dim(q, i * QB, QB, axis=1)
        rows = q_row0 + i * QB + jnp.arange(QB, dtype=jnp.int32)
        idx = rows[None, :, None] - offs[:, None, :]
        valid = idx >= 0
        idx = jnp.maximum(idx, 0)
        k_g = jnp.take(k_all, idx, axis=1)
        v_g = jnp.take(v_all, idx, axis=1)
        s = jnp.einsum('bqghd,bgqjhd->bghqj', q_blk, k_g).astype(jnp.float32) * scale
        s = jnp.where(valid[None, :, None], s, -jnp.inf)
        lse = jax.nn.logsumexp(s, axis=-1)
        p = jnp.exp(s - lse[..., None])
        o = jnp.einsum('bghqj,bgqjhd->bqghd', p, v_g.astype(jnp.float32))
        w = jax.nn.softmax(lse, axis=1)
        out = jnp.einsum('bghq,bqghd->bqhd', w, o)
        return out.astype(q.dtype)

    out = lax.map(block, jnp.arange(nb, dtype=jnp.int32))
    return jnp.transpose(out, (1, 0, 2, 3, 4)).reshape(b, Lq, KV_HEADS, HEAD_DIM)


def run_trunk(x, pos, conv_state, ssm_state, k_past, v_past, chunked,
              a_norm, a_w_in, a_conv_w, a_conv_b, a_dt_bias, a_log, a_d, a_gate_norm, a_w_out,
              kv_norm, w_kv, b_norm, b_w_q, b_w_o, ffn_norm, ffn_w_gu, ffn_w_down, final_norm):
    b, L, _ = x.shape
    convs, ssms = [], []
    k_new = v_new = k_all = v_all = None
    for l in range(DEPTH):
        if l < N_A_LAYERS:
            o, c, s = mamba2_mixer(rmsnorm(x, a_norm[l]), conv_state[l], ssm_state[l], chunked,
                                   a_w_in[l], a_conv_w[l], a_conv_b[l], a_dt_bias[l], a_log[l],
                                   a_d[l], a_gate_norm[l], a_w_out[l])
            x = x + o
            convs.append(c)
            ssms.append(s)
        else:
            j = l - N_A_LAYERS
            if j == 0:
                kv = rmsnorm(x, kv_norm) @ w_kv
                k_new = rope(kv[..., :KV_HEADS * HEAD_DIM].reshape(b, L, KV_HEADS, HEAD_DIM), pos)
                v_new = kv[..., KV_HEADS * HEAD_DIM:].reshape(b, L, KV_HEADS, HEAD_DIM)
                k_all = jnp.concatenate([k_past.astype(k_new.dtype), k_new], axis=1)
                v_all = jnp.concatenate([v_past.astype(v_new.dtype), v_new], axis=1)
            q = (rmsnorm(x, b_norm[j]) @ b_w_q[j]).reshape(b, L, Q_HEADS, HEAD_DIM)
            q = rope(q, pos).reshape(b, L, N_DIL, KV_HEADS, HEAD_DIM)
            o = dilated_attention(q, k_all, v_all, k_past.shape[1])
            x = x + o.reshape(b, L, KV_HEADS * HEAD_DIM) @ b_w_o[j]
        x = x + swiglu(rmsnorm(x, ffn_norm[l]), ffn_w_gu[l], ffn_w_down[l])
    return rmsnorm(x, final_norm), jnp.stack(convs), jnp.stack(ssms), k_new, v_new


def setup_inputs(seed: int = 0) -> dict:
    key = jax.random.key(seed)
    ks = iter(jax.random.split(key, 40))
    nrm = lambda shape, scale: jax.random.normal(next(ks), shape, jnp.float32) * scale
    kv_buf = min(W_MAX, PAST_LEN)
    dt0 = jnp.exp(jax.random.uniform(next(ks), (N_A_LAYERS, SSM_HEADS), jnp.float32,
                                     minval=math.log(1e-3), maxval=math.log(1e-1)))
    return {
        "x_prompt": nrm((BATCH, SEQ, D_MODEL), 1.0),
        "x_sample": nrm((DEC_BATCH, DEC_SEQ, D_MODEL), 1.0),
        "state_conv": nrm((N_A_LAYERS, DEC_BATCH, CONV_W - 1, CONV_DIM), 1.0),
        "state_ssm": nrm((N_A_LAYERS, DEC_BATCH, SSM_HEADS, SSM_HEAD_DIM, SSM_STATE), 0.1),
        "cache_k": nrm((DEC_BATCH, kv_buf, KV_HEADS, HEAD_DIM), 1.0),
        "cache_v": nrm((DEC_BATCH, kv_buf, KV_HEADS, HEAD_DIM), 1.0),
        "a_norm": 1.0 + nrm((N_A_LAYERS, D_MODEL), 0.01),
        "a_w_in": nrm((N_A_LAYERS, D_MODEL, IN_PROJ_DIM), D_MODEL ** -0.5),
        "a_conv_w": nrm((N_A_LAYERS, CONV_W, CONV_DIM), CONV_W ** -0.5),
        "a_conv_b": nrm((N_A_LAYERS, CONV_DIM), 0.01),
        "a_dt_bias": dt0 + jnp.log(-jnp.expm1(-dt0)),
        "a_log": jnp.log(jax.random.uniform(next(ks), (N_A_LAYERS, SSM_HEADS), jnp.float32, minval=1.0, maxval=16.0)),
        "a_d": 1.0 + nrm((N_A_LAYERS, SSM_HEADS), 0.01),
        "a_gate_norm": 1.0 + nrm((N_A_LAYERS, D_INNER), 0.01),
        "a_w_out": nrm((N_A_LAYERS, D_INNER, D_MODEL), D_INNER ** -0.5),
        "kv_norm": 1.0 + nrm((D_MODEL,), 0.01),
        "w_kv": nrm((D_MODEL, 2 * KV_HEADS * HEAD_DIM), D_MODEL ** -0.5),
        "b_norm": 1.0 + nrm((N_B_LAYERS, D_MODEL), 0.01),
        "b_w_q": nrm((N_B_LAYERS, D_MODEL, Q_HEADS * HEAD_DIM), D_MODEL ** -0.5),
        "b_w_o": nrm((N_B_LAYERS, KV_HEADS * HEAD_DIM, D_MODEL), (KV_HEADS * HEAD_DIM) ** -0.5),
        "ffn_norm": 1.0 + nrm((DEPTH, D_MODEL), 0.01),
        "ffn_w_gu": nrm((DEPTH, D_MODEL, 2 * FFN_HIDDEN), D_MODEL ** -0.5),
        "ffn_w_down": nrm((DEPTH, FFN_HIDDEN, D_MODEL), FFN_HIDDEN ** -0.5),
        "final_norm": 1.0 + nrm((D_MODEL,), 0.01),
    }


def reference(x_prompt, x_sample, state_conv, state_ssm, cache_k, cache_v,
              a_norm, a_w_in, a_conv_w, a_conv_b, a_dt_bias, a_log, a_d, a_gate_norm, a_w_out,
              kv_norm, w_kv, b_norm, b_w_q, b_w_o, ffn_norm, ffn_w_gu, ffn_w_down, final_norm):
    weights = (a_norm, a_w_in, a_conv_w, a_conv_b, a_dt_bias, a_log, a_d, a_gate_norm, a_w_out,
               kv_norm, w_kv, b_norm, b_w_q, b_w_o, ffn_norm, ffn_w_gu, ffn_w_down, final_norm)
    bp, Lp, _ = x_prompt.shape
    pos_p = jnp.arange(Lp, dtype=jnp.int32)
    zc = jnp.zeros((N_A_LAYERS, bp, CONV_W - 1, CONV_DIM), x_prompt.dtype)
    zs = jnp.zeros((N_A_LAYERS, bp, SSM_HEADS, SSM_HEAD_DIM, SSM_STATE), jnp.float32)
    zk = jnp.zeros((bp, 0, KV_HEADS, HEAD_DIM), x_prompt.dtype)
    y_prompt, p_conv, p_ssm, p_k, p_v = run_trunk(x_prompt, pos_p, zc, zs, zk, zk, True, *weights)
    keep = min(W_MAX, Lp)
    p_k = p_k[:, Lp - keep:]
    p_v = p_v[:, Lp - keep:]
    Ls = x_sample.shape[1]
    pos_s = PAST_LEN + jnp.arange(Ls, dtype=jnp.int32)
    y_sample, s_conv, s_ssm, s_k, s_v = run_trunk(x_sample, pos_s, state_conv, state_ssm, cache_k, cache_v, False, *weights)
    return (y_prompt, y_sample, p_conv, p_ssm, p_k, p_v, s_conv, s_ssm, s_k, s_v)
```

```python
import functools

import jax
import jax.numpy as jnp
from jax import lax
from jax.experimental import pallas as pl
from jax.experimental.pallas import tpu as pltpu

F32 = jnp.float32
BF16 = jnp.bfloat16
HIGHEST = lax.Precision.HIGHEST

EPS = 1e-6
ROPE_THETA = 10000.0
SSD_CHUNK = 128
SSM_HEAD_DIM = 64
SSM_STATE = 128
SSM_GROUPS = 8
CONV_W = 4
HEAD_DIM = 128
KV_HEADS = 8
DIL_RATES = (1, 4, 16)
DIL_SLOTS = 128
ATT_BLOCK = 128
PAST_LEN = 16384

V7X_VMEM_BYTES = 64 * 1024 * 1024
VMEM_LIMIT = V7X_VMEM_BYTES - 8 * 1024 * 1024
LANES = 128
CONV_TAIL = 8

NT_DIMS = (((1,), (1,)), ((), ()))
TN_DIMS = (((0,), (0,)), ((), ()))


def _params(*sem):
    return pltpu.CompilerParams(dimension_semantics=sem, vmem_limit_bytes=VMEM_LIMIT)


def _sigmoid(x):
    return 1.0 / (1.0 + jnp.exp(-x))


def _silu(x):
    return x * _sigmoid(x)


def _softplus(x):
    return jnp.maximum(x, 0.0) + jnp.log1p(jnp.exp(-jnp.abs(x)))


def _rms_scale(x):
    return lax.rsqrt(jnp.mean(x * x, axis=-1, keepdims=True) + EPS)


def _proj_kernel(*refs, norm, rope, has_res, n_out):
    it = iter(refs)
    x_ref = next(it)
    g_ref = next(it) if norm else None
    w_ref = next(it)
    cos_ref = next(it) if rope else None
    sin_ref = next(it) if rope else None
    res_ref = next(it) if has_res else None
    out_refs = [next(it) for _ in range(n_out)]
    xn_ref = next(it) if norm else None

    if norm:
        @pl.when(pl.program_id(1) == 0)
        def _():
            x = x_ref[...]
            xn_ref[...] = ((x * _rms_scale(x)) * g_ref[...]).astype(BF16)
        lhs = xn_ref[...]
    else:
        lhs = x_ref[...]
    acc = jnp.dot(lhs, w_ref[...], preferred_element_type=F32)
    if rope:
        cos = cos_ref[...]
        sin = sin_ref[...]
        heads = []
        for h in range(acc.shape[1] // HEAD_DIM):
            a = acc[:, h * HEAD_DIM:(h + 1) * HEAD_DIM]
            heads.append(a * cos + pltpu.roll(a, HEAD_DIM // 2, 1) * sin)
        acc = heads[0] if len(heads) == 1 else jnp.concatenate(heads, axis=1)
    if has_res:
        acc = acc + res_ref[...]
    for o_ref in out_refs:
        o_ref[...] = acc.astype(o_ref.dtype)


def _proj(x, w, *, n_cols, col_off=0, gain=None, rope=None, res=None,
          out_dtypes=(F32,), tm=512, tn=512, rows_per_seq=None):
    M, K = x.shape
    tm = min(tm, M)
    assert M % tm == 0 and n_cols % tn == 0 and col_off % tn == 0
    norm = gain is not None
    off = col_off // tn
    args = [x]
    in_specs = [pl.BlockSpec((tm, K), lambda i, j: (i, 0))]
    if norm:
        args.append(gain.reshape(1, K))
        in_specs.append(pl.BlockSpec((1, K), lambda i, j: (0, 0)))
    args.append(w)
    in_specs.append(pl.BlockSpec((K, tn), lambda i, j: (0, j + off)))
    if rope is not None:
        cos, sin = rope
        nseq = rows_per_seq // tm if rows_per_seq is not None else 1
        for t in (cos, sin):
            args.append(t)
            in_specs.append(pl.BlockSpec((tm, HEAD_DIM), lambda i, j: (i % nseq, 0)))
    if res is not None:
        args.append(res)
        in_specs.append(pl.BlockSpec((tm, tn), lambda i, j: (i, j)))
    out_shape = [jax.ShapeDtypeStruct((M, n_cols), d) for d in out_dtypes]
    out_specs = [pl.BlockSpec((tm, tn), lambda i, j: (i, j)) for _ in out_dtypes]
    outs = pl.pallas_call(
        functools.partial(_proj_kernel, norm=norm, rope=rope is not None,
                          has_res=res is not None, n_out=len(out_dtypes)),
        grid=(M // tm, n_cols // tn),
        in_specs=in_specs,
        out_specs=out_specs,
        out_shape=out_shape,
        scratch_shapes=[pltpu.VMEM((tm, K), BF16)] if norm else [],
        compiler_params=_params("parallel", "arbitrary"),
        name="proj",
    )(*args)
    return outs[0] if len(outs) == 1 else outs


def _gateup_kernel(x_ref, g_ref, wg_ref, wu_ref, o_ref, xn_ref):
    @pl.when(pl.program_id(1) == 0)
    def _():
        x = x_ref[...]
        xn_ref[...] = ((x * _rms_scale(x)) * g_ref[...]).astype(BF16)
    xn = xn_ref[...]
    g = jnp.dot(xn, wg_ref[...], preferred_element_type=F32)
    u = jnp.dot(xn, wu_ref[...], preferred_element_type=F32)
    o_ref[...] = (_silu(g) * u).astype(o_ref.dtype)


def _gateup(x, gain, w_gu, *, tm=512, tn=512):
    M, K = x.shape
    hidden = w_gu.shape[1] // 2
    tm = min(tm, M)
    assert M % tm == 0 and hidden % tn == 0
    nj = hidden // tn
    return pl.pallas_call(
        _gateup_kernel,
        grid=(M // tm, nj),
        in_specs=[pl.BlockSpec((tm, K), lambda i, j: (i, 0)),
                  pl.BlockSpec((1, K), lambda i, j: (0, 0)),
                  pl.BlockSpec((K, tn), lambda i, j: (0, j)),
                  pl.BlockSpec((K, tn), lambda i, j: (0, j + nj))],
        out_specs=pl.BlockSpec((tm, tn), lambda i, j: (i, j)),
        out_shape=jax.ShapeDtypeStruct((M, hidden), BF16),
        scratch_shapes=[pltpu.VMEM((tm, K), BF16)],
        compiler_params=_params("parallel", "arbitrary"),
        name="gateup",
    )(x, gain.reshape(1, K), w_gu, w_gu)


def _dt_kernel(x_ref, g_ref, w_ref, wt_ref, dt_ref, dtt_ref):
    x = x_ref[...]
    xn = (x * _rms_scale(x)) * g_ref[...]
    dt_ref[...] = jnp.dot(xn, w_ref[...], precision=HIGHEST, preferred_element_type=F32)
    dtt_ref[...] = lax.dot_general(wt_ref[...], xn, NT_DIMS, precision=HIGHEST,
                                   preferred_element_type=F32)


def _dt_proj(x, gain, w_dt, *, tm=512):
    M, K = x.shape
    H = w_dt.shape[1]
    tm = min(tm, M)
    assert M % tm == 0
    return pl.pallas_call(
        _dt_kernel,
        grid=(M // tm,),
        in_specs=[pl.BlockSpec((tm, K), lambda i: (i, 0)),
                  pl.BlockSpec((1, K), lambda i: (0, 0)),
                  pl.BlockSpec((K, H), lambda i: (0, 0)),
                  pl.BlockSpec((H, K), lambda i: (0, 0))],
        out_specs=[pl.BlockSpec((tm, H), lambda i: (i, 0)),
                   pl.BlockSpec((H, tm), lambda i: (0, i))],
        out_shape=[jax.ShapeDtypeStruct((M, H), F32), jax.ShapeDtypeStruct((H, M), F32)],
        compiler_params=_params("parallel"),
        name="dt_proj",
    )(x, gain.reshape(1, K), w_dt, w_dt.T)


def _conv_silu(full_ref, lo, hi, cw_ref, cb_ref, woff):
    acc = cb_ref[:, woff + lo:woff + hi]
    for k in range(CONV_W):
        r0 = CONV_TAIL - (CONV_W - 1) + k
        acc = acc + full_ref[r0:r0 + SSD_CHUNK, lo:hi] * cw_ref[k:k + 1, woff + lo:woff + hi]
    return _silu(acc)


def _ssd_kernel(z_ref, x_ref, b_ref, c_ref, dt_ref, dtt_ref, cw_ref, cb_ref,
                dtb_ref, dtbt_ref, alog_ref, alogt_ref, dexp_ref, gn_ref,
                y_ref, hout_ref,
                h_ref, xf_ref, bf_ref, cf_ref):
    Q = SSD_CHUNK
    P = SSM_HEAD_DIM
    N = SSM_STATE
    d_inner = x_ref.shape[1]
    n_heads = d_inner // P
    hpg = n_heads // SSM_GROUPS
    gw = hpg * P
    c = pl.program_id(1)

    @pl.when(c == 0)
    def _():
        h_ref[...] = jnp.zeros_like(h_ref)
        xf_ref[0:CONV_TAIL, :] = jnp.zeros((CONV_TAIL, xf_ref.shape[1]), F32)
        bf_ref[0:CONV_TAIL, :] = jnp.zeros((CONV_TAIL, bf_ref.shape[1]), F32)
        cf_ref[0:CONV_TAIL, :] = jnp.zeros((CONV_TAIL, cf_ref.shape[1]), F32)

    xf_ref[CONV_TAIL:, :] = x_ref[...]
    bf_ref[CONV_TAIL:, :] = b_ref[...]
    cf_ref[CONV_TAIL:, :] = c_ref[...]

    dt = _softplus(dt_ref[...] + dtb_ref[...])
    dA = dt * (-jnp.exp(alog_ref[...]))
    dtT = _softplus(dtt_ref[...] + dtbt_ref[...])
    dAT = dtT * (-jnp.exp(alogt_ref[...]))
    row = lax.broadcasted_iota(jnp.int32, (Q, Q), 0)
    col = lax.broadcasted_iota(jnp.int32, (Q, Q), 1)
    causal = row >= col
    tri = jnp.where(causal, 1.0, 0.0).astype(F32)
    triT = jnp.where(row <= col, 1.0, 0.0).astype(F32)
    cs = jnp.dot(tri, dA, precision=HIGHEST, preferred_element_type=F32)
    csT = jnp.dot(dAT, triT, precision=HIGHEST, preferred_element_type=F32)
    decay_end = jnp.exp(cs[Q - 1:Q, :] - cs)
    ecs = jnp.exp(cs)
    chunk_decay = jnp.broadcast_to(jnp.exp(csT[:, Q - 1:Q]), (n_heads, N))

    lane_lo = lax.broadcasted_iota(jnp.int32, (Q, 2 * P), 1) < P

    def pair_cols(arr, q):
        a0 = jnp.broadcast_to(arr[:, 2 * q:2 * q + 1], (Q, 2 * P))
        a1 = jnp.broadcast_to(arr[:, 2 * q + 1:2 * q + 2], (Q, 2 * P))
        return jnp.where(lane_lo, a0, a1)

    woff_b = d_inner
    woff_c = d_inner + SSM_GROUPS * N
    for g in range(SSM_GROUPS):
        Bg = _conv_silu(bf_ref, g * N, (g + 1) * N, cw_ref, cb_ref, woff_b).astype(BF16)
        Cg = _conv_silu(cf_ref, g * N, (g + 1) * N, cw_ref, cb_ref, woff_c).astype(BF16)
        CB = lax.dot_general(Cg, Bg, NT_DIMS, preferred_element_type=F32)
        Hg = h_ref[g * hpg:(g + 1) * hpg].reshape(gw, N)
        y_off = lax.dot_general(Cg, Hg.astype(BF16), NT_DIMS, preferred_element_type=F32)

        ys = []
        xds = []
        for qq in range(hpg // 2):
            q = g * (hpg // 2) + qq
            lo, hi = q * 2 * P, (q + 1) * 2 * P
            xs = _conv_silu(xf_ref, lo, hi, cw_ref, cb_ref, 0)
            xdt = xs * pair_cols(dt, q)
            xds.append((xdt * pair_cols(decay_end, q)).astype(BF16))
            atts = []
            for hh in (2 * q, 2 * q + 1):
                seg = cs[:, hh:hh + 1] - csT[hh:hh + 1, :]
                atts.append((CB * jnp.exp(jnp.where(causal, seg, -jnp.inf))).astype(BF16))
            att = jnp.concatenate(atts, axis=1)
            xbd = jnp.concatenate([jnp.where(lane_lo, xdt, 0.0).astype(BF16),
                                   jnp.where(lane_lo, 0.0, xdt).astype(BF16)], axis=0)
            y = jnp.dot(att, xbd, preferred_element_type=F32)
            y = y + y_off[:, qq * 2 * P:(qq + 1) * 2 * P] * pair_cols(ecs, q)
            y = y + xs * dexp_ref[:, lo:hi]
            ys.append(y * _silu(z_ref[:, lo:hi]))

        ssq = ys[0] * ys[0]
        for y in ys[1:]:
            ssq = ssq + y * y
        scale = lax.rsqrt(jnp.sum(ssq, axis=-1, keepdims=True) / gw + EPS)
        for qq, y in enumerate(ys):
            lo = g * gw + qq * 2 * P
            y_ref[:, lo:lo + 2 * P] = ((y * scale) * gn_ref[:, lo:lo + 2 * P]).astype(y_ref.dtype)

        S = lax.dot_general(jnp.concatenate(xds, axis=1), Bg, TN_DIMS, preferred_element_type=F32)
        for r in range(hpg):
            hh = g * hpg + r
            h_ref[hh] = h_ref[hh] * chunk_decay[hh:hh + 1, :] + S[r * P:(r + 1) * P, :]

    xf_ref[0:CONV_TAIL, :] = xf_ref[Q:Q + CONV_TAIL, :]
    bf_ref[0:CONV_TAIL, :] = bf_ref[Q:Q + CONV_TAIL, :]
    cf_ref[0:CONV_TAIL, :] = cf_ref[Q:Q + CONV_TAIL, :]

    @pl.when(c == pl.num_programs(1) - 1)
    def _():
        hout_ref[0] = h_ref[...]


def _ssd_prompt(zxbc, dt, dtT, batch, conv_w, conv_b, dt_bias, a_log, d_skip, gate_norm):
    M = zxbc.shape[0]
    L = M // batch
    H = dt.shape[1]
    d_inner = H * SSM_HEAD_DIM
    bc = SSM_GROUPS * SSM_STATE
    conv_dim = d_inner + 2 * bc
    Q = SSD_CHUNK
    nc = L // Q
    assert L % Q == 0 and d_inner % bc == 0
    xb = d_inner // bc
    row_map = lambda col: (lambda b, c: (b * nc + c, col))
    const = lambda b, c: (0, 0)
    d_exp = jnp.repeat(d_skip.astype(F32), SSM_HEAD_DIM).reshape(1, d_inner)
    y, h_final = pl.pallas_call(
        _ssd_kernel,
        grid=(batch, nc),
        in_specs=[pl.BlockSpec((Q, d_inner), row_map(0)),
                  pl.BlockSpec((Q, d_inner), row_map(1)),
                  pl.BlockSpec((Q, bc), row_map(2 * xb)),
                  pl.BlockSpec((Q, bc), row_map(2 * xb + 1)),
                  pl.BlockSpec((Q, H), row_map(0)),
                  pl.BlockSpec((H, Q), lambda b, c: (0, b * nc + c)),
                  pl.BlockSpec((CONV_W, conv_dim), const),
                  pl.BlockSpec((1, conv_dim), const),
                  pl.BlockSpec((1, H), const),
                  pl.BlockSpec((H, 1), const),
                  pl.BlockSpec((1, H), const),
                  pl.BlockSpec((H, 1), const),
                  pl.BlockSpec((1, d_inner), const),
                  pl.BlockSpec((1, d_inner), const)],
        out_specs=[pl.BlockSpec((Q, d_inner), row_map(0)),
                   pl.BlockSpec((1, H, SSM_HEAD_DIM, SSM_STATE), lambda b, c: (b, 0, 0, 0))],
        out_shape=[jax.ShapeDtypeStruct((M, d_inner), BF16),
                   jax.ShapeDtypeStruct((batch, H, SSM_HEAD_DIM, SSM_STATE), F32)],
        scratch_shapes=[pltpu.VMEM((H, SSM_HEAD_DIM, SSM_STATE), F32),
                        pltpu.VMEM((Q + CONV_TAIL, d_inner), F32),
                        pltpu.VMEM((Q + CONV_TAIL, bc), F32),
                        pltpu.VMEM((Q + CONV_TAIL, bc), F32)],
        compiler_params=_params("parallel", "arbitrary"),
        name="ssd_prompt",
    )(zxbc, zxbc, zxbc, zxbc, dt, dtT, conv_w, conv_b.reshape(1, conv_dim),
      dt_bias.reshape(1, H), dt_bias.reshape(H, 1), a_log.reshape(1, H), a_log.reshape(H, 1),
      d_exp, gate_norm.reshape(1, d_inner))
    return y, h_final


def _ssd_step_kernel(zxbc_ref, dt_ref, cs_ref, h0_ref, cw_ref, cb_ref, dtb_ref, alog_ref,
                     dexp_ref, gn_ref, y_ref, cso_ref, ho_ref):
    P = SSM_HEAD_DIM
    N = SSM_STATE
    n_heads = h0_ref.shape[2]
    d_inner = n_heads * P
    hpg = n_heads // SSM_GROUPS
    gw = hpg * P
    conv_dim = cw_ref.shape[1]

    raw = zxbc_ref[0, :, d_inner:d_inner + conv_dim]
    prev = cs_ref[0, 0]
    acc = cb_ref[...]
    for k in range(CONV_W - 1):
        acc = acc + prev[k:k + 1, :] * cw_ref[k:k + 1, :]
    acc = acc + raw * cw_ref[CONV_W - 1:CONV_W, :]
    xbc = _silu(acc)
    cso_ref[0, 0, 0:CONV_W - 2, :] = prev[1:CONV_W - 1, :]
    cso_ref[0, 0, CONV_W - 2:CONV_W - 1, :] = raw

    dt = _softplus(dt_ref[0] + dtb_ref[...])
    decay = jnp.exp(dt * (-jnp.exp(alog_ref[...])))
    z = zxbc_ref[0, :, 0:d_inner]

    eye = (lax.broadcasted_iota(jnp.int32, (P, P), 0) == lax.broadcasted_iota(jnp.int32, (P, P), 1))
    ys = []
    for g in range(SSM_GROUPS):
        Bg = xbc[:, d_inner + g * N:d_inner + (g + 1) * N]
        Cg = xbc[:, d_inner + SSM_GROUPS * N + g * N:d_inner + SSM_GROUPS * N + (g + 1) * N]
        Bb = jnp.broadcast_to(Bg, (P, N))
        for r in range(hpg):
            hh = g * hpg + r
            xs = xbc[:, hh * P:(hh + 1) * P]
            xdt = xs * dt[:, hh:hh + 1]
            xdiag = jnp.where(eye, jnp.broadcast_to(xdt, (P, P)), 0.0)
            outer = jnp.dot(xdiag, Bb, precision=HIGHEST, preferred_element_type=F32)
            ho_ref[0, 0, hh] = h0_ref[0, 0, hh] * decay[:, hh:hh + 1] + outer
        Hg = ho_ref[0, 0, g * hpg:(g + 1) * hpg].reshape(gw, N)
        yg = lax.dot_general(Cg, Hg, NT_DIMS, precision=HIGHEST, preferred_element_type=F32)
        lo, hi = g * gw, (g + 1) * gw
        yg = yg + xbc[:, lo:hi] * dexp_ref[:, lo:hi]
        yg = yg * _silu(z[:, lo:hi])
        yg = yg * lax.rsqrt(jnp.mean(yg * yg, axis=-1, keepdims=True) + EPS)
        ys.append(yg * gn_ref[:, lo:hi])
    y_ref[0] = jnp.concatenate(ys, axis=1).astype(y_ref.dtype)


def _ssd_step(zxbc, dt, conv_state, ssm_state, conv_w, conv_b, dt_bias, a_log, d_skip, gate_norm):
    B = zxbc.shape[0]
    H = dt.shape[1]
    d_inner = H * SSM_HEAD_DIM
    conv_dim = conv_w.shape[1]
    const = lambda b: (0, 0)
    d_exp = jnp.repeat(d_skip.astype(F32), SSM_HEAD_DIM).reshape(1, d_inner)
    cs4 = conv_state.reshape(1, B, CONV_W - 1, conv_dim)
    h5 = ssm_state.reshape(1, B, H, SSM_HEAD_DIM, SSM_STATE)
    return pl.pallas_call(
        _ssd_step_kernel,
        grid=(B,),
        in_specs=[pl.BlockSpec((1, 1, zxbc.shape[1]), lambda b: (b, 0, 0)),
                  pl.BlockSpec((1, 1, H), lambda b: (b, 0, 0)),
                  pl.BlockSpec((1, 1, CONV_W - 1, conv_dim), lambda b: (0, b, 0, 0)),
                  pl.BlockSpec((1, 1, H, SSM_HEAD_DIM, SSM_STATE), lambda b: (0, b, 0, 0, 0)),
                  pl.BlockSpec((CONV_W, conv_dim), const),
                  pl.BlockSpec((1, conv_dim), const),
                  pl.BlockSpec((1, H), const),
                  pl.BlockSpec((1, H), const),
                  pl.BlockSpec((1, d_inner), const),
                  pl.BlockSpec((1, d_inner), const)],
        out_specs=[pl.BlockSpec((1, 1, d_inner), lambda b: (b, 0, 0)),
                   pl.BlockSpec((1, 1, CONV_W - 1, conv_dim), lambda b: (0, b, 0, 0)),
                   pl.BlockSpec((1, 1, H, SSM_HEAD_DIM, SSM_STATE), lambda b: (0, b, 0, 0, 0))],
        out_shape=[jax.ShapeDtypeStruct((B, 1, d_inner), BF16),
                   jax.ShapeDtypeStruct(cs4.shape, F32),
                   jax.ShapeDtypeStruct(h5.shape, F32)],
        compiler_params=_params("arbitrary"),
        name="ssd_step",
    )(zxbc.reshape(B, 1, -1), dt.reshape(B, 1, H), cs4, h5, conv_w, conv_b.reshape(1, conv_dim),
      dt_bias.reshape(1, H), a_log.reshape(1, H), d_exp, gate_norm.reshape(1, d_inner))


def _attn_kernel(q0_ref, q1_ref, q2_ref, k_ref, v_ref, o_ref):
    QB = ATT_BLOCK
    i0 = pl.program_id(2) * QB
    scale = HEAD_DIM ** -0.5
    scores = []
    vwins = []
    for q_ref, rate in zip((q0_ref, q1_ref, q2_ref), DIL_RATES):
        span = DIL_SLOTS * rate
        W = span + QB
        start = pl.multiple_of(jnp.maximum(i0 - span, 0), QB)
        kw = k_ref[pl.ds(start, W), :]
        vwins.append(v_ref[pl.ds(start, W), :])
        s = lax.dot_general(q_ref[...], kw, NT_DIMS, preferred_element_type=F32) * scale
        e = (lax.broadcasted_iota(jnp.int32, (QB, W), 0)
             - lax.broadcasted_iota(jnp.int32, (QB, W), 1))
        if rate > 1:
            e = jnp.where((e & (rate - 1)) == 0, e, 2 ** 30)
        d = e + (i0 - start)
        s = jnp.where(d >= 0, s, -jnp.inf)
        s = jnp.where(d <= span, s, -jnp.inf)
        scores.append(s)
    m = functools.reduce(jnp.maximum, [s.max(axis=1, keepdims=True) for s in scores])
    l = jnp.zeros((QB, 1), F32)
    o = jnp.zeros((QB, HEAD_DIM), F32)
    for s, vw in zip(scores, vwins):
        p = jnp.exp(s - m)
        l = l + p.sum(axis=1, keepdims=True)
        o = o + jnp.dot(p.astype(BF16), vw, preferred_element_type=F32)
    o_ref[...] = (o / l).astype(o_ref.dtype)


def _attn_prompt(q, k, v, batch):
    M = q.shape[0]
    L = M // batch
    nq = L // ATT_BLOCK
    assert L % ATT_BLOCK == 0 and L >= DIL_SLOTS * max(DIL_RATES) + ATT_BLOCK
    qspec = lambda g: pl.BlockSpec((ATT_BLOCK, HEAD_DIM), lambda b, h, i: (b * nq + i, g * KV_HEADS + h))
    kvspec = pl.BlockSpec((L, HEAD_DIM), lambda b, h, i: (b, h))
    return pl.pallas_call(
        _attn_kernel,
        grid=(batch, KV_HEADS, nq),
        in_specs=[qspec(0), qspec(1), qspec(2), kvspec, kvspec],
        out_specs=pl.BlockSpec((ATT_BLOCK, HEAD_DIM), lambda b, h, i: (b * nq + i, h)),
        out_shape=jax.ShapeDtypeStruct((M, KV_HEADS * HEAD_DIM), BF16),
        compiler_params=_params("parallel", "parallel", "arbitrary"),
        name="attn_prompt",
    )(q, q, q, k, v)


def _attn_step_kernel(q_ref, kc_ref, vc_ref, kn_ref, vn_ref, o_ref):
    G = q_ref.shape[2]
    T = kc_ref.shape[1]
    scale = HEAD_DIM ** -0.5
    q = q_ref[0, 0]
    s = lax.dot_general(q, kc_ref[0].astype(BF16), NT_DIMS, preferred_element_type=F32) * scale
    grp = lax.broadcasted_iota(jnp.int32, (G, T), 0)
    dist = T - lax.broadcasted_iota(jnp.int32, (G, T), 1)
    rate_mask = jnp.zeros((G, T), jnp.int32)
    span = jnp.full((G, T), -1, jnp.int32)
    for g, rate in enumerate(DIL_RATES):
        rate_mask = jnp.where(grp == g, rate - 1, rate_mask)
        span = jnp.where(grp == g, DIL_SLOTS * rate, span)
    s = jnp.where((dist & rate_mask) == 0, s, -jnp.inf)
    s = jnp.where(dist <= span, s, -jnp.inf)
    kn = kn_ref[0, 0].astype(BF16).astype(F32)
    grp1 = lax.broadcasted_iota(jnp.int32, (G, 1), 0)
    s_new = jnp.sum(q.astype(F32) * kn, axis=1, keepdims=True) * scale
    s_new = jnp.where(grp1 < len(DIL_RATES), s_new, -jnp.inf)
    m = jnp.maximum(s.max(axis=1, keepdims=True), s_new).max(axis=0, keepdims=True)
    p = jnp.exp(s - m)
    p_new = jnp.exp(s_new - m)
    l = (p.sum(axis=1, keepdims=True) + p_new).sum(axis=0, keepdims=True)
    o = jnp.dot(p.astype(BF16), vc_ref[0].astype(BF16), preferred_element_type=F32)
    vn = vn_ref[0, 0].astype(BF16).astype(F32)
    o = o + p_new.astype(BF16).astype(F32) * vn
    o_ref[0, 0] = (o.sum(axis=0, keepdims=True) / l).astype(o_ref.dtype)


def _attn_step(q, cache_k, cache_v, k_new, v_new):
    B, T = cache_k.shape[0], cache_k.shape[1]
    G = 8
    qg = q.reshape(B, len(DIL_RATES), KV_HEADS, HEAD_DIM).transpose(0, 2, 1, 3)
    qg = jnp.pad(qg, ((0, 0), (0, 0), (0, G - len(DIL_RATES)), (0, 0)))
    kc = cache_k.reshape(B, T, KV_HEADS * HEAD_DIM)
    vc = cache_v.reshape(B, T, KV_HEADS * HEAD_DIM)
    kn = k_new.reshape(B, KV_HEADS, 1, HEAD_DIM)
    vn = v_new.reshape(B, KV_HEADS, 1, HEAD_DIM)
    cspec = pl.BlockSpec((1, T, HEAD_DIM), lambda b, h: (b, 0, h))
    nspec = pl.BlockSpec((1, 1, 1, HEAD_DIM), lambda b, h: (b, h, 0, 0))
    o = pl.pallas_call(
        _attn_step_kernel,
        grid=(B, KV_HEADS),
        in_specs=[pl.BlockSpec((1, 1, G, HEAD_DIM), lambda b, h: (b, h, 0, 0)),
                  cspec, cspec, nspec, nspec],
        out_specs=nspec,
        out_shape=jax.ShapeDtypeStruct((B, KV_HEADS, 1, HEAD_DIM), BF16),
        compiler_params=_params("parallel", "arbitrary"),
        name="attn_step",
    )(qg, kc, vc, kn, vn)
    return o.reshape(B, KV_HEADS * HEAD_DIM)


def _rmsnorm_kernel(x_ref, g_ref, o_ref):
    x = x_ref[...]
    o_ref[...] = (x * _rms_scale(x)) * g_ref[...]


def _rmsnorm(x, gain, *, tm=512):
    M, K = x.shape
    tm = min(tm, M)
    assert M % tm == 0
    return pl.pallas_call(
        _rmsnorm_kernel,
        grid=(M // tm,),
        in_specs=[pl.BlockSpec((tm, K), lambda i: (i, 0)), pl.BlockSpec((1, K), lambda i: (0, 0))],
        out_specs=pl.BlockSpec((tm, K), lambda i: (i, 0)),
        out_shape=jax.ShapeDtypeStruct((M, K), F32),
        compiler_params=_params("parallel"),
        name="final_norm",
    )(x, gain.reshape(1, K))


def _rope_tables(pos):
    half = HEAD_DIM // 2
    inv = jnp.power(jnp.float32(ROPE_THETA), -jnp.arange(half, dtype=jnp.float32) / half)
    ang = pos.astype(jnp.float32)[:, None] * inv[None, :]
    cos = jnp.cos(ang)
    sin = jnp.sin(ang)
    return jnp.concatenate([cos, cos], axis=1), jnp.concatenate([-sin, sin], axis=1)


def _trunk(x, pos, W, prompt, conv_state=None, ssm_state=None, cache_k=None, cache_v=None):
    batch, L, D = x.shape
    M = batch * L
    x0 = x.reshape(M, D)
    d_inner = W["a_w_out"].shape[0]
    n_heads = W["a_dt_bias"].shape[0]
    conv_dim = W["a_conv_w"].shape[1]
    kv_dim = KV_HEADS * HEAD_DIM
    cos, sin = _rope_tables(pos)
    if not prompt:
        cos, sin = jnp.broadcast_to(cos, (M, HEAD_DIM)), jnp.broadcast_to(sin, (M, HEAD_DIM))
    rope = (cos, sin)
    rows_per_seq = L if prompt else M

    zxbc = _proj(x0, W["a_w_in"], n_cols=d_inner + conv_dim, gain=W["a_norm"])
    dt, dtT = _dt_proj(x0, W["a_norm"], W["a_w_dt"])
    if prompt:
        y, ssm_new = _ssd_prompt(zxbc, dt, dtT, batch, W["a_conv_w"], W["a_conv_b"], W["a_dt_bias"],
                                 W["a_log"], W["a_d"], W["a_gate_norm"])
        conv_new = zxbc.reshape(batch, L, -1)[:, L - (CONV_W - 1):, d_inner:]
    else:
        y, conv_new, ssm_new = _ssd_step(zxbc, dt, conv_state, ssm_state, W["a_conv_w"], W["a_conv_b"],
                                         W["a_dt_bias"], W["a_log"], W["a_d"], W["a_gate_norm"])
        y = y.reshape(M, d_inner)
        conv_new = conv_new[0]
        ssm_new = ssm_new[0]
    x1 = _proj(y, W["a_w_out"], n_cols=D, res=x0)
    h = _gateup(x1, W["ffn_norm"][0], W["ffn_w_gu"][0])
    x2 = _proj(h, W["ffn_w_down"][0], n_cols=D, res=x1)

    k_f32, k_bf = _proj(x2, W["w_kv"], n_cols=kv_dim, gain=W["kv_norm"], rope=rope, rows_per_seq=rows_per_seq,
                        out_dtypes=(F32, BF16))
    v_f32, v_bf = _proj(x2, W["w_kv"], n_cols=kv_dim, col_off=kv_dim, gain=W["kv_norm"],
                        out_dtypes=(F32, BF16))
    q = _proj(x2, W["b_w_q"], n_cols=len(DIL_RATES) * kv_dim, gain=W["b_norm"], rope=rope, rows_per_seq=rows_per_seq,
              out_dtypes=(BF16,))
    if prompt:
        o = _attn_prompt(q, k_bf, v_bf, batch)
    else:
        o = _attn_step(q, cache_k, cache_v, k_f32, v_f32)
    x3 = _proj(o, W["b_w_o"], n_cols=D, res=x2)
    h = _gateup(x3, W["ffn_norm"][1], W["ffn_w_gu"][1])
    x4 = _proj(h, W["ffn_w_down"][1], n_cols=D, res=x3)
    out = _rmsnorm(x4, W["final_norm"]).reshape(batch, L, D)
    k_new = k_f32.reshape(batch, L, KV_HEADS, HEAD_DIM)
    v_new = v_f32.reshape(batch, L, KV_HEADS, HEAD_DIM)
    return out, conv_new[None], ssm_new[None], k_new, v_new


def kernel(x_prompt, x_sample, state_conv, state_ssm, cache_k, cache_v, a_norm, a_w_in, a_conv_w, a_conv_b,
           a_dt_bias, a_log, a_d, a_gate_norm, a_w_out, kv_norm, w_kv, b_norm, b_w_q, b_w_o, ffn_norm,
           ffn_w_gu, ffn_w_down, final_norm):
    assert a_norm.shape[0] == 1 and b_norm.shape[0] == 1, "one Mamba-2 layer followed by one attention layer"
    d_inner = a_w_out.shape[1]
    conv_dim = a_conv_w.shape[2]
    W = dict(
        a_norm=a_norm[0], a_w_in=a_w_in[0].astype(BF16), a_w_dt=a_w_in[0][:, d_inner + conv_dim:],
        a_conv_w=a_conv_w[0], a_conv_b=a_conv_b[0], a_dt_bias=a_dt_bias[0], a_log=a_log[0], a_d=a_d[0],
        a_gate_norm=a_gate_norm[0], a_w_out=a_w_out[0].astype(BF16),
        kv_norm=kv_norm, w_kv=w_kv.astype(BF16), b_norm=b_norm[0], b_w_q=b_w_q[0].astype(BF16),
        b_w_o=b_w_o[0].astype(BF16), ffn_norm=ffn_norm, ffn_w_gu=ffn_w_gu.astype(BF16),
        ffn_w_down=ffn_w_down.astype(BF16), final_norm=final_norm)

    Lp = x_prompt.shape[1]
    y_p, p_conv, p_ssm, p_k, p_v = _trunk(x_prompt, jnp.arange(Lp, dtype=jnp.int32), W, True)
    keep = min(DIL_SLOTS * max(DIL_RATES), Lp)
    p_k = p_k[:, Lp - keep:]
    p_v = p_v[:, Lp - keep:]

    Ls = x_sample.shape[1]
    assert Ls == 1, "sample group decodes one token per sequence"
    pos_s = PAST_LEN + jnp.arange(Ls, dtype=jnp.int32)
    y_s, s_conv, s_ssm, s_k, s_v = _trunk(x_sample, pos_s, W, False, state_conv[0], state_ssm[0],
                                          cache_k, cache_v)
    return (y_p, y_s, p_conv, p_ssm, p_k, p_v, s_conv, s_ssm, s_k, s_v)
```

```python
import functools

import jax
import jax.numpy as jnp
from jax import lax
from jax.experimental import pallas as pl
from jax.experimental.pallas import tpu as pltpu

F32 = jnp.float32
BF16 = jnp.bfloat16
HIGHEST = lax.Precision.HIGHEST

EPS = 1e-6
ROPE_THETA = 10000.0
SSD_CHUNK = 128
SSM_HEAD_DIM = 64
SSM_STATE = 128
SSM_GROUPS = 8
CONV_W = 4
HEAD_DIM = 128
KV_HEADS = 8
DIL_RATES = (1, 4, 16)
DIL_SLOTS = 128
ATT_BLOCK = 128
PAST_LEN = 16384

V7X_VMEM_BYTES = 64 * 1024 * 1024
VMEM_LIMIT = V7X_VMEM_BYTES - 8 * 1024 * 1024
LANES = 128
CONV_TAIL = 8
IN_PROJ_TN = 768

NT_DIMS = (((1,), (1,)), ((), ()))
TN_DIMS = (((0,), (0,)), ((), ()))


def _params(*sem):
    return pltpu.CompilerParams(dimension_semantics=sem, vmem_limit_bytes=VMEM_LIMIT)


def _sigmoid(x):
    return 1.0 / (1.0 + jnp.exp(-x))


def _silu(x):
    return x * _sigmoid(x)


def _softplus(x):
    return jnp.maximum(x, 0.0) + jnp.log1p(jnp.exp(-jnp.abs(x)))


def _rms_scale(x):
    return lax.rsqrt(jnp.mean(x * x, axis=-1, keepdims=True) + EPS)


def _proj_kernel(*refs, norm, rope_tiles, has_res, n_out):
    it = iter(refs)
    x_ref = next(it)
    g_ref = next(it) if norm else None
    w_ref = next(it)
    cos_ref = next(it) if rope_tiles else None
    sin_ref = next(it) if rope_tiles else None
    res_ref = next(it) if has_res else None
    out_refs = [next(it) for _ in range(n_out)]
    xn_ref = next(it) if norm else None
    j = pl.program_id(1)

    if norm:
        @pl.when(j == 0)
        def _():
            x = x_ref[...]
            xn_ref[...] = ((x * _rms_scale(x)) * g_ref[...]).astype(BF16)
        lhs = xn_ref[...]
    else:
        lhs = x_ref[...]
    acc = jnp.dot(lhs, w_ref[...], preferred_element_type=F32)
    if has_res:
        acc = acc + res_ref[...]

    def store(vals):
        for o_ref in out_refs:
            o_ref[...] = vals.astype(o_ref.dtype)

    if not rope_tiles:
        store(acc)
    else:
        @pl.when(j < rope_tiles)
        def _():
            cos = cos_ref[...]
            sin = sin_ref[...]
            heads = []
            for h in range(acc.shape[1] // HEAD_DIM):
                a = acc[:, h * HEAD_DIM:(h + 1) * HEAD_DIM]
                heads.append(a * cos + pltpu.roll(a, HEAD_DIM // 2, 1) * sin)
            store(heads[0] if len(heads) == 1 else jnp.concatenate(heads, axis=1))

        @pl.when(j >= rope_tiles)
        def _():
            store(acc)


def _proj(x, w, *, gain=None, rope=None, rope_cols=None, res=None,
          out_dtypes=(F32,), tm=1024, tn=512, rows_per_seq=None):
    M, K = x.shape
    n_cols = w.shape[1]
    tm = min(tm, M)
    assert M % tm == 0 and n_cols % tn == 0
    norm = gain is not None
    rope_tiles = 0
    if rope is not None:
        rope_cols = n_cols if rope_cols is None else rope_cols
        assert rope_cols % tn == 0 and res is None
        rope_tiles = rope_cols // tn
    args = [x]
    in_specs = [pl.BlockSpec((tm, K), lambda i, j: (i, 0))]
    if norm:
        args.append(gain.reshape(1, K))
        in_specs.append(pl.BlockSpec((1, K), lambda i, j: (0, 0)))
    args.append(w)
    in_specs.append(pl.BlockSpec((K, tn), lambda i, j: (0, j)))
    if rope is not None:
        cos, sin = rope
        nseq = rows_per_seq // tm if rows_per_seq is not None else 1
        for t in (cos, sin):
            args.append(t)
            in_specs.append(pl.BlockSpec((tm, HEAD_DIM), lambda i, j: (i % nseq, 0)))
    if res is not None:
        args.append(res)
        in_specs.append(pl.BlockSpec((tm, tn), lambda i, j: (i, j)))
    out_shape = [jax.ShapeDtypeStruct((M, n_cols), d) for d in out_dtypes]
    out_specs = [pl.BlockSpec((tm, tn), lambda i, j: (i, j)) for _ in out_dtypes]
    outs = pl.pallas_call(
        functools.partial(_proj_kernel, norm=norm, rope_tiles=rope_tiles,
                          has_res=res is not None, n_out=len(out_dtypes)),
        grid=(M // tm, n_cols // tn),
        in_specs=in_specs,
        out_specs=out_specs,
        out_shape=out_shape,
        scratch_shapes=[pltpu.VMEM((tm, K), BF16)] if norm else [],
        compiler_params=_params("parallel", "arbitrary"),
        name="proj",
    )(*args)
    return outs[0] if len(outs) == 1 else outs


def _gateup_kernel(x_ref, g_ref, wg_ref, wu_ref, o_ref, xn_ref):
    @pl.when(pl.program_id(1) == 0)
    def _():
        x = x_ref[...]
        xn_ref[...] = ((x * _rms_scale(x)) * g_ref[...]).astype(BF16)
    xn = xn_ref[...]
    g = jnp.dot(xn, wg_ref[...], preferred_element_type=F32)
    u = jnp.dot(xn, wu_ref[...], preferred_element_type=F32)
    o_ref[...] = (_silu(g) * u).astype(o_ref.dtype)


def _gateup(x, gain, w_gu, *, tm=1024, tn=512):
    M, K = x.shape
    hidden = w_gu.shape[1] // 2
    tm = min(tm, M)
    assert M % tm == 0 and hidden % tn == 0
    nj = hidden // tn
    return pl.pallas_call(
        _gateup_kernel,
        grid=(M // tm, nj),
        in_specs=[pl.BlockSpec((tm, K), lambda i, j: (i, 0)),
                  pl.BlockSpec((1, K), lambda i, j: (0, 0)),
                  pl.BlockSpec((K, tn), lambda i, j: (0, j)),
                  pl.BlockSpec((K, tn), lambda i, j: (0, j + nj))],
        out_specs=pl.BlockSpec((tm, tn), lambda i, j: (i, j)),
        out_shape=jax.ShapeDtypeStruct((M, hidden), BF16),
        scratch_shapes=[pltpu.VMEM((tm, K), BF16)],
        compiler_params=_params("parallel", "arbitrary"),
        name="gateup",
    )(x, gain.reshape(1, K), w_gu, w_gu)


def _conv_silu(full_ref, lo, hi, cw_ref, cb_ref, woff):
    acc = cb_ref[:, woff + lo:woff + hi]
    for k in range(CONV_W):
        r0 = CONV_TAIL - (CONV_W - 1) + k
        acc = acc + full_ref[r0:r0 + SSD_CHUNK, lo:hi] * cw_ref[k:k + 1, woff + lo:woff + hi]
    return _silu(acc)


def _ssd_kernel(z_ref, x_ref, b_ref, c_ref, dt_ref, cw_ref, cb_ref,
                dtb_ref, alog_ref, dexp_ref, gn_ref,
                y_ref, hout_ref,
                h_ref, xf_ref, bf_ref, cf_ref):
    Q = SSD_CHUNK
    P = SSM_HEAD_DIM
    N = SSM_STATE
    d_inner = x_ref.shape[1]
    n_heads = d_inner // P
    hpg = n_heads // SSM_GROUPS
    gw = hpg * P
    c = pl.program_id(1)

    @pl.when(c == 0)
    def _():
        h_ref[...] = jnp.zeros_like(h_ref)
        xf_ref[0:CONV_TAIL, :] = jnp.zeros((CONV_TAIL, xf_ref.shape[1]), F32)
        bf_ref[0:CONV_TAIL, :] = jnp.zeros((CONV_TAIL, bf_ref.shape[1]), F32)
        cf_ref[0:CONV_TAIL, :] = jnp.zeros((CONV_TAIL, cf_ref.shape[1]), F32)

    xf_ref[CONV_TAIL:, :] = x_ref[...]
    bf_ref[CONV_TAIL:, :] = b_ref[...]
    cf_ref[CONV_TAIL:, :] = c_ref[...]

    dt = _softplus(dt_ref[...] + dtb_ref[...])
    dA = dt * (-jnp.exp(alog_ref[...]))
    row = lax.broadcasted_iota(jnp.int32, (Q, Q), 0)
    col = lax.broadcasted_iota(jnp.int32, (Q, Q), 1)
    causal = row >= col
    tri = jnp.where(causal, 1.0, 0.0).astype(F32)
    cs = jnp.dot(tri, dA, precision=HIGHEST, preferred_element_type=F32)
    csT = cs.T
    decay_end = jnp.exp(cs[Q - 1:Q, :] - cs)
    ecs = jnp.exp(cs)
    chunk_decay = jnp.broadcast_to(jnp.exp(csT[0:n_heads, Q - 1:Q]), (n_heads, N))

    lane_lo = lax.broadcasted_iota(jnp.int32, (Q, 2 * P), 1) < P

    def pair_cols(arr, q):
        a0 = jnp.broadcast_to(arr[:, 2 * q:2 * q + 1], (Q, 2 * P))
        a1 = jnp.broadcast_to(arr[:, 2 * q + 1:2 * q + 2], (Q, 2 * P))
        return jnp.where(lane_lo, a0, a1)

    woff_b = d_inner
    woff_c = d_inner + SSM_GROUPS * N
    for g in range(SSM_GROUPS):
        Bg = _conv_silu(bf_ref, g * N, (g + 1) * N, cw_ref, cb_ref, woff_b).astype(BF16)
        Cg = _conv_silu(cf_ref, g * N, (g + 1) * N, cw_ref, cb_ref, woff_c).astype(BF16)
        CB = lax.dot_general(Cg, Bg, NT_DIMS, preferred_element_type=F32)
        Hg = h_ref[g * hpg:(g + 1) * hpg].reshape(gw, N)
        y_off = lax.dot_general(Cg, Hg.astype(BF16), NT_DIMS, preferred_element_type=F32)

        ys = []
        xds = []
        for qq in range(hpg // 2):
            q = g * (hpg // 2) + qq
            lo, hi = q * 2 * P, (q + 1) * 2 * P
            xs = _conv_silu(xf_ref, lo, hi, cw_ref, cb_ref, 0)
            xdt = xs * pair_cols(dt, q)
            xds.append((xdt * pair_cols(decay_end, q)).astype(BF16))
            atts = []
            for hh in (2 * q, 2 * q + 1):
                seg = cs[:, hh:hh + 1] - csT[hh:hh + 1, :]
                atts.append((CB * jnp.exp(jnp.where(causal, seg, -jnp.inf))).astype(BF16))
            att = jnp.concatenate(atts, axis=1)
            xbd = jnp.concatenate([jnp.where(lane_lo, xdt, 0.0).astype(BF16),
                                   jnp.where(lane_lo, 0.0, xdt).astype(BF16)], axis=0)
            y = jnp.dot(att, xbd, preferred_element_type=F32)
            y = y + y_off[:, qq * 2 * P:(qq + 1) * 2 * P] * pair_cols(ecs, q)
            y = y + xs * dexp_ref[:, lo:hi]
            ys.append(y * _silu(z_ref[:, lo:hi]))

        ssq = ys[0] * ys[0]
        for y in ys[1:]:
            ssq = ssq + y * y
        scale = lax.rsqrt(jnp.sum(ssq, axis=-1, keepdims=True) / gw + EPS)
        for qq, y in enumerate(ys):
            lo = g * gw + qq * 2 * P
            y_ref[:, lo:lo + 2 * P] = ((y * scale) * gn_ref[:, lo:lo + 2 * P]).astype(y_ref.dtype)

        S = lax.dot_general(jnp.concatenate(xds, axis=1), Bg, TN_DIMS, preferred_element_type=F32)
        for r in range(hpg):
            hh = g * hpg + r
            h_ref[hh] = h_ref[hh] * chunk_decay[hh:hh + 1, :] + S[r * P:(r + 1) * P, :]

    xf_ref[0:CONV_TAIL, :] = xf_ref[Q:Q + CONV_TAIL, :]
    bf_ref[0:CONV_TAIL, :] = bf_ref[Q:Q + CONV_TAIL, :]
    cf_ref[0:CONV_TAIL, :] = cf_ref[Q:Q + CONV_TAIL, :]

    @pl.when(c == pl.num_programs(1) - 1)
    def _():
        hout_ref[0] = h_ref[...]


def _ssd_prompt(zxbcdt, batch, conv_w, conv_b, dt_bias, a_log, d_skip, gate_norm):
    M = zxbcdt.shape[0]
    L = M // batch
    H = dt_bias.shape[0]
    d_inner = H * SSM_HEAD_DIM
    bc = SSM_GROUPS * SSM_STATE
    conv_dim = d_inner + 2 * bc
    Q = SSD_CHUNK
    nc = L // Q
    assert L % Q == 0 and d_inner % bc == 0
    xb = d_inner // bc
    row_map = lambda col: (lambda b, c: (b * nc + c, col))
    const = lambda b, c: (0, 0)
    d_exp = jnp.repeat(d_skip.astype(F32), SSM_HEAD_DIM).reshape(1, d_inner)
    assert H <= LANES
    lane_pad = lambda v: jnp.pad(v.reshape(1, H), ((0, 0), (0, LANES - H)))
    y, h_final = pl.pallas_call(
        _ssd_kernel,
        grid=(batch, nc),
        in_specs=[pl.BlockSpec((Q, d_inner), row_map(0)),
                  pl.BlockSpec((Q, d_inner), row_map(1)),
                  pl.BlockSpec((Q, bc), row_map(2 * xb)),
                  pl.BlockSpec((Q, bc), row_map(2 * xb + 1)),
                  pl.BlockSpec((Q, LANES), row_map((d_inner + conv_dim) // LANES)),
                  pl.BlockSpec((CONV_W, conv_dim), const),
                  pl.BlockSpec((1, conv_dim), const),
                  pl.BlockSpec((1, LANES), const),
                  pl.BlockSpec((1, LANES), const),
                  pl.BlockSpec((1, d_inner), const),
                  pl.BlockSpec((1, d_inner), const)],
        out_specs=[pl.BlockSpec((Q, d_inner), row_map(0)),
                   pl.BlockSpec((1, H, SSM_HEAD_DIM, SSM_STATE), lambda b, c: (b, 0, 0, 0))],
        out_shape=[jax.ShapeDtypeStruct((M, d_inner), BF16),
                   jax.ShapeDtypeStruct((batch, H, SSM_HEAD_DIM, SSM_STATE), F32)],
        scratch_shapes=[pltpu.VMEM((H, SSM_HEAD_DIM, SSM_STATE), F32),
                        pltpu.VMEM((Q + CONV_TAIL, d_inner), F32),
                        pltpu.VMEM((Q + CONV_TAIL, bc), F32),
                        pltpu.VMEM((Q + CONV_TAIL, bc), F32)],
        compiler_params=_params("parallel", "arbitrary"),
        name="ssd_prompt",
    )(zxbcdt, zxbcdt, zxbcdt, zxbcdt, zxbcdt, conv_w, conv_b.reshape(1, conv_dim),
      lane_pad(dt_bias), lane_pad(a_log), d_exp, gate_norm.reshape(1, d_inner))
    return y, h_final


def _ssd_step_kernel(zxbc_ref, cs_ref, h0_ref, cw_ref, cb_ref, dtb_ref, alog_ref,
                     dexp_ref, gn_ref, y_ref, cso_ref, ho_ref):
    P = SSM_HEAD_DIM
    N = SSM_STATE
    n_heads = h0_ref.shape[2]
    d_inner = n_heads * P
    hpg = n_heads // SSM_GROUPS
    gw = hpg * P
    conv_dim = cw_ref.shape[1]

    raw = zxbc_ref[0, :, d_inner:d_inner + conv_dim]
    prev = cs_ref[0, 0]
    acc = cb_ref[...]
    for k in range(CONV_W - 1):
        acc = acc + prev[k:k + 1, :] * cw_ref[k:k + 1, :]
    acc = acc + raw * cw_ref[CONV_W - 1:CONV_W, :]
    xbc = _silu(acc)
    cso_ref[0, 0, 0:CONV_W - 2, :] = prev[1:CONV_W - 1, :]
    cso_ref[0, 0, CONV_W - 2:CONV_W - 1, :] = raw

    dt_raw = zxbc_ref[0, :, d_inner + conv_dim:d_inner + conv_dim + n_heads]
    dt = _softplus(dt_raw + dtb_ref[...])
    decay = jnp.exp(dt * (-jnp.exp(alog_ref[...])))
    z = zxbc_ref[0, :, 0:d_inner]

    eye = (lax.broadcasted_iota(jnp.int32, (P, P), 0) == lax.broadcasted_iota(jnp.int32, (P, P), 1))
    ys = []
    for g in range(SSM_GROUPS):
        Bg = xbc[:, d_inner + g * N:d_inner + (g + 1) * N]
        Cg = xbc[:, d_inner + SSM_GROUPS * N + g * N:d_inner + SSM_GROUPS * N + (g + 1) * N]
        Bb = jnp.broadcast_to(Bg, (P, N))
        for r in range(hpg):
            hh = g * hpg + r
            xs = xbc[:, hh * P:(hh + 1) * P]
            xdt = xs * dt[:, hh:hh + 1]
            xdiag = jnp.where(eye, jnp.broadcast_to(xdt, (P, P)), 0.0)
            outer = jnp.dot(xdiag, Bb, precision=HIGHEST, preferred_element_type=F32)
            ho_ref[0, 0, hh] = h0_ref[0, 0, hh] * decay[:, hh:hh + 1] + outer
        Hg = ho_ref[0, 0, g * hpg:(g + 1) * hpg].reshape(gw, N)
        yg = lax.dot_general(Cg, Hg, NT_DIMS, precision=HIGHEST, preferred_element_type=F32)
        lo, hi = g * gw, (g + 1) * gw
        yg = yg + xbc[:, lo:hi] * dexp_ref[:, lo:hi]
        yg = yg * _silu(z[:, lo:hi])
        yg = yg * lax.rsqrt(jnp.mean(yg * yg, axis=-1, keepdims=True) + EPS)
        ys.append(yg * gn_ref[:, lo:hi])
    y_ref[0] = jnp.concatenate(ys, axis=1).astype(y_ref.dtype)


def _ssd_step(zxbc, conv_state, ssm_state, conv_w, conv_b, dt_bias, a_log, d_skip, gate_norm):
    B = zxbc.shape[0]
    H = dt_bias.shape[0]
    d_inner = H * SSM_HEAD_DIM
    conv_dim = conv_w.shape[1]
    const = lambda b: (0, 0)
    d_exp = jnp.repeat(d_skip.astype(F32), SSM_HEAD_DIM).reshape(1, d_inner)
    cs4 = conv_state.reshape(1, B, CONV_W - 1, conv_dim)
    h5 = ssm_state.reshape(1, B, H, SSM_HEAD_DIM, SSM_STATE)
    return pl.pallas_call(
        _ssd_step_kernel,
        grid=(B,),
        in_specs=[pl.BlockSpec((1, 1, zxbc.shape[1]), lambda b: (b, 0, 0)),
                  pl.BlockSpec((1, 1, CONV_W - 1, conv_dim), lambda b: (0, b, 0, 0)),
                  pl.BlockSpec((1, 1, H, SSM_HEAD_DIM, SSM_STATE), lambda b: (0, b, 0, 0, 0)),
                  pl.BlockSpec((CONV_W, conv_dim), const),
                  pl.BlockSpec((1, conv_dim), const),
                  pl.BlockSpec((1, H), const),
                  pl.BlockSpec((1, H), const),
                  pl.BlockSpec((1, d_inner), const),
                  pl.BlockSpec((1, d_inner), const)],
        out_specs=[pl.BlockSpec((1, 1, d_inner), lambda b: (b, 0, 0)),
                   pl.BlockSpec((1, 1, CONV_W - 1, conv_dim), lambda b: (0, b, 0, 0)),
                   pl.BlockSpec((1, 1, H, SSM_HEAD_DIM, SSM_STATE), lambda b: (0, b, 0, 0, 0))],
        out_shape=[jax.ShapeDtypeStruct((B, 1, d_inner), BF16),
                   jax.ShapeDtypeStruct(cs4.shape, F32),
                   jax.ShapeDtypeStruct(h5.shape, F32)],
        compiler_params=_params("arbitrary"),
        name="ssd_step",
    )(zxbc.reshape(B, 1, -1), cs4, h5, conv_w, conv_b.reshape(1, conv_dim),
      dt_bias.reshape(1, H), a_log.reshape(1, H), d_exp, gate_norm.reshape(1, d_inner))


def _attn_kernel(q0_ref, q1_ref, q2_ref, k_ref, v_ref, o_ref):
    QB = ATT_BLOCK
    i0 = pl.program_id(2) * QB
    scale = HEAD_DIM ** -0.5
    scores = []
    vwins = []
    for q_ref, rate in zip((q0_ref, q1_ref, q2_ref), DIL_RATES):
        span = DIL_SLOTS * rate
        W = span + QB
        start = pl.multiple_of(jnp.maximum(i0 - span, 0), QB)
        kw = k_ref[pl.ds(start, W), :]
        vwins.append(v_ref[pl.ds(start, W), :])
        s = lax.dot_general(q_ref[...], kw, NT_DIMS, preferred_element_type=F32) * scale
        e = (lax.broadcasted_iota(jnp.int32, (QB, W), 0)
             - lax.broadcasted_iota(jnp.int32, (QB, W), 1))
        if rate > 1:
            e = jnp.where((e & (rate - 1)) == 0, e, 2 ** 30)
        d = e + (i0 - start)
        s = jnp.where(d >= 0, s, -jnp.inf)
        s = jnp.where(d <= span, s, -jnp.inf)
        scores.append(s)
    m = functools.reduce(jnp.maximum, [s.max(axis=1, keepdims=True) for s in scores])
    l = jnp.zeros((QB, 1), F32)
    o = jnp.zeros((QB, HEAD_DIM), F32)
    for s, vw in zip(scores, vwins):
        p = jnp.exp(s - m)
        l = l + p.sum(axis=1, keepdims=True)
        o = o + jnp.dot(p.astype(BF16), vw, preferred_element_type=F32)
    o_ref[...] = (o / l).astype(o_ref.dtype)


def _attn_prompt(q, kv, batch):
    M = q.shape[0]
    L = M // batch
    nq = L // ATT_BLOCK
    assert L % ATT_BLOCK == 0 and L >= DIL_SLOTS * max(DIL_RATES) + ATT_BLOCK
    qspec = lambda g: pl.BlockSpec((ATT_BLOCK, HEAD_DIM), lambda b, h, i: (b * nq + i, g * KV_HEADS + h))
    kspec = pl.BlockSpec((L, HEAD_DIM), lambda b, h, i: (b, h))
    vspec = pl.BlockSpec((L, HEAD_DIM), lambda b, h, i: (b, KV_HEADS + h))
    return pl.pallas_call(
        _attn_kernel,
        grid=(batch, KV_HEADS, nq),
        in_specs=[qspec(0), qspec(1), qspec(2), kspec, vspec],
        out_specs=pl.BlockSpec((ATT_BLOCK, HEAD_DIM), lambda b, h, i: (b * nq + i, h)),
        out_shape=jax.ShapeDtypeStruct((M, KV_HEADS * HEAD_DIM), BF16),
        compiler_params=_params("parallel", "parallel", "arbitrary"),
        name="attn_prompt",
    )(q, q, q, kv, kv)


def _bf16_round(a):
    return a.astype(BF16).astype(F32)


def _attn_step_kernel(q_ref, *refs):
    n = len(DIL_RATES)
    k_refs, v_refs = refs[0:n], refs[n:2 * n]
    kn_ref, vn_ref, o_ref = refs[2 * n:]
    scale = HEAD_DIM ** -0.5
    kn = _bf16_round(kn_ref[0, 0])
    vn = _bf16_round(vn_ref[0, 0])
    scores, new_scores = [], []
    for g, k_ref in enumerate(k_refs):
        qg = q_ref[0, g].astype(F32)
        kg = _bf16_round(k_ref[0, :, 0])
        scores.append(jnp.sum(kg * qg[None], axis=-1, keepdims=True) * scale)
        new_scores.append(jnp.sum(kn * qg, axis=-1, keepdims=True) * scale)
    m = functools.reduce(jnp.maximum, [s.max(axis=0) for s in scores] + new_scores)
    l = jnp.zeros_like(m)
    o = jnp.zeros((KV_HEADS, HEAD_DIM), F32)
    for s, s_new, v_ref in zip(scores, new_scores, v_refs):
        p = jnp.exp(s - m[None])
        p_new = jnp.exp(s_new - m)
        l = l + p.sum(axis=0) + p_new
        o = o + (_bf16_round(p) * _bf16_round(v_ref[0, :, 0])).sum(axis=0) + _bf16_round(p_new) * vn
    o_ref[0, 0] = (o / l).astype(o_ref.dtype)


def _attn_step(q, cache_k, cache_v, k_new, v_new):
    B, T = cache_k.shape[0], cache_k.shape[1]
    S = DIL_SLOTS
    assert T == S * max(DIL_RATES), "every slot of every dilation group lies inside the cached window"
    qg = q.reshape(B, len(DIL_RATES), KV_HEADS, HEAD_DIM)
    row_shape = (B, 1, KV_HEADS, HEAD_DIM)
    row_spec = pl.BlockSpec((1, 1, KV_HEADS, HEAD_DIM), lambda b: (b, 0, 0, 0))
    args = [qg]
    in_specs = [pl.BlockSpec((1, len(DIL_RATES), KV_HEADS, HEAD_DIM), lambda b: (b, 0, 0, 0))]
    for cache in (cache_k, cache_v):
        for rate in DIL_RATES:
            args.append(cache.reshape(B, T // rate, rate, KV_HEADS, HEAD_DIM))
            in_specs.append(pl.BlockSpec((1, S, 1, KV_HEADS, HEAD_DIM),
                                         lambda b, blk=T // rate // S - 1: (b, blk, 0, 0, 0)))
    args += [k_new.reshape(row_shape), v_new.reshape(row_shape)]
    in_specs += [row_spec, row_spec]
    o = pl.pallas_call(
        _attn_step_kernel,
        grid=(B,),
        in_specs=in_specs,
        out_specs=row_spec,
        out_shape=jax.ShapeDtypeStruct(row_shape, BF16),
        compiler_params=_params("parallel"),
        name="attn_step",
    )(*args)
    return o.reshape(B, KV_HEADS * HEAD_DIM)


def _rmsnorm_kernel(x_ref, g_ref, o_ref):
    x = x_ref[...]
    o_ref[...] = (x * _rms_scale(x)) * g_ref[...]


def _rmsnorm(x, gain, *, tm=512):
    M, K = x.shape
    tm = min(tm, M)
    assert M % tm == 0
    return pl.pallas_call(
        _rmsnorm_kernel,
        grid=(M // tm,),
        in_specs=[pl.BlockSpec((tm, K), lambda i: (i, 0)), pl.BlockSpec((1, K), lambda i: (0, 0))],
        out_specs=pl.BlockSpec((tm, K), lambda i: (i, 0)),
        out_shape=jax.ShapeDtypeStruct((M, K), F32),
        compiler_params=_params("parallel"),
        name="final_norm",
    )(x, gain.reshape(1, K))


def _rope_tables(pos):
    half = HEAD_DIM // 2
    inv = jnp.power(jnp.float32(ROPE_THETA), -jnp.arange(half, dtype=jnp.float32) / half)
    ang = pos.astype(jnp.float32)[:, None] * inv[None, :]
    cos = jnp.cos(ang)
    sin = jnp.sin(ang)
    return jnp.concatenate([cos, cos], axis=1), jnp.concatenate([-sin, sin], axis=1)


def _trunk(x, pos, W, prompt, conv_state=None, ssm_state=None, cache_k=None, cache_v=None):
    batch, L, D = x.shape
    M = batch * L
    x0 = x.reshape(M, D)
    d_inner = W["a_w_out"].shape[0]
    n_heads = W["a_dt_bias"].shape[0]
    conv_dim = W["a_conv_w"].shape[1]
    kv_dim = KV_HEADS * HEAD_DIM
    cos, sin = _rope_tables(pos)
    if not prompt:
        cos, sin = jnp.broadcast_to(cos, (M, HEAD_DIM)), jnp.broadcast_to(sin, (M, HEAD_DIM))
    rope = (cos, sin)
    rows_per_seq = L if prompt else M

    zxbcdt = _proj(x0, W["a_w_in"], gain=W["a_norm"], tn=IN_PROJ_TN)
    if prompt:
        y, ssm_new = _ssd_prompt(zxbcdt, batch, W["a_conv_w"], W["a_conv_b"], W["a_dt_bias"],
                                 W["a_log"], W["a_d"], W["a_gate_norm"])
        conv_new = zxbcdt.reshape(batch, L, -1)[:, L - (CONV_W - 1):, d_inner:d_inner + conv_dim]
    else:
        y, conv_new, ssm_new = _ssd_step(zxbcdt, conv_state, ssm_state, W["a_conv_w"], W["a_conv_b"],
                                         W["a_dt_bias"], W["a_log"], W["a_d"], W["a_gate_norm"])
        y = y.reshape(M, d_inner)
        conv_new = conv_new[0]
        ssm_new = ssm_new[0]
    x1 = _proj(y, W["a_w_out"], res=x0, tn=1024)
    h = _gateup(x1, W["ffn_norm"][0], W["ffn_w_gu"][0])
    x2 = _proj(h, W["ffn_w_down"][0], res=x1)

    kv_f32, kv_bf = _proj(x2, W["w_kv"], gain=W["kv_norm"], rope=rope, rope_cols=kv_dim,
                          rows_per_seq=rows_per_seq, out_dtypes=(F32, BF16), tn=1024)
    q = _proj(x2, W["b_w_q"], gain=W["b_norm"], rope=rope, rows_per_seq=rows_per_seq,
              out_dtypes=(BF16,), tn=1024)
    if prompt:
        o = _attn_prompt(q, kv_bf, batch)
    else:
        o = _attn_step(q, cache_k, cache_v, kv_f32[:, :kv_dim], kv_f32[:, kv_dim:])
    x3 = _proj(o, W["b_w_o"], res=x2, tn=1024)
    h = _gateup(x3, W["ffn_norm"][1], W["ffn_w_gu"][1])
    x4 = _proj(h, W["ffn_w_down"][1], res=x3)
    out = _rmsnorm(x4, W["final_norm"]).reshape(batch, L, D)
    k_new = kv_f32[:, :kv_dim].reshape(batch, L, KV_HEADS, HEAD_DIM)
    v_new = kv_f32[:, kv_dim:].reshape(batch, L, KV_HEADS, HEAD_DIM)
    return out, conv_new[None], ssm_new[None], k_new, v_new


def kernel(x_prompt, x_sample, state_conv, state_ssm, cache_k, cache_v, a_norm, a_w_in, a_conv_w, a_conv_b,
           a_dt_bias, a_log, a_d, a_gate_norm, a_w_out, kv_norm, w_kv, b_norm, b_w_q, b_w_o, ffn_norm,
           ffn_w_gu, ffn_w_down, final_norm):
    assert a_norm.shape[0] == 1 and b_norm.shape[0] == 1, "one Mamba-2 layer followed by one attention layer"
    n_in = a_w_in.shape[2]
    n_in_pad = -(-(n_in - a_dt_bias.shape[1] + LANES) // IN_PROJ_TN) * IN_PROJ_TN
    W = dict(
        a_norm=a_norm[0], a_w_in=jnp.pad(a_w_in[0].astype(BF16), ((0, 0), (0, n_in_pad - n_in))),
        a_conv_w=a_conv_w[0], a_conv_b=a_conv_b[0], a_dt_bias=a_dt_bias[0], a_log=a_log[0], a_d=a_d[0],
        a_gate_norm=a_gate_norm[0], a_w_out=a_w_out[0].astype(BF16),
        kv_norm=kv_norm, w_kv=w_kv.astype(BF16), b_norm=b_norm[0], b_w_q=b_w_q[0].astype(BF16),
        b_w_o=b_w_o[0].astype(BF16), ffn_norm=ffn_norm,
        ffn_w_gu=[w.astype(BF16) for w in ffn_w_gu], ffn_w_down=[w.astype(BF16) for w in ffn_w_down],
        final_norm=final_norm)

    Lp = x_prompt.shape[1]
    y_p, p_conv, p_ssm, p_k, p_v = _trunk(x_prompt, jnp.arange(Lp, dtype=jnp.int32), W, True)
    keep = min(DIL_SLOTS * max(DIL_RATES), Lp)
    p_k = p_k[:, Lp - keep:]
    p_v = p_v[:, Lp - keep:]

    Ls = x_sample.shape[1]
    assert Ls == 1, "sample group decodes one token per sequence"
    pos_s = PAST_LEN + jnp.arange(Ls, dtype=jnp.int32)
    y_s, s_conv, s_ssm, s_k, s_v = _trunk(x_sample, pos_s, W, False, state_conv[0], state_ssm[0],
                                          cache_k, cache_v)
    return (y_p, y_s, p_conv, p_ssm, p_k, p_v, s_conv, s_ssm, s_k, s_v)
```

```python
import functools

import jax
import jax.numpy as jnp
from jax import lax
from jax.experimental import pallas as pl
from jax.experimental.pallas import tpu as pltpu

F32 = jnp.float32
BF16 = jnp.bfloat16
HIGHEST = lax.Precision.HIGHEST

EPS = 1e-6
ROPE_THETA = 10000.0
SSD_CHUNK = 128
SSM_HEAD_DIM = 64
SSM_STATE = 128
SSM_GROUPS = 8
CONV_W = 4
HEAD_DIM = 128
KV_HEADS = 8
DIL_RATES = (1, 4, 16)
DIL_SLOTS = 128
ATT_BLOCK = 128
ATT_UNROLL = 8
ATT_PRESTRIDE = 4
PAST_LEN = 16384

V7X_VMEM_BYTES = 64 * 1024 * 1024
VMEM_LIMIT = V7X_VMEM_BYTES - 8 * 1024 * 1024
LANES = 128
CONV_TAIL = 8

NT_DIMS = (((1,), (1,)), ((), ()))
TN_DIMS = (((0,), (0,)), ((), ()))


def _params(*sem):
    return pltpu.CompilerParams(dimension_semantics=sem, vmem_limit_bytes=VMEM_LIMIT)


def _sigmoid(x):
    return 1.0 / (1.0 + jnp.exp(-x))


def _silu(x):
    return x * _sigmoid(x)


def _softplus(x):
    return jnp.maximum(x, 0.0) + jnp.log1p(jnp.exp(-jnp.abs(x)))


def _rms_scale(x):
    return lax.rsqrt(jnp.mean(x * x, axis=-1, keepdims=True) + EPS)


def _proj_kernel(*refs, norm, rope_tiles, has_res, has_aux):
    it = iter(refs)
    x_ref = next(it)
    g_ref = next(it) if norm else None
    w_ref = next(it)
    auxw_ref = next(it) if has_aux else None
    cos_ref = next(it) if rope_tiles else None
    sin_ref = next(it) if rope_tiles else None
    res_ref = next(it) if has_res else None
    o_ref = next(it)
    auxo_ref = next(it) if has_aux else None
    xn_ref = next(it) if norm else None
    j = pl.program_id(1)

    if norm:
        @pl.when(j == 0)
        def _():
            x = x_ref[...]
            xn_ref[...] = ((x * _rms_scale(x)) * g_ref[...]).astype(BF16)
        lhs = xn_ref[...]
    else:
        lhs = x_ref[...]
    acc = jnp.dot(lhs, w_ref[...], preferred_element_type=F32)
    if has_res:
        acc = acc + res_ref[...]

    if not rope_tiles:
        o_ref[...] = acc.astype(o_ref.dtype)
    else:
        @pl.when(j < rope_tiles)
        def _():
            cos = cos_ref[...]
            sin = sin_ref[...]
            for h in range(acc.shape[1] // HEAD_DIM):
                a = acc[:, h * HEAD_DIM:(h + 1) * HEAD_DIM]
                o_ref[:, h * HEAD_DIM:(h + 1) * HEAD_DIM] = (
                    a * cos + pltpu.roll(a, HEAD_DIM // 2, 1) * sin).astype(o_ref.dtype)

        @pl.when(j >= rope_tiles)
        def _():
            o_ref[...] = acc.astype(o_ref.dtype)

    if has_aux:
        @pl.when(j == pl.num_programs(1) - 1)
        def _():
            auxo_ref[...] = jnp.dot(lhs, auxw_ref[...], preferred_element_type=F32)


def _proj(x, w, *, n_cols=None, layer=0, gain=None, aux_w=None, rope=None, rope_cols=None, res=None,
          out_dtype=F32, tm=1024, tn=512, rows_per_seq=None):
    M, K = x.shape
    n_cols = w.shape[-1] if n_cols is None else n_cols
    tm = min(tm, M)
    assert M % tm == 0 and n_cols % tn == 0
    norm = gain is not None
    rope_tiles = 0
    if rope is not None:
        rope_cols = n_cols if rope_cols is None else rope_cols
        assert rope_cols % tn == 0 and res is None
        rope_tiles = rope_cols // tn
    args = [x]
    in_specs = [pl.BlockSpec((tm, K), lambda i, j: (i, 0))]
    if norm:
        args.append(gain.reshape(1, K))
        in_specs.append(pl.BlockSpec((1, K), lambda i, j: (0, 0)))
    args.append(w)
    if w.ndim == 3:
        in_specs.append(pl.BlockSpec((None, K, tn), lambda i, j: (layer, 0, j)))
    else:
        in_specs.append(pl.BlockSpec((K, tn), lambda i, j: (0, j)))
    if aux_w is not None:
        args.append(aux_w)
        in_specs.append(pl.BlockSpec(aux_w.shape, lambda i, j: (0, 0)))
    if rope is not None:
        cos, sin = rope
        nseq = rows_per_seq // tm if rows_per_seq is not None else 1
        for t in (cos, sin):
            args.append(t)
            in_specs.append(pl.BlockSpec((tm, HEAD_DIM), lambda i, j: (i % nseq, 0)))
    if res is not None:
        args.append(res)
        in_specs.append(pl.BlockSpec((tm, tn), lambda i, j: (i, j)))
    out_shape = [jax.ShapeDtypeStruct((M, n_cols), out_dtype)]
    out_specs = [pl.BlockSpec((tm, tn), lambda i, j: (i, j))]
    if aux_w is not None:
        out_shape.append(jax.ShapeDtypeStruct((M, aux_w.shape[1]), F32))
        out_specs.append(pl.BlockSpec((tm, aux_w.shape[1]), lambda i, j: (i, 0)))
    outs = pl.pallas_call(
        functools.partial(_proj_kernel, norm=norm, rope_tiles=rope_tiles,
                          has_res=res is not None, has_aux=aux_w is not None),
        grid=(M // tm, n_cols // tn),
        in_specs=in_specs,
        out_specs=out_specs,
        out_shape=out_shape,
        scratch_shapes=[pltpu.VMEM((tm, K), BF16)] if norm else [],
        compiler_params=_params("parallel", "arbitrary"),
        name="proj",
    )(*args)
    return outs[0] if aux_w is None else outs


def _gateup_kernel(x_ref, g_ref, wg_ref, wu_ref, o_ref, xn_ref):
    @pl.when(pl.program_id(1) == 0)
    def _():
        x = x_ref[...]
        xn_ref[...] = ((x * _rms_scale(x)) * g_ref[...]).astype(BF16)
    xn = xn_ref[...]
    g = jnp.dot(xn, wg_ref[...], preferred_element_type=F32)
    u = jnp.dot(xn, wu_ref[...], preferred_element_type=F32)
    o_ref[...] = (_silu(g) * u).astype(o_ref.dtype)


def _gateup(x, gain, w_gu, layer, *, tm=1024, tn=512):
    M, K = x.shape
    hidden = w_gu.shape[2] // 2
    tm = min(tm, M)
    assert M % tm == 0 and hidden % tn == 0
    nj = hidden // tn
    return pl.pallas_call(
        _gateup_kernel,
        grid=(M // tm, nj),
        in_specs=[pl.BlockSpec((tm, K), lambda i, j: (i, 0)),
                  pl.BlockSpec((1, K), lambda i, j: (0, 0)),
                  pl.BlockSpec((None, K, tn), lambda i, j: (layer, 0, j)),
                  pl.BlockSpec((None, K, tn), lambda i, j: (layer, 0, j + nj))],
        out_specs=pl.BlockSpec((tm, tn), lambda i, j: (i, j)),
        out_shape=jax.ShapeDtypeStruct((M, hidden), BF16),
        scratch_shapes=[pltpu.VMEM((tm, K), BF16)],
        compiler_params=_params("parallel", "arbitrary"),
        name="gateup",
    )(x, gain.reshape(1, K), w_gu, w_gu)


def _conv_silu(full_ref, lo, hi, cw_ref, cb_ref, woff):
    acc = cb_ref[:, woff + lo:woff + hi]
    for k in range(CONV_W):
        r0 = CONV_TAIL - (CONV_W - 1) + k
        acc = acc + full_ref[r0:r0 + SSD_CHUNK, lo:hi] * cw_ref[k:k + 1, woff + lo:woff + hi]
    return _silu(acc)


def _ssd_kernel(z_ref, x_ref, b_ref, c_ref, dt_ref, cw_ref, cb_ref,
                dtb_ref, alog_ref, dexp_ref, gn_ref,
                y_ref, hout_ref,
                h_ref, xf_ref, bf_ref, cf_ref):
    Q = SSD_CHUNK
    P = SSM_HEAD_DIM
    N = SSM_STATE
    d_inner = x_ref.shape[1]
    n_heads = d_inner // P
    hpg = n_heads // SSM_GROUPS
    gw = hpg * P
    c = pl.program_id(1)

    @pl.when(c == 0)
    def _():
        h_ref[...] = jnp.zeros_like(h_ref)
        xf_ref[0:CONV_TAIL, :] = jnp.zeros((CONV_TAIL, xf_ref.shape[1]), F32)
        bf_ref[0:CONV_TAIL, :] = jnp.zeros((CONV_TAIL, bf_ref.shape[1]), F32)
        cf_ref[0:CONV_TAIL, :] = jnp.zeros((CONV_TAIL, cf_ref.shape[1]), F32)

    xf_ref[CONV_TAIL:, :] = x_ref[...]
    bf_ref[CONV_TAIL:, :] = b_ref[...]
    cf_ref[CONV_TAIL:, :] = c_ref[...]

    dt = _softplus(dt_ref[...] + dtb_ref[...])
    dA = dt * (-jnp.exp(alog_ref[...]))
    row = lax.broadcasted_iota(jnp.int32, (Q, Q), 0)
    col = lax.broadcasted_iota(jnp.int32, (Q, Q), 1)
    causal = row >= col
    tri = jnp.where(causal, 1.0, 0.0).astype(F32)
    cs = jnp.dot(tri, dA, precision=HIGHEST, preferred_element_type=F32)
    csT = cs.T
    decay_end = jnp.exp(cs[Q - 1:Q, :] - cs)
    ecs = jnp.exp(cs)
    chunk_decay = jnp.broadcast_to(jnp.exp(csT[0:n_heads, Q - 1:Q]), (n_heads, N))

    lane_lo = lax.broadcasted_iota(jnp.int32, (Q, 2 * P), 1) < P

    def pair_cols(arr, q):
        a0 = jnp.broadcast_to(arr[:, 2 * q:2 * q + 1], (Q, 2 * P))
        a1 = jnp.broadcast_to(arr[:, 2 * q + 1:2 * q + 2], (Q, 2 * P))
        return jnp.where(lane_lo, a0, a1)

    woff_b = d_inner
    woff_c = d_inner + SSM_GROUPS * N
    for g in range(SSM_GROUPS):
        Bg = _conv_silu(bf_ref, g * N, (g + 1) * N, cw_ref, cb_ref, woff_b).astype(BF16)
        Cg = _conv_silu(cf_ref, g * N, (g + 1) * N, cw_ref, cb_ref, woff_c).astype(BF16)
        CB = lax.dot_general(Cg, Bg, NT_DIMS, preferred_element_type=F32)
        Hg = h_ref[g * hpg:(g + 1) * hpg].reshape(gw, N)
        y_off = lax.dot_general(Cg, Hg.astype(BF16), NT_DIMS, preferred_element_type=F32)

        ys = []
        xds = []
        for qq in range(hpg // 2):
            q = g * (hpg // 2) + qq
            lo, hi = q * 2 * P, (q + 1) * 2 * P
            xs = _conv_silu(xf_ref, lo, hi, cw_ref, cb_ref, 0)
            xdt = xs * pair_cols(dt, q)
            xds.append((xdt * pair_cols(decay_end, q)).astype(BF16))
            atts = []
            for hh in (2 * q, 2 * q + 1):
                seg = cs[:, hh:hh + 1] - csT[hh:hh + 1, :]
                atts.append((CB * jnp.exp(jnp.where(causal, seg, -jnp.inf))).astype(BF16))
            att = jnp.concatenate(atts, axis=1)
            xbd = jnp.concatenate([jnp.where(lane_lo, xdt, 0.0).astype(BF16),
                                   jnp.where(lane_lo, 0.0, xdt).astype(BF16)], axis=0)
            y = jnp.dot(att, xbd, preferred_element_type=F32)
            y = y + y_off[:, qq * 2 * P:(qq + 1) * 2 * P] * pair_cols(ecs, q)
            y = y + xs * dexp_ref[:, lo:hi]
            ys.append(y * _silu(z_ref[:, lo:hi]))

        ssq = ys[0] * ys[0]
        for y in ys[1:]:
            ssq = ssq + y * y
        scale = lax.rsqrt(jnp.sum(ssq, axis=-1, keepdims=True) / gw + EPS)
        for qq, y in enumerate(ys):
            lo = g * gw + qq * 2 * P
            y_ref[:, lo:lo + 2 * P] = ((y * scale) * gn_ref[:, lo:lo + 2 * P]).astype(y_ref.dtype)

        S = lax.dot_general(jnp.concatenate(xds, axis=1), Bg, TN_DIMS, preferred_element_type=F32)
        for r in range(hpg):
            hh = g * hpg + r
            h_ref[hh] = h_ref[hh] * chunk_decay[hh:hh + 1, :] + S[r * P:(r + 1) * P, :]

    xf_ref[0:CONV_TAIL, :] = xf_ref[Q:Q + CONV_TAIL, :]
    bf_ref[0:CONV_TAIL, :] = bf_ref[Q:Q + CONV_TAIL, :]
    cf_ref[0:CONV_TAIL, :] = cf_ref[Q:Q + CONV_TAIL, :]

    @pl.when(c == pl.num_programs(1) - 1)
    def _():
        hout_ref[0] = h_ref[...]


def _ssd_prompt(zxbc, dt, batch, conv_w, conv_b, dt_bias, a_log, d_skip, gate_norm):
    M = zxbc.shape[0]
    L = M // batch
    H = dt_bias.shape[0]
    d_inner = H * SSM_HEAD_DIM
    bc = SSM_GROUPS * SSM_STATE
    conv_dim = d_inner + 2 * bc
    Q = SSD_CHUNK
    nc = L // Q
    assert L % Q == 0 and d_inner % bc == 0
    xb = d_inner // bc
    row_map = lambda col: (lambda b, c: (b * nc + c, col))
    const = lambda b, c: (0, 0)
    d_exp = jnp.repeat(d_skip.astype(F32), SSM_HEAD_DIM).reshape(1, d_inner)
    assert H <= LANES
    lane_pad = lambda v: jnp.pad(v.reshape(1, H), ((0, 0), (0, LANES - H)))
    y, h_final = pl.pallas_call(
        _ssd_kernel,
        grid=(batch, nc),
        in_specs=[pl.BlockSpec((Q, d_inner), row_map(0)),
                  pl.BlockSpec((Q, d_inner), row_map(1)),
                  pl.BlockSpec((Q, bc), row_map(2 * xb)),
                  pl.BlockSpec((Q, bc), row_map(2 * xb + 1)),
                  pl.BlockSpec((Q, LANES), row_map(0)),
                  pl.BlockSpec((CONV_W, conv_dim), const),
                  pl.BlockSpec((1, conv_dim), const),
                  pl.BlockSpec((1, LANES), const),
                  pl.BlockSpec((1, LANES), const),
                  pl.BlockSpec((1, d_inner), const),
                  pl.BlockSpec((1, d_inner), const)],
        out_specs=[pl.BlockSpec((Q, d_inner), row_map(0)),
                   pl.BlockSpec((1, H, SSM_HEAD_DIM, SSM_STATE), lambda b, c: (b, 0, 0, 0))],
        out_shape=[jax.ShapeDtypeStruct((M, d_inner), BF16),
                   jax.ShapeDtypeStruct((batch, H, SSM_HEAD_DIM, SSM_STATE), F32)],
        scratch_shapes=[pltpu.VMEM((H, SSM_HEAD_DIM, SSM_STATE), F32),
                        pltpu.VMEM((Q + CONV_TAIL, d_inner), F32),
                        pltpu.VMEM((Q + CONV_TAIL, bc), F32),
                        pltpu.VMEM((Q + CONV_TAIL, bc), F32)],
        compiler_params=_params("parallel", "arbitrary"),
        name="ssd_prompt",
    )(zxbc, zxbc, zxbc, zxbc, dt, conv_w, conv_b.reshape(1, conv_dim),
      lane_pad(dt_bias), lane_pad(a_log), d_exp, gate_norm.reshape(1, d_inner))
    return y, h_final


def _ssd_step_kernel(zxbc_ref, dt_ref, cs_ref, h0_ref, cw_ref, cb_ref, dtb_ref, alog_ref,
                     dexp_ref, gn_ref, y_ref, cso_ref, ho_ref):
    P = SSM_HEAD_DIM
    N = SSM_STATE
    n_heads = h0_ref.shape[2]
    d_inner = n_heads * P
    hpg = n_heads // SSM_GROUPS
    gw = hpg * P
    conv_dim = cw_ref.shape[1]

    raw = zxbc_ref[0, :, d_inner:d_inner + conv_dim]
    prev = cs_ref[0, 0]
    acc = cb_ref[...]
    for k in range(CONV_W - 1):
        acc = acc + prev[k:k + 1, :] * cw_ref[k:k + 1, :]
    acc = acc + raw * cw_ref[CONV_W - 1:CONV_W, :]
    xbc = _silu(acc)
    cso_ref[0, 0, 0:CONV_W - 2, :] = prev[1:CONV_W - 1, :]
    cso_ref[0, 0, CONV_W - 2:CONV_W - 1, :] = raw

    dt = _softplus(dt_ref[0, :, 0:n_heads] + dtb_ref[...])
    decay = jnp.exp(dt * (-jnp.exp(alog_ref[...])))
    z = zxbc_ref[0, :, 0:d_inner]

    eye = (lax.broadcasted_iota(jnp.int32, (P, P), 0) == lax.broadcasted_iota(jnp.int32, (P, P), 1))
    ys = []
    for g in range(SSM_GROUPS):
        Bg = xbc[:, d_inner + g * N:d_inner + (g + 1) * N]
        Cg = xbc[:, d_inner + SSM_GROUPS * N + g * N:d_inner + SSM_GROUPS * N + (g + 1) * N]
        Bb = jnp.broadcast_to(Bg, (P, N))
        for r in range(hpg):
            hh = g * hpg + r
            xs = xbc[:, hh * P:(hh + 1) * P]
            xdt = xs * dt[:, hh:hh + 1]
            xdiag = jnp.where(eye, jnp.broadcast_to(xdt, (P, P)), 0.0)
            outer = jnp.dot(xdiag, Bb, precision=HIGHEST, preferred_element_type=F32)
            ho_ref[0, 0, hh] = h0_ref[0, 0, hh] * decay[:, hh:hh + 1] + outer
        Hg = ho_ref[0, 0, g * hpg:(g + 1) * hpg].reshape(gw, N)
        yg = lax.dot_general(Cg, Hg, NT_DIMS, precision=HIGHEST, preferred_element_type=F32)
        lo, hi = g * gw, (g + 1) * gw
        yg = yg + xbc[:, lo:hi] * dexp_ref[:, lo:hi]
        yg = yg * _silu(z[:, lo:hi])
        yg = yg * lax.rsqrt(jnp.mean(yg * yg, axis=-1, keepdims=True) + EPS)
        ys.append(yg * gn_ref[:, lo:hi])
    y_ref[0] = jnp.concatenate(ys, axis=1).astype(y_ref.dtype)


def _ssd_step(zxbc, dt, conv_state, ssm_state, conv_w, conv_b, dt_bias, a_log, d_skip, gate_norm):
    B = zxbc.shape[0]
    H = dt_bias.shape[0]
    d_inner = H * SSM_HEAD_DIM
    conv_dim = conv_w.shape[1]
    const = lambda b: (0, 0)
    d_exp = jnp.repeat(d_skip.astype(F32), SSM_HEAD_DIM).reshape(1, d_inner)
    cs4 = conv_state.reshape(1, B, CONV_W - 1, conv_dim)
    h5 = ssm_state.reshape(1, B, H, SSM_HEAD_DIM, SSM_STATE)
    return pl.pallas_call(
        _ssd_step_kernel,
        grid=(B,),
        in_specs=[pl.BlockSpec((1, 1, zxbc.shape[1]), lambda b: (b, 0, 0)),
                  pl.BlockSpec((1, 1, dt.shape[1]), lambda b: (b, 0, 0)),
                  pl.BlockSpec((1, 1, CONV_W - 1, conv_dim), lambda b: (0, b, 0, 0)),
                  pl.BlockSpec((1, 1, H, SSM_HEAD_DIM, SSM_STATE), lambda b: (0, b, 0, 0, 0)),
                  pl.BlockSpec((CONV_W, conv_dim), const),
                  pl.BlockSpec((1, conv_dim), const),
                  pl.BlockSpec((1, H), const),
                  pl.BlockSpec((1, H), const),
                  pl.BlockSpec((1, d_inner), const),
                  pl.BlockSpec((1, d_inner), const)],
        out_specs=[pl.BlockSpec((1, 1, d_inner), lambda b: (b, 0, 0)),
                   pl.BlockSpec((1, 1, CONV_W - 1, conv_dim), lambda b: (0, b, 0, 0)),
                   pl.BlockSpec((1, 1, H, SSM_HEAD_DIM, SSM_STATE), lambda b: (0, b, 0, 0, 0))],
        out_shape=[jax.ShapeDtypeStruct((B, 1, d_inner), BF16),
                   jax.ShapeDtypeStruct(cs4.shape, F32),
                   jax.ShapeDtypeStruct(h5.shape, F32)],
        compiler_params=_params("arbitrary"),
        name="ssd_step",
    )(zxbc.reshape(B, 1, -1), dt.reshape(B, 1, -1), cs4, h5, conv_w, conv_b.reshape(1, conv_dim),
      dt_bias.reshape(1, H), a_log.reshape(1, H), d_exp, gate_norm.reshape(1, d_inner))


def _attn_kernel(q0_ref, q1_ref, q2_ref, k_ref, v_ref, o_ref, on_ref, lse_ref, qs_ref, ks_ref, vs_ref):
    L = k_ref.shape[0]
    QB = ATT_BLOCK
    KW = QB + DIL_SLOTS
    P = ATT_PRESTRIDE
    p_shift = P.bit_length() - 1
    scale = HEAD_DIM ** -0.5
    e = lax.broadcasted_iota(jnp.int32, (QB, KW), 0) - lax.broadcasted_iota(jnp.int32, (QB, KW), 1)

    staged = {rate: rate > P for rate in DIL_RATES}
    for src_ref, dst_ref, needed in ((q2_ref, qs_ref, staged[DIL_RATES[2]]), (k_ref, ks_ref, any(staged.values())),
                                     (v_ref, vs_ref, any(staged.values()))):
        if needed:
            def stage(i, carry, src_ref=src_ref, dst_ref=dst_ref):
                cp = i & (P - 1)
                t = i >> p_shift
                dst = pl.multiple_of(cp * (L // P) + t * QB, QB)
                dst_ref[pl.ds(dst, QB), :] = src_ref[pl.ds(cp + P * QB * t, QB, stride=P), :]
                return carry
            lax.fori_loop(0, L // QB, stage, 0, unroll=4)

    for g, (q_ref, rate) in enumerate(zip((q0_ref, q1_ref, q2_ref), DIL_RATES)):
        shift = rate.bit_length() - 1

        def unit(n, carry, g=g, q_ref=q_ref, rate=rate, shift=shift):
            c = n & (rate - 1)
            u0 = (n >> shift) * QB
            v0 = jnp.maximum(u0 - DIL_SLOTS, 0)
            q_rows = pl.ds(c + rate * u0, QB, stride=rate)
            if staged[rate]:
                base = (c & (P - 1)) * (L // P) + (c >> p_shift)
                q = qs_ref[pl.ds(base + (rate // P) * u0, QB, stride=rate // P), :].astype(BF16)
                k_rows = pl.ds(base + (rate // P) * v0, KW, stride=rate // P)
                k = ks_ref[k_rows, :].astype(BF16)
                v = vs_ref[k_rows, :].astype(BF16)
            else:
                k_rows = pl.ds(c + rate * v0, KW, stride=rate)
                q = q_ref[q_rows, :].astype(BF16)
                k = k_ref[k_rows, :].astype(BF16)
                v = v_ref[k_rows, :].astype(BF16)
            s = lax.dot_general(q, k, NT_DIMS, preferred_element_type=F32) * scale
            d = e + (u0 - v0)
            s = jnp.where(d >= 0, s, -jnp.inf)
            s = jnp.where(d <= DIL_SLOTS, s, -jnp.inf)
            m = s.max(axis=1, keepdims=True)
            p = jnp.exp(s - m)
            l = p.sum(axis=1, keepdims=True)
            o = jnp.dot(p.astype(BF16), v, preferred_element_type=F32)
            on_ref[g, q_rows, :] = o / l
            lse_ref[g, q_rows, :] = jnp.broadcast_to(m + jnp.log(l), (QB, HEAD_DIM))
            return carry

        lax.fori_loop(0, L // QB, unit, 0, unroll=ATT_UNROLL)

    def mix(i, carry):
        rows = pl.ds(pl.multiple_of(i * QB, QB), QB)
        lses = [lse_ref[g, rows, :] for g in range(len(DIL_RATES))]
        m = functools.reduce(jnp.maximum, lses)
        ws = [jnp.exp(x - m) for x in lses]
        num = functools.reduce(jnp.add, [w * on_ref[g, rows, :] for g, w in enumerate(ws)])
        o_ref[rows, :] = (num / functools.reduce(jnp.add, ws)).astype(o_ref.dtype)
        return carry

    lax.fori_loop(0, L // QB, mix, 0, unroll=2)


def _attn_prompt(q, kv, batch):
    M = q.shape[0]
    L = M // batch
    n_grp = len(DIL_RATES)
    assert all(r & (r - 1) == 0 and L % (ATT_BLOCK * r) == 0 and L >= r * (ATT_BLOCK + DIL_SLOTS)
               for r in DIL_RATES)
    qspec = lambda g: pl.BlockSpec((L, HEAD_DIM), lambda b, h: (b, g * KV_HEADS + h))
    kspec = pl.BlockSpec((L, HEAD_DIM), lambda b, h: (b, h))
    vspec = pl.BlockSpec((L, HEAD_DIM), lambda b, h: (b, KV_HEADS + h))
    return pl.pallas_call(
        _attn_kernel,
        grid=(batch, KV_HEADS),
        in_specs=[qspec(0), qspec(1), qspec(2), kspec, vspec],
        out_specs=pl.BlockSpec((L, HEAD_DIM), lambda b, h: (b, h)),
        out_shape=jax.ShapeDtypeStruct((M, KV_HEADS * HEAD_DIM), BF16),
        scratch_shapes=[pltpu.VMEM((n_grp, L, HEAD_DIM), F32), pltpu.VMEM((n_grp, L, HEAD_DIM), F32)]
        + [pltpu.VMEM((L, HEAD_DIM), F32)] * 3,
        compiler_params=_params("parallel", "arbitrary"),
        name="attn_prompt",
    )(q, q, q, kv, kv)


def _bf16_round(a):
    return a.astype(BF16).astype(F32)


def _attn_step_kernel(q_ref, *refs):
    n = len(DIL_RATES)
    k_refs, v_refs = refs[0:n], refs[n:2 * n]
    kn_ref, vn_ref, o_ref = refs[2 * n:]
    scale = HEAD_DIM ** -0.5
    kn = _bf16_round(kn_ref[0, 0])
    vn = _bf16_round(vn_ref[0, 0])
    scores, new_scores = [], []
    for g, k_ref in enumerate(k_refs):
        qg = _bf16_round(q_ref[0, g])
        kg = _bf16_round(k_ref[0, :, 0])
        scores.append(jnp.sum(kg * qg[None], axis=-1, keepdims=True) * scale)
        new_scores.append(jnp.sum(kn * qg, axis=-1, keepdims=True) * scale)
    m = functools.reduce(jnp.maximum, [s.max(axis=0) for s in scores] + new_scores)
    l = jnp.zeros_like(m)
    o = jnp.zeros((KV_HEADS, HEAD_DIM), F32)
    for s, s_new, v_ref in zip(scores, new_scores, v_refs):
        p = jnp.exp(s - m[None])
        p_new = jnp.exp(s_new - m)
        l = l + p.sum(axis=0) + p_new
        o = o + (_bf16_round(p) * _bf16_round(v_ref[0, :, 0])).sum(axis=0) + _bf16_round(p_new) * vn
    o_ref[0, 0] = (o / l).astype(o_ref.dtype)


def _attn_step(q, cache_k, cache_v, k_new, v_new):
    B, T = cache_k.shape[0], cache_k.shape[1]
    S = DIL_SLOTS
    assert T == S * max(DIL_RATES), "every slot of every dilation group lies inside the cached window"
    qg = q.reshape(B, len(DIL_RATES), KV_HEADS, HEAD_DIM)
    row_shape = (B, 1, KV_HEADS, HEAD_DIM)
    row_spec = pl.BlockSpec((1, 1, KV_HEADS, HEAD_DIM), lambda b: (b, 0, 0, 0))
    args = [qg]
    in_specs = [pl.BlockSpec((1, len(DIL_RATES), KV_HEADS, HEAD_DIM), lambda b: (b, 0, 0, 0))]
    for cache in (cache_k, cache_v):
        for rate in DIL_RATES:
            args.append(cache.reshape(B, T // rate, rate, KV_HEADS, HEAD_DIM))
            in_specs.append(pl.BlockSpec((1, S, 1, KV_HEADS, HEAD_DIM),
                                         lambda b, blk=T // rate // S - 1: (b, blk, 0, 0, 0)))
    args += [k_new.reshape(row_shape), v_new.reshape(row_shape)]
    in_specs += [row_spec, row_spec]
    o = pl.pallas_call(
        _attn_step_kernel,
        grid=(B,),
        in_specs=in_specs,
        out_specs=row_spec,
        out_shape=jax.ShapeDtypeStruct(row_shape, BF16),
        compiler_params=_params("parallel"),
        name="attn_step",
    )(*args)
    return o.reshape(B, KV_HEADS * HEAD_DIM)


def _rmsnorm_kernel(x_ref, g_ref, o_ref):
    x = x_ref[...]
    o_ref[...] = (x * _rms_scale(x)) * g_ref[...]


def _rmsnorm(x, gain, *, tm=512):
    M, K = x.shape
    tm = min(tm, M)
    assert M % tm == 0
    return pl.pallas_call(
        _rmsnorm_kernel,
        grid=(M // tm,),
        in_specs=[pl.BlockSpec((tm, K), lambda i: (i, 0)), pl.BlockSpec((1, K), lambda i: (0, 0))],
        out_specs=pl.BlockSpec((tm, K), lambda i: (i, 0)),
        out_shape=jax.ShapeDtypeStruct((M, K), F32),
        compiler_params=_params("parallel"),
        name="final_norm",
    )(x, gain.reshape(1, K))


def _rope_tables(pos):
    half = HEAD_DIM // 2
    inv = jnp.power(jnp.float32(ROPE_THETA), -jnp.arange(half, dtype=jnp.float32) / half)
    ang = pos.astype(jnp.float32)[:, None] * inv[None, :]
    cos = jnp.cos(ang)
    sin = jnp.sin(ang)
    return jnp.concatenate([cos, cos], axis=1), jnp.concatenate([-sin, sin], axis=1)


def _trunk(x, pos, W, prompt, conv_state=None, ssm_state=None, cache_k=None, cache_v=None):
    batch, L, D = x.shape
    M = batch * L
    x0 = x.reshape(M, D)
    d_inner = W["a_w_out"].shape[0]
    n_heads = W["a_dt_bias"].shape[0]
    conv_dim = W["a_conv_w"].shape[1]
    kv_dim = KV_HEADS * HEAD_DIM
    cos, sin = _rope_tables(pos)
    if not prompt:
        cos, sin = jnp.broadcast_to(cos, (M, HEAD_DIM)), jnp.broadcast_to(sin, (M, HEAD_DIM))
    rope = (cos, sin)
    rows_per_seq = L if prompt else M

    zxbc, dt = _proj(x0, W["a_w_in"], n_cols=d_inner + conv_dim, gain=W["a_norm"], aux_w=W["a_w_dt"], tn=1024)
    if prompt:
        y, ssm_new = _ssd_prompt(zxbc, dt, batch, W["a_conv_w"], W["a_conv_b"], W["a_dt_bias"],
                                 W["a_log"], W["a_d"], W["a_gate_norm"])
        conv_new = zxbc.reshape(batch, L, -1)[:, L - (CONV_W - 1):, d_inner:]
    else:
        y, conv_new, ssm_new = _ssd_step(zxbc, dt, conv_state, ssm_state, W["a_conv_w"], W["a_conv_b"],
                                         W["a_dt_bias"], W["a_log"], W["a_d"], W["a_gate_norm"])
        y = y.reshape(M, d_inner)
        conv_new = conv_new[0]
        ssm_new = ssm_new[0]
    x1 = _proj(y, W["a_w_out"], res=x0, tn=1024)
    h = _gateup(x1, W["ffn_norm"][0], W["ffn_w_gu"], 0)
    x2 = _proj(h, W["ffn_w_down"], layer=0, res=x1)

    kv_f32 = _proj(x2, W["w_kv"], gain=W["kv_norm"], rope=rope, rope_cols=kv_dim,
                   rows_per_seq=rows_per_seq, tn=1024)
    q = _proj(x2, W["b_w_q"], gain=W["b_norm"], rope=rope, rows_per_seq=rows_per_seq, tn=1024)
    if prompt:
        o = _attn_prompt(q, kv_f32, batch)
    else:
        o = _attn_step(q, cache_k, cache_v, kv_f32[:, :kv_dim], kv_f32[:, kv_dim:])
    x3 = _proj(o, W["b_w_o"], res=x2, tn=1024)
    h = _gateup(x3, W["ffn_norm"][1], W["ffn_w_gu"], 1)
    x4 = _proj(h, W["ffn_w_down"], layer=1, res=x3)
    out = _rmsnorm(x4, W["final_norm"]).reshape(batch, L, D)
    k_new = kv_f32[:, :kv_dim].reshape(batch, L, KV_HEADS, HEAD_DIM)
    v_new = kv_f32[:, kv_dim:].reshape(batch, L, KV_HEADS, HEAD_DIM)
    return out, conv_new[None], ssm_new[None], k_new, v_new


def kernel(x_prompt, x_sample, state_conv, state_ssm, cache_k, cache_v, a_norm, a_w_in, a_conv_w, a_conv_b,
           a_dt_bias, a_log, a_d, a_gate_norm, a_w_out, kv_norm, w_kv, b_norm, b_w_q, b_w_o, ffn_norm,
           ffn_w_gu, ffn_w_down, final_norm):
    assert a_norm.shape[0] == 1 and b_norm.shape[0] == 1, "one Mamba-2 layer followed by one attention layer"
    n_heads = a_dt_bias.shape[1]
    n_zxbc = a_w_in.shape[2] - n_heads
    w_dt = jnp.pad(a_w_in[0][:, n_zxbc:], ((0, 0), (0, LANES - n_heads))).astype(BF16)
    W = dict(
        a_norm=a_norm[0], a_w_in=a_w_in[0].astype(BF16), a_w_dt=w_dt,
        a_conv_w=a_conv_w[0], a_conv_b=a_conv_b[0], a_dt_bias=a_dt_bias[0], a_log=a_log[0], a_d=a_d[0],
        a_gate_norm=a_gate_norm[0], a_w_out=a_w_out[0].astype(BF16),
        kv_norm=kv_norm, w_kv=w_kv.astype(BF16), b_norm=b_norm[0], b_w_q=b_w_q[0].astype(BF16),
        b_w_o=b_w_o[0].astype(BF16), ffn_norm=ffn_norm,
        ffn_w_gu=ffn_w_gu.astype(BF16), ffn_w_down=ffn_w_down.astype(BF16),
        final_norm=final_norm)

    Lp = x_prompt.shape[1]
    y_p, p_conv, p_ssm, p_k, p_v = _trunk(x_prompt, jnp.arange(Lp, dtype=jnp.int32), W, True)
    keep = min(DIL_SLOTS * max(DIL_RATES), Lp)
    p_k = p_k[:, Lp - keep:]
    p_v = p_v[:, Lp - keep:]

    Ls = x_sample.shape[1]
    assert Ls == 1, "sample group decodes one token per sequence"
    pos_s = PAST_LEN + jnp.arange(Ls, dtype=jnp.int32)
    y_s, s_conv, s_ssm, s_k, s_v = _trunk(x_sample, pos_s, W, False, state_conv[0], state_ssm[0],
                                          cache_k, cache_v)
    return (y_p, y_s, p_conv, p_ssm, p_k, p_v, s_conv, s_ssm, s_k, s_v)
```

```python
import functools

import jax
import jax.numpy as jnp
from jax import lax
from jax.experimental import pallas as pl
from jax.experimental.pallas import tpu as pltpu

F32 = jnp.float32
BF16 = jnp.bfloat16
HIGHEST = lax.Precision.HIGHEST

EPS = 1e-6
ROPE_THETA = 10000.0
SSD_CHUNK = 128
SSM_HEAD_DIM = 64
SSM_STATE = 128
SSM_GROUPS = 8
CONV_W = 4
HEAD_DIM = 128
KV_HEADS = 8
DIL_RATES = (1, 4, 16)
DIL_SLOTS = 128
ATT_BLOCK = 128
ATT_UNROLL = 8
ATT_PRESTRIDE = 4
PAST_LEN = 16384

V7X_VMEM_BYTES = 64 * 1024 * 1024
VMEM_LIMIT = V7X_VMEM_BYTES - 8 * 1024 * 1024
LANES = 128
CONV_TAIL = 8
SAMPLE_ROWS = 8

NT_DIMS = (((1,), (1,)), ((), ()))
TN_DIMS = (((0,), (0,)), ((), ()))


def _params(*sem):
    return pltpu.CompilerParams(dimension_semantics=sem, vmem_limit_bytes=VMEM_LIMIT)


def _silu(x):
    h = 0.5 * x
    return h + h * jnp.tanh(h)


def _softplus(x):
    return jnp.maximum(x, 0.0) + jnp.log1p(jnp.exp(-jnp.abs(x)))


def _rms_scale(x):
    return lax.rsqrt(jnp.mean(x * x, axis=-1, keepdims=True) + EPS)


def _normed(x_ref, g_ref):
    x = x_ref[...]
    return ((x * _rms_scale(x)) * g_ref[...]).astype(BF16)


def _sample_block(i, j):
    return (jnp.minimum(i, 1), j)


def _proj_kernel(*refs, norm, rope_tiles, has_res, has_aux):
    it = iter(refs)
    xp_ref, xs_ref = next(it), next(it)
    g_ref = next(it) if norm else None
    w_ref = next(it)
    auxw_ref = next(it) if has_aux else None
    rope_p = (next(it), next(it)) if rope_tiles else None
    rope_s = (next(it), next(it)) if rope_tiles else None
    resp_ref, ress_ref = (next(it), next(it)) if has_res else (None, None)
    op_ref, os_ref = next(it), next(it)
    auxp_ref, auxs_ref = (next(it), next(it)) if has_aux else (None, None)
    xnp_ref, xns_ref = (next(it), next(it)) if norm else (None, None)
    i = pl.program_id(0)
    j = pl.program_id(1)
    w = w_ref[...].astype(BF16)

    if norm:
        @pl.when(j == 0)
        def _():
            xnp_ref[...] = _normed(xp_ref, g_ref)

        @pl.when((i == 0) & (j == 0))
        def _():
            xns_ref[...] = _normed(xs_ref, g_ref)

    lhs_p = xnp_ref[...] if norm else xp_ref[...]

    def lhs_s():
        return xns_ref[...] if norm else xs_ref[...].astype(BF16)

    def emit(lhs, res_ref, o_ref, rope):
        acc = jnp.dot(lhs, w, preferred_element_type=F32)
        if has_res:
            acc = acc + res_ref[...]
        if not rope_tiles:
            o_ref[...] = acc.astype(o_ref.dtype)
            return

        @pl.when(j < rope_tiles)
        def _():
            cos = rope[0][...]
            sin = rope[1][...]
            for h in range(acc.shape[1] // HEAD_DIM):
                a = acc[:, h * HEAD_DIM:(h + 1) * HEAD_DIM]
                o_ref[:, h * HEAD_DIM:(h + 1) * HEAD_DIM] = (
                    a * cos + pltpu.roll(a, HEAD_DIM // 2, 1) * sin).astype(o_ref.dtype)

        @pl.when(j >= rope_tiles)
        def _():
            o_ref[...] = acc.astype(o_ref.dtype)

    emit(lhs_p, resp_ref, op_ref, rope_p)

    @pl.when(i == 0)
    def _():
        emit(lhs_s(), ress_ref, os_ref, rope_s)

    @pl.when(i > 0)
    def _():
        os_ref[...] = jnp.zeros_like(os_ref)

    if has_aux:
        @pl.when(j == pl.num_programs(1) - 1)
        def _():
            auxw = auxw_ref[...].astype(BF16)
            auxp_ref[...] = jnp.dot(lhs_p, auxw, preferred_element_type=F32)

            @pl.when(i == 0)
            def _():
                auxs_ref[...] = jnp.dot(lhs_s(), auxw, preferred_element_type=F32)

            @pl.when(i > 0)
            def _():
                auxs_ref[...] = jnp.zeros_like(auxs_ref)


def _proj(xp, xs, w, *, n_cols=None, layer=0, gain=None, aux_w=None, rope=None, rope_cols=None, res=None,
          out_dtype=F32, tm=1024, tn=512):
    Mp, K = xp.shape
    S = SAMPLE_ROWS
    assert xs.shape == (S, K) and w.ndim == 3
    n_cols = w.shape[-1] if n_cols is None else n_cols
    assert Mp % tm == 0 and n_cols % tn == 0
    norm = gain is not None
    rope_tiles = 0
    if rope is not None:
        rope_cols = n_cols if rope_cols is None else rope_cols
        assert rope_cols % tn == 0 and res is None
        rope_tiles = rope_cols // tn
    row = lambda i, j: (i, 0)
    const = lambda i, j: (0, 0)
    args = [xp, xs]
    in_specs = [pl.BlockSpec((tm, K), row), pl.BlockSpec((S, K), const)]
    if norm:
        args.append(gain.reshape(1, K))
        in_specs.append(pl.BlockSpec((1, K), const))
    args.append(w)
    in_specs.append(pl.BlockSpec((None, K, tn), lambda i, j: (layer, 0, j)))
    if aux_w is not None:
        args.append(aux_w)
        in_specs.append(pl.BlockSpec(aux_w.shape, const))
    if rope is not None:
        (cos_p, sin_p, rows_per_seq), (cos_s, sin_s) = rope
        nseq = rows_per_seq // tm
        args += [cos_p, sin_p, cos_s, sin_s]
        in_specs += [pl.BlockSpec((tm, HEAD_DIM), lambda i, j: (i % nseq, 0))] * 2
        in_specs += [pl.BlockSpec((S, HEAD_DIM), const)] * 2
    if res is not None:
        args += list(res)
        in_specs += [pl.BlockSpec((tm, tn), lambda i, j: (i, j)), pl.BlockSpec((S, tn), lambda i, j: (0, j))]
    out_shape = [jax.ShapeDtypeStruct((Mp, n_cols), out_dtype), jax.ShapeDtypeStruct((2 * S, n_cols), F32)]
    out_specs = [pl.BlockSpec((tm, tn), lambda i, j: (i, j)), pl.BlockSpec((S, tn), _sample_block)]
    if aux_w is not None:
        na = aux_w.shape[1]
        out_shape += [jax.ShapeDtypeStruct((Mp, na), F32), jax.ShapeDtypeStruct((2 * S, na), F32)]
        out_specs += [pl.BlockSpec((tm, na), row), pl.BlockSpec((S, na), lambda i, j: (jnp.minimum(i, 1), 0))]
    outs = pl.pallas_call(
        functools.partial(_proj_kernel, norm=norm, rope_tiles=rope_tiles,
                          has_res=res is not None, has_aux=aux_w is not None),
        grid=(Mp // tm, n_cols // tn),
        in_specs=in_specs,
        out_specs=out_specs,
        out_shape=out_shape,
        scratch_shapes=[pltpu.VMEM((tm, K), BF16), pltpu.VMEM((S, K), BF16)] if norm else [],
        compiler_params=_params("arbitrary", "arbitrary"),
        name="proj",
    )(*args)
    return [o if n % 2 == 0 else o[:S] for n, o in enumerate(outs)]


def _gateup_kernel(xp_ref, xs_ref, g_ref, wg_ref, wu_ref, op_ref, os_ref, xnp_ref, xns_ref):
    i = pl.program_id(0)
    j = pl.program_id(1)
    wg = wg_ref[...].astype(BF16)
    wu = wu_ref[...].astype(BF16)

    @pl.when(j == 0)
    def _():
        xnp_ref[...] = _normed(xp_ref, g_ref)

    @pl.when((i == 0) & (j == 0))
    def _():
        xns_ref[...] = _normed(xs_ref, g_ref)

    def swiglu(xn):
        g = jnp.dot(xn, wg, preferred_element_type=F32)
        u = jnp.dot(xn, wu, preferred_element_type=F32)
        return _silu(g) * u

    op_ref[...] = swiglu(xnp_ref[...]).astype(op_ref.dtype)

    @pl.when(i == 0)
    def _():
        os_ref[...] = swiglu(xns_ref[...])

    @pl.when(i > 0)
    def _():
        os_ref[...] = jnp.zeros_like(os_ref)


def _gateup(xp, xs, gain, w_gu, layer, *, tm=1024, tn=512):
    Mp, K = xp.shape
    S = SAMPLE_ROWS
    hidden = w_gu.shape[2] // 2
    assert xs.shape == (S, K) and Mp % tm == 0 and hidden % tn == 0
    nj = hidden // tn
    hp, hs = pl.pallas_call(
        _gateup_kernel,
        grid=(Mp // tm, nj),
        in_specs=[pl.BlockSpec((tm, K), lambda i, j: (i, 0)),
                  pl.BlockSpec((S, K), lambda i, j: (0, 0)),
                  pl.BlockSpec((1, K), lambda i, j: (0, 0)),
                  pl.BlockSpec((None, K, tn), lambda i, j: (layer, 0, j)),
                  pl.BlockSpec((None, K, tn), lambda i, j: (layer, 0, j + nj))],
        out_specs=[pl.BlockSpec((tm, tn), lambda i, j: (i, j)), pl.BlockSpec((S, tn), _sample_block)],
        out_shape=[jax.ShapeDtypeStruct((Mp, hidden), BF16), jax.ShapeDtypeStruct((2 * S, hidden), F32)],
        scratch_shapes=[pltpu.VMEM((tm, K), BF16), pltpu.VMEM((S, K), BF16)],
        compiler_params=_params("arbitrary", "arbitrary"),
        name="gateup",
    )(xp, xs, gain.reshape(1, K), w_gu, w_gu)
    return hp, hs[:S]


def _conv_silu(full_ref, lo, hi, cw_ref, cb_ref, woff):
    acc = cb_ref[:, woff + lo:woff + hi]
    for k in range(CONV_W):
        r0 = CONV_TAIL - (CONV_W - 1) + k
        acc = acc + full_ref[r0:r0 + SSD_CHUNK, lo:hi] * cw_ref[k:k + 1, woff + lo:woff + hi]
    return _silu(acc)


def _ssd_kernel(z_ref, x_ref, b_ref, c_ref, dt_ref, cw_ref, cb_ref,
                dtb_ref, alog_ref, dexp_ref, gn_ref,
                y_ref, hout_ref,
                h_ref, xf_ref, bf_ref, cf_ref):
    Q = SSD_CHUNK
    P = SSM_HEAD_DIM
    N = SSM_STATE
    d_inner = x_ref.shape[1]
    n_heads = d_inner // P
    hpg = n_heads // SSM_GROUPS
    gw = hpg * P
    c = pl.program_id(1)

    @pl.when(c == 0)
    def _():
        h_ref[...] = jnp.zeros_like(h_ref)
        xf_ref[0:CONV_TAIL, :] = jnp.zeros((CONV_TAIL, xf_ref.shape[1]), F32)
        bf_ref[0:CONV_TAIL, :] = jnp.zeros((CONV_TAIL, bf_ref.shape[1]), F32)
        cf_ref[0:CONV_TAIL, :] = jnp.zeros((CONV_TAIL, cf_ref.shape[1]), F32)

    xf_ref[CONV_TAIL:, :] = x_ref[...]
    bf_ref[CONV_TAIL:, :] = b_ref[...]
    cf_ref[CONV_TAIL:, :] = c_ref[...]

    dt = _softplus(dt_ref[...] + dtb_ref[...])
    dA = dt * (-jnp.exp(alog_ref[...]))
    row = lax.broadcasted_iota(jnp.int32, (Q, Q), 0)
    col = lax.broadcasted_iota(jnp.int32, (Q, Q), 1)
    causal = row >= col
    tri = jnp.where(causal, 1.0, 0.0).astype(F32)
    cs = jnp.dot(tri, dA, precision=HIGHEST, preferred_element_type=F32)
    csT = cs.T
    decay_end = jnp.exp(cs[Q - 1:Q, :] - cs)
    ecs = jnp.exp(cs)
    chunk_decay = jnp.broadcast_to(jnp.exp(csT[0:n_heads, Q - 1:Q]), (n_heads, N))

    lane_lo = lax.broadcasted_iota(jnp.int32, (Q, 2 * P), 1) < P

    def pair_cols(arr, q):
        a0 = jnp.broadcast_to(arr[:, 2 * q:2 * q + 1], (Q, 2 * P))
        a1 = jnp.broadcast_to(arr[:, 2 * q + 1:2 * q + 2], (Q, 2 * P))
        return jnp.where(lane_lo, a0, a1)

    woff_b = d_inner
    woff_c = d_inner + SSM_GROUPS * N
    for g in range(SSM_GROUPS):
        Bg = _conv_silu(bf_ref, g * N, (g + 1) * N, cw_ref, cb_ref, woff_b).astype(BF16)
        Cg = _conv_silu(cf_ref, g * N, (g + 1) * N, cw_ref, cb_ref, woff_c).astype(BF16)
        CB = lax.dot_general(Cg, Bg, NT_DIMS, preferred_element_type=F32)
        Hg = h_ref[g * hpg:(g + 1) * hpg].reshape(gw, N)
        y_off = lax.dot_general(Cg, Hg.astype(BF16), NT_DIMS, preferred_element_type=F32)

        ys = []
        xds = []
        for qq in range(hpg // 2):
            q = g * (hpg // 2) + qq
            lo, hi = q * 2 * P, (q + 1) * 2 * P
            xs = _conv_silu(xf_ref, lo, hi, cw_ref, cb_ref, 0)
            xdt = xs * pair_cols(dt, q)
            xds.append((xdt * pair_cols(decay_end, q)).astype(BF16))
            atts = []
            for hh in (2 * q, 2 * q + 1):
                seg = cs[:, hh:hh + 1] - csT[hh:hh + 1, :]
                atts.append((CB * jnp.exp(jnp.where(causal, seg, -jnp.inf))).astype(BF16))
            att = jnp.concatenate(atts, axis=1)
            xbd = jnp.concatenate([jnp.where(lane_lo, xdt, 0.0).astype(BF16),
                                   jnp.where(lane_lo, 0.0, xdt).astype(BF16)], axis=0)
            y = jnp.dot(att, xbd, preferred_element_type=F32)
            y = y + y_off[:, qq * 2 * P:(qq + 1) * 2 * P] * pair_cols(ecs, q)
            y = y + xs * dexp_ref[:, lo:hi]
            ys.append(y * _silu(z_ref[:, lo:hi]))

        ssq = ys[0] * ys[0]
        for y in ys[1:]:
            ssq = ssq + y * y
        scale = lax.rsqrt(jnp.sum(ssq, axis=-1, keepdims=True) / gw + EPS)
        for qq, y in enumerate(ys):
            lo = g * gw + qq * 2 * P
            y_ref[:, lo:lo + 2 * P] = ((y * scale) * gn_ref[:, lo:lo + 2 * P]).astype(y_ref.dtype)

        S = lax.dot_general(jnp.concatenate(xds, axis=1), Bg, TN_DIMS, preferred_element_type=F32)
        for r in range(hpg):
            hh = g * hpg + r
            h_ref[hh] = h_ref[hh] * chunk_decay[hh:hh + 1, :] + S[r * P:(r + 1) * P, :]

    xf_ref[0:CONV_TAIL, :] = xf_ref[Q:Q + CONV_TAIL, :]
    bf_ref[0:CONV_TAIL, :] = bf_ref[Q:Q + CONV_TAIL, :]
    cf_ref[0:CONV_TAIL, :] = cf_ref[Q:Q + CONV_TAIL, :]

    @pl.when(c == pl.num_programs(1) - 1)
    def _():
        hout_ref[0] = h_ref[...]


def _ssd_prompt(zxbc, dt, batch, conv_w, conv_b, dt_bias, a_log, d_skip, gate_norm):
    M = zxbc.shape[0]
    L = M // batch
    H = dt_bias.shape[0]
    d_inner = H * SSM_HEAD_DIM
    bc = SSM_GROUPS * SSM_STATE
    conv_dim = d_inner + 2 * bc
    Q = SSD_CHUNK
    nc = L // Q
    assert L % Q == 0 and d_inner % bc == 0
    xb = d_inner // bc
    row_map = lambda col: (lambda b, c: (b * nc + c, col))
    const = lambda b, c: (0, 0)
    d_exp = jnp.repeat(d_skip.astype(F32), SSM_HEAD_DIM).reshape(1, d_inner)
    assert H <= LANES
    lane_pad = lambda v: jnp.pad(v.reshape(1, H), ((0, 0), (0, LANES - H)))
    y, h_final = pl.pallas_call(
        _ssd_kernel,
        grid=(batch, nc),
        in_specs=[pl.BlockSpec((Q, d_inner), row_map(0)),
                  pl.BlockSpec((Q, d_inner), row_map(1)),
                  pl.BlockSpec((Q, bc), row_map(2 * xb)),
                  pl.BlockSpec((Q, bc), row_map(2 * xb + 1)),
                  pl.BlockSpec((Q, LANES), row_map(0)),
                  pl.BlockSpec((CONV_W, conv_dim), const),
                  pl.BlockSpec((1, conv_dim), const),
                  pl.BlockSpec((1, LANES), const),
                  pl.BlockSpec((1, LANES), const),
                  pl.BlockSpec((1, d_inner), const),
                  pl.BlockSpec((1, d_inner), const)],
        out_specs=[pl.BlockSpec((Q, d_inner), row_map(0)),
                   pl.BlockSpec((1, H, SSM_HEAD_DIM, SSM_STATE), lambda b, c: (b, 0, 0, 0))],
        out_shape=[jax.ShapeDtypeStruct((M, d_inner), BF16),
                   jax.ShapeDtypeStruct((batch, H, SSM_HEAD_DIM, SSM_STATE), F32)],
        scratch_shapes=[pltpu.VMEM((H, SSM_HEAD_DIM, SSM_STATE), F32),
                        pltpu.VMEM((Q + CONV_TAIL, d_inner), F32),
                        pltpu.VMEM((Q + CONV_TAIL, bc), F32),
                        pltpu.VMEM((Q + CONV_TAIL, bc), F32)],
        compiler_params=_params("parallel", "arbitrary"),
        name="ssd_prompt",
    )(zxbc, zxbc, zxbc, zxbc, dt, conv_w, conv_b.reshape(1, conv_dim),
      lane_pad(dt_bias), lane_pad(a_log), d_exp, gate_norm.reshape(1, d_inner))
    return y, h_final


def _ssd_step_kernel(zxbc_ref, dt_ref, cs_ref, h0_ref, cw_ref, cb_ref, dtb_ref, alog_ref,
                     dexp_ref, gn_ref, y_ref, cso_ref, ho_ref):
    P = SSM_HEAD_DIM
    N = SSM_STATE
    n_heads = h0_ref.shape[2]
    d_inner = n_heads * P
    hpg = n_heads // SSM_GROUPS
    gw = hpg * P
    conv_dim = cw_ref.shape[1]

    raw = zxbc_ref[0, :, d_inner:d_inner + conv_dim]
    prev = cs_ref[0, 0]
    acc = cb_ref[...]
    for k in range(CONV_W - 1):
        acc = acc + prev[k:k + 1, :] * cw_ref[k:k + 1, :]
    acc = acc + raw * cw_ref[CONV_W - 1:CONV_W, :]
    xbc = _silu(acc)
    cso_ref[0, 0, 0:CONV_W - 2, :] = prev[1:CONV_W - 1, :]
    cso_ref[0, 0, CONV_W - 2:CONV_W - 1, :] = raw

    dt = _softplus(dt_ref[0, :, 0:n_heads] + dtb_ref[...])
    decay = jnp.exp(dt * (-jnp.exp(alog_ref[...])))
    z = zxbc_ref[0, :, 0:d_inner]

    eye = (lax.broadcasted_iota(jnp.int32, (P, P), 0) == lax.broadcasted_iota(jnp.int32, (P, P), 1))
    ys = []
    for g in range(SSM_GROUPS):
        Bg = xbc[:, d_inner + g * N:d_inner + (g + 1) * N]
        Cg = xbc[:, d_inner + SSM_GROUPS * N + g * N:d_inner + SSM_GROUPS * N + (g + 1) * N]
        Bb = jnp.broadcast_to(Bg, (P, N))
        for r in range(hpg):
            hh = g * hpg + r
            xs = xbc[:, hh * P:(hh + 1) * P]
            xdt = xs * dt[:, hh:hh + 1]
            xdiag = jnp.where(eye, jnp.broadcast_to(xdt, (P, P)), 0.0)
            outer = jnp.dot(xdiag, Bb, precision=HIGHEST, preferred_element_type=F32)
            ho_ref[0, 0, hh] = h0_ref[0, 0, hh] * decay[:, hh:hh + 1] + outer
        Hg = ho_ref[0, 0, g * hpg:(g + 1) * hpg].reshape(gw, N)
        yg = lax.dot_general(Cg, Hg, NT_DIMS, precision=HIGHEST, preferred_element_type=F32)
        lo, hi = g * gw, (g + 1) * gw
        yg = yg + xbc[:, lo:hi] * dexp_ref[:, lo:hi]
        yg = yg * _silu(z[:, lo:hi])
        yg = yg * lax.rsqrt(jnp.mean(yg * yg, axis=-1, keepdims=True) + EPS)
        ys.append(yg * gn_ref[:, lo:hi])
    y_ref[0] = jnp.concatenate(ys, axis=1).astype(y_ref.dtype)


def _ssd_step(zxbc, dt, conv_state, ssm_state, conv_w, conv_b, dt_bias, a_log, d_skip, gate_norm):
    B = zxbc.shape[0]
    H = dt_bias.shape[0]
    d_inner = H * SSM_HEAD_DIM
    conv_dim = conv_w.shape[1]
    const = lambda b: (0, 0)
    d_exp = jnp.repeat(d_skip.astype(F32), SSM_HEAD_DIM).reshape(1, d_inner)
    cs4 = conv_state.reshape(1, B, CONV_W - 1, conv_dim)
    h5 = ssm_state.reshape(1, B, H, SSM_HEAD_DIM, SSM_STATE)
    return pl.pallas_call(
        _ssd_step_kernel,
        grid=(B,),
        in_specs=[pl.BlockSpec((1, 1, zxbc.shape[1]), lambda b: (b, 0, 0)),
                  pl.BlockSpec((1, 1, dt.shape[1]), lambda b: (b, 0, 0)),
                  pl.BlockSpec((1, 1, CONV_W - 1, conv_dim), lambda b: (0, b, 0, 0)),
                  pl.BlockSpec((1, 1, H, SSM_HEAD_DIM, SSM_STATE), lambda b: (0, b, 0, 0, 0)),
                  pl.BlockSpec((CONV_W, conv_dim), const),
                  pl.BlockSpec((1, conv_dim), const),
                  pl.BlockSpec((1, H), const),
                  pl.BlockSpec((1, H), const),
                  pl.BlockSpec((1, d_inner), const),
                  pl.BlockSpec((1, d_inner), const)],
        out_specs=[pl.BlockSpec((1, 1, d_inner), lambda b: (b, 0, 0)),
                   pl.BlockSpec((1, 1, CONV_W - 1, conv_dim), lambda b: (0, b, 0, 0)),
                   pl.BlockSpec((1, 1, H, SSM_HEAD_DIM, SSM_STATE), lambda b: (0, b, 0, 0, 0))],
        out_shape=[jax.ShapeDtypeStruct((B, 1, d_inner), F32),
                   jax.ShapeDtypeStruct(cs4.shape, F32),
                   jax.ShapeDtypeStruct(h5.shape, F32)],
        compiler_params=_params("arbitrary"),
        name="ssd_step",
    )(zxbc.reshape(B, 1, -1), dt.reshape(B, 1, -1), cs4, h5, conv_w, conv_b.reshape(1, conv_dim),
      dt_bias.reshape(1, H), a_log.reshape(1, H), d_exp, gate_norm.reshape(1, d_inner))


def _attn_kernel(q0_ref, q1_ref, q2_ref, k_ref, v_ref, o_ref, on_ref, lse_ref, qs_ref, ks_ref, vs_ref):
    L = k_ref.shape[0]
    QB = ATT_BLOCK
    KW = QB + DIL_SLOTS
    P = ATT_PRESTRIDE
    p_shift = P.bit_length() - 1
    scale = HEAD_DIM ** -0.5
    e = lax.broadcasted_iota(jnp.int32, (QB, KW), 0) - lax.broadcasted_iota(jnp.int32, (QB, KW), 1)

    staged = {rate: rate > P for rate in DIL_RATES}
    for src_ref, dst_ref, needed in ((q2_ref, qs_ref, staged[DIL_RATES[2]]), (k_ref, ks_ref, any(staged.values())),
                                     (v_ref, vs_ref, any(staged.values()))):
        if needed:
            def stage(i, carry, src_ref=src_ref, dst_ref=dst_ref):
                cp = i & (P - 1)
                t = i >> p_shift
                dst = pl.multiple_of(cp * (L // P) + t * QB, QB)
                dst_ref[pl.ds(dst, QB), :] = src_ref[pl.ds(cp + P * QB * t, QB, stride=P), :]
                return carry
            lax.fori_loop(0, L // QB, stage, 0, unroll=4)

    for g, (q_ref, rate) in enumerate(zip((q0_ref, q1_ref, q2_ref), DIL_RATES)):
        shift = rate.bit_length() - 1

        def unit(n, carry, g=g, q_ref=q_ref, rate=rate, shift=shift):
            c = n & (rate - 1)
            u0 = (n >> shift) * QB
            v0 = jnp.maximum(u0 - DIL_SLOTS, 0)
            q_rows = pl.ds(c + rate * u0, QB, stride=rate)
            if staged[rate]:
                base = (c & (P - 1)) * (L // P) + (c >> p_shift)
                q = qs_ref[pl.ds(base + (rate // P) * u0, QB, stride=rate // P), :].astype(BF16)
                k_rows = pl.ds(base + (rate // P) * v0, KW, stride=rate // P)
                k = ks_ref[k_rows, :].astype(BF16)
                v = vs_ref[k_rows, :].astype(BF16)
            else:
                k_rows = pl.ds(c + rate * v0, KW, stride=rate)
                q = q_ref[q_rows, :].astype(BF16)
                k = k_ref[k_rows, :].astype(BF16)
                v = v_ref[k_rows, :].astype(BF16)
            s = lax.dot_general(q, k, NT_DIMS, preferred_element_type=F32) * scale
            d = e + (u0 - v0)
            s = jnp.where(d >= 0, s, -jnp.inf)
            s = jnp.where(d <= DIL_SLOTS, s, -jnp.inf)
            m = s.max(axis=1, keepdims=True)
            p = jnp.exp(s - m)
            l = p.sum(axis=1, keepdims=True)
            o = jnp.dot(p.astype(BF16), v, preferred_element_type=F32)
            on_ref[g, q_rows, :] = o / l
            lse_ref[g, q_rows, :] = jnp.broadcast_to(m + jnp.log(l), (QB, HEAD_DIM))
            return carry

        lax.fori_loop(0, L // QB, unit, 0, unroll=ATT_UNROLL)

    def mix(i, carry):
        rows = pl.ds(pl.multiple_of(i * QB, QB), QB)
        lses = [lse_ref[g, rows, :] for g in range(len(DIL_RATES))]
        m = functools.reduce(jnp.maximum, lses)
        ws = [jnp.exp(x - m) for x in lses]
        num = functools.reduce(jnp.add, [w * on_ref[g, rows, :] for g, w in enumerate(ws)])
        o_ref[rows, :] = (num / functools.reduce(jnp.add, ws)).astype(o_ref.dtype)
        return carry

    lax.fori_loop(0, L // QB, mix, 0, unroll=2)


def _attn_prompt(q, kv, batch):
    M = q.shape[0]
    L = M // batch
    n_grp = len(DIL_RATES)
    assert all(r & (r - 1) == 0 and L % (ATT_BLOCK * r) == 0 and L >= r * (ATT_BLOCK + DIL_SLOTS)
               for r in DIL_RATES)
    qspec = lambda g: pl.BlockSpec((L, HEAD_DIM), lambda b, h: (b, g * KV_HEADS + h))
    kspec = pl.BlockSpec((L, HEAD_DIM), lambda b, h: (b, h))
    vspec = pl.BlockSpec((L, HEAD_DIM), lambda b, h: (b, KV_HEADS + h))
    return pl.pallas_call(
        _attn_kernel,
        grid=(batch, KV_HEADS),
        in_specs=[qspec(0), qspec(1), qspec(2), kspec, vspec],
        out_specs=pl.BlockSpec((L, HEAD_DIM), lambda b, h: (b, h)),
        out_shape=jax.ShapeDtypeStruct((M, KV_HEADS * HEAD_DIM), BF16),
        scratch_shapes=[pltpu.VMEM((n_grp, L, HEAD_DIM), F32), pltpu.VMEM((n_grp, L, HEAD_DIM), F32)]
        + [pltpu.VMEM((L, HEAD_DIM), F32)] * 3,
        compiler_params=_params("parallel", "arbitrary"),
        name="attn_prompt",
    )(q, q, q, kv, kv)


def _bf16_round(a):
    return a.astype(BF16).astype(F32)


def _attn_step_kernel(q_ref, *refs):
    n = len(DIL_RATES)
    k_refs, v_refs = refs[0:n], refs[n:2 * n]
    kn_ref, vn_ref, o_ref = refs[2 * n:]
    scale = HEAD_DIM ** -0.5
    kn = _bf16_round(kn_ref[0, 0])
    vn = _bf16_round(vn_ref[0, 0])
    scores, new_scores = [], []
    for g, k_ref in enumerate(k_refs):
        qg = _bf16_round(q_ref[0, g])
        kg = _bf16_round(k_ref[0, :, 0])
        scores.append(jnp.sum(kg * qg[None], axis=-1, keepdims=True) * scale)
        new_scores.append(jnp.sum(kn * qg, axis=-1, keepdims=True) * scale)
    m = functools.reduce(jnp.maximum, [s.max(axis=0) for s in scores] + new_scores)
    l = jnp.zeros_like(m)
    o = jnp.zeros((KV_HEADS, HEAD_DIM), F32)
    for s, s_new, v_ref in zip(scores, new_scores, v_refs):
        p = jnp.exp(s - m[None])
        p_new = jnp.exp(s_new - m)
        l = l + p.sum(axis=0) + p_new
        o = o + (_bf16_round(p) * _bf16_round(v_ref[0, :, 0])).sum(axis=0) + _bf16_round(p_new) * vn
    o_ref[0, 0] = (o / l).astype(o_ref.dtype)


def _attn_step(q, cache_k, cache_v, k_new, v_new):
    B, T = cache_k.shape[0], cache_k.shape[1]
    S = DIL_SLOTS
    assert T == S * max(DIL_RATES), "every slot of every dilation group lies inside the cached window"
    qg = q.reshape(B, len(DIL_RATES), KV_HEADS, HEAD_DIM)
    row_shape = (B, 1, KV_HEADS, HEAD_DIM)
    row_spec = pl.BlockSpec((1, 1, KV_HEADS, HEAD_DIM), lambda b: (b, 0, 0, 0))
    args = [qg]
    in_specs = [pl.BlockSpec((1, len(DIL_RATES), KV_HEADS, HEAD_DIM), lambda b: (b, 0, 0, 0))]
    for cache in (cache_k, cache_v):
        for rate in DIL_RATES:
            args.append(cache.reshape(B, T // rate, rate, KV_HEADS, HEAD_DIM))
            in_specs.append(pl.BlockSpec((1, S, 1, KV_HEADS, HEAD_DIM),
                                         lambda b, blk=T // rate // S - 1: (b, blk, 0, 0, 0)))
    args += [k_new.reshape(row_shape), v_new.reshape(row_shape)]
    in_specs += [row_spec, row_spec]
    o = pl.pallas_call(
        _attn_step_kernel,
        grid=(B,),
        in_specs=in_specs,
        out_specs=row_spec,
        out_shape=jax.ShapeDtypeStruct(row_shape, F32),
        compiler_params=_params("parallel"),
        name="attn_step",
    )(*args)
    return o.reshape(B, KV_HEADS * HEAD_DIM)


def _rmsnorm_kernel(x_ref, g_ref, o_ref):
    x = x_ref[...]
    o_ref[...] = (x * _rms_scale(x)) * g_ref[...]


def _rmsnorm(x, gain, *, tm=512):
    M, K = x.shape
    tm = min(tm, M)
    assert M % tm == 0
    return pl.pallas_call(
        _rmsnorm_kernel,
        grid=(M // tm,),
        in_specs=[pl.BlockSpec((tm, K), lambda i: (i, 0)), pl.BlockSpec((1, K), lambda i: (0, 0))],
        out_specs=pl.BlockSpec((tm, K), lambda i: (i, 0)),
        out_shape=jax.ShapeDtypeStruct((M, K), F32),
        compiler_params=_params("parallel"),
        name="final_norm",
    )(x, gain.reshape(1, K))


def _rope_tables(pos):
    half = HEAD_DIM // 2
    inv = jnp.power(jnp.float32(ROPE_THETA), -jnp.arange(half, dtype=jnp.float32) / half)
    ang = pos.astype(jnp.float32)[:, None] * inv[None, :]
    cos = jnp.cos(ang)
    sin = jnp.sin(ang)
    return jnp.concatenate([cos, cos], axis=1), jnp.concatenate([-sin, sin], axis=1)


def kernel(x_prompt, x_sample, state_conv, state_ssm, cache_k, cache_v, a_norm, a_w_in, a_conv_w, a_conv_b,
           a_dt_bias, a_log, a_d, a_gate_norm, a_w_out, kv_norm, w_kv, b_norm, b_w_q, b_w_o, ffn_norm,
           ffn_w_gu, ffn_w_down, final_norm):
    assert a_norm.shape[0] == 1 and b_norm.shape[0] == 1, "one Mamba-2 layer followed by one attention layer"
    Bp, Lp, D = x_prompt.shape
    Bs, Ls, _ = x_sample.shape
    assert Ls == 1 and Bs == SAMPLE_ROWS, "sample group decodes one token for SAMPLE_ROWS sequences"
    n_heads = a_dt_bias.shape[1]
    d_inner = a_w_out.shape[1]
    conv_dim = a_conv_w.shape[2]
    kv_dim = KV_HEADS * HEAD_DIM
    w_dt = jnp.pad(a_w_in[0][:, d_inner + conv_dim:], ((0, 0), (0, LANES - n_heads))).astype(BF16)
    ssd_w = (a_conv_w[0], a_conv_b[0], a_dt_bias[0], a_log[0], a_d[0], a_gate_norm[0])
    a_w_in, a_w_out, b_w_q, b_w_o, ffn_w_gu, ffn_w_down = (
        w.astype(BF16) for w in (a_w_in, a_w_out, b_w_q, b_w_o, ffn_w_gu, ffn_w_down))
    w_kv = w_kv.astype(BF16)

    cos_p, sin_p = _rope_tables(jnp.arange(Lp, dtype=jnp.int32))
    cos_s, sin_s = _rope_tables(PAST_LEN + jnp.arange(Ls, dtype=jnp.int32))
    rope = ((cos_p, sin_p, Lp), (jnp.broadcast_to(cos_s, (Bs, HEAD_DIM)), jnp.broadcast_to(sin_s, (Bs, HEAD_DIM))))

    xp0 = x_prompt.reshape(Bp * Lp, D)
    xs0 = x_sample.reshape(Bs, D)

    zxbc_p, zxbc_s, dt_p, dt_s = _proj(xp0, xs0, a_w_in, n_cols=d_inner + conv_dim, gain=a_norm[0],
                                       aux_w=w_dt, tn=1024)
    y_p, p_ssm = _ssd_prompt(zxbc_p, dt_p, Bp, *ssd_w)
    p_conv = zxbc_p.reshape(Bp, Lp, -1)[:, Lp - (CONV_W - 1):, d_inner:]
    y_s, s_conv, s_ssm = _ssd_step(zxbc_s, dt_s, state_conv[0], state_ssm[0], *ssd_w)
    x1 = _proj(y_p, y_s.reshape(Bs, d_inner), a_w_out, res=(xp0, xs0))
    h = _gateup(*x1, ffn_norm[0], ffn_w_gu, 0)
    x2 = _proj(*h, ffn_w_down, layer=0, res=x1)

    kv_p, kv_s = _proj(*x2, w_kv[None], gain=kv_norm, rope=rope, rope_cols=kv_dim, tn=1024)
    q_p, q_s = _proj(*x2, b_w_q, gain=b_norm[0], rope=rope, tn=1024)
    o_p = _attn_prompt(q_p, kv_p, Bp)
    o_s = _attn_step(q_s, cache_k, cache_v, kv_s[:, :kv_dim], kv_s[:, kv_dim:])
    x3 = _proj(o_p, o_s, b_w_o, res=x2, tn=1024)
    h = _gateup(*x3, ffn_norm[1], ffn_w_gu, 1)
    x4 = _proj(*h, ffn_w_down, layer=1, res=x3)

    y_prompt = _rmsnorm(x4[0], final_norm).reshape(Bp, Lp, D)
    y_sample = _rmsnorm(x4[1], final_norm).reshape(Bs, Ls, D)
    keep = min(DIL_SLOTS * max(DIL_RATES), Lp)
    p_kv = kv_p.reshape(Bp, Lp, 2 * KV_HEADS, HEAD_DIM)[:, Lp - keep:]
    s_kv = kv_s.reshape(Bs, Ls, 2 * KV_HEADS, HEAD_DIM)
    return (y_prompt, y_sample, p_conv[None], p_ssm[None], p_kv[:, :, :KV_HEADS], p_kv[:, :, KV_HEADS:],
            s_conv, s_ssm, s_kv[:, :, :KV_HEADS], s_kv[:, :, KV_HEADS:])
```

```python
import functools

import jax
import jax.numpy as jnp
from jax import lax
from jax.experimental import pallas as pl
from jax.experimental.pallas import tpu as pltpu

F32 = jnp.float32
BF16 = jnp.bfloat16
HIGHEST = lax.Precision.HIGHEST

EPS = 1e-6
ROPE_THETA = 10000.0
SSD_CHUNK = 128
SSM_HEAD_DIM = 64
SSM_STATE = 128
SSM_GROUPS = 8
CONV_W = 4
HEAD_DIM = 128
KV_HEADS = 8
DIL_RATES = (1, 4, 16)
DIL_SLOTS = 128
ATT_BLOCK = 128
ATT_UNROLL = 8
ATT_PRESTRIDE = 4
PAST_LEN = 16384

V7X_VMEM_BYTES = 64 * 1024 * 1024
VMEM_LIMIT = V7X_VMEM_BYTES - 8 * 1024 * 1024
LANES = 128
CONV_TAIL = 8
SAMPLE_ROWS = 8

NT_DIMS = (((1,), (1,)), ((), ()))
TN_DIMS = (((0,), (0,)), ((), ()))


def _params(*sem):
    return pltpu.CompilerParams(dimension_semantics=sem, vmem_limit_bytes=VMEM_LIMIT)


def _silu(x):
    h = 0.5 * x
    return h + h * jnp.tanh(h)


def _softplus(x):
    return jnp.maximum(x, 0.0) + jnp.log1p(jnp.exp(-jnp.abs(x)))


def _rms_scale(x):
    return lax.rsqrt(jnp.mean(x * x, axis=-1, keepdims=True) + EPS)


def _normed(x_ref, g_ref):
    x = x_ref[...]
    return ((x * _rms_scale(x)) * g_ref[...]).astype(BF16)


def _sample_block(i, j):
    return (i, j)


def _proj_kernel(*refs, norm, rope_tiles, has_res, has_aux):
    it = iter(refs)
    xp_ref, xs_ref = next(it), next(it)
    g_ref = next(it) if norm else None
    w_ref = next(it)
    auxw_ref = next(it) if has_aux else None
    rope_p = (next(it), next(it)) if rope_tiles else None
    rope_s = (next(it), next(it)) if rope_tiles else None
    resp_ref, ress_ref = (next(it), next(it)) if has_res else (None, None)
    op_ref, os_ref = next(it), next(it)
    auxp_ref, auxs_ref = (next(it), next(it)) if has_aux else (None, None)
    xnp_ref, xns_ref = (next(it), next(it)) if norm else (None, None)
    i = pl.program_id(0)
    j = pl.program_id(1)

    if norm:
        @pl.when(j == 0)
        def _():
            xnp_ref[...] = _normed(xp_ref, g_ref)

        @pl.when((i == 0) & (j == 0))
        def _():
            xns_ref[...] = _normed(xs_ref, g_ref)

    lhs_p = xnp_ref[...] if norm else xp_ref[...]

    def lhs_s():
        return xns_ref[...] if norm else xs_ref[...].astype(BF16)

    def emit(lhs, res_ref, o_ref, rope):
        acc = jnp.dot(lhs, w_ref[...], preferred_element_type=F32)
        if has_res:
            acc = acc + res_ref[...]
        if not rope_tiles:
            o_ref[...] = acc.astype(o_ref.dtype)
            return

        @pl.when(j < rope_tiles)
        def _():
            cos = rope[0][...]
            sin = rope[1][...]
            for h in range(acc.shape[1] // HEAD_DIM):
                a = acc[:, h * HEAD_DIM:(h + 1) * HEAD_DIM]
                o_ref[:, h * HEAD_DIM:(h + 1) * HEAD_DIM] = (
                    a * cos + pltpu.roll(a, HEAD_DIM // 2, 1) * sin).astype(o_ref.dtype)

        @pl.when(j >= rope_tiles)
        def _():
            o_ref[...] = acc.astype(o_ref.dtype)

    emit(lhs_p, resp_ref, op_ref, rope_p)

    @pl.when(i == 0)
    def _():
        emit(lhs_s(), ress_ref, os_ref, rope_s)

    @pl.when(i > 0)
    def _():
        os_ref[...] = jnp.zeros_like(os_ref)

    if has_aux:
        @pl.when(j == pl.num_programs(1) - 1)
        def _():
            auxp_ref[...] = jnp.dot(lhs_p, auxw_ref[...], preferred_element_type=F32)

            @pl.when(i == 0)
            def _():
                auxs_ref[...] = jnp.dot(lhs_s(), auxw_ref[...], preferred_element_type=F32)

            @pl.when(i > 0)
            def _():
                auxs_ref[...] = jnp.zeros_like(auxs_ref)


def _proj(xp, xs, w, *, n_cols=None, layer=0, gain=None, aux_w=None, rope=None, rope_cols=None, res=None,
          out_dtype=F32, tm=1024, tn=512):
    Mp, K = xp.shape
    S = SAMPLE_ROWS
    assert xs.shape == (S, K) and w.ndim == 3 and w.dtype == BF16
    n_cols = w.shape[-1] if n_cols is None else n_cols
    assert Mp % tm == 0 and n_cols % tn == 0
    norm = gain is not None
    rope_tiles = 0
    if rope is not None:
        rope_cols = n_cols if rope_cols is None else rope_cols
        assert rope_cols % tn == 0 and res is None
        rope_tiles = rope_cols // tn
    row = lambda i, j: (i, 0)
    const = lambda i, j: (0, 0)
    args = [xp, xs]
    in_specs = [pl.BlockSpec((tm, K), row), pl.BlockSpec((S, K), const)]
    if norm:
        args.append(gain.reshape(1, K))
        in_specs.append(pl.BlockSpec((1, K), const))
    args.append(w)
    in_specs.append(pl.BlockSpec((None, K, tn), lambda i, j: (layer, 0, j)))
    if aux_w is not None:
        args.append(aux_w)
        in_specs.append(pl.BlockSpec(aux_w.shape, const))
    if rope is not None:
        (cos_p, sin_p, rows_per_seq), (cos_s, sin_s) = rope
        nseq = rows_per_seq // tm
        args += [cos_p, sin_p, cos_s, sin_s]
        in_specs += [pl.BlockSpec((tm, HEAD_DIM), lambda i, j: (i % nseq, 0))] * 2
        in_specs += [pl.BlockSpec((S, HEAD_DIM), const)] * 2
    if res is not None:
        args += list(res)
        in_specs += [pl.BlockSpec((tm, tn), lambda i, j: (i, j)), pl.BlockSpec((S, tn), lambda i, j: (0, j))]
    ni = Mp // tm
    out_shape = [jax.ShapeDtypeStruct((Mp, n_cols), out_dtype), jax.ShapeDtypeStruct((ni * S, n_cols), F32)]
    out_specs = [pl.BlockSpec((tm, tn), lambda i, j: (i, j)), pl.BlockSpec((S, tn), _sample_block)]
    if aux_w is not None:
        na = aux_w.shape[1]
        out_shape += [jax.ShapeDtypeStruct((Mp, na), F32), jax.ShapeDtypeStruct((ni * S, na), F32)]
        out_specs += [pl.BlockSpec((tm, na), row), pl.BlockSpec((S, na), row)]
    outs = pl.pallas_call(
        functools.partial(_proj_kernel, norm=norm, rope_tiles=rope_tiles,
                          has_res=res is not None, has_aux=aux_w is not None),
        grid=(ni, n_cols // tn),
        in_specs=in_specs,
        out_specs=out_specs,
        out_shape=out_shape,
        scratch_shapes=[pltpu.VMEM((tm, K), BF16), pltpu.VMEM((S, K), BF16)] if norm else [],
        compiler_params=_params("arbitrary", "arbitrary"),
        name="proj",
    )(*args)
    return [o if n % 2 == 0 else o[:S] for n, o in enumerate(outs)]


def _gateup_kernel(xp_ref, xs_ref, g_ref, wg_ref, wu_ref, op_ref, os_ref, xnp_ref, xns_ref):
    i = pl.program_id(0)
    j = pl.program_id(1)

    @pl.when(j == 0)
    def _():
        xnp_ref[...] = _normed(xp_ref, g_ref)

    @pl.when((i == 0) & (j == 0))
    def _():
        xns_ref[...] = _normed(xs_ref, g_ref)

    def swiglu(xn):
        g = jnp.dot(xn, wg_ref[...], preferred_element_type=F32)
        u = jnp.dot(xn, wu_ref[...], preferred_element_type=F32)
        return _silu(g) * u

    op_ref[...] = swiglu(xnp_ref[...]).astype(op_ref.dtype)

    @pl.when(i == 0)
    def _():
        os_ref[...] = swiglu(xns_ref[...])

    @pl.when(i > 0)
    def _():
        os_ref[...] = jnp.zeros_like(os_ref)


def _gateup(xp, xs, gain, w_gu, layer, *, tm=1024, tn=512):
    Mp, K = xp.shape
    S = SAMPLE_ROWS
    hidden = w_gu.shape[2] // 2
    assert xs.shape == (S, K) and Mp % tm == 0 and hidden % tn == 0 and w_gu.dtype == BF16
    nj = hidden // tn
    hp, hs = pl.pallas_call(
        _gateup_kernel,
        grid=(Mp // tm, nj),
        in_specs=[pl.BlockSpec((tm, K), lambda i, j: (i, 0)),
                  pl.BlockSpec((S, K), lambda i, j: (0, 0)),
                  pl.BlockSpec((1, K), lambda i, j: (0, 0)),
                  pl.BlockSpec((None, K, tn), lambda i, j: (layer, 0, j)),
                  pl.BlockSpec((None, K, tn), lambda i, j: (layer, 0, j + nj))],
        out_specs=[pl.BlockSpec((tm, tn), lambda i, j: (i, j)), pl.BlockSpec((S, tn), _sample_block)],
        out_shape=[jax.ShapeDtypeStruct((Mp, hidden), BF16), jax.ShapeDtypeStruct((Mp // tm * S, hidden), F32)],
        scratch_shapes=[pltpu.VMEM((tm, K), BF16), pltpu.VMEM((S, K), BF16)],
        compiler_params=_params("arbitrary", "arbitrary"),
        name="gateup",
    )(xp, xs, gain.reshape(1, K), w_gu, w_gu)
    return hp, hs[:S]


def _conv_silu(full_ref, lo, hi, cw_ref, cb_ref, woff):
    acc = cb_ref[:, woff + lo:woff + hi]
    for k in range(CONV_W):
        r0 = CONV_TAIL - (CONV_W - 1) + k
        acc = acc + full_ref[r0:r0 + SSD_CHUNK, lo:hi] * cw_ref[k:k + 1, woff + lo:woff + hi]
    return _silu(acc)


def _ssd_kernel(z_ref, x_ref, b_ref, c_ref, dt_ref, cw_ref, cb_ref,
                dtb_ref, alog_ref, dexp_ref, gn_ref,
                y_ref, hout_ref,
                h_ref, xf_ref, bf_ref, cf_ref):
    Q = SSD_CHUNK
    P = SSM_HEAD_DIM
    N = SSM_STATE
    d_inner = x_ref.shape[1]
    n_heads = d_inner // P
    hpg = n_heads // SSM_GROUPS
    gw = hpg * P
    c = pl.program_id(1)

    @pl.when(c == 0)
    def _():
        h_ref[...] = jnp.zeros_like(h_ref)
        xf_ref[0:CONV_TAIL, :] = jnp.zeros((CONV_TAIL, xf_ref.shape[1]), F32)
        bf_ref[0:CONV_TAIL, :] = jnp.zeros((CONV_TAIL, bf_ref.shape[1]), F32)
        cf_ref[0:CONV_TAIL, :] = jnp.zeros((CONV_TAIL, cf_ref.shape[1]), F32)

    xf_ref[CONV_TAIL:, :] = x_ref[...]
    bf_ref[CONV_TAIL:, :] = b_ref[...]
    cf_ref[CONV_TAIL:, :] = c_ref[...]

    dt = _softplus(dt_ref[...] + dtb_ref[...])
    dA = dt * (-jnp.exp(alog_ref[...]))
    row = lax.broadcasted_iota(jnp.int32, (Q, Q), 0)
    col = lax.broadcasted_iota(jnp.int32, (Q, Q), 1)
    causal = row >= col
    tri = jnp.where(causal, 1.0, 0.0).astype(F32)
    cs = jnp.dot(tri, dA, precision=HIGHEST, preferred_element_type=F32)
    csT = cs.T
    decay_end = jnp.exp(cs[Q - 1:Q, :] - cs)
    ecs = jnp.exp(cs)
    chunk_decay = jnp.broadcast_to(jnp.exp(csT[0:n_heads, Q - 1:Q]), (n_heads, N))

    lane_lo = lax.broadcasted_iota(jnp.int32, (Q, 2 * P), 1) < P

    def pair_cols(arr, q):
        a0 = jnp.broadcast_to(arr[:, 2 * q:2 * q + 1], (Q, 2 * P))
        a1 = jnp.broadcast_to(arr[:, 2 * q + 1:2 * q + 2], (Q, 2 * P))
        return jnp.where(lane_lo, a0, a1)

    woff_b = d_inner
    woff_c = d_inner + SSM_GROUPS * N
    for g in range(SSM_GROUPS):
        Bg = _conv_silu(bf_ref, g * N, (g + 1) * N, cw_ref, cb_ref, woff_b).astype(BF16)
        Cg = _conv_silu(cf_ref, g * N, (g + 1) * N, cw_ref, cb_ref, woff_c).astype(BF16)
        CB = lax.dot_general(Cg, Bg, NT_DIMS, preferred_element_type=F32)
        Hg = h_ref[g * hpg:(g + 1) * hpg].reshape(gw, N)
        y_off = lax.dot_general(Cg, Hg.astype(BF16), NT_DIMS, preferred_element_type=F32)

        ys = []
        xds = []
        for qq in range(hpg // 2):
            q = g * (hpg // 2) + qq
            lo, hi = q * 2 * P, (q + 1) * 2 * P
            xs = _conv_silu(xf_ref, lo, hi, cw_ref, cb_ref, 0)
            xdt = xs * pair_cols(dt, q)
            xds.append((xdt * pair_cols(decay_end, q)).astype(BF16))
            atts = []
            for hh in (2 * q, 2 * q + 1):
                seg = cs[:, hh:hh + 1] - csT[hh:hh + 1, :]
                atts.append((CB * jnp.exp(jnp.where(causal, seg, -jnp.inf))).astype(BF16))
            att = jnp.concatenate(atts, axis=1)
            xbd = jnp.concatenate([jnp.where(lane_lo, xdt, 0.0).astype(BF16),
                                   jnp.where(lane_lo, 0.0, xdt).astype(BF16)], axis=0)
            y = jnp.dot(att, xbd, preferred_element_type=F32)
            y = y + y_off[:, qq * 2 * P:(qq + 1) * 2 * P] * pair_cols(ecs, q)
            y = y + xs * dexp_ref[:, lo:hi]
            ys.append(y * _silu(z_ref[:, lo:hi]))

        ssq = ys[0] * ys[0]
        for y in ys[1:]:
            ssq = ssq + y * y
        scale = lax.rsqrt(jnp.sum(ssq, axis=-1, keepdims=True) / gw + EPS)
        for qq, y in enumerate(ys):
            lo = g * gw + qq * 2 * P
            y_ref[:, lo:lo + 2 * P] = ((y * scale) * gn_ref[:, lo:lo + 2 * P]).astype(y_ref.dtype)

        S = lax.dot_general(jnp.concatenate(xds, axis=1), Bg, TN_DIMS, preferred_element_type=F32)
        for r in range(hpg):
            hh = g * hpg + r
            h_ref[hh] = h_ref[hh] * chunk_decay[hh:hh + 1, :] + S[r * P:(r + 1) * P, :]

    xf_ref[0:CONV_TAIL, :] = xf_ref[Q:Q + CONV_TAIL, :]
    bf_ref[0:CONV_TAIL, :] = bf_ref[Q:Q + CONV_TAIL, :]
    cf_ref[0:CONV_TAIL, :] = cf_ref[Q:Q + CONV_TAIL, :]

    @pl.when(c == pl.num_programs(1) - 1)
    def _():
        hout_ref[0] = h_ref[...]


def _ssd_prompt(zxbc, dt, batch, conv_w, conv_b, dt_bias, a_log, d_skip, gate_norm):
    M = zxbc.shape[0]
    L = M // batch
    H = dt_bias.shape[0]
    d_inner = H * SSM_HEAD_DIM
    bc = SSM_GROUPS * SSM_STATE
    conv_dim = d_inner + 2 * bc
    Q = SSD_CHUNK
    nc = L // Q
    assert L % Q == 0 and d_inner % bc == 0
    xb = d_inner // bc
    row_map = lambda col: (lambda b, c: (b * nc + c, col))
    const = lambda b, c: (0, 0)
    d_exp = jnp.repeat(d_skip.astype(F32), SSM_HEAD_DIM).reshape(1, d_inner)
    assert H <= LANES
    lane_pad = lambda v: jnp.pad(v.reshape(1, H), ((0, 0), (0, LANES - H)))
    y, h_final = pl.pallas_call(
        _ssd_kernel,
        grid=(batch, nc),
        in_specs=[pl.BlockSpec((Q, d_inner), row_map(0)),
                  pl.BlockSpec((Q, d_inner), row_map(1)),
                  pl.BlockSpec((Q, bc), row_map(2 * xb)),
                  pl.BlockSpec((Q, bc), row_map(2 * xb + 1)),
                  pl.BlockSpec((Q, LANES), row_map(0)),
                  pl.BlockSpec((CONV_W, conv_dim), const),
                  pl.BlockSpec((1, conv_dim), const),
                  pl.BlockSpec((1, LANES), const),
                  pl.BlockSpec((1, LANES), const),
                  pl.BlockSpec((1, d_inner), const),
                  pl.BlockSpec((1, d_inner), const)],
        out_specs=[pl.BlockSpec((Q, d_inner), row_map(0)),
                   pl.BlockSpec((1, H, SSM_HEAD_DIM, SSM_STATE), lambda b, c: (b, 0, 0, 0))],
        out_shape=[jax.ShapeDtypeStruct((M, d_inner), BF16),
                   jax.ShapeDtypeStruct((batch, H, SSM_HEAD_DIM, SSM_STATE), F32)],
        scratch_shapes=[pltpu.VMEM((H, SSM_HEAD_DIM, SSM_STATE), F32),
                        pltpu.VMEM((Q + CONV_TAIL, d_inner), F32),
                        pltpu.VMEM((Q + CONV_TAIL, bc), F32),
                        pltpu.VMEM((Q + CONV_TAIL, bc), F32)],
        compiler_params=_params("parallel", "arbitrary"),
        name="ssd_prompt",
    )(zxbc, zxbc, zxbc, zxbc, dt, conv_w, conv_b.reshape(1, conv_dim),
      lane_pad(dt_bias), lane_pad(a_log), d_exp, gate_norm.reshape(1, d_inner))
    return y, h_final


def _ssd_step_kernel(zxbc_ref, dt_ref, cs_ref, h0_ref, cw_ref, cb_ref, dtb_ref, alog_ref,
                     dexp_ref, gn_ref, y_ref, cso_ref, ho_ref):
    P = SSM_HEAD_DIM
    N = SSM_STATE
    n_heads = h0_ref.shape[2]
    d_inner = n_heads * P
    hpg = n_heads // SSM_GROUPS
    gw = hpg * P
    conv_dim = cw_ref.shape[1]

    raw = zxbc_ref[0, :, d_inner:d_inner + conv_dim]
    prev = cs_ref[0, 0]
    acc = cb_ref[...]
    for k in range(CONV_W - 1):
        acc = acc + prev[k:k + 1, :] * cw_ref[k:k + 1, :]
    acc = acc + raw * cw_ref[CONV_W - 1:CONV_W, :]
    xbc = _silu(acc)
    cso_ref[0, 0, 0:CONV_W - 2, :] = prev[1:CONV_W - 1, :]
    cso_ref[0, 0, CONV_W - 2:CONV_W - 1, :] = raw

    dt = _softplus(dt_ref[0, :, 0:n_heads] + dtb_ref[...])
    decay = jnp.exp(dt * (-jnp.exp(alog_ref[...])))
    z = zxbc_ref[0, :, 0:d_inner]

    eye = (lax.broadcasted_iota(jnp.int32, (P, P), 0) == lax.broadcasted_iota(jnp.int32, (P, P), 1))
    ys = []
    for g in range(SSM_GROUPS):
        Bg = xbc[:, d_inner + g * N:d_inner + (g + 1) * N]
        Cg = xbc[:, d_inner + SSM_GROUPS * N + g * N:d_inner + SSM_GROUPS * N + (g + 1) * N]
        Bb = jnp.broadcast_to(Bg, (P, N))
        for r in range(hpg):
            hh = g * hpg + r
            xs = xbc[:, hh * P:(hh + 1) * P]
            xdt = xs * dt[:, hh:hh + 1]
            xdiag = jnp.where(eye, jnp.broadcast_to(xdt, (P, P)), 0.0)
            outer = jnp.dot(xdiag, Bb, precision=HIGHEST, preferred_element_type=F32)
            ho_ref[0, 0, hh] = h0_ref[0, 0, hh] * decay[:, hh:hh + 1] + outer
        Hg = ho_ref[0, 0, g * hpg:(g + 1) * hpg].reshape(gw, N)
        yg = lax.dot_general(Cg, Hg, NT_DIMS, precision=HIGHEST, preferred_element_type=F32)
        lo, hi = g * gw, (g + 1) * gw
        yg = yg + xbc[:, lo:hi] * dexp_ref[:, lo:hi]
        yg = yg * _silu(z[:, lo:hi])
        yg = yg * lax.rsqrt(jnp.mean(yg * yg, axis=-1, keepdims=True) + EPS)
        ys.append(yg * gn_ref[:, lo:hi])
    y_ref[0] = jnp.concatenate(ys, axis=1).astype(y_ref.dtype)


def _ssd_step(zxbc, dt, conv_state, ssm_state, conv_w, conv_b, dt_bias, a_log, d_skip, gate_norm):
    B = zxbc.shape[0]
    H = dt_bias.shape[0]
    d_inner = H * SSM_HEAD_DIM
    conv_dim = conv_w.shape[1]
    const = lambda b: (0, 0)
    d_exp = jnp.repeat(d_skip.astype(F32), SSM_HEAD_DIM).reshape(1, d_inner)
    cs4 = conv_state.reshape(1, B, CONV_W - 1, conv_dim)
    h5 = ssm_state.reshape(1, B, H, SSM_HEAD_DIM, SSM_STATE)
    return pl.pallas_call(
        _ssd_step_kernel,
        grid=(B,),
        in_specs=[pl.BlockSpec((1, 1, zxbc.shape[1]), lambda b: (b, 0, 0)),
                  pl.BlockSpec((1, 1, dt.shape[1]), lambda b: (b, 0, 0)),
                  pl.BlockSpec((1, 1, CONV_W - 1, conv_dim), lambda b: (0, b, 0, 0)),
                  pl.BlockSpec((1, 1, H, SSM_HEAD_DIM, SSM_STATE), lambda b: (0, b, 0, 0, 0)),
                  pl.BlockSpec((CONV_W, conv_dim), const),
                  pl.BlockSpec((1, conv_dim), const),
                  pl.BlockSpec((1, H), const),
                  pl.BlockSpec((1, H), const),
                  pl.BlockSpec((1, d_inner), const),
                  pl.BlockSpec((1, d_inner), const)],
        out_specs=[pl.BlockSpec((1, 1, d_inner), lambda b: (b, 0, 0)),
                   pl.BlockSpec((1, 1, CONV_W - 1, conv_dim), lambda b: (0, b, 0, 0)),
                   pl.BlockSpec((1, 1, H, SSM_HEAD_DIM, SSM_STATE), lambda b: (0, b, 0, 0, 0))],
        out_shape=[jax.ShapeDtypeStruct((B, 1, d_inner), F32),
                   jax.ShapeDtypeStruct(cs4.shape, F32),
                   jax.ShapeDtypeStruct(h5.shape, F32)],
        compiler_params=_params("arbitrary"),
        name="ssd_step",
    )(zxbc.reshape(B, 1, -1), dt.reshape(B, 1, -1), cs4, h5, conv_w, conv_b.reshape(1, conv_dim),
      dt_bias.reshape(1, H), a_log.reshape(1, H), d_exp, gate_norm.reshape(1, d_inner))


def _attn_kernel(q0_ref, q1_ref, q2_ref, k_ref, v_ref, o_ref, on_ref, lse_ref, qs_ref, ks_ref, vs_ref):
    L = k_ref.shape[0]
    QB = ATT_BLOCK
    KW = QB + DIL_SLOTS
    P = ATT_PRESTRIDE
    p_shift = P.bit_length() - 1
    scale = HEAD_DIM ** -0.5
    e = lax.broadcasted_iota(jnp.int32, (QB, KW), 0) - lax.broadcasted_iota(jnp.int32, (QB, KW), 1)

    staged = {rate: rate > P for rate in DIL_RATES}
    for src_ref, dst_ref, needed in ((q2_ref, qs_ref, staged[DIL_RATES[2]]), (k_ref, ks_ref, any(staged.values())),
                                     (v_ref, vs_ref, any(staged.values()))):
        if needed:
            def stage(i, carry, src_ref=src_ref, dst_ref=dst_ref):
                cp = i & (P - 1)
                t = i >> p_shift
                dst = pl.multiple_of(cp * (L // P) + t * QB, QB)
                dst_ref[pl.ds(dst, QB), :] = src_ref[pl.ds(cp + P * QB * t, QB, stride=P), :]
                return carry
            lax.fori_loop(0, L // QB, stage, 0, unroll=4)

    for g, (q_ref, rate) in enumerate(zip((q0_ref, q1_ref, q2_ref), DIL_RATES)):
        shift = rate.bit_length() - 1

        def unit(n, carry, g=g, q_ref=q_ref, rate=rate, shift=shift):
            c = n & (rate - 1)
            u0 = (n >> shift) * QB
            v0 = jnp.maximum(u0 - DIL_SLOTS, 0)
            q_rows = pl.ds(c + rate * u0, QB, stride=rate)
            if staged[rate]:
                base = (c & (P - 1)) * (L // P) + (c >> p_shift)
                q = qs_ref[pl.ds(base + (rate // P) * u0, QB, stride=rate // P), :].astype(BF16)
                k_rows = pl.ds(base + (rate // P) * v0, KW, stride=rate // P)
                k = ks_ref[k_rows, :].astype(BF16)
                v = vs_ref[k_rows, :].astype(BF16)
            else:
                k_rows = pl.ds(c + rate * v0, KW, stride=rate)
                q = q_ref[q_rows, :].astype(BF16)
                k = k_ref[k_rows, :].astype(BF16)
                v = v_ref[k_rows, :].astype(BF16)
            s = lax.dot_general(q, k, NT_DIMS, preferred_element_type=F32) * scale
            d = e + (u0 - v0)
            s = jnp.where(d >= 0, s, -jnp.inf)
            s = jnp.where(d <= DIL_SLOTS, s, -jnp.inf)
            m = s.max(axis=1, keepdims=True)
            p = jnp.exp(s - m)
            l = p.sum(axis=1, keepdims=True)
            o = jnp.dot(p.astype(BF16), v, preferred_element_type=F32)
            on_ref[g, q_rows, :] = o / l
            lse_ref[g, q_rows, :] = jnp.broadcast_to(m + jnp.log(l), (QB, HEAD_DIM))
            return carry

        lax.fori_loop(0, L // QB, unit, 0, unroll=ATT_UNROLL)

    def mix(i, carry):
        rows = pl.ds(pl.multiple_of(i * QB, QB), QB)
        lses = [lse_ref[g, rows, :] for g in range(len(DIL_RATES))]
        m = functools.reduce(jnp.maximum, lses)
        ws = [jnp.exp(x - m) for x in lses]
        num = functools.reduce(jnp.add, [w * on_ref[g, rows, :] for g, w in enumerate(ws)])
        o_ref[rows, :] = (num / functools.reduce(jnp.add, ws)).astype(o_ref.dtype)
        return carry

    lax.fori_loop(0, L // QB, mix, 0, unroll=2)


def _attn_prompt(q, kv, batch):
    M = q.shape[0]
    L = M // batch
    n_grp = len(DIL_RATES)
    assert all(r & (r - 1) == 0 and L % (ATT_BLOCK * r) == 0 and L >= r * (ATT_BLOCK + DIL_SLOTS)
               for r in DIL_RATES)
    qspec = lambda g: pl.BlockSpec((L, HEAD_DIM), lambda b, h: (b, g * KV_HEADS + h))
    kspec = pl.BlockSpec((L, HEAD_DIM), lambda b, h: (b, h))
    vspec = pl.BlockSpec((L, HEAD_DIM), lambda b, h: (b, KV_HEADS + h))
    return pl.pallas_call(
        _attn_kernel,
        grid=(batch, KV_HEADS),
        in_specs=[qspec(0), qspec(1), qspec(2), kspec, vspec],
        out_specs=pl.BlockSpec((L, HEAD_DIM), lambda b, h: (b, h)),
        out_shape=jax.ShapeDtypeStruct((M, KV_HEADS * HEAD_DIM), BF16),
        scratch_shapes=[pltpu.VMEM((n_grp, L, HEAD_DIM), F32), pltpu.VMEM((n_grp, L, HEAD_DIM), F32)]
        + [pltpu.VMEM((L, HEAD_DIM), F32)] * 3,
        compiler_params=_params("parallel", "arbitrary"),
        name="attn_prompt",
    )(q, q, q, kv, kv)


def _bf16_round(a):
    return a.astype(BF16).astype(F32)


def _attn_step_kernel(q_ref, *refs):
    n = len(DIL_RATES)
    k_refs, v_refs = refs[0:n], refs[n:2 * n]
    kn_ref, vn_ref, o_ref = refs[2 * n:]
    scale = HEAD_DIM ** -0.5
    kn = _bf16_round(kn_ref[0, 0])
    vn = _bf16_round(vn_ref[0, 0])
    scores, new_scores = [], []
    for g, k_ref in enumerate(k_refs):
        qg = _bf16_round(q_ref[0, g])
        kg = _bf16_round(k_ref[0, :, 0])
        scores.append(jnp.sum(kg * qg[None], axis=-1, keepdims=True) * scale)
        new_scores.append(jnp.sum(kn * qg, axis=-1, keepdims=True) * scale)
    m = functools.reduce(jnp.maximum, [s.max(axis=0) for s in scores] + new_scores)
    l = jnp.zeros_like(m)
    o = jnp.zeros((KV_HEADS, HEAD_DIM), F32)
    for s, s_new, v_ref in zip(scores, new_scores, v_refs):
        p = jnp.exp(s - m[None])
        p_new = jnp.exp(s_new - m)
        l = l + p.sum(axis=0) + p_new
        o = o + (_bf16_round(p) * _bf16_round(v_ref[0, :, 0])).sum(axis=0) + _bf16_round(p_new) * vn
    o_ref[0, 0] = (o / l).astype(o_ref.dtype)


def _attn_step(q, cache_k, cache_v, k_new, v_new):
    B, T = cache_k.shape[0], cache_k.shape[1]
    S = DIL_SLOTS
    assert T == S * max(DIL_RATES), "every slot of every dilation group lies inside the cached window"
    qg = q.reshape(B, len(DIL_RATES), KV_HEADS, HEAD_DIM)
    row_shape = (B, 1, KV_HEADS, HEAD_DIM)
    row_spec = pl.BlockSpec((1, 1, KV_HEADS, HEAD_DIM), lambda b: (b, 0, 0, 0))
    args = [qg]
    in_specs = [pl.BlockSpec((1, len(DIL_RATES), KV_HEADS, HEAD_DIM), lambda b: (b, 0, 0, 0))]
    for cache in (cache_k, cache_v):
        for rate in DIL_RATES:
            args.append(cache.reshape(B, T // rate, rate, KV_HEADS, HEAD_DIM))
            in_specs.append(pl.BlockSpec((1, S, 1, KV_HEADS, HEAD_DIM),
                                         lambda b, blk=T // rate // S - 1: (b, blk, 0, 0, 0)))
    args += [k_new.reshape(row_shape), v_new.reshape(row_shape)]
    in_specs += [row_spec, row_spec]
    o = pl.pallas_call(
        _attn_step_kernel,
        grid=(B,),
        in_specs=in_specs,
        out_specs=row_spec,
        out_shape=jax.ShapeDtypeStruct(row_shape, F32),
        compiler_params=_params("parallel"),
        name="attn_step",
    )(*args)
    return o.reshape(B, KV_HEADS * HEAD_DIM)


def _rmsnorm_kernel(x_ref, g_ref, o_ref):
    x = x_ref[...]
    o_ref[...] = (x * _rms_scale(x)) * g_ref[...]


def _rmsnorm(x, gain, *, tm=512):
    M, K = x.shape
    tm = min(tm, M)
    assert M % tm == 0
    return pl.pallas_call(
        _rmsnorm_kernel,
        grid=(M // tm,),
        in_specs=[pl.BlockSpec((tm, K), lambda i: (i, 0)), pl.BlockSpec((1, K), lambda i: (0, 0))],
        out_specs=pl.BlockSpec((tm, K), lambda i: (i, 0)),
        out_shape=jax.ShapeDtypeStruct((M, K), F32),
        compiler_params=_params("parallel"),
        name="final_norm",
    )(x, gain.reshape(1, K))


def _rope_tables(pos):
    half = HEAD_DIM // 2
    inv = jnp.power(jnp.float32(ROPE_THETA), -jnp.arange(half, dtype=jnp.float32) / half)
    ang = pos.astype(jnp.float32)[:, None] * inv[None, :]
    cos = jnp.cos(ang)
    sin = jnp.sin(ang)
    return jnp.concatenate([cos, cos], axis=1), jnp.concatenate([-sin, sin], axis=1)


def kernel(x_prompt, x_sample, state_conv, state_ssm, cache_k, cache_v, a_norm, a_w_in, a_conv_w, a_conv_b,
           a_dt_bias, a_log, a_d, a_gate_norm, a_w_out, kv_norm, w_kv, b_norm, b_w_q, b_w_o, ffn_norm,
           ffn_w_gu, ffn_w_down, final_norm):
    assert a_norm.shape[0] == 1 and b_norm.shape[0] == 1, "one Mamba-2 layer followed by one attention layer"
    Bp, Lp, D = x_prompt.shape
    Bs, Ls, _ = x_sample.shape
    assert Ls == 1 and Bs == SAMPLE_ROWS, "sample group decodes one token for SAMPLE_ROWS sequences"
    n_heads = a_dt_bias.shape[1]
    d_inner = a_w_out.shape[1]
    conv_dim = a_conv_w.shape[2]
    kv_dim = KV_HEADS * HEAD_DIM
    w_dt = jnp.pad(a_w_in[0][:, d_inner + conv_dim:], ((0, 0), (0, LANES - n_heads))).astype(BF16)
    ssd_w = (a_conv_w[0], a_conv_b[0], a_dt_bias[0], a_log[0], a_d[0], a_gate_norm[0])
    a_w_in, a_w_out, b_w_q, b_w_o, ffn_w_gu, ffn_w_down = (
        w.astype(BF16) for w in (a_w_in, a_w_out, b_w_q, b_w_o, ffn_w_gu, ffn_w_down))
    w_kv = w_kv.astype(BF16)

    cos_p, sin_p = _rope_tables(jnp.arange(Lp, dtype=jnp.int32))
    cos_s, sin_s = _rope_tables(PAST_LEN + jnp.arange(Ls, dtype=jnp.int32))
    rope = ((cos_p, sin_p, Lp), (jnp.broadcast_to(cos_s, (Bs, HEAD_DIM)), jnp.broadcast_to(sin_s, (Bs, HEAD_DIM))))

    xp0 = x_prompt.reshape(Bp * Lp, D)
    xs0 = x_sample.reshape(Bs, D)

    zxbc_p, zxbc_s, dt_p, dt_s = _proj(xp0, xs0, a_w_in, n_cols=d_inner + conv_dim, gain=a_norm[0],
                                       aux_w=w_dt, tn=1024)
    y_p, p_ssm = _ssd_prompt(zxbc_p, dt_p, Bp, *ssd_w)
    p_conv = zxbc_p.reshape(Bp, Lp, -1)[:, Lp - (CONV_W - 1):, d_inner:]
    y_s, s_conv, s_ssm = _ssd_step(zxbc_s, dt_s, state_conv[0], state_ssm[0], *ssd_w)
    x1 = _proj(y_p, y_s.reshape(Bs, d_inner), a_w_out, res=(xp0, xs0))
    h = _gateup(*x1, ffn_norm[0], ffn_w_gu, 0)
    x2 = _proj(*h, ffn_w_down, layer=0, res=x1)

    kv_p, kv_s = _proj(*x2, w_kv[None], gain=kv_norm, rope=rope, rope_cols=kv_dim, tn=1024)
    q_p, q_s = _proj(*x2, b_w_q, gain=b_norm[0], rope=rope, tn=1024)
    o_p = _attn_prompt(q_p, kv_p, Bp)
    o_s = _attn_step(q_s, cache_k, cache_v, kv_s[:, :kv_dim], kv_s[:, kv_dim:])
    x3 = _proj(o_p, o_s, b_w_o, res=x2, tn=1024)
    h = _gateup(*x3, ffn_norm[1], ffn_w_gu, 1)
    x4 = _proj(*h, ffn_w_down, layer=1, res=x3)

    y_prompt = _rmsnorm(x4[0], final_norm).reshape(Bp, Lp, D)
    y_sample = _rmsnorm(x4[1], final_norm).reshape(Bs, Ls, D)
    keep = min(DIL_SLOTS * max(DIL_RATES), Lp)
    p_kv = kv_p.reshape(Bp, Lp, 2 * KV_HEADS, HEAD_DIM)[:, Lp - keep:]
    s_kv = kv_s.reshape(Bs, Ls, 2 * KV_HEADS, HEAD_DIM)
    return (y_prompt, y_sample, p_conv[None], p_ssm[None], p_kv[:, :, :KV_HEADS], p_kv[:, :, KV_HEADS:],
            s_conv, s_ssm, s_kv[:, :, :KV_HEADS], s_kv[:, :, KV_HEADS:])
```

```python
import functools

import jax
import jax.numpy as jnp
from jax import lax
from jax.experimental import pallas as pl
from jax.experimental.pallas import tpu as pltpu

F32 = jnp.float32
BF16 = jnp.bfloat16
HIGHEST = lax.Precision.HIGHEST

EPS = 1e-6
ROPE_THETA = 10000.0
SSD_CHUNK = 128
SSM_HEAD_DIM = 64
SSM_STATE = 128
SSM_GROUPS = 8
CONV_W = 4
HEAD_DIM = 128
KV_HEADS = 8
DIL_RATES = (1, 4, 16)
DIL_SLOTS = 128
ATT_BLOCK = 128
ATT_UNROLL = 8
ATT_PRESTRIDE = 4
PAST_LEN = 16384

V7X_VMEM_BYTES = 64 * 1024 * 1024
VMEM_LIMIT = V7X_VMEM_BYTES - 8 * 1024 * 1024
LANES = 128
CONV_TAIL = 8
SAMPLE_ROWS = 8

NT_DIMS = (((1,), (1,)), ((), ()))
TN_DIMS = (((0,), (0,)), ((), ()))


def _params(*sem):
    return pltpu.CompilerParams(dimension_semantics=sem, vmem_limit_bytes=VMEM_LIMIT)


def _silu(x):
    h = 0.5 * x
    return h + h * jnp.tanh(h)


def _softplus(x):
    return jnp.maximum(x, 0.0) + jnp.log1p(jnp.exp(-jnp.abs(x)))


def _rms_scale(x):
    return lax.rsqrt(jnp.mean(x * x, axis=-1, keepdims=True) + EPS)


def _normed(x_ref, g_ref):
    x = x_ref[...]
    return ((x * _rms_scale(x)) * g_ref[...]).astype(BF16)


def _sample_block(i, j):
    return (i, j)


def _proj_kernel(*refs, norm, rope_tiles, has_res, has_aux):
    it = iter(refs)
    xp_ref, xs_ref = next(it), next(it)
    g_ref = next(it) if norm else None
    w_ref = next(it)
    auxw_ref = next(it) if has_aux else None
    rope_p = (next(it), next(it)) if rope_tiles else None
    rope_s = (next(it), next(it)) if rope_tiles else None
    resp_ref, ress_ref = (next(it), next(it)) if has_res else (None, None)
    op_ref, os_ref = next(it), next(it)
    auxp_ref, auxs_ref = (next(it), next(it)) if has_aux else (None, None)
    xnp_ref, xns_ref = (next(it), next(it)) if norm else (None, None)
    i = pl.program_id(0)
    j = pl.program_id(1)

    if norm:
        @pl.when(j == 0)
        def _():
            xnp_ref[...] = _normed(xp_ref, g_ref)

        @pl.when((i == 0) & (j == 0))
        def _():
            xns_ref[...] = _normed(xs_ref, g_ref)

    lhs_p = xnp_ref[...] if norm else xp_ref[...]

    def lhs_s():
        return xns_ref[...] if norm else xs_ref[...].astype(BF16)

    def emit(lhs, res_ref, o_ref, rope):
        acc = jnp.dot(lhs, w_ref[...], preferred_element_type=F32)
        if has_res:
            acc = acc + res_ref[...]
        if not rope_tiles:
            o_ref[...] = acc.astype(o_ref.dtype)
            return

        @pl.when(j < rope_tiles)
        def _():
            cos = rope[0][...]
            sin = rope[1][...]
            for h in range(acc.shape[1] // HEAD_DIM):
                a = acc[:, h * HEAD_DIM:(h + 1) * HEAD_DIM]
                o_ref[:, h * HEAD_DIM:(h + 1) * HEAD_DIM] = (
                    a * cos + pltpu.roll(a, HEAD_DIM // 2, 1) * sin).astype(o_ref.dtype)

        @pl.when(j >= rope_tiles)
        def _():
            o_ref[...] = acc.astype(o_ref.dtype)

    emit(lhs_p, resp_ref, op_ref, rope_p)

    @pl.when(i == 0)
    def _():
        emit(lhs_s(), ress_ref, os_ref, rope_s)

    @pl.when(i > 0)
    def _():
        os_ref[...] = jnp.zeros_like(os_ref)

    if has_aux:
        @pl.when(j == pl.num_programs(1) - 1)
        def _():
            auxp_ref[...] = jnp.dot(lhs_p, auxw_ref[...], preferred_element_type=F32)

            @pl.when(i == 0)
            def _():
                auxs_ref[...] = jnp.dot(lhs_s(), auxw_ref[...], preferred_element_type=F32)

            @pl.when(i > 0)
            def _():
                auxs_ref[...] = jnp.zeros_like(auxs_ref)


def _proj(xp, xs, w, *, n_cols=None, layer=0, gain=None, aux_w=None, rope=None, rope_cols=None, res=None,
          out_dtype=F32, tm=1024, tn=512):
    Mp, K = xp.shape
    S = SAMPLE_ROWS
    assert xs.shape == (S, K) and w.ndim == 3 and w.dtype == BF16
    n_cols = w.shape[-1] if n_cols is None else n_cols
    assert Mp % tm == 0 and n_cols % tn == 0
    norm = gain is not None
    rope_tiles = 0
    if rope is not None:
        rope_cols = n_cols if rope_cols is None else rope_cols
        assert rope_cols % tn == 0 and res is None
        rope_tiles = rope_cols // tn
    row = lambda i, j: (i, 0)
    const = lambda i, j: (0, 0)
    args = [xp, xs]
    in_specs = [pl.BlockSpec((tm, K), row), pl.BlockSpec((S, K), const)]
    if norm:
        args.append(gain.reshape(1, K))
        in_specs.append(pl.BlockSpec((1, K), const))
    args.append(w)
    in_specs.append(pl.BlockSpec((None, K, tn), lambda i, j: (layer, 0, j)))
    if aux_w is not None:
        args.append(aux_w)
        in_specs.append(pl.BlockSpec(aux_w.shape, const))
    if rope is not None:
        (cos_p, sin_p, rows_per_seq), (cos_s, sin_s) = rope
        nseq = rows_per_seq // tm
        args += [cos_p, sin_p, cos_s, sin_s]
        in_specs += [pl.BlockSpec((tm, HEAD_DIM), lambda i, j: (i % nseq, 0))] * 2
        in_specs += [pl.BlockSpec((S, HEAD_DIM), const)] * 2
    if res is not None:
        args += list(res)
        in_specs += [pl.BlockSpec((tm, tn), lambda i, j: (i, j)), pl.BlockSpec((S, tn), lambda i, j: (0, j))]
    ni = Mp // tm
    out_shape = [jax.ShapeDtypeStruct((Mp, n_cols), out_dtype), jax.ShapeDtypeStruct((ni * S, n_cols), F32)]
    out_specs = [pl.BlockSpec((tm, tn), lambda i, j: (i, j)), pl.BlockSpec((S, tn), _sample_block)]
    if aux_w is not None:
        na = aux_w.shape[1]
        out_shape += [jax.ShapeDtypeStruct((Mp, na), F32), jax.ShapeDtypeStruct((ni * S, na), F32)]
        out_specs += [pl.BlockSpec((tm, na), row), pl.BlockSpec((S, na), row)]
    outs = pl.pallas_call(
        functools.partial(_proj_kernel, norm=norm, rope_tiles=rope_tiles,
                          has_res=res is not None, has_aux=aux_w is not None),
        grid=(ni, n_cols // tn),
        in_specs=in_specs,
        out_specs=out_specs,
        out_shape=out_shape,
        scratch_shapes=[pltpu.VMEM((tm, K), BF16), pltpu.VMEM((S, K), BF16)] if norm else [],
        compiler_params=_params("arbitrary", "arbitrary"),
        name="proj",
    )(*args)
    return [o if n % 2 == 0 else o[:S] for n, o in enumerate(outs)]


def _gateup_kernel(xp_ref, xs_ref, g_ref, wg_ref, wu_ref, op_ref, os_ref, xnp_ref, xns_ref):
    i = pl.program_id(0)
    j = pl.program_id(1)

    @pl.when(j == 0)
    def _():
        xnp_ref[...] = _normed(xp_ref, g_ref)

    @pl.when((i == 0) & (j == 0))
    def _():
        xns_ref[...] = _normed(xs_ref, g_ref)

    def swiglu(xn):
        g = jnp.dot(xn, wg_ref[...], preferred_element_type=F32)
        u = jnp.dot(xn, wu_ref[...], preferred_element_type=F32)
        return _silu(g) * u

    op_ref[...] = swiglu(xnp_ref[...]).astype(op_ref.dtype)

    @pl.when(i == 0)
    def _():
        os_ref[...] = swiglu(xns_ref[...])

    @pl.when(i > 0)
    def _():
        os_ref[...] = jnp.zeros_like(os_ref)


def _gateup(xp, xs, gain, w_gu, layer, *, tm=1024, tn=512):
    Mp, K = xp.shape
    S = SAMPLE_ROWS
    hidden = w_gu.shape[2] // 2
    assert xs.shape == (S, K) and Mp % tm == 0 and hidden % tn == 0 and w_gu.dtype == BF16
    nj = hidden // tn
    hp, hs = pl.pallas_call(
        _gateup_kernel,
        grid=(Mp // tm, nj),
        in_specs=[pl.BlockSpec((tm, K), lambda i, j: (i, 0)),
                  pl.BlockSpec((S, K), lambda i, j: (0, 0)),
                  pl.BlockSpec((1, K), lambda i, j: (0, 0)),
                  pl.BlockSpec((None, K, tn), lambda i, j: (layer, 0, j)),
                  pl.BlockSpec((None, K, tn), lambda i, j: (layer, 0, j + nj))],
        out_specs=[pl.BlockSpec((tm, tn), lambda i, j: (i, j)), pl.BlockSpec((S, tn), _sample_block)],
        out_shape=[jax.ShapeDtypeStruct((Mp, hidden), BF16), jax.ShapeDtypeStruct((Mp // tm * S, hidden), F32)],
        scratch_shapes=[pltpu.VMEM((tm, K), BF16), pltpu.VMEM((S, K), BF16)],
        compiler_params=_params("arbitrary", "arbitrary"),
        name="gateup",
    )(xp, xs, gain.reshape(1, K), w_gu, w_gu)
    return hp, hs[:S]


def _conv_silu_slab(src, s, raw_ref, act_ref, cw_ref, cb_ref):
    Q = SSD_CHUNK
    raw_ref[s, CONV_TAIL:, :] = src
    for parity in (0, 1):
        acc = cb_ref[s]
        for k in range(CONV_W):
            first = CONV_TAIL - (CONV_W - 1) + k + parity
            acc = acc + raw_ref[s, pl.ds(first, Q // 2, stride=2), :] * cw_ref[s, k:k + 1, :]
        act_ref[s, pl.ds(parity, Q // 2, stride=2), :] = _silu(acc)
    raw_ref[s, 0:CONV_TAIL, :] = raw_ref[s, Q:Q + CONV_TAIL, :]


def _ssd_kernel(z_ref, x_ref, b_ref, c_ref, dt_ref, cw_ref, cb_ref,
                dtb_ref, alog_ref, dexp_ref, gn_ref,
                y_ref, hout_ref,
                h_ref, raw_ref, act_ref):
    Q = SSD_CHUNK
    P = SSM_HEAD_DIM
    N = SSM_STATE
    d_inner = x_ref.shape[1]
    n_heads = d_inner // P
    hpg = n_heads // SSM_GROUPS
    gw = hpg * P
    n_xs = d_inner // LANES
    n_bs = b_ref.shape[1] // LANES
    c = pl.program_id(1)

    @pl.when(c == 0)
    def _():
        h_ref[...] = jnp.zeros_like(h_ref)
        raw_ref[:, 0:CONV_TAIL, :] = jnp.zeros((raw_ref.shape[0], CONV_TAIL, LANES), F32)

    for s in range(n_xs + 2 * n_bs):
        if s < n_xs:
            src = x_ref[:, s * LANES:(s + 1) * LANES]
        elif s < n_xs + n_bs:
            src = b_ref[:, (s - n_xs) * LANES:(s - n_xs + 1) * LANES]
        else:
            src = c_ref[:, (s - n_xs - n_bs) * LANES:(s - n_xs - n_bs + 1) * LANES]
        _conv_silu_slab(src, s, raw_ref, act_ref, cw_ref, cb_ref)

    dt = _softplus(dt_ref[...] + dtb_ref[...])
    dA = dt * (-jnp.exp(alog_ref[...]))
    row = lax.broadcasted_iota(jnp.int32, (Q, Q), 0)
    col = lax.broadcasted_iota(jnp.int32, (Q, Q), 1)
    causal = row >= col
    tri = jnp.where(causal, 1.0, 0.0).astype(F32)
    cs = jnp.dot(tri, dA, precision=HIGHEST, preferred_element_type=F32)
    csT = cs.T
    dtT = dt.T
    dt_decay_end = dt * jnp.exp(cs[Q - 1:Q, :] - cs)
    chunk_decay = jnp.broadcast_to(jnp.exp(csT[0:n_heads, Q - 1:Q]), (n_heads, N))

    lane_lo = lax.broadcasted_iota(jnp.int32, (Q, 2 * P), 1) < P

    def pair_cols(arr, q):
        a0 = jnp.broadcast_to(arr[:, 2 * q:2 * q + 1], (Q, 2 * P))
        a1 = jnp.broadcast_to(arr[:, 2 * q + 1:2 * q + 2], (Q, 2 * P))
        return jnp.where(lane_lo, a0, a1)

    for g in range(SSM_GROUPS):
        Bg = act_ref[n_xs + g].astype(BF16)
        Cg = act_ref[n_xs + n_bs + g].astype(BF16)
        CB = lax.dot_general(Cg, Bg, NT_DIMS, preferred_element_type=F32)
        Hg = h_ref[g * hpg:(g + 1) * hpg].reshape(gw, N)
        y_off = lax.dot_general(Cg, Hg.astype(BF16), NT_DIMS, preferred_element_type=F32)

        ys = []
        xds = []
        for qq in range(hpg // 2):
            q = g * (hpg // 2) + qq
            lo, hi = q * 2 * P, (q + 1) * 2 * P
            xs = act_ref[q]
            xds.append((xs * pair_cols(dt_decay_end, q)).astype(BF16))
            cs_cols = [jnp.broadcast_to(cs[:, hh:hh + 1], (Q, Q)) for hh in (2 * q, 2 * q + 1)]
            atts = []
            for hh, cs_col in zip((2 * q, 2 * q + 1), cs_cols):
                seg = cs_col - csT[hh:hh + 1, :]
                decay = jnp.exp(jnp.where(causal, seg, -jnp.inf))
                atts.append(((CB * decay) * dtT[hh:hh + 1, :]).astype(BF16))
            att = jnp.concatenate(atts, axis=1)
            xbd = jnp.concatenate([jnp.where(lane_lo, xs, 0.0).astype(BF16),
                                   jnp.where(lane_lo, 0.0, xs).astype(BF16)], axis=0)
            y = jnp.dot(att, xbd, preferred_element_type=F32)
            y = y + y_off[:, qq * 2 * P:(qq + 1) * 2 * P] * jnp.exp(jnp.where(lane_lo, *cs_cols))
            y = y + xs * dexp_ref[:, lo:hi]
            ys.append(y * _silu(z_ref[:, lo:hi]))

        ssq = ys[0] * ys[0]
        for y in ys[1:]:
            ssq = ssq + y * y
        scale = lax.rsqrt(jnp.sum(ssq, axis=-1, keepdims=True) / gw + EPS)
        for qq, y in enumerate(ys):
            lo = g * gw + qq * 2 * P
            y_ref[:, lo:lo + 2 * P] = ((y * scale) * gn_ref[:, lo:lo + 2 * P]).astype(y_ref.dtype)

        S = lax.dot_general(jnp.concatenate(xds, axis=1), Bg, TN_DIMS, preferred_element_type=F32)
        for r in range(hpg):
            hh = g * hpg + r
            h_ref[hh] = h_ref[hh] * chunk_decay[hh:hh + 1, :] + S[r * P:(r + 1) * P, :]

    @pl.when(c == pl.num_programs(1) - 1)
    def _():
        hout_ref[0] = h_ref[...]


def _ssd_prompt(zxbc, dt, batch, conv_w, conv_b, dt_bias, a_log, d_skip, gate_norm):
    M = zxbc.shape[0]
    L = M // batch
    H = dt_bias.shape[0]
    d_inner = H * SSM_HEAD_DIM
    bc = SSM_GROUPS * SSM_STATE
    conv_dim = d_inner + 2 * bc
    Q = SSD_CHUNK
    nc = L // Q
    assert L % Q == 0 and d_inner % bc == 0
    xb = d_inner // bc
    row_map = lambda col: (lambda b, c: (b * nc + c, col))
    const = lambda b, c: (0, 0)
    d_exp = jnp.repeat(d_skip.astype(F32), SSM_HEAD_DIM).reshape(1, d_inner)
    assert H <= LANES and SSM_STATE == LANES and 2 * SSM_HEAD_DIM == LANES
    lane_pad = lambda v: jnp.pad(v.reshape(1, H), ((0, 0), (0, LANES - H)))
    n_slabs = conv_dim // LANES
    cw_slabs = conv_w.reshape(CONV_W, n_slabs, LANES).transpose(1, 0, 2)
    cb_slabs = conv_b.reshape(n_slabs, 1, LANES)
    const3 = lambda b, c: (0, 0, 0)
    y, h_final = pl.pallas_call(
        _ssd_kernel,
        grid=(batch, nc),
        in_specs=[pl.BlockSpec((Q, d_inner), row_map(0)),
                  pl.BlockSpec((Q, d_inner), row_map(1)),
                  pl.BlockSpec((Q, bc), row_map(2 * xb)),
                  pl.BlockSpec((Q, bc), row_map(2 * xb + 1)),
                  pl.BlockSpec((Q, LANES), row_map(0)),
                  pl.BlockSpec((n_slabs, CONV_W, LANES), const3),
                  pl.BlockSpec((n_slabs, 1, LANES), const3),
                  pl.BlockSpec((1, LANES), const),
                  pl.BlockSpec((1, LANES), const),
                  pl.BlockSpec((1, d_inner), const),
                  pl.BlockSpec((1, d_inner), const)],
        out_specs=[pl.BlockSpec((Q, d_inner), row_map(0)),
                   pl.BlockSpec((1, H, SSM_HEAD_DIM, SSM_STATE), lambda b, c: (b, 0, 0, 0))],
        out_shape=[jax.ShapeDtypeStruct((M, d_inner), BF16),
                   jax.ShapeDtypeStruct((batch, H, SSM_HEAD_DIM, SSM_STATE), F32)],
        scratch_shapes=[pltpu.VMEM((H, SSM_HEAD_DIM, SSM_STATE), F32),
                        pltpu.VMEM((n_slabs, Q + CONV_TAIL, LANES), F32),
                        pltpu.VMEM((n_slabs, Q, LANES), F32)],
        compiler_params=_params("parallel", "arbitrary"),
        name="ssd_prompt",
    )(zxbc, zxbc, zxbc, zxbc, dt, cw_slabs, cb_slabs,
      lane_pad(dt_bias), lane_pad(a_log), d_exp, gate_norm.reshape(1, d_inner))
    return y, h_final


def _ssd_step_kernel(zxbc_ref, dt_ref, cs_ref, h0_ref, cw_ref, cb_ref, dtb_ref, alog_ref,
                     dexp_ref, gn_ref, y_ref, cso_ref, ho_ref):
    P = SSM_HEAD_DIM
    N = SSM_STATE
    n_heads = h0_ref.shape[2]
    d_inner = n_heads * P
    hpg = n_heads // SSM_GROUPS
    gw = hpg * P
    conv_dim = cw_ref.shape[1]

    raw = zxbc_ref[0, :, d_inner:d_inner + conv_dim]
    prev = cs_ref[0, 0]
    acc = cb_ref[...]
    for k in range(CONV_W - 1):
        acc = acc + prev[k:k + 1, :] * cw_ref[k:k + 1, :]
    acc = acc + raw * cw_ref[CONV_W - 1:CONV_W, :]
    xbc = _silu(acc)
    cso_ref[0, 0, 0:CONV_W - 2, :] = prev[1:CONV_W - 1, :]
    cso_ref[0, 0, CONV_W - 2:CONV_W - 1, :] = raw

    dt = _softplus(dt_ref[0, :, 0:n_heads] + dtb_ref[...])
    decay = jnp.exp(dt * (-jnp.exp(alog_ref[...])))
    z = zxbc_ref[0, :, 0:d_inner]

    eye = (lax.broadcasted_iota(jnp.int32, (P, P), 0) == lax.broadcasted_iota(jnp.int32, (P, P), 1))
    ys = []
    for g in range(SSM_GROUPS):
        Bg = xbc[:, d_inner + g * N:d_inner + (g + 1) * N]
        Cg = xbc[:, d_inner + SSM_GROUPS * N + g * N:d_inner + SSM_GROUPS * N + (g + 1) * N]
        Bb = jnp.broadcast_to(Bg, (P, N))
        for r in range(hpg):
            hh = g * hpg + r
            xs = xbc[:, hh * P:(hh + 1) * P]
            xdt = xs * dt[:, hh:hh + 1]
            xdiag = jnp.where(eye, jnp.broadcast_to(xdt, (P, P)), 0.0)
            outer = jnp.dot(xdiag, Bb, precision=HIGHEST, preferred_element_type=F32)
            ho_ref[0, 0, hh] = h0_ref[0, 0, hh] * decay[:, hh:hh + 1] + outer
        Hg = ho_ref[0, 0, g * hpg:(g + 1) * hpg].reshape(gw, N)
        yg = lax.dot_general(Cg, Hg, NT_DIMS, precision=HIGHEST, preferred_element_type=F32)
        lo, hi = g * gw, (g + 1) * gw
        yg = yg + xbc[:, lo:hi] * dexp_ref[:, lo:hi]
        yg = yg * _silu(z[:, lo:hi])
        yg = yg * lax.rsqrt(jnp.mean(yg * yg, axis=-1, keepdims=True) + EPS)
        ys.append(yg * gn_ref[:, lo:hi])
    y_ref[0] = jnp.concatenate(ys, axis=1).astype(y_ref.dtype)


def _ssd_step(zxbc, dt, conv_state, ssm_state, conv_w, conv_b, dt_bias, a_log, d_skip, gate_norm):
    B = zxbc.shape[0]
    H = dt_bias.shape[0]
    d_inner = H * SSM_HEAD_DIM
    conv_dim = conv_w.shape[1]
    const = lambda b: (0, 0)
    d_exp = jnp.repeat(d_skip.astype(F32), SSM_HEAD_DIM).reshape(1, d_inner)
    cs4 = conv_state.reshape(1, B, CONV_W - 1, conv_dim)
    h5 = ssm_state.reshape(1, B, H, SSM_HEAD_DIM, SSM_STATE)
    return pl.pallas_call(
        _ssd_step_kernel,
        grid=(B,),
        in_specs=[pl.BlockSpec((1, 1, zxbc.shape[1]), lambda b: (b, 0, 0)),
                  pl.BlockSpec((1, 1, dt.shape[1]), lambda b: (b, 0, 0)),
                  pl.BlockSpec((1, 1, CONV_W - 1, conv_dim), lambda b: (0, b, 0, 0)),
                  pl.BlockSpec((1, 1, H, SSM_HEAD_DIM, SSM_STATE), lambda b: (0, b, 0, 0, 0)),
                  pl.BlockSpec((CONV_W, conv_dim), const),
                  pl.BlockSpec((1, conv_dim), const),
                  pl.BlockSpec((1, H), const),
                  pl.BlockSpec((1, H), const),
                  pl.BlockSpec((1, d_inner), const),
                  pl.BlockSpec((1, d_inner), const)],
        out_specs=[pl.BlockSpec((1, 1, d_inner), lambda b: (b, 0, 0)),
                   pl.BlockSpec((1, 1, CONV_W - 1, conv_dim), lambda b: (0, b, 0, 0)),
                   pl.BlockSpec((1, 1, H, SSM_HEAD_DIM, SSM_STATE), lambda b: (0, b, 0, 0, 0))],
        out_shape=[jax.ShapeDtypeStruct((B, 1, d_inner), F32),
                   jax.ShapeDtypeStruct(cs4.shape, F32),
                   jax.ShapeDtypeStruct(h5.shape, F32)],
        compiler_params=_params("arbitrary"),
        name="ssd_step",
    )(zxbc.reshape(B, 1, -1), dt.reshape(B, 1, -1), cs4, h5, conv_w, conv_b.reshape(1, conv_dim),
      dt_bias.reshape(1, H), a_log.reshape(1, H), d_exp, gate_norm.reshape(1, d_inner))


def _attn_kernel(q0_ref, q1_ref, q2_ref, k_ref, v_ref, o_ref, on_ref, lse_ref, qs_ref, ks_ref, vs_ref):
    L = k_ref.shape[0]
    QB = ATT_BLOCK
    KW = QB + DIL_SLOTS
    P = ATT_PRESTRIDE
    p_shift = P.bit_length() - 1
    scale = HEAD_DIM ** -0.5
    e = lax.broadcasted_iota(jnp.int32, (QB, KW), 0) - lax.broadcasted_iota(jnp.int32, (QB, KW), 1)

    staged = {rate: rate > P for rate in DIL_RATES}
    for src_ref, dst_ref, needed in ((q2_ref, qs_ref, staged[DIL_RATES[2]]), (k_ref, ks_ref, any(staged.values())),
                                     (v_ref, vs_ref, any(staged.values()))):
        if needed:
            def stage(i, carry, src_ref=src_ref, dst_ref=dst_ref):
                cp = i & (P - 1)
                t = i >> p_shift
                dst = pl.multiple_of(cp * (L // P) + t * QB, QB)
                dst_ref[pl.ds(dst, QB), :] = src_ref[pl.ds(cp + P * QB * t, QB, stride=P), :]
                return carry
            lax.fori_loop(0, L // QB, stage, 0, unroll=4)

    for g, (q_ref, rate) in enumerate(zip((q0_ref, q1_ref, q2_ref), DIL_RATES)):
        shift = rate.bit_length() - 1

        def unit(n, carry, g=g, q_ref=q_ref, rate=rate, shift=shift):
            c = n & (rate - 1)
            u0 = (n >> shift) * QB
            v0 = jnp.maximum(u0 - DIL_SLOTS, 0)
            q_rows = pl.ds(c + rate * u0, QB, stride=rate)
            if staged[rate]:
                base = (c & (P - 1)) * (L // P) + (c >> p_shift)
                q = qs_ref[pl.ds(base + (rate // P) * u0, QB, stride=rate // P), :].astype(BF16)
                k_rows = pl.ds(base + (rate // P) * v0, KW, stride=rate // P)
                k = ks_ref[k_rows, :].astype(BF16)
                v = vs_ref[k_rows, :].astype(BF16)
            else:
                k_rows = pl.ds(c + rate * v0, KW, stride=rate)
                q = q_ref[q_rows, :].astype(BF16)
                k = k_ref[k_rows, :].astype(BF16)
                v = v_ref[k_rows, :].astype(BF16)
            s = lax.dot_general(q, k, NT_DIMS, preferred_element_type=F32) * scale
            d = e + (u0 - v0)
            s = jnp.where(d >= 0, s, -jnp.inf)
            s = jnp.where(d <= DIL_SLOTS, s, -jnp.inf)
            m = s.max(axis=1, keepdims=True)
            p = jnp.exp(s - m)
            l = p.sum(axis=1, keepdims=True)
            o = jnp.dot(p.astype(BF16), v, preferred_element_type=F32)
            on_ref[g, q_rows, :] = o / l
            lse_ref[g, q_rows, :] = jnp.broadcast_to(m + jnp.log(l), (QB, HEAD_DIM))
            return carry

        lax.fori_loop(0, L // QB, unit, 0, unroll=ATT_UNROLL)

    def mix(i, carry):
        rows = pl.ds(pl.multiple_of(i * QB, QB), QB)
        lses = [lse_ref[g, rows, :] for g in range(len(DIL_RATES))]
        m = functools.reduce(jnp.maximum, lses)
        ws = [jnp.exp(x - m) for x in lses]
        num = functools.reduce(jnp.add, [w * on_ref[g, rows, :] for g, w in enumerate(ws)])
        o_ref[rows, :] = (num / functools.reduce(jnp.add, ws)).astype(o_ref.dtype)
        return carry

    lax.fori_loop(0, L // QB, mix, 0, unroll=2)


def _attn_prompt(q, kv, batch):
    M = q.shape[0]
    L = M // batch
    n_grp = len(DIL_RATES)
    assert all(r & (r - 1) == 0 and L % (ATT_BLOCK * r) == 0 and L >= r * (ATT_BLOCK + DIL_SLOTS)
               for r in DIL_RATES)
    qspec = lambda g: pl.BlockSpec((L, HEAD_DIM), lambda b, h: (b, g * KV_HEADS + h))
    kspec = pl.BlockSpec((L, HEAD_DIM), lambda b, h: (b, h))
    vspec = pl.BlockSpec((L, HEAD_DIM), lambda b, h: (b, KV_HEADS + h))
    return pl.pallas_call(
        _attn_kernel,
        grid=(batch, KV_HEADS),
        in_specs=[qspec(0), qspec(1), qspec(2), kspec, vspec],
        out_specs=pl.BlockSpec((L, HEAD_DIM), lambda b, h: (b, h)),
        out_shape=jax.ShapeDtypeStruct((M, KV_HEADS * HEAD_DIM), BF16),
        scratch_shapes=[pltpu.VMEM((n_grp, L, HEAD_DIM), F32), pltpu.VMEM((n_grp, L, HEAD_DIM), F32)]
        + [pltpu.VMEM((L, HEAD_DIM), F32)] * 3,
        compiler_params=_params("parallel", "arbitrary"),
        name="attn_prompt",
    )(q, q, q, kv, kv)


def _bf16_round(a):
    return a.astype(BF16).astype(F32)


def _attn_step_kernel(q_ref, *refs):
    n = len(DIL_RATES)
    k_refs, v_refs = refs[0:n], refs[n:2 * n]
    kn_ref, vn_ref, o_ref = refs[2 * n:]
    scale = HEAD_DIM ** -0.5
    kn = _bf16_round(kn_ref[0, 0])
    vn = _bf16_round(vn_ref[0, 0])
    scores, new_scores = [], []
    for g, k_ref in enumerate(k_refs):
        qg = _bf16_round(q_ref[0, g])
        kg = _bf16_round(k_ref[0, :, 0])
        scores.append(jnp.sum(kg * qg[None], axis=-1, keepdims=True) * scale)
        new_scores.append(jnp.sum(kn * qg, axis=-1, keepdims=True) * scale)
    m = functools.reduce(jnp.maximum, [s.max(axis=0) for s in scores] + new_scores)
    l = jnp.zeros_like(m)
    o = jnp.zeros((KV_HEADS, HEAD_DIM), F32)
    for s, s_new, v_ref in zip(scores, new_scores, v_refs):
        p = jnp.exp(s - m[None])
        p_new = jnp.exp(s_new - m)
        l = l + p.sum(axis=0) + p_new
        o = o + (_bf16_round(p) * _bf16_round(v_ref[0, :, 0])).sum(axis=0) + _bf16_round(p_new) * vn
    o_ref[0, 0] = (o / l).astype(o_ref.dtype)


def _attn_step(q, cache_k, cache_v, k_new, v_new):
    B, T = cache_k.shape[0], cache_k.shape[1]
    S = DIL_SLOTS
    assert T == S * max(DIL_RATES), "every slot of every dilation group lies inside the cached window"
    qg = q.reshape(B, len(DIL_RATES), KV_HEADS, HEAD_DIM)
    row_shape = (B, 1, KV_HEADS, HEAD_DIM)
    row_spec = pl.BlockSpec((1, 1, KV_HEADS, HEAD_DIM), lambda b: (b, 0, 0, 0))
    args = [qg]
    in_specs = [pl.BlockSpec((1, len(DIL_RATES), KV_HEADS, HEAD_DIM), lambda b: (b, 0, 0, 0))]
    for cache in (cache_k, cache_v):
        for rate in DIL_RATES:
            args.append(cache.reshape(B, T // rate, rate, KV_HEADS, HEAD_DIM))
            in_specs.append(pl.BlockSpec((1, S, 1, KV_HEADS, HEAD_DIM),
                                         lambda b, blk=T // rate // S - 1: (b, blk, 0, 0, 0)))
    args += [k_new.reshape(row_shape), v_new.reshape(row_shape)]
    in_specs += [row_spec, row_spec]
    o = pl.pallas_call(
        _attn_step_kernel,
        grid=(B,),
        in_specs=in_specs,
        out_specs=row_spec,
        out_shape=jax.ShapeDtypeStruct(row_shape, F32),
        compiler_params=_params("parallel"),
        name="attn_step",
    )(*args)
    return o.reshape(B, KV_HEADS * HEAD_DIM)


def _rmsnorm_kernel(x_ref, g_ref, o_ref):
    x = x_ref[...]
    o_ref[...] = (x * _rms_scale(x)) * g_ref[...]


def _rmsnorm(x, gain, *, tm=512):
    M, K = x.shape
    tm = min(tm, M)
    assert M % tm == 0
    return pl.pallas_call(
        _rmsnorm_kernel,
        grid=(M // tm,),
        in_specs=[pl.BlockSpec((tm, K), lambda i: (i, 0)), pl.BlockSpec((1, K), lambda i: (0, 0))],
        out_specs=pl.BlockSpec((tm, K), lambda i: (i, 0)),
        out_shape=jax.ShapeDtypeStruct((M, K), F32),
        compiler_params=_params("parallel"),
        name="final_norm",
    )(x, gain.reshape(1, K))


def _rope_tables(pos):
    half = HEAD_DIM // 2
    inv = jnp.power(jnp.float32(ROPE_THETA), -jnp.arange(half, dtype=jnp.float32) / half)
    ang = pos.astype(jnp.float32)[:, None] * inv[None, :]
    cos = jnp.cos(ang)
    sin = jnp.sin(ang)
    return jnp.concatenate([cos, cos], axis=1), jnp.concatenate([-sin, sin], axis=1)


def kernel(x_prompt, x_sample, state_conv, state_ssm, cache_k, cache_v, a_norm, a_w_in, a_conv_w, a_conv_b,
           a_dt_bias, a_log, a_d, a_gate_norm, a_w_out, kv_norm, w_kv, b_norm, b_w_q, b_w_o, ffn_norm,
           ffn_w_gu, ffn_w_down, final_norm):
    assert a_norm.shape[0] == 1 and b_norm.shape[0] == 1, "one Mamba-2 layer followed by one attention layer"
    Bp, Lp, D = x_prompt.shape
    Bs, Ls, _ = x_sample.shape
    assert Ls == 1 and Bs == SAMPLE_ROWS, "sample group decodes one token for SAMPLE_ROWS sequences"
    n_heads = a_dt_bias.shape[1]
    d_inner = a_w_out.shape[1]
    conv_dim = a_conv_w.shape[2]
    kv_dim = KV_HEADS * HEAD_DIM
    w_dt = jnp.pad(a_w_in[0][:, d_inner + conv_dim:], ((0, 0), (0, LANES - n_heads))).astype(BF16)
    ssd_w = (a_conv_w[0], a_conv_b[0], a_dt_bias[0], a_log[0], a_d[0], a_gate_norm[0])
    a_w_in, a_w_out, b_w_q, b_w_o, ffn_w_gu, ffn_w_down = (
        w.astype(BF16) for w in (a_w_in, a_w_out, b_w_q, b_w_o, ffn_w_gu, ffn_w_down))
    w_kv = w_kv.astype(BF16)

    cos_p, sin_p = _rope_tables(jnp.arange(Lp, dtype=jnp.int32))
    cos_s, sin_s = _rope_tables(PAST_LEN + jnp.arange(Ls, dtype=jnp.int32))
    rope = ((cos_p, sin_p, Lp), (jnp.broadcast_to(cos_s, (Bs, HEAD_DIM)), jnp.broadcast_to(sin_s, (Bs, HEAD_DIM))))

    xp0 = x_prompt.reshape(Bp * Lp, D)
    xs0 = x_sample.reshape(Bs, D)

    zxbc_p, zxbc_s, dt_p, dt_s = _proj(xp0, xs0, a_w_in, n_cols=d_inner + conv_dim, gain=a_norm[0],
                                       aux_w=w_dt, tn=1024)
    y_p, p_ssm = _ssd_prompt(zxbc_p, dt_p, Bp, *ssd_w)
    p_conv = zxbc_p.reshape(Bp, Lp, -1)[:, Lp - (CONV_W - 1):, d_inner:]
    y_s, s_conv, s_ssm = _ssd_step(zxbc_s, dt_s, state_conv[0], state_ssm[0], *ssd_w)
    x1 = _proj(y_p, y_s.reshape(Bs, d_inner), a_w_out, res=(xp0, xs0))
    h = _gateup(*x1, ffn_norm[0], ffn_w_gu, 0)
    x2 = _proj(*h, ffn_w_down, layer=0, res=x1)

    kv_p, kv_s = _proj(*x2, w_kv[None], gain=kv_norm, rope=rope, rope_cols=kv_dim, tn=1024)
    q_p, q_s = _proj(*x2, b_w_q, gain=b_norm[0], rope=rope, tn=1024)
    o_p = _attn_prompt(q_p, kv_p, Bp)
    o_s = _attn_step(q_s, cache_k, cache_v, kv_s[:, :kv_dim], kv_s[:, kv_dim:])
    x3 = _proj(o_p, o_s, b_w_o, res=x2, tn=1024)
    h = _gateup(*x3, ffn_norm[1], ffn_w_gu, 1)
    x4 = _proj(*h, ffn_w_down, layer=1, res=x3)

    y_prompt = _rmsnorm(x4[0], final_norm).reshape(Bp, Lp, D)
    y_sample = _rmsnorm(x4[1], final_norm).reshape(Bs, Ls, D)
    keep = min(DIL_SLOTS * max(DIL_RATES), Lp)
    p_kv = kv_p.reshape(Bp, Lp, 2 * KV_HEADS, HEAD_DIM)[:, Lp - keep:]
    s_kv = kv_s.reshape(Bs, Ls, 2 * KV_HEADS, HEAD_DIM)
    return (y_prompt, y_sample, p_conv[None], p_ssm[None], p_kv[:, :, :KV_HEADS], p_kv[:, :, KV_HEADS:],
            s_conv, s_ssm, s_kv[:, :, :KV_HEADS], s_kv[:, :, KV_HEADS:])
```

```python
import functools

import jax
import jax.numpy as jnp
from jax import lax
from jax.experimental import pallas as pl
from jax.experimental.pallas import tpu as pltpu

F32 = jnp.float32
BF16 = jnp.bfloat16
HIGHEST = lax.Precision.HIGHEST

EPS = 1e-6
ROPE_THETA = 10000.0
SSD_CHUNK = 128
SSM_HEAD_DIM = 64
SSM_STATE = 128
SSM_GROUPS = 8
CONV_W = 4
HEAD_DIM = 128
KV_HEADS = 8
DIL_RATES = (1, 4, 16)
DIL_SLOTS = 128
ATT_BLOCK = 128
ATT_UNROLL = 8
ATT_PRESTRIDE = 4
PAST_LEN = 16384

V7X_VMEM_BYTES = 64 * 1024 * 1024
VMEM_LIMIT = V7X_VMEM_BYTES - 8 * 1024 * 1024
LANES = 128
CONV_TAIL = 8
SAMPLE_ROWS = 8

NT_DIMS = (((1,), (1,)), ((), ()))
TN_DIMS = (((0,), (0,)), ((), ()))


def _params(*sem):
    return pltpu.CompilerParams(dimension_semantics=sem, vmem_limit_bytes=VMEM_LIMIT)


def _silu(x):
    h = 0.5 * x
    return h + h * jnp.tanh(h)


def _softplus(x):
    return jnp.maximum(x, 0.0) + jnp.log1p(jnp.exp(-jnp.abs(x)))


def _rms_scale(x):
    return lax.rsqrt(jnp.mean(x * x, axis=-1, keepdims=True) + EPS)


def _normed(x_ref, g_ref):
    x = x_ref[...]
    return ((x * _rms_scale(x)) * g_ref[...]).astype(BF16)


def _sample_block(i, j):
    return (i, j)


def _cast_specs(casts, n_steps, nj):
    BF16_ROWS = 16
    args, in_specs, out_shape, out_specs, tiles = [], [], [], [], []
    for src, layer in casts:
        R, C = src.shape[1:]
        rows = next(r for r in range(BF16_ROWS, R + 1, BF16_ROWS) if R % r == 0 and R // r <= n_steps)
        nt = R // rows
        tile = lambda i, j, nt=nt: jnp.minimum(i * nj + j, nt - 1)
        args.append(src)
        in_specs.append(pl.BlockSpec((None, rows, C), lambda i, j, layer=layer, tile=tile: (layer, tile(i, j), 0)))
        out_shape.append(jax.ShapeDtypeStruct((1, R, C), BF16))
        out_specs.append(pl.BlockSpec((None, rows, C), lambda i, j, tile=tile: (0, tile(i, j), 0)))
        tiles.append(nt)
    return args, in_specs, out_shape, out_specs, tuple(tiles)


def _run_casts(src_refs, dst_refs, tiles):
    step = pl.program_id(0) * pl.num_programs(1) + pl.program_id(1)
    for src_ref, dst_ref, nt in zip(src_refs, dst_refs, tiles):
        @pl.when(step < nt)
        def _(src_ref=src_ref, dst_ref=dst_ref):
            dst_ref[...] = src_ref[...].astype(BF16)


def _proj_kernel(*refs, segs, norm, has_rope, has_res, has_aux, cast_tiles):
    it = iter(refs)
    xp_ref, xs_ref = next(it), next(it)
    g_refs = [next(it) for _ in segs] if norm else None
    w_refs = [next(it) for _ in segs]
    auxw_ref = next(it) if has_aux else None
    rope_p = (next(it), next(it)) if has_rope else None
    rope_s = (next(it), next(it)) if has_rope else None
    resp_ref, ress_ref = (next(it), next(it)) if has_res else (None, None)
    cast_srcs = [next(it) for _ in cast_tiles]
    out_refs = [(next(it), next(it)) for _ in segs]
    auxp_ref, auxs_ref = (next(it), next(it)) if has_aux else (None, None)
    cast_dsts = [next(it) for _ in cast_tiles]
    xnp_ref, xns_ref = (next(it), next(it)) if norm else (None, None)
    i = pl.program_id(0)
    j = pl.program_id(1)
    _run_casts(cast_srcs, cast_dsts, cast_tiles)

    if norm:
        @pl.when(j == 0)
        def _():
            for s, g_ref in enumerate(g_refs):
                xnp_ref[s] = _normed(xp_ref, g_ref)

        @pl.when((i == 0) & (j == 0))
        def _():
            for s, g_ref in enumerate(g_refs):
                xns_ref[s] = _normed(xs_ref, g_ref)

    def lhs_p(s):
        return xnp_ref[s] if norm else xp_ref[...]

    def lhs_s(s):
        return xns_ref[s] if norm else xs_ref[...].astype(BF16)

    def emit(lhs, w_ref, res_ref, o_ref, rope, lo, n_tiles, n_rope):
        acc = jnp.dot(lhs, w_ref[...].astype(BF16), preferred_element_type=F32)
        if has_res:
            acc = acc + res_ref[...]

        def plain():
            o_ref[...] = acc.astype(o_ref.dtype)

        def rotary():
            cos = rope[0][...]
            sin = rope[1][...]
            for h in range(acc.shape[1] // HEAD_DIM):
                a = acc[:, h * HEAD_DIM:(h + 1) * HEAD_DIM]
                o_ref[:, h * HEAD_DIM:(h + 1) * HEAD_DIM] = (
                    a * cos + pltpu.roll(a, HEAD_DIM // 2, 1) * sin).astype(o_ref.dtype)

        if n_rope == 0:
            plain()
        elif n_rope == n_tiles:
            rotary()
        else:
            pl.when(j < lo + n_rope)(rotary)
            pl.when(j >= lo + n_rope)(plain)

    for s, (lo, n_tiles, n_rope) in enumerate(segs):
        def segment(s=s, lo=lo, n_tiles=n_tiles, n_rope=n_rope):
            emit(lhs_p(s), w_refs[s], resp_ref, out_refs[s][0], rope_p, lo, n_tiles, n_rope)

            @pl.when(i == 0)
            def _():
                emit(lhs_s(s), w_refs[s], ress_ref, out_refs[s][1], rope_s, lo, n_tiles, n_rope)

        if len(segs) == 1:
            segment()
        else:
            pl.when((j >= lo) & (j < lo + n_tiles))(segment)

    @pl.when(i > 0)
    def _():
        for _, os_ref in out_refs:
            os_ref[...] = jnp.zeros_like(os_ref)

    if has_aux:
        @pl.when(j == pl.num_programs(1) - 1)
        def _():
            auxp_ref[...] = jnp.dot(lhs_p(0), auxw_ref[...], preferred_element_type=F32)

            @pl.when(i == 0)
            def _():
                auxs_ref[...] = jnp.dot(lhs_s(0), auxw_ref[...], preferred_element_type=F32)

            @pl.when(i > 0)
            def _():
                auxs_ref[...] = jnp.zeros_like(auxs_ref)


def _seg(w, layer=0, col0=0, n_cols=None, rope_cols=0):
    return (w, layer, col0, w.shape[-1] - col0 if n_cols is None else n_cols, rope_cols)


def _proj(xp, xs, ws, *, gains=None, aux_w=None, rope=None, res=None, casts=(), out_dtype=F32, tm=1024, tn=512):
    Mp, K = xp.shape
    S = SAMPLE_ROWS
    assert xs.shape == (S, K) and Mp % tm == 0
    norm = gains is not None
    segs, lo = [], 0
    for w, layer, col0, n_cols, rope_cols in ws:
        assert w.ndim == 3 and n_cols % tn == 0 and col0 % tn == 0 and rope_cols % tn == 0
        segs.append((lo, n_cols // tn, rope_cols // tn))
        lo += n_cols // tn
    nj, ni = lo, Mp // tm
    has_rope = any(r for _, _, r in segs)
    assert not (has_rope and res is not None) and (rope is not None) == has_rope
    row = lambda i, j: (i, 0)
    const = lambda i, j: (0, 0)
    args = [xp, xs]
    in_specs = [pl.BlockSpec((tm, K), row), pl.BlockSpec((S, K), const)]
    if norm:
        assert len(gains) == len(ws)
        args += [g.reshape(1, K) for g in gains]
        in_specs += [pl.BlockSpec((1, K), const)] * len(gains)
    for (w, layer, col0, _, _), (lo, n_tiles, _) in zip(ws, segs):
        args.append(w)
        in_specs.append(pl.BlockSpec(
            (None, K, tn), lambda i, j, layer=layer, lo=lo, n=n_tiles, c0=col0 // tn:
            (layer, 0, c0 + jnp.clip(j - lo, 0, n - 1))))
    if aux_w is not None:
        args.append(aux_w)
        in_specs.append(pl.BlockSpec(aux_w.shape, const))
    if has_rope:
        (cos_p, sin_p, rows_per_seq), (cos_s, sin_s) = rope
        nseq = rows_per_seq // tm
        args += [cos_p, sin_p, cos_s, sin_s]
        in_specs += [pl.BlockSpec((tm, HEAD_DIM), lambda i, j: (i % nseq, 0))] * 2
        in_specs += [pl.BlockSpec((S, HEAD_DIM), const)] * 2
    if res is not None:
        args += list(res)
        in_specs += [pl.BlockSpec((tm, tn), lambda i, j: (i, j)), pl.BlockSpec((S, tn), lambda i, j: (0, j))]
    assert res is None or len(ws) == 1
    out_shape, out_specs = [], []
    for lo, n_tiles, _ in segs:
        col = lambda i, j, lo=lo, n=n_tiles: (i, jnp.clip(j - lo, 0, n - 1))
        out_shape += [jax.ShapeDtypeStruct((Mp, n_tiles * tn), out_dtype),
                      jax.ShapeDtypeStruct((ni * S, n_tiles * tn), F32)]
        out_specs += [pl.BlockSpec((tm, tn), col), pl.BlockSpec((S, tn), col)]
    if aux_w is not None:
        na = aux_w.shape[1]
        out_shape += [jax.ShapeDtypeStruct((Mp, na), F32), jax.ShapeDtypeStruct((ni * S, na), F32)]
        out_specs += [pl.BlockSpec((tm, na), row), pl.BlockSpec((S, na), row)]
    n_main = len(out_shape)
    c_args, c_in, c_shape, c_out, cast_tiles = _cast_specs(casts, ni * nj, nj)
    args += c_args
    in_specs += c_in
    out_shape += c_shape
    out_specs += c_out
    outs = pl.pallas_call(
        functools.partial(_proj_kernel, segs=tuple(segs), norm=norm, has_rope=has_rope,
                          has_res=res is not None, has_aux=aux_w is not None, cast_tiles=cast_tiles),
        grid=(ni, nj),
        in_specs=in_specs,
        out_specs=out_specs,
        out_shape=out_shape,
        scratch_shapes=[pltpu.VMEM((len(ws), tm, K), BF16), pltpu.VMEM((len(ws), S, K), BF16)] if norm else [],
        compiler_params=_params("arbitrary", "arbitrary"),
        name="proj",
    )(*args)
    return [o[:S] if n < n_main and n % 2 == 1 else o for n, o in enumerate(outs)]


def _gateup_kernel(xp_ref, xs_ref, g_ref, wg_ref, wu_ref, *refs, cast_tiles):
    n = len(cast_tiles)
    cast_srcs = refs[:n]
    op_ref, os_ref = refs[n:n + 2]
    cast_dsts = refs[n + 2:2 * n + 2]
    xnp_ref, xns_ref = refs[2 * n + 2:]
    i = pl.program_id(0)
    j = pl.program_id(1)
    _run_casts(cast_srcs, cast_dsts, cast_tiles)

    @pl.when(j == 0)
    def _():
        xnp_ref[...] = _normed(xp_ref, g_ref)

    @pl.when((i == 0) & (j == 0))
    def _():
        xns_ref[...] = _normed(xs_ref, g_ref)

    def swiglu(xn):
        g = jnp.dot(xn, wg_ref[...], preferred_element_type=F32)
        u = jnp.dot(xn, wu_ref[...], preferred_element_type=F32)
        return _silu(g) * u

    op_ref[...] = swiglu(xnp_ref[...]).astype(op_ref.dtype)

    @pl.when(i == 0)
    def _():
        os_ref[...] = swiglu(xns_ref[...])

    @pl.when(i > 0)
    def _():
        os_ref[...] = jnp.zeros_like(os_ref)


def _gateup(xp, xs, gain, w_gu, layer, *, casts=(), tm=1024, tn=512):
    Mp, K = xp.shape
    S = SAMPLE_ROWS
    hidden = w_gu.shape[2] // 2
    assert xs.shape == (S, K) and Mp % tm == 0 and hidden % tn == 0 and w_gu.dtype == BF16
    nj = hidden // tn
    ni = Mp // tm
    c_args, c_in, c_shape, c_out, cast_tiles = _cast_specs(casts, ni * nj, nj)
    outs = pl.pallas_call(
        functools.partial(_gateup_kernel, cast_tiles=cast_tiles),
        grid=(ni, nj),
        in_specs=[pl.BlockSpec((tm, K), lambda i, j: (i, 0)),
                  pl.BlockSpec((S, K), lambda i, j: (0, 0)),
                  pl.BlockSpec((1, K), lambda i, j: (0, 0)),
                  pl.BlockSpec((None, K, tn), lambda i, j: (layer, 0, j)),
                  pl.BlockSpec((None, K, tn), lambda i, j: (layer, 0, j + nj))] + c_in,
        out_specs=[pl.BlockSpec((tm, tn), lambda i, j: (i, j)), pl.BlockSpec((S, tn), _sample_block)] + c_out,
        out_shape=[jax.ShapeDtypeStruct((Mp, hidden), BF16), jax.ShapeDtypeStruct((ni * S, hidden), F32)] + c_shape,
        scratch_shapes=[pltpu.VMEM((tm, K), BF16), pltpu.VMEM((S, K), BF16)],
        compiler_params=_params("arbitrary", "arbitrary"),
        name="gateup",
    )(xp, xs, gain.reshape(1, K), w_gu, w_gu, *c_args)
    return [outs[0], outs[1][:S]] + list(outs[2:])


def _conv_silu_slab(src, s, raw_ref, act_ref, cw_ref, cb_ref):
    Q = SSD_CHUNK
    raw_ref[s, CONV_TAIL:, :] = src
    for parity in (0, 1):
        acc = cb_ref[s]
        for k in range(CONV_W):
            first = CONV_TAIL - (CONV_W - 1) + k + parity
            acc = acc + raw_ref[s, pl.ds(first, Q // 2, stride=2), :] * cw_ref[s, k:k + 1, :]
        act_ref[s, pl.ds(parity, Q // 2, stride=2), :] = _silu(acc)
    raw_ref[s, 0:CONV_TAIL, :] = raw_ref[s, Q:Q + CONV_TAIL, :]


def _ssd_kernel(z_ref, x_ref, b_ref, c_ref, dt_ref, cw_ref, cb_ref,
                dtb_ref, alog_ref, dexp_ref, gn_ref,
                y_ref, hout_ref,
                h_ref, raw_ref, act_ref):
    Q = SSD_CHUNK
    P = SSM_HEAD_DIM
    N = SSM_STATE
    d_inner = x_ref.shape[1]
    n_heads = d_inner // P
    hpg = n_heads // SSM_GROUPS
    gw = hpg * P
    n_xs = d_inner // LANES
    n_bs = b_ref.shape[1] // LANES
    c = pl.program_id(1)

    @pl.when(c == 0)
    def _():
        h_ref[...] = jnp.zeros_like(h_ref)
        raw_ref[:, 0:CONV_TAIL, :] = jnp.zeros((raw_ref.shape[0], CONV_TAIL, LANES), F32)

    for s in range(n_xs + 2 * n_bs):
        if s < n_xs:
            src = x_ref[:, s * LANES:(s + 1) * LANES]
        elif s < n_xs + n_bs:
            src = b_ref[:, (s - n_xs) * LANES:(s - n_xs + 1) * LANES]
        else:
            src = c_ref[:, (s - n_xs - n_bs) * LANES:(s - n_xs - n_bs + 1) * LANES]
        _conv_silu_slab(src, s, raw_ref, act_ref, cw_ref, cb_ref)

    dt = _softplus(dt_ref[...] + dtb_ref[...])
    dA = dt * (-jnp.exp(alog_ref[...]))
    row = lax.broadcasted_iota(jnp.int32, (Q, Q), 0)
    col = lax.broadcasted_iota(jnp.int32, (Q, Q), 1)
    causal = row >= col
    tri = jnp.where(causal, 1.0, 0.0).astype(F32)
    cs = jnp.dot(tri, dA, precision=HIGHEST, preferred_element_type=F32)
    csT = cs.T
    dtT = dt.T
    dt_decay_end = dt * jnp.exp(cs[Q - 1:Q, :] - cs)
    chunk_decay = jnp.broadcast_to(jnp.exp(csT[0:n_heads, Q - 1:Q]), (n_heads, N))

    lane_lo = lax.broadcasted_iota(jnp.int32, (Q, 2 * P), 1) < P

    def pair_cols(arr, q):
        a0 = jnp.broadcast_to(arr[:, 2 * q:2 * q + 1], (Q, 2 * P))
        a1 = jnp.broadcast_to(arr[:, 2 * q + 1:2 * q + 2], (Q, 2 * P))
        return jnp.where(lane_lo, a0, a1)

    for g in range(SSM_GROUPS):
        Bg = act_ref[n_xs + g].astype(BF16)
        Cg = act_ref[n_xs + n_bs + g].astype(BF16)
        CB = lax.dot_general(Cg, Bg, NT_DIMS, preferred_element_type=F32)
        Hg = h_ref[g * hpg:(g + 1) * hpg].reshape(gw, N)
        y_off = lax.dot_general(Cg, Hg.astype(BF16), NT_DIMS, preferred_element_type=F32)

        ys = []
        xds = []
        for qq in range(hpg // 2):
            q = g * (hpg // 2) + qq
            lo, hi = q * 2 * P, (q + 1) * 2 * P
            xs = act_ref[q]
            xds.append((xs * pair_cols(dt_decay_end, q)).astype(BF16))
            cs_cols = [jnp.broadcast_to(cs[:, hh:hh + 1], (Q, Q)) for hh in (2 * q, 2 * q + 1)]
            atts = []
            for hh, cs_col in zip((2 * q, 2 * q + 1), cs_cols):
                seg = cs_col - csT[hh:hh + 1, :]
                decay = jnp.exp(jnp.where(causal, seg, -jnp.inf))
                atts.append(((CB * decay) * dtT[hh:hh + 1, :]).astype(BF16))
            att = jnp.concatenate(atts, axis=1)
            xbd = jnp.concatenate([jnp.where(lane_lo, xs, 0.0).astype(BF16),
                                   jnp.where(lane_lo, 0.0, xs).astype(BF16)], axis=0)
            y = jnp.dot(att, xbd, preferred_element_type=F32)
            y = y + y_off[:, qq * 2 * P:(qq + 1) * 2 * P] * jnp.exp(jnp.where(lane_lo, *cs_cols))
            y = y + xs * dexp_ref[:, lo:hi]
            ys.append(y * _silu(z_ref[:, lo:hi]))

        ssq = ys[0] * ys[0]
        for y in ys[1:]:
            ssq = ssq + y * y
        scale = lax.rsqrt(jnp.sum(ssq, axis=-1, keepdims=True) / gw + EPS)
        for qq, y in enumerate(ys):
            lo = g * gw + qq * 2 * P
            y_ref[:, lo:lo + 2 * P] = ((y * scale) * gn_ref[:, lo:lo + 2 * P]).astype(y_ref.dtype)

        S = lax.dot_general(jnp.concatenate(xds, axis=1), Bg, TN_DIMS, preferred_element_type=F32)
        for r in range(hpg):
            hh = g * hpg + r
            h_ref[hh] = h_ref[hh] * chunk_decay[hh:hh + 1, :] + S[r * P:(r + 1) * P, :]

    @pl.when(c == pl.num_programs(1) - 1)
    def _():
        hout_ref[0] = h_ref[...]


def _ssd_prompt(zxbc, dt, batch, conv_w, conv_b, dt_bias, a_log, d_skip, gate_norm):
    M = zxbc.shape[0]
    L = M // batch
    H = dt_bias.shape[0]
    d_inner = H * SSM_HEAD_DIM
    bc = SSM_GROUPS * SSM_STATE
    conv_dim = d_inner + 2 * bc
    Q = SSD_CHUNK
    nc = L // Q
    assert L % Q == 0 and d_inner % bc == 0
    xb = d_inner // bc
    row_map = lambda col: (lambda b, c: (b * nc + c, col))
    const = lambda b, c: (0, 0)
    d_exp = jnp.repeat(d_skip.astype(F32), SSM_HEAD_DIM).reshape(1, d_inner)
    assert H <= LANES and SSM_STATE == LANES and 2 * SSM_HEAD_DIM == LANES
    lane_pad = lambda v: jnp.pad(v.reshape(1, H), ((0, 0), (0, LANES - H)))
    n_slabs = conv_dim // LANES
    cw_slabs = conv_w.reshape(CONV_W, n_slabs, LANES).transpose(1, 0, 2)
    cb_slabs = conv_b.reshape(n_slabs, 1, LANES)
    const3 = lambda b, c: (0, 0, 0)
    y, h_final = pl.pallas_call(
        _ssd_kernel,
        grid=(batch, nc),
        in_specs=[pl.BlockSpec((Q, d_inner), row_map(0)),
                  pl.BlockSpec((Q, d_inner), row_map(1)),
                  pl.BlockSpec((Q, bc), row_map(2 * xb)),
                  pl.BlockSpec((Q, bc), row_map(2 * xb + 1)),
                  pl.BlockSpec((Q, LANES), row_map(0)),
                  pl.BlockSpec((n_slabs, CONV_W, LANES), const3),
                  pl.BlockSpec((n_slabs, 1, LANES), const3),
                  pl.BlockSpec((1, LANES), const),
                  pl.BlockSpec((1, LANES), const),
                  pl.BlockSpec((1, d_inner), const),
                  pl.BlockSpec((1, d_inner), const)],
        out_specs=[pl.BlockSpec((Q, d_inner), row_map(0)),
                   pl.BlockSpec((1, H, SSM_HEAD_DIM, SSM_STATE), lambda b, c: (b, 0, 0, 0))],
        out_shape=[jax.ShapeDtypeStruct((M, d_inner), BF16),
                   jax.ShapeDtypeStruct((batch, H, SSM_HEAD_DIM, SSM_STATE), F32)],
        scratch_shapes=[pltpu.VMEM((H, SSM_HEAD_DIM, SSM_STATE), F32),
                        pltpu.VMEM((n_slabs, Q + CONV_TAIL, LANES), F32),
                        pltpu.VMEM((n_slabs, Q, LANES), F32)],
        compiler_params=_params("parallel", "arbitrary"),
        name="ssd_prompt",
    )(zxbc, zxbc, zxbc, zxbc, dt, cw_slabs, cb_slabs,
      lane_pad(dt_bias), lane_pad(a_log), d_exp, gate_norm.reshape(1, d_inner))
    return y, h_final


def _ssd_step_kernel(zxbc_ref, dt_ref, cs_ref, h0_ref, cw_ref, cb_ref, dtb_ref, alog_ref,
                     dexp_ref, gn_ref, y_ref, cso_ref, ho_ref):
    P = SSM_HEAD_DIM
    N = SSM_STATE
    n_heads = h0_ref.shape[2]
    d_inner = n_heads * P
    hpg = n_heads // SSM_GROUPS
    gw = hpg * P
    conv_dim = cw_ref.shape[1]

    raw = zxbc_ref[0, :, d_inner:d_inner + conv_dim]
    prev = cs_ref[0, 0]
    acc = cb_ref[...]
    for k in range(CONV_W - 1):
        acc = acc + prev[k:k + 1, :] * cw_ref[k:k + 1, :]
    acc = acc + raw * cw_ref[CONV_W - 1:CONV_W, :]
    xbc = _silu(acc)
    cso_ref[0, 0, 0:CONV_W - 2, :] = prev[1:CONV_W - 1, :]
    cso_ref[0, 0, CONV_W - 2:CONV_W - 1, :] = raw

    dt = _softplus(dt_ref[0, :, 0:n_heads] + dtb_ref[...])
    decay = jnp.exp(dt * (-jnp.exp(alog_ref[...])))
    z = zxbc_ref[0, :, 0:d_inner]

    eye = (lax.broadcasted_iota(jnp.int32, (P, P), 0) == lax.broadcasted_iota(jnp.int32, (P, P), 1))
    ys = []
    for g in range(SSM_GROUPS):
        Bg = xbc[:, d_inner + g * N:d_inner + (g + 1) * N]
        Cg = xbc[:, d_inner + SSM_GROUPS * N + g * N:d_inner + SSM_GROUPS * N + (g + 1) * N]
        Bb = jnp.broadcast_to(Bg, (P, N))
        for r in range(hpg):
            hh = g * hpg + r
            xs = xbc[:, hh * P:(hh + 1) * P]
            xdt = xs * dt[:, hh:hh + 1]
            xdiag = jnp.where(eye, jnp.broadcast_to(xdt, (P, P)), 0.0)
            outer = jnp.dot(xdiag, Bb, precision=HIGHEST, preferred_element_type=F32)
            ho_ref[0, 0, hh] = h0_ref[0, 0, hh] * decay[:, hh:hh + 1] + outer
        Hg = ho_ref[0, 0, g * hpg:(g + 1) * hpg].reshape(gw, N)
        yg = lax.dot_general(Cg, Hg, NT_DIMS, precision=HIGHEST, preferred_element_type=F32)
        lo, hi = g * gw, (g + 1) * gw
        yg = yg + xbc[:, lo:hi] * dexp_ref[:, lo:hi]
        yg = yg * _silu(z[:, lo:hi])
        yg = yg * lax.rsqrt(jnp.mean(yg * yg, axis=-1, keepdims=True) + EPS)
        ys.append(yg * gn_ref[:, lo:hi])
    y_ref[0] = jnp.concatenate(ys, axis=1).astype(y_ref.dtype)


def _ssd_step(zxbc, dt, conv_state, ssm_state, conv_w, conv_b, dt_bias, a_log, d_skip, gate_norm):
    B = zxbc.shape[0]
    H = dt_bias.shape[0]
    d_inner = H * SSM_HEAD_DIM
    conv_dim = conv_w.shape[1]
    const = lambda b: (0, 0)
    d_exp = jnp.repeat(d_skip.astype(F32), SSM_HEAD_DIM).reshape(1, d_inner)
    cs4 = conv_state.reshape(1, B, CONV_W - 1, conv_dim)
    h5 = ssm_state.reshape(1, B, H, SSM_HEAD_DIM, SSM_STATE)
    return pl.pallas_call(
        _ssd_step_kernel,
        grid=(B,),
        in_specs=[pl.BlockSpec((1, 1, zxbc.shape[1]), lambda b: (b, 0, 0)),
                  pl.BlockSpec((1, 1, dt.shape[1]), lambda b: (b, 0, 0)),
                  pl.BlockSpec((1, 1, CONV_W - 1, conv_dim), lambda b: (0, b, 0, 0)),
                  pl.BlockSpec((1, 1, H, SSM_HEAD_DIM, SSM_STATE), lambda b: (0, b, 0, 0, 0)),
                  pl.BlockSpec((CONV_W, conv_dim), const),
                  pl.BlockSpec((1, conv_dim), const),
                  pl.BlockSpec((1, H), const),
                  pl.BlockSpec((1, H), const),
                  pl.BlockSpec((1, d_inner), const),
                  pl.BlockSpec((1, d_inner), const)],
        out_specs=[pl.BlockSpec((1, 1, d_inner), lambda b: (b, 0, 0)),
                   pl.BlockSpec((1, 1, CONV_W - 1, conv_dim), lambda b: (0, b, 0, 0)),
                   pl.BlockSpec((1, 1, H, SSM_HEAD_DIM, SSM_STATE), lambda b: (0, b, 0, 0, 0))],
        out_shape=[jax.ShapeDtypeStruct((B, 1, d_inner), F32),
                   jax.ShapeDtypeStruct(cs4.shape, F32),
                   jax.ShapeDtypeStruct(h5.shape, F32)],
        compiler_params=_params("arbitrary"),
        name="ssd_step",
    )(zxbc.reshape(B, 1, -1), dt.reshape(B, 1, -1), cs4, h5, conv_w, conv_b.reshape(1, conv_dim),
      dt_bias.reshape(1, H), a_log.reshape(1, H), d_exp, gate_norm.reshape(1, d_inner))


def _attn_kernel(q0_ref, q1_ref, q2_ref, k_ref, v_ref, o_ref, on_ref, lse_ref, qs_ref, ks_ref, vs_ref):
    L = k_ref.shape[0]
    QB = ATT_BLOCK
    KW = QB + DIL_SLOTS
    P = ATT_PRESTRIDE
    p_shift = P.bit_length() - 1
    scale = HEAD_DIM ** -0.5
    e = lax.broadcasted_iota(jnp.int32, (QB, KW), 0) - lax.broadcasted_iota(jnp.int32, (QB, KW), 1)

    staged = {rate: rate > P for rate in DIL_RATES}
    for src_ref, dst_ref, needed in ((q2_ref, qs_ref, staged[DIL_RATES[2]]), (k_ref, ks_ref, any(staged.values())),
                                     (v_ref, vs_ref, any(staged.values()))):
        if needed:
            def stage(i, carry, src_ref=src_ref, dst_ref=dst_ref):
                cp = i & (P - 1)
                t = i >> p_shift
                dst = pl.multiple_of(cp * (L // P) + t * QB, QB)
                dst_ref[pl.ds(dst, QB), :] = src_ref[pl.ds(cp + P * QB * t, QB, stride=P), :]
                return carry
            lax.fori_loop(0, L // QB, stage, 0, unroll=4)

    for g, (q_ref, rate) in enumerate(zip((q0_ref, q1_ref, q2_ref), DIL_RATES)):
        shift = rate.bit_length() - 1

        def unit(n, carry, g=g, q_ref=q_ref, rate=rate, shift=shift):
            c = n & (rate - 1)
            u0 = (n >> shift) * QB
            v0 = jnp.maximum(u0 - DIL_SLOTS, 0)
            q_rows = pl.ds(c + rate * u0, QB, stride=rate)
            if staged[rate]:
                base = (c & (P - 1)) * (L // P) + (c >> p_shift)
                q = qs_ref[pl.ds(base + (rate // P) * u0, QB, stride=rate // P), :].astype(BF16)
                k_rows = pl.ds(base + (rate // P) * v0, KW, stride=rate // P)
                k = ks_ref[k_rows, :].astype(BF16)
                v = vs_ref[k_rows, :].astype(BF16)
            else:
                k_rows = pl.ds(c + rate * v0, KW, stride=rate)
                q = q_ref[q_rows, :].astype(BF16)
                k = k_ref[k_rows, :].astype(BF16)
                v = v_ref[k_rows, :].astype(BF16)
            s = lax.dot_general(q, k, NT_DIMS, preferred_element_type=F32) * scale
            d = e + (u0 - v0)
            s = jnp.where(d >= 0, s, -jnp.inf)
            s = jnp.where(d <= DIL_SLOTS, s, -jnp.inf)
            m = s.max(axis=1, keepdims=True)
            p = jnp.exp(s - m)
            l = p.sum(axis=1, keepdims=True)
            o = jnp.dot(p.astype(BF16), v, preferred_element_type=F32)
            on_ref[g, q_rows, :] = o / l
            lse_ref[g, q_rows, :] = jnp.broadcast_to(m + jnp.log(l), (QB, HEAD_DIM))
            return carry

        lax.fori_loop(0, L // QB, unit, 0, unroll=ATT_UNROLL)

    def mix(i, carry):
        rows = pl.ds(pl.multiple_of(i * QB, QB), QB)
        lses = [lse_ref[g, rows, :] for g in range(len(DIL_RATES))]
        m = functools.reduce(jnp.maximum, lses)
        ws = [jnp.exp(x - m) for x in lses]
        num = functools.reduce(jnp.add, [w * on_ref[g, rows, :] for g, w in enumerate(ws)])
        o_ref[rows, :] = (num / functools.reduce(jnp.add, ws)).astype(o_ref.dtype)
        return carry

    lax.fori_loop(0, L // QB, mix, 0, unroll=2)


def _attn_prompt(q, kv, batch):
    M = q.shape[0]
    L = M // batch
    n_grp = len(DIL_RATES)
    assert all(r & (r - 1) == 0 and L % (ATT_BLOCK * r) == 0 and L >= r * (ATT_BLOCK + DIL_SLOTS)
               for r in DIL_RATES)
    qspec = lambda g: pl.BlockSpec((L, HEAD_DIM), lambda b, h: (b, g * KV_HEADS + h))
    kspec = pl.BlockSpec((L, HEAD_DIM), lambda b, h: (b, h))
    vspec = pl.BlockSpec((L, HEAD_DIM), lambda b, h: (b, KV_HEADS + h))
    return pl.pallas_call(
        _attn_kernel,
        grid=(batch, KV_HEADS),
        in_specs=[qspec(0), qspec(1), qspec(2), kspec, vspec],
        out_specs=pl.BlockSpec((L, HEAD_DIM), lambda b, h: (b, h)),
        out_shape=jax.ShapeDtypeStruct((M, KV_HEADS * HEAD_DIM), BF16),
        scratch_shapes=[pltpu.VMEM((n_grp, L, HEAD_DIM), F32), pltpu.VMEM((n_grp, L, HEAD_DIM), F32)]
        + [pltpu.VMEM((L, HEAD_DIM), F32)] * 3,
        compiler_params=_params("parallel", "arbitrary"),
        name="attn_prompt",
    )(q, q, q, kv, kv)


def _bf16_round(a):
    return a.astype(BF16).astype(F32)


def _attn_step_kernel(q_ref, *refs):
    n = len(DIL_RATES)
    k_refs, v_refs = refs[0:n], refs[n:2 * n]
    kn_ref, vn_ref, o_ref = refs[2 * n:]
    scale = HEAD_DIM ** -0.5
    kn = _bf16_round(kn_ref[0, 0])
    vn = _bf16_round(vn_ref[0, 0])
    scores, new_scores = [], []
    for g, k_ref in enumerate(k_refs):
        qg = _bf16_round(q_ref[0, g])
        kg = _bf16_round(k_ref[0, :, 0])
        scores.append(jnp.sum(kg * qg[None], axis=-1, keepdims=True) * scale)
        new_scores.append(jnp.sum(kn * qg, axis=-1, keepdims=True) * scale)
    m = functools.reduce(jnp.maximum, [s.max(axis=0) for s in scores] + new_scores)
    l = jnp.zeros_like(m)
    o = jnp.zeros((KV_HEADS, HEAD_DIM), F32)
    for s, s_new, v_ref in zip(scores, new_scores, v_refs):
        p = jnp.exp(s - m[None])
        p_new = jnp.exp(s_new - m)
        l = l + p.sum(axis=0) + p_new
        o = o + (_bf16_round(p) * _bf16_round(v_ref[0, :, 0])).sum(axis=0) + _bf16_round(p_new) * vn
    o_ref[0, 0] = (o / l).astype(o_ref.dtype)


def _attn_step(q, cache_k, cache_v, k_new, v_new):
    B, T = cache_k.shape[0], cache_k.shape[1]
    S = DIL_SLOTS
    assert T == S * max(DIL_RATES), "every slot of every dilation group lies inside the cached window"
    qg = q.reshape(B, len(DIL_RATES), KV_HEADS, HEAD_DIM)
    row_shape = (B, 1, KV_HEADS, HEAD_DIM)
    row_spec = pl.BlockSpec((1, 1, KV_HEADS, HEAD_DIM), lambda b: (b, 0, 0, 0))
    args = [qg]
    in_specs = [pl.BlockSpec((1, len(DIL_RATES), KV_HEADS, HEAD_DIM), lambda b: (b, 0, 0, 0))]
    for cache in (cache_k, cache_v):
        for rate in DIL_RATES:
            args.append(cache.reshape(B, T // rate, rate, KV_HEADS, HEAD_DIM))
            in_specs.append(pl.BlockSpec((1, S, 1, KV_HEADS, HEAD_DIM),
                                         lambda b, blk=T // rate // S - 1: (b, blk, 0, 0, 0)))
    args += [k_new.reshape(row_shape), v_new.reshape(row_shape)]
    in_specs += [row_spec, row_spec]
    o = pl.pallas_call(
        _attn_step_kernel,
        grid=(B,),
        in_specs=in_specs,
        out_specs=row_spec,
        out_shape=jax.ShapeDtypeStruct(row_shape, F32),
        compiler_params=_params("parallel"),
        name="attn_step",
    )(*args)
    return o.reshape(B, KV_HEADS * HEAD_DIM)


def _rmsnorm_kernel(x_ref, g_ref, o_ref):
    x = x_ref[...]
    o_ref[...] = (x * _rms_scale(x)) * g_ref[...]


def _rmsnorm(x, gain, *, tm=512):
    M, K = x.shape
    tm = min(tm, M)
    assert M % tm == 0
    return pl.pallas_call(
        _rmsnorm_kernel,
        grid=(M // tm,),
        in_specs=[pl.BlockSpec((tm, K), lambda i: (i, 0)), pl.BlockSpec((1, K), lambda i: (0, 0))],
        out_specs=pl.BlockSpec((tm, K), lambda i: (i, 0)),
        out_shape=jax.ShapeDtypeStruct((M, K), F32),
        compiler_params=_params("parallel"),
        name="final_norm",
    )(x, gain.reshape(1, K))


def _rope_tables(pos):
    half = HEAD_DIM // 2
    inv = jnp.power(jnp.float32(ROPE_THETA), -jnp.arange(half, dtype=jnp.float32) / half)
    ang = pos.astype(jnp.float32)[:, None] * inv[None, :]
    cos = jnp.cos(ang)
    sin = jnp.sin(ang)
    return jnp.concatenate([cos, cos], axis=1), jnp.concatenate([-sin, sin], axis=1)


def kernel(x_prompt, x_sample, state_conv, state_ssm, cache_k, cache_v, a_norm, a_w_in, a_conv_w, a_conv_b,
           a_dt_bias, a_log, a_d, a_gate_norm, a_w_out, kv_norm, w_kv, b_norm, b_w_q, b_w_o, ffn_norm,
           ffn_w_gu, ffn_w_down, final_norm):
    assert a_norm.shape[0] == 1 and b_norm.shape[0] == 1, "one Mamba-2 layer followed by one attention layer"
    Bp, Lp, D = x_prompt.shape
    Bs, Ls, _ = x_sample.shape
    assert Ls == 1 and Bs == SAMPLE_ROWS, "sample group decodes one token for SAMPLE_ROWS sequences"
    n_heads = a_dt_bias.shape[1]
    d_inner = a_w_out.shape[1]
    conv_dim = a_conv_w.shape[2]
    kv_dim = KV_HEADS * HEAD_DIM
    w_dt = jnp.pad(a_w_in[0][:, d_inner + conv_dim:], ((0, 0), (0, LANES - n_heads))).astype(BF16)
    ssd_w = (a_conv_w[0], a_conv_b[0], a_dt_bias[0], a_log[0], a_d[0], a_gate_norm[0])
    w_kv = w_kv[None]

    cos_p, sin_p = _rope_tables(jnp.arange(Lp, dtype=jnp.int32))
    cos_s, sin_s = _rope_tables(PAST_LEN + jnp.arange(Ls, dtype=jnp.int32))
    rope = ((cos_p, sin_p, Lp), (jnp.broadcast_to(cos_s, (Bs, HEAD_DIM)), jnp.broadcast_to(sin_s, (Bs, HEAD_DIM))))

    xp0 = x_prompt.reshape(Bp * Lp, D)
    xs0 = x_sample.reshape(Bs, D)

    zxbc_p, zxbc_s, dt_p, dt_s, w_out_bf, w_gu0_bf, w_down0_bf = _proj(
        xp0, xs0, [_seg(a_w_in.astype(BF16), n_cols=d_inner + conv_dim)], gains=[a_norm[0]], aux_w=w_dt, tn=1024,
        casts=[(a_w_out, 0), (ffn_w_gu, 0), (ffn_w_down, 0)])
    y_p, p_ssm = _ssd_prompt(zxbc_p, dt_p, Bp, *ssd_w)
    p_conv = zxbc_p.reshape(Bp, Lp, -1)[:, Lp - (CONV_W - 1):, d_inner:]
    y_s, s_conv, s_ssm = _ssd_step(zxbc_s, dt_s, state_conv[0], state_ssm[0], *ssd_w)
    *x1, w_gu1_bf = _proj(y_p, y_s.reshape(Bs, d_inner), [_seg(w_out_bf)], res=(xp0, xs0), casts=[(ffn_w_gu, 1)])
    *h, w_kv_bf, w_q_bf, w_o_bf = _gateup(*x1, ffn_norm[0], w_gu0_bf, 0, casts=[(w_kv, 0), (b_w_q, 0), (b_w_o, 0)])
    x2 = _proj(*h, [_seg(w_down0_bf)], res=x1)

    kv_p, kv_s, q_p, q_s = _proj(*x2, [_seg(w_kv_bf, rope_cols=kv_dim), _seg(w_q_bf, rope_cols=b_w_q.shape[2])],
                                 gains=[kv_norm, b_norm[0]], rope=rope)
    o_p = _attn_prompt(q_p, kv_p, Bp)
    o_s = _attn_step(q_s, cache_k, cache_v, kv_s[:, :kv_dim], kv_s[:, kv_dim:])
    x3 = _proj(o_p, o_s, [_seg(w_o_bf)], res=x2, tn=1024)
    *h, w_down1_bf = _gateup(*x3, ffn_norm[1], w_gu1_bf, 0, casts=[(ffn_w_down, 1)])
    x4 = _proj(*h, [_seg(w_down1_bf)], res=x3)

    y_prompt = _rmsnorm(x4[0], final_norm).reshape(Bp, Lp, D)
    y_sample = _rmsnorm(x4[1], final_norm).reshape(Bs, Ls, D)
    keep = min(DIL_SLOTS * max(DIL_RATES), Lp)
    p_kv = kv_p.reshape(Bp, Lp, 2 * KV_HEADS, HEAD_DIM)[:, Lp - keep:]
    s_kv = kv_s.reshape(Bs, Ls, 2 * KV_HEADS, HEAD_DIM)
    return (y_prompt, y_sample, p_conv[None], p_ssm[None], p_kv[:, :, :KV_HEADS], p_kv[:, :, KV_HEADS:],
            s_conv, s_ssm, s_kv[:, :, :KV_HEADS], s_kv[:, :, KV_HEADS:])
```

```python
import functools

import jax
import jax.numpy as jnp
from jax import lax
from jax.experimental import pallas as pl
from jax.experimental.pallas import tpu as pltpu

F32 = jnp.float32
BF16 = jnp.bfloat16
HIGHEST = lax.Precision.HIGHEST

EPS = 1e-6
ROPE_THETA = 10000.0
SSD_CHUNK = 128
SSM_HEAD_DIM = 64
SSM_STATE = 128
SSM_GROUPS = 8
CONV_W = 4
HEAD_DIM = 128
KV_HEADS = 8
DIL_RATES = (1, 4, 16)
DIL_SLOTS = 128
ATT_BLOCK = 128
ATT_UNROLL = 8
ATT_PRESTRIDE = 4
PAST_LEN = 16384

V7X_VMEM_BYTES = 64 * 1024 * 1024
VMEM_LIMIT = V7X_VMEM_BYTES - 8 * 1024 * 1024
LANES = 128
CONV_TAIL = 8
SAMPLE_ROWS = 8

NT_DIMS = (((1,), (1,)), ((), ()))
TN_DIMS = (((0,), (0,)), ((), ()))


def _params(*sem):
    return pltpu.CompilerParams(dimension_semantics=sem, vmem_limit_bytes=VMEM_LIMIT)


def _silu(x):
    h = 0.5 * x
    return h + h * jnp.tanh(h)


def _softplus(x):
    return jnp.maximum(x, 0.0) + jnp.log1p(jnp.exp(-jnp.abs(x)))


def _rms_scale(x):
    return lax.rsqrt(jnp.mean(x * x, axis=-1, keepdims=True) + EPS)


def _normed(x_ref, g_ref):
    x = x_ref[...]
    return ((x * _rms_scale(x)) * g_ref[...]).astype(BF16)


def _sample_block(i, j):
    return (i, j)


def _cast_specs(casts, n_steps, nj):
    BF16_ROWS = 16
    args, in_specs, out_shape, out_specs, tiles = [], [], [], [], []
    for src, layer in casts:
        R, C = src.shape[1:]
        rows = next(r for r in range(BF16_ROWS, R + 1, BF16_ROWS) if R % r == 0 and R // r <= n_steps)
        nt = R // rows
        tile = lambda i, j, nt=nt: jnp.minimum(i * nj + j, nt - 1)
        args.append(src)
        in_specs.append(pl.BlockSpec((None, rows, C), lambda i, j, layer=layer, tile=tile: (layer, tile(i, j), 0)))
        out_shape.append(jax.ShapeDtypeStruct((1, R, C), BF16))
        out_specs.append(pl.BlockSpec((None, rows, C), lambda i, j, tile=tile: (0, tile(i, j), 0)))
        tiles.append(nt)
    return args, in_specs, out_shape, out_specs, tuple(tiles)


def _run_casts(src_refs, dst_refs, tiles):
    step = pl.program_id(0) * pl.num_programs(1) + pl.program_id(1)
    for src_ref, dst_ref, nt in zip(src_refs, dst_refs, tiles):
        @pl.when(step < nt)
        def _(src_ref=src_ref, dst_ref=dst_ref):
            dst_ref[...] = src_ref[...].astype(BF16)


def _proj_kernel(*refs, segs, norm, has_rope, has_res, has_post, has_aux, cast_tiles):
    it = iter(refs)
    xp_ref, xs_ref = next(it), next(it)
    g_refs = [next(it) for _ in segs] if norm else None
    w_refs = [next(it) for _ in segs]
    auxw_ref = next(it) if has_aux else None
    rope_p = (next(it), next(it)) if has_rope else None
    rope_s = (next(it), next(it)) if has_rope else None
    resp_ref, ress_ref = (next(it), next(it)) if has_res else (None, None)
    pg_ref = next(it) if has_post else None
    cast_srcs = [next(it) for _ in cast_tiles]
    out_refs =[(next(it), next(it)) for _ in segs]
    auxp_ref, auxs_ref = (next(it), next(it)) if has_aux else (None, None)
    cast_dsts = [next(it) for _ in cast_tiles]
    xnp_ref, xns_ref = (next(it), next(it)) if norm else (None, None)
    i = pl.program_id(0)
    j = pl.program_id(1)
    _run_casts(cast_srcs, cast_dsts, cast_tiles)

    if norm:
        @pl.when(j == 0)
        def _():
            for s, g_ref in enumerate(g_refs):
                xnp_ref[s] = _normed(xp_ref, g_ref)

        @pl.when((i == 0) & (j == 0))
        def _():
            for s, g_ref in enumerate(g_refs):
                xns_ref[s] = _normed(xs_ref, g_ref)

    def lhs_p(s):
        return xnp_ref[s] if norm else xp_ref[...]

    def lhs_s(s):
        return xns_ref[s] if norm else xs_ref[...].astype(BF16)

    def emit(lhs, w_ref, res_ref, o_ref, rope, lo, n_tiles, n_rope):
        acc = jnp.dot(lhs, w_ref[...].astype(BF16), preferred_element_type=F32)
        if has_res:
            acc = acc + res_ref[...]
        if has_post:
            acc = (acc * _rms_scale(acc)) * pg_ref[...]

        def plain():
            o_ref[...] = acc.astype(o_ref.dtype)

        def rotary():
            cos = rope[0][...]
            sin = rope[1][...]
            for h in range(acc.shape[1] // HEAD_DIM):
                a = acc[:, h * HEAD_DIM:(h + 1) * HEAD_DIM]
                o_ref[:, h * HEAD_DIM:(h + 1) * HEAD_DIM] = (
                    a * cos + pltpu.roll(a, HEAD_DIM // 2, 1) * sin).astype(o_ref.dtype)

        if n_rope == 0:
            plain()
        elif n_rope == n_tiles:
            rotary()
        else:
            pl.when(j < lo + n_rope)(rotary)
            pl.when(j >= lo + n_rope)(plain)

    for s, (lo, n_tiles, n_rope) in enumerate(segs):
        def segment(s=s, lo=lo, n_tiles=n_tiles, n_rope=n_rope):
            emit(lhs_p(s), w_refs[s], resp_ref, out_refs[s][0], rope_p, lo, n_tiles, n_rope)

            @pl.when(i == 0)
            def _():
                emit(lhs_s(s), w_refs[s], ress_ref, out_refs[s][1], rope_s, lo, n_tiles, n_rope)

        if len(segs) == 1:
            segment()
        else:
            pl.when((j >= lo) & (j < lo + n_tiles))(segment)

    @pl.when(i > 0)
    def _():
        for _, os_ref in out_refs:
            os_ref[...] = jnp.zeros_like(os_ref)

    if has_aux:
        @pl.when(j == pl.num_programs(1) - 1)
        def _():
            auxp_ref[...] = jnp.dot(lhs_p(0), auxw_ref[...], preferred_element_type=F32)

            @pl.when(i == 0)
            def _():
                auxs_ref[...] = jnp.dot(lhs_s(0), auxw_ref[...], preferred_element_type=F32)

            @pl.when(i > 0)
            def _():
                auxs_ref[...] = jnp.zeros_like(auxs_ref)


def _seg(w, layer=0, col0=0, n_cols=None, rope_cols=0):
    return (w, layer, col0, w.shape[-1] - col0 if n_cols is None else n_cols, rope_cols)


def _proj(xp, xs, ws, *, gains=None, aux_w=None, rope=None, res=None, post_gain=None, casts=(), out_dtype=F32,
          tm=1024, tn=512):
    Mp, K = xp.shape
    S = SAMPLE_ROWS
    assert xs.shape == (S, K) and Mp % tm == 0
    norm = gains is not None
    segs, lo = [], 0
    for w, layer, col0, n_cols, rope_cols in ws:
        assert w.ndim == 3 and n_cols % tn == 0 and col0 % tn == 0 and rope_cols % tn == 0
        segs.append((lo, n_cols // tn, rope_cols // tn))
        lo += n_cols // tn
    nj, ni = lo, Mp // tm
    has_rope = any(r for _, _, r in segs)
    assert not (has_rope and res is not None) and (rope is not None) == has_rope
    row = lambda i, j: (i, 0)
    const = lambda i, j: (0, 0)
    args = [xp, xs]
    in_specs = [pl.BlockSpec((tm, K), row), pl.BlockSpec((S, K), const)]
    if norm:
        assert len(gains) == len(ws)
        args += [g.reshape(1, K) for g in gains]
        in_specs += [pl.BlockSpec((1, K), const)] * len(gains)
    for (w, layer, col0, _, _), (lo, n_tiles, _) in zip(ws, segs):
        args.append(w)
        in_specs.append(pl.BlockSpec(
            (None, K, tn), lambda i, j, layer=layer, lo=lo, n=n_tiles, c0=col0 // tn:
            (layer, 0, c0 + jnp.clip(j - lo, 0, n - 1)),
            pipeline_mode=pl.Buffered(1) if nj == 1 else None))
    if aux_w is not None:
        args.append(aux_w)
        in_specs.append(pl.BlockSpec(aux_w.shape, const))
    if has_rope:
        (cos_p, sin_p, rows_per_seq), (cos_s, sin_s) = rope
        nseq = rows_per_seq // tm
        args += [cos_p, sin_p, cos_s, sin_s]
        in_specs += [pl.BlockSpec((tm, HEAD_DIM), lambda i, j: (i % nseq, 0))] * 2
        in_specs += [pl.BlockSpec((S, HEAD_DIM), const)] * 2
    if res is not None:
        args += list(res)
        in_specs += [pl.BlockSpec((tm, tn), lambda i, j: (i, j)), pl.BlockSpec((S, tn), lambda i, j: (0, j))]
    if post_gain is not None:
        assert nj == 1
        args.append(post_gain.reshape(1, tn))
        in_specs.append(pl.BlockSpec((1, tn), const))
    assert res is None or len(ws) == 1
    out_shape, out_specs = [], []
    for lo, n_tiles, _ in segs:
        col = lambda i, j, lo=lo, n=n_tiles: (i, jnp.clip(j - lo, 0, n - 1))
        out_shape += [jax.ShapeDtypeStruct((Mp, n_tiles * tn), out_dtype),
                      jax.ShapeDtypeStruct((ni * S, n_tiles * tn), F32)]
        out_specs += [pl.BlockSpec((tm, tn), col), pl.BlockSpec((S, tn), col)]
    if aux_w is not None:
        na = aux_w.shape[1]
        out_shape += [jax.ShapeDtypeStruct((Mp, na), F32), jax.ShapeDtypeStruct((ni * S, na), F32)]
        out_specs += [pl.BlockSpec((tm, na), row), pl.BlockSpec((S, na), row)]
    n_main = len(out_shape)
    c_args, c_in, c_shape, c_out, cast_tiles = _cast_specs(casts, ni * nj, nj)
    args += c_args
    in_specs += c_in
    out_shape += c_shape
    out_specs += c_out
    outs = pl.pallas_call(
        functools.partial(_proj_kernel, segs=tuple(segs), norm=norm, has_rope=has_rope,
                          has_res=res is not None, has_post=post_gain is not None, has_aux=aux_w is not None,
                          cast_tiles=cast_tiles),
        grid=(ni, nj),
        in_specs=in_specs,
        out_specs=out_specs,
        out_shape=out_shape,
        scratch_shapes=[pltpu.VMEM((len(ws), tm, K), BF16), pltpu.VMEM((len(ws), S, K), BF16)] if norm else [],
        compiler_params=_params("arbitrary", "arbitrary"),
        name="proj",
    )(*args)
    return [o[:S] if n < n_main and n % 2 == 1 else o for n, o in enumerate(outs)]


def _gateup_kernel(xp_ref, xs_ref, g_ref, wg_ref, wu_ref, *refs, cast_tiles):
    n = len(cast_tiles)
    cast_srcs = refs[:n]
    op_ref, os_ref = refs[n:n + 2]
    cast_dsts = refs[n + 2:2 * n + 2]
    xnp_ref, xns_ref = refs[2 * n + 2:]
    i = pl.program_id(0)
    j = pl.program_id(1)
    _run_casts(cast_srcs, cast_dsts, cast_tiles)

    @pl.when(j == 0)
    def _():
        xnp_ref[...] = _normed(xp_ref, g_ref)

    @pl.when((i == 0) & (j == 0))
    def _():
        xns_ref[...] = _normed(xs_ref, g_ref)

    def swiglu(xn):
        g = jnp.dot(xn, wg_ref[...], preferred_element_type=F32)
        u = jnp.dot(xn, wu_ref[...], preferred_element_type=F32)
        return _silu(g) * u

    op_ref[...] = swiglu(xnp_ref[...]).astype(op_ref.dtype)

    @pl.when(i == 0)
    def _():
        os_ref[...] = swiglu(xns_ref[...])

    @pl.when(i > 0)
    def _():
        os_ref[...] = jnp.zeros_like(os_ref)


def _gateup(xp, xs, gain, w_gu, layer, *, casts=(), tm=1024, tn=512):
    Mp, K = xp.shape
    S = SAMPLE_ROWS
    hidden = w_gu.shape[2] // 2
    assert xs.shape == (S, K) and Mp % tm == 0 and hidden % tn == 0 and w_gu.dtype == BF16
    nj = hidden // tn
    ni = Mp // tm
    c_args, c_in, c_shape, c_out, cast_tiles = _cast_specs(casts, ni * nj, nj)
    outs = pl.pallas_call(
        functools.partial(_gateup_kernel, cast_tiles=cast_tiles),
        grid=(ni, nj),
        in_specs=[pl.BlockSpec((tm, K), lambda i, j: (i, 0)),
                  pl.BlockSpec((S, K), lambda i, j: (0, 0)),
                  pl.BlockSpec((1, K), lambda i, j: (0, 0)),
                  pl.BlockSpec((None, K, tn), lambda i, j: (layer, 0, j)),
                  pl.BlockSpec((None, K, tn), lambda i, j: (layer, 0, j + nj))] + c_in,
        out_specs=[pl.BlockSpec((tm, tn), lambda i, j: (i, j)), pl.BlockSpec((S, tn), _sample_block)] + c_out,
        out_shape=[jax.ShapeDtypeStruct((Mp, hidden), BF16), jax.ShapeDtypeStruct((ni * S, hidden), F32)] + c_shape,
        scratch_shapes=[pltpu.VMEM((tm, K), BF16), pltpu.VMEM((S, K), BF16)],
        compiler_params=_params("arbitrary", "arbitrary"),
        name="gateup",
    )(xp, xs, gain.reshape(1, K), w_gu, w_gu, *c_args)
    return [outs[0], outs[1][:S]] + list(outs[2:])


def _conv_silu_slab(src, s, raw_ref, act_ref, cw_ref, cb_ref):
    Q = SSD_CHUNK
    raw_ref[s, CONV_TAIL:, :] = src
    for parity in (0, 1):
        acc = cb_ref[s]
        for k in range(CONV_W):
            first = CONV_TAIL - (CONV_W - 1) + k + parity
            acc = acc + raw_ref[s, pl.ds(first, Q // 2, stride=2), :] * cw_ref[s, k:k + 1, :]
        act_ref[s, pl.ds(parity, Q // 2, stride=2), :] = _silu(acc)
    raw_ref[s, 0:CONV_TAIL, :] = raw_ref[s, Q:Q + CONV_TAIL, :]


def _ssd_kernel(z_ref, x_ref, b_ref, c_ref, dt_ref, cw_ref, cb_ref,
                dtb_ref, alog_ref, dexp_ref, gn_ref,
                y_ref, hout_ref,
                h_ref, raw_ref, act_ref):
    Q = SSD_CHUNK
    P = SSM_HEAD_DIM
    N = SSM_STATE
    d_inner = x_ref.shape[1]
    n_heads = d_inner // P
    hpg = n_heads // SSM_GROUPS
    gw = hpg * P
    n_xs = d_inner // LANES
    n_bs = b_ref.shape[1] // LANES
    c = pl.program_id(1)

    @pl.when(c == 0)
    def _():
        h_ref[...] = jnp.zeros_like(h_ref)
        raw_ref[:, 0:CONV_TAIL, :] = jnp.zeros((raw_ref.shape[0], CONV_TAIL, LANES), F32)

    for s in range(n_xs + 2 * n_bs):
        if s < n_xs:
            src = x_ref[:, s * LANES:(s + 1) * LANES]
        elif s < n_xs + n_bs:
            src = b_ref[:, (s - n_xs) * LANES:(s - n_xs + 1) * LANES]
        else:
            src = c_ref[:, (s - n_xs - n_bs) * LANES:(s - n_xs - n_bs + 1) * LANES]
        _conv_silu_slab(src, s, raw_ref, act_ref, cw_ref, cb_ref)

    dt = _softplus(dt_ref[...] + dtb_ref[...])
    dA = dt * (-jnp.exp(alog_ref[...]))
    row = lax.broadcasted_iota(jnp.int32, (Q, Q), 0)
    col = lax.broadcasted_iota(jnp.int32, (Q, Q), 1)
    causal = row >= col
    tri = jnp.where(causal, 1.0, 0.0).astype(F32)
    cs = jnp.dot(tri, dA, precision=HIGHEST, preferred_element_type=F32)
    csT = cs.T
    dtT = dt.T
    dt_decay_end = dt * jnp.exp(cs[Q - 1:Q, :] - cs)
    chunk_decay = jnp.broadcast_to(jnp.exp(csT[0:n_heads, Q - 1:Q]), (n_heads, N))

    lane_lo = lax.broadcasted_iota(jnp.int32, (Q, 2 * P), 1) < P

    def pair_cols(arr, q):
        a0 = jnp.broadcast_to(arr[:, 2 * q:2 * q + 1], (Q, 2 * P))
        a1 = jnp.broadcast_to(arr[:, 2 * q + 1:2 * q + 2], (Q, 2 * P))
        return jnp.where(lane_lo, a0, a1)

    for g in range(SSM_GROUPS):
        Bg = act_ref[n_xs + g].astype(BF16)
        Cg = act_ref[n_xs + n_bs + g].astype(BF16)
        CB = lax.dot_general(Cg, Bg, NT_DIMS, preferred_element_type=F32)
        Hg = h_ref[g * hpg:(g + 1) * hpg].reshape(gw, N)
        y_off = lax.dot_general(Cg, Hg.astype(BF16), NT_DIMS, preferred_element_type=F32)

        ys = []
        xds = []
        for qq in range(hpg // 2):
            q = g * (hpg // 2) + qq
            lo, hi = q * 2 * P, (q + 1) * 2 * P
            xs = act_ref[q]
            xds.append((xs * pair_cols(dt_decay_end, q)).astype(BF16))
            cs_cols = [jnp.broadcast_to(cs[:, hh:hh + 1], (Q, Q)) for hh in (2 * q, 2 * q + 1)]
            atts = []
            for hh, cs_col in zip((2 * q, 2 * q + 1), cs_cols):
                seg = cs_col - csT[hh:hh + 1, :]
                decay = jnp.exp(jnp.where(causal, seg, -jnp.inf))
                atts.append(((CB * decay) * dtT[hh:hh + 1, :]).astype(BF16))
            att = jnp.concatenate(atts, axis=1)
            xbd = jnp.concatenate([jnp.where(lane_lo, xs, 0.0).astype(BF16),
                                   jnp.where(lane_lo, 0.0, xs).astype(BF16)], axis=0)
            y = jnp.dot(att, xbd, preferred_element_type=F32)
            y = y + y_off[:, qq * 2 * P:(qq + 1) * 2 * P] * jnp.exp(jnp.where(lane_lo, *cs_cols))
            y = y + xs * dexp_ref[:, lo:hi]
            ys.append(y * _silu(z_ref[:, lo:hi]))

        ssq = ys[0] * ys[0]
        for y in ys[1:]:
            ssq = ssq + y * y
        scale = lax.rsqrt(jnp.sum(ssq, axis=-1, keepdims=True) / gw + EPS)
        for qq, y in enumerate(ys):
            lo = g * gw + qq * 2 * P
            y_ref[:, lo:lo + 2 * P] = ((y * scale) * gn_ref[:, lo:lo + 2 * P]).astype(y_ref.dtype)

        S = lax.dot_general(jnp.concatenate(xds, axis=1), Bg, TN_DIMS, preferred_element_type=F32)
        for r in range(hpg):
            hh = g * hpg + r
            h_ref[hh] = h_ref[hh] * chunk_decay[hh:hh + 1, :] + S[r * P:(r + 1) * P, :]

    @pl.when(c == pl.num_programs(1) - 1)
    def _():
        hout_ref[0] = h_ref[...]


def _ssd_prompt(zxbc, dt, batch, conv_w, conv_b, dt_bias, a_log, d_skip, gate_norm):
    M = zxbc.shape[0]
    L = M // batch
    H = dt_bias.shape[0]
    d_inner = H * SSM_HEAD_DIM
    bc = SSM_GROUPS * SSM_STATE
    conv_dim = d_inner + 2 * bc
    Q = SSD_CHUNK
    nc = L // Q
    assert L % Q == 0 and d_inner % bc == 0
    xb = d_inner // bc
    row_map = lambda col: (lambda b, c: (b * nc + c, col))
    const = lambda b, c: (0, 0)
    d_exp = jnp.repeat(d_skip.astype(F32), SSM_HEAD_DIM).reshape(1, d_inner)
    assert H <= LANES and SSM_STATE == LANES and 2 * SSM_HEAD_DIM == LANES
    lane_pad = lambda v: jnp.pad(v.reshape(1, H), ((0, 0), (0, LANES - H)))
    n_slabs = conv_dim // LANES
    cw_slabs = conv_w.reshape(CONV_W, n_slabs, LANES).transpose(1, 0, 2)
    cb_slabs = conv_b.reshape(n_slabs, 1, LANES)
    const3 = lambda b, c: (0, 0, 0)
    y, h_final = pl.pallas_call(
        _ssd_kernel,
        grid=(batch, nc),
        in_specs=[pl.BlockSpec((Q, d_inner), row_map(0)),
                  pl.BlockSpec((Q, d_inner), row_map(1)),
                  pl.BlockSpec((Q, bc), row_map(2 * xb)),
                  pl.BlockSpec((Q, bc), row_map(2 * xb + 1)),
                  pl.BlockSpec((Q, LANES), row_map(0)),
                  pl.BlockSpec((n_slabs, CONV_W, LANES), const3),
                  pl.BlockSpec((n_slabs, 1, LANES), const3),
                  pl.BlockSpec((1, LANES), const),
                  pl.BlockSpec((1, LANES), const),
                  pl.BlockSpec((1, d_inner), const),
                  pl.BlockSpec((1, d_inner), const)],
        out_specs=[pl.BlockSpec((Q, d_inner), row_map(0)),
                   pl.BlockSpec((1, H, SSM_HEAD_DIM, SSM_STATE), lambda b, c: (b, 0, 0, 0))],
        out_shape=[jax.ShapeDtypeStruct((M, d_inner), BF16),
                   jax.ShapeDtypeStruct((batch, H, SSM_HEAD_DIM, SSM_STATE), F32)],
        scratch_shapes=[pltpu.VMEM((H, SSM_HEAD_DIM, SSM_STATE), F32),
                        pltpu.VMEM((n_slabs, Q + CONV_TAIL, LANES), F32),
                        pltpu.VMEM((n_slabs, Q, LANES), F32)],
        compiler_params=_params("parallel", "arbitrary"),
        name="ssd_prompt",
    )(zxbc, zxbc, zxbc, zxbc, dt, cw_slabs, cb_slabs,
      lane_pad(dt_bias), lane_pad(a_log), d_exp, gate_norm.reshape(1, d_inner))
    return y, h_final


def _ssd_step_kernel(zxbc_ref, dt_ref, cs_ref, h0_ref, cw_ref, cb_ref, dtb_ref, alog_ref,
                     dexp_ref, gn_ref, y_ref, cso_ref, ho_ref):
    P = SSM_HEAD_DIM
    N = SSM_STATE
    n_heads = h0_ref.shape[2]
    d_inner = n_heads * P
    hpg = n_heads // SSM_GROUPS
    gw = hpg * P
    conv_dim = cw_ref.shape[1]

    raw = zxbc_ref[0, :, d_inner:d_inner + conv_dim]
    prev = cs_ref[0, 0]
    acc = cb_ref[...]
    for k in range(CONV_W - 1):
        acc = acc + prev[k:k + 1, :] * cw_ref[k:k + 1, :]
    acc = acc + raw * cw_ref[CONV_W - 1:CONV_W, :]
    xbc = _silu(acc)
    cso_ref[0, 0, 0:CONV_W - 2, :] = prev[1:CONV_W - 1, :]
    cso_ref[0, 0, CONV_W - 2:CONV_W - 1, :] = raw

    dt = _softplus(dt_ref[0, :, 0:n_heads] + dtb_ref[...])
    decay = jnp.exp(dt * (-jnp.exp(alog_ref[...])))
    z = zxbc_ref[0, :, 0:d_inner]

    eye = (lax.broadcasted_iota(jnp.int32, (P, P), 0) == lax.broadcasted_iota(jnp.int32, (P, P), 1))
    ys = []
    for g in range(SSM_GROUPS):
        Bg = xbc[:, d_inner + g * N:d_inner + (g + 1) * N]
        Cg = xbc[:, d_inner + SSM_GROUPS * N + g * N:d_inner + SSM_GROUPS * N + (g + 1) * N]
        Bb = jnp.broadcast_to(Bg, (P, N))
        for r in range(hpg):
            hh = g * hpg + r
            xs = xbc[:, hh * P:(hh + 1) * P]
            xdt = xs * dt[:, hh:hh + 1]
            xdiag = jnp.where(eye, jnp.broadcast_to(xdt, (P, P)), 0.0)
            outer = jnp.dot(xdiag, Bb, precision=HIGHEST, preferred_element_type=F32)
            ho_ref[0, 0, hh] = h0_ref[0, 0, hh] * decay[:, hh:hh + 1] + outer
        Hg = ho_ref[0, 0, g * hpg:(g + 1) * hpg].reshape(gw, N)
        yg = lax.dot_general(Cg, Hg, NT_DIMS, precision=HIGHEST, preferred_element_type=F32)
        lo, hi = g * gw, (g + 1) * gw
        yg = yg + xbc[:, lo:hi] * dexp_ref[:, lo:hi]
        yg = yg * _silu(z[:, lo:hi])
        yg = yg * lax.rsqrt(jnp.mean(yg * yg, axis=-1, keepdims=True) + EPS)
        ys.append(yg * gn_ref[:, lo:hi])
    y_ref[0] = jnp.concatenate(ys, axis=1).astype(y_ref.dtype)


def _ssd_step(zxbc, dt, conv_state, ssm_state, conv_w, conv_b, dt_bias, a_log, d_skip, gate_norm):
    B = zxbc.shape[0]
    H = dt_bias.shape[0]
    d_inner = H * SSM_HEAD_DIM
    conv_dim = conv_w.shape[1]
    const = lambda b: (0, 0)
    d_exp = jnp.repeat(d_skip.astype(F32), SSM_HEAD_DIM).reshape(1, d_inner)
    cs4 = conv_state.reshape(1, B, CONV_W - 1, conv_dim)
    h5 = ssm_state.reshape(1, B, H, SSM_HEAD_DIM, SSM_STATE)
    return pl.pallas_call(
        _ssd_step_kernel,
        grid=(B,),
        in_specs=[pl.BlockSpec((1, 1, zxbc.shape[1]), lambda b: (b, 0, 0)),
                  pl.BlockSpec((1, 1, dt.shape[1]), lambda b: (b, 0, 0)),
                  pl.BlockSpec((1, 1, CONV_W - 1, conv_dim), lambda b: (0, b, 0, 0)),
                  pl.BlockSpec((1, 1, H, SSM_HEAD_DIM, SSM_STATE), lambda b: (0, b, 0, 0, 0)),
                  pl.BlockSpec((CONV_W, conv_dim), const),
                  pl.BlockSpec((1, conv_dim), const),
                  pl.BlockSpec((1, H), const),
                  pl.BlockSpec((1, H), const),
                  pl.BlockSpec((1, d_inner), const),
                  pl.BlockSpec((1, d_inner), const)],
        out_specs=[pl.BlockSpec((1, 1, d_inner), lambda b: (b, 0, 0)),
                   pl.BlockSpec((1, 1, CONV_W - 1, conv_dim), lambda b: (0, b, 0, 0)),
                   pl.BlockSpec((1, 1, H, SSM_HEAD_DIM, SSM_STATE), lambda b: (0, b, 0, 0, 0))],
        out_shape=[jax.ShapeDtypeStruct((B, 1, d_inner), F32),
                   jax.ShapeDtypeStruct(cs4.shape, F32),
                   jax.ShapeDtypeStruct(h5.shape, F32)],
        compiler_params=_params("arbitrary"),
        name="ssd_step",
    )(zxbc.reshape(B, 1, -1), dt.reshape(B, 1, -1), cs4, h5, conv_w, conv_b.reshape(1, conv_dim),
      dt_bias.reshape(1, H), a_log.reshape(1, H), d_exp, gate_norm.reshape(1, d_inner))


def _attn_kernel(q0_ref, q1_ref, q2_ref, k_ref, v_ref, o_ref, on_ref, lse_ref, qs_ref, ks_ref, vs_ref):
    L = k_ref.shape[0]
    QB = ATT_BLOCK
    KW = QB + DIL_SLOTS
    P = ATT_PRESTRIDE
    p_shift = P.bit_length() - 1
    scale = HEAD_DIM ** -0.5
    e = lax.broadcasted_iota(jnp.int32, (QB, KW), 0) - lax.broadcasted_iota(jnp.int32, (QB, KW), 1)

    staged = {rate: rate > P for rate in DIL_RATES}
    for src_ref, dst_ref, needed in ((q2_ref, qs_ref, staged[DIL_RATES[2]]), (k_ref, ks_ref, any(staged.values())),
                                     (v_ref, vs_ref, any(staged.values()))):
        if needed:
            def stage(i, carry, src_ref=src_ref, dst_ref=dst_ref):
                cp = i & (P - 1)
                t = i >> p_shift
                dst = pl.multiple_of(cp * (L // P) + t * QB, QB)
                dst_ref[pl.ds(dst, QB), :] = src_ref[pl.ds(cp + P * QB * t, QB, stride=P), :]
                return carry
            lax.fori_loop(0, L // QB, stage, 0, unroll=4)

    for g, (q_ref, rate) in enumerate(zip((q0_ref, q1_ref, q2_ref), DIL_RATES)):
        shift = rate.bit_length() - 1

        def unit(n, carry, g=g, q_ref=q_ref, rate=rate, shift=shift):
            c = n & (rate - 1)
            u0 = (n >> shift) * QB
            v0 = jnp.maximum(u0 - DIL_SLOTS, 0)
            q_rows = pl.ds(c + rate * u0, QB, stride=rate)
            if staged[rate]:
                base = (c & (P - 1)) * (L // P) + (c >> p_shift)
                q = qs_ref[pl.ds(base + (rate // P) * u0, QB, stride=rate // P), :].astype(BF16)
                k_rows = pl.ds(base + (rate // P) * v0, KW, stride=rate // P)
                k = ks_ref[k_rows, :].astype(BF16)
                v = vs_ref[k_rows, :].astype(BF16)
            else:
                k_rows = pl.ds(c + rate * v0, KW, stride=rate)
                q = q_ref[q_rows, :].astype(BF16)
                k = k_ref[k_rows, :].astype(BF16)
                v = v_ref[k_rows, :].astype(BF16)
            s = lax.dot_general(q, k, NT_DIMS, preferred_element_type=F32) * scale
            d = e + (u0 - v0)
            s = jnp.where(d >= 0, s, -jnp.inf)
            s = jnp.where(d <= DIL_SLOTS, s, -jnp.inf)
            m = s.max(axis=1, keepdims=True)
            p = jnp.exp(s - m)
            l = p.sum(axis=1, keepdims=True)
            o = jnp.dot(p.astype(BF16), v, preferred_element_type=F32)
            on_ref[g, q_rows, :] = o / l
            lse_ref[g, q_rows, :] = jnp.broadcast_to(m + jnp.log(l), (QB, HEAD_DIM))
            return carry

        lax.fori_loop(0, L // QB, unit, 0, unroll=ATT_UNROLL)

    def mix(i, carry):
        rows = pl.ds(pl.multiple_of(i * QB, QB), QB)
        lses = [lse_ref[g, rows, :] for g in range(len(DIL_RATES))]
        m = functools.reduce(jnp.maximum, lses)
        ws = [jnp.exp(x - m) for x in lses]
        num = functools.reduce(jnp.add, [w * on_ref[g, rows, :] for g, w in enumerate(ws)])
        o_ref[rows, :] = (num / functools.reduce(jnp.add, ws)).astype(o_ref.dtype)
        return carry

    lax.fori_loop(0, L // QB, mix, 0, unroll=2)


def _attn_prompt(q, kv, batch):
    M = q.shape[0]
    L = M // batch
    n_grp = len(DIL_RATES)
    assert all(r & (r - 1) == 0 and L % (ATT_BLOCK * r) == 0 and L >= r * (ATT_BLOCK + DIL_SLOTS)
               for r in DIL_RATES)
    qspec = lambda g: pl.BlockSpec((L, HEAD_DIM), lambda b, h: (b, g * KV_HEADS + h))
    kspec = pl.BlockSpec((L, HEAD_DIM), lambda b, h: (b, h))
    vspec = pl.BlockSpec((L, HEAD_DIM), lambda b, h: (b, KV_HEADS + h))
    return pl.pallas_call(
        _attn_kernel,
        grid=(batch, KV_HEADS),
        in_specs=[qspec(0), qspec(1), qspec(2), kspec, vspec],
        out_specs=pl.BlockSpec((L, HEAD_DIM), lambda b, h: (b, h)),
        out_shape=jax.ShapeDtypeStruct((M, KV_HEADS * HEAD_DIM), BF16),
        scratch_shapes=[pltpu.VMEM((n_grp, L, HEAD_DIM), F32), pltpu.VMEM((n_grp, L, HEAD_DIM), F32)]
        + [pltpu.VMEM((L, HEAD_DIM), F32)] * 3,
        compiler_params=_params("parallel", "arbitrary"),
        name="attn_prompt",
    )(q, q, q, kv, kv)


def _bf16_round(a):
    return a.astype(BF16).astype(F32)


def _attn_step_kernel(q_ref, *refs):
    n = len(DIL_RATES)
    k_refs, v_refs = refs[0:n], refs[n:2 * n]
    kn_ref, vn_ref, o_ref = refs[2 * n:]
    scale = HEAD_DIM ** -0.5
    kn = _bf16_round(kn_ref[0, 0])
    vn = _bf16_round(vn_ref[0, 0])
    scores, new_scores = [], []
    for g, k_ref in enumerate(k_refs):
        qg = _bf16_round(q_ref[0, g])
        kg = _bf16_round(k_ref[0, :, 0])
        scores.append(jnp.sum(kg * qg[None], axis=-1, keepdims=True) * scale)
        new_scores.append(jnp.sum(kn * qg, axis=-1, keepdims=True) * scale)
    m = functools.reduce(jnp.maximum, [s.max(axis=0) for s in scores] + new_scores)
    l = jnp.zeros_like(m)
    o = jnp.zeros((KV_HEADS, HEAD_DIM), F32)
    for s, s_new, v_ref in zip(scores, new_scores, v_refs):
        p = jnp.exp(s - m[None])
        p_new = jnp.exp(s_new - m)
        l = l + p.sum(axis=0) + p_new
        o = o + (_bf16_round(p) * _bf16_round(v_ref[0, :, 0])).sum(axis=0) + _bf16_round(p_new) * vn
    o_ref[0, 0] = (o / l).astype(o_ref.dtype)


def _attn_step(q, cache_k, cache_v, k_new, v_new):
    B, T = cache_k.shape[0], cache_k.shape[1]
    S = DIL_SLOTS
    assert T == S * max(DIL_RATES), "every slot of every dilation group lies inside the cached window"
    qg = q.reshape(B, len(DIL_RATES), KV_HEADS, HEAD_DIM)
    row_shape = (B, 1, KV_HEADS, HEAD_DIM)
    row_spec = pl.BlockSpec((1, 1, KV_HEADS, HEAD_DIM), lambda b: (b, 0, 0, 0))
    args = [qg]
    in_specs = [pl.BlockSpec((1, len(DIL_RATES), KV_HEADS, HEAD_DIM), lambda b: (b, 0, 0, 0))]
    for cache in (cache_k, cache_v):
        for rate in DIL_RATES:
            args.append(cache.reshape(B, T // rate, rate, KV_HEADS, HEAD_DIM))
            in_specs.append(pl.BlockSpec((1, S, 1, KV_HEADS, HEAD_DIM),
                                         lambda b, blk=T // rate // S - 1: (b, blk, 0, 0, 0)))
    args += [k_new.reshape(row_shape), v_new.reshape(row_shape)]
    in_specs += [row_spec, row_spec]
    o = pl.pallas_call(
        _attn_step_kernel,
        grid=(B,),
        in_specs=in_specs,
        out_specs=row_spec,
        out_shape=jax.ShapeDtypeStruct(row_shape, F32),
        compiler_params=_params("parallel"),
        name="attn_step",
    )(*args)
    return o.reshape(B, KV_HEADS * HEAD_DIM)


def _rope_tables(pos):
    half = HEAD_DIM // 2
    inv = jnp.power(jnp.float32(ROPE_THETA), -jnp.arange(half, dtype=jnp.float32) / half)
    ang = pos.astype(jnp.float32)[:, None] * inv[None, :]
    cos = jnp.cos(ang)
    sin = jnp.sin(ang)
    return jnp.concatenate([cos, cos], axis=1), jnp.concatenate([-sin, sin], axis=1)


def kernel(x_prompt, x_sample, state_conv, state_ssm, cache_k, cache_v, a_norm, a_w_in, a_conv_w, a_conv_b,
           a_dt_bias, a_log, a_d, a_gate_norm, a_w_out, kv_norm, w_kv, b_norm, b_w_q, b_w_o, ffn_norm,
           ffn_w_gu, ffn_w_down, final_norm):
    assert a_norm.shape[0] == 1 and b_norm.shape[0] == 1, "one Mamba-2 layer followed by one attention layer"
    Bp, Lp, D = x_prompt.shape
    Bs, Ls, _ = x_sample.shape
    assert Ls == 1 and Bs == SAMPLE_ROWS, "sample group decodes one token for SAMPLE_ROWS sequences"
    n_heads = a_dt_bias.shape[1]
    d_inner = a_w_out.shape[1]
    conv_dim = a_conv_w.shape[2]
    kv_dim = KV_HEADS * HEAD_DIM
    w_dt = jnp.pad(a_w_in[0][:, d_inner + conv_dim:], ((0, 0), (0, LANES - n_heads))).astype(BF16)
    ssd_w = (a_conv_w[0], a_conv_b[0], a_dt_bias[0], a_log[0], a_d[0], a_gate_norm[0])
    w_kv = w_kv[None]

    cos_p, sin_p = _rope_tables(jnp.arange(Lp, dtype=jnp.int32))
    cos_s, sin_s = _rope_tables(PAST_LEN + jnp.arange(Ls, dtype=jnp.int32))
    rope = ((cos_p, sin_p, Lp), (jnp.broadcast_to(cos_s, (Bs, HEAD_DIM)), jnp.broadcast_to(sin_s, (Bs, HEAD_DIM))))

    xp0 = x_prompt.reshape(Bp * Lp, D)
    xs0 = x_sample.reshape(Bs, D)

    zxbc_p, zxbc_s, dt_p, dt_s, w_out_bf, w_gu0_bf, w_down0_bf = _proj(
        xp0, xs0, [_seg(a_w_in.astype(BF16), n_cols=d_inner + conv_dim)], gains=[a_norm[0]], aux_w=w_dt, tn=1024,
        casts=[(a_w_out, 0), (ffn_w_gu, 0), (ffn_w_down, 0)])
    y_p, p_ssm = _ssd_prompt(zxbc_p, dt_p, Bp, *ssd_w)
    p_conv = zxbc_p.reshape(Bp, Lp, -1)[:, Lp - (CONV_W - 1):, d_inner:]
    y_s, s_conv, s_ssm = _ssd_step(zxbc_s, dt_s, state_conv[0], state_ssm[0], *ssd_w)
    *x1, w_gu1_bf = _proj(y_p, y_s.reshape(Bs, d_inner), [_seg(w_out_bf)], res=(xp0, xs0), casts=[(ffn_w_gu, 1)],
                          tm=256, tn=D)
    *h, w_kv_bf, w_q_bf, w_o_bf = _gateup(*x1, ffn_norm[0], w_gu0_bf, 0, casts=[(w_kv, 0), (b_w_q, 0), (b_w_o, 0)])
    x2 = _proj(*h, [_seg(w_down0_bf)], res=x1, tm=256, tn=D)

    kv_p, kv_s, q_p, q_s = _proj(*x2, [_seg(w_kv_bf, rope_cols=kv_dim), _seg(w_q_bf, rope_cols=b_w_q.shape[2])],
                                 gains=[kv_norm, b_norm[0]], rope=rope)
    o_p = _attn_prompt(q_p, kv_p, Bp)
    o_s = _attn_step(q_s, cache_k, cache_v, kv_s[:, :kv_dim], kv_s[:, kv_dim:])
    x3 = _proj(o_p, o_s, [_seg(w_o_bf)], res=x2, tn=1024)
    *h, w_down1_bf = _gateup(*x3, ffn_norm[1], w_gu1_bf, 0, casts=[(ffn_w_down, 1)])
    y_prompt, y_sample = _proj(*h, [_seg(w_down1_bf)], res=x3, post_gain=final_norm, tm=256, tn=D)
    y_prompt = y_prompt.reshape(Bp, Lp, D)
    y_sample = y_sample.reshape(Bs, Ls, D)
    keep = min(DIL_SLOTS * max(DIL_RATES), Lp)
    p_kv = kv_p.reshape(Bp, Lp, 2 * KV_HEADS, HEAD_DIM)[:, Lp - keep:]
    s_kv = kv_s.reshape(Bs, Ls, 2 * KV_HEADS, HEAD_DIM)
    return (y_prompt, y_sample, p_conv[None], p_ssm[None], p_kv[:, :, :KV_HEADS], p_kv[:, :, KV_HEADS:],
            s_conv, s_ssm, s_kv[:, :, :KV_HEADS], s_kv[:, :, KV_HEADS:])
```

```python
import functools

import jax
import jax.numpy as jnp
from jax import lax
from jax.experimental import pallas as pl
from jax.experimental.pallas import tpu as pltpu

F32 = jnp.float32
BF16 = jnp.bfloat16
HIGHEST = lax.Precision.HIGHEST

EPS = 1e-6
ROPE_THETA = 10000.0
SSD_CHUNK = 128
SSM_HEAD_DIM = 64
SSM_STATE = 128
SSM_GROUPS = 8
CONV_W = 4
HEAD_DIM = 128
KV_HEADS = 8
DIL_RATES = (1, 4, 16)
DIL_SLOTS = 128
ATT_BLOCK = 128
ATT_UNROLL = 8
ATT_PRESTRIDE = 4
PAST_LEN = 16384

V7X_VMEM_BYTES = 64 * 1024 * 1024
VMEM_LIMIT = V7X_VMEM_BYTES - 8 * 1024 * 1024
LANES = 128
CONV_TAIL = 8
SAMPLE_ROWS = 8

NT_DIMS = (((1,), (1,)), ((), ()))
TN_DIMS = (((0,), (0,)), ((), ()))


def _params(*sem):
    return pltpu.CompilerParams(dimension_semantics=sem, vmem_limit_bytes=VMEM_LIMIT)


def _silu(x):
    h = 0.5 * x
    return h + h * jnp.tanh(h)


def _softplus(x):
    return jnp.maximum(x, 0.0) + jnp.log1p(jnp.exp(-jnp.abs(x)))


def _rms_scale(x):
    return lax.rsqrt(jnp.mean(x * x, axis=-1, keepdims=True) + EPS)


def _normed(x_ref, g_ref):
    x = x_ref[...]
    return ((x * _rms_scale(x)) * g_ref[...]).astype(BF16)


def _sample_block(i, j):
    return (i, j)


def _cast_specs(casts, n_steps, nj):
    BF16_ROWS = 16
    args, in_specs, out_shape, out_specs, tiles = [], [], [], [], []
    for src, layer in casts:
        R, C = src.shape[1:]
        rows = next(r for r in range(BF16_ROWS, R + 1, BF16_ROWS) if R % r == 0 and R // r <= n_steps)
        nt = R // rows
        tile = lambda i, j, nt=nt: jnp.minimum(i * nj + j, nt - 1)
        args.append(src)
        in_specs.append(pl.BlockSpec((None, rows, C), lambda i, j, layer=layer, tile=tile: (layer, tile(i, j), 0)))
        out_shape.append(jax.ShapeDtypeStruct((1, R, C), BF16))
        out_specs.append(pl.BlockSpec((None, rows, C), lambda i, j, tile=tile: (0, tile(i, j), 0)))
        tiles.append(nt)
    return args, in_specs, out_shape, out_specs, tuple(tiles)


def _run_casts(src_refs, dst_refs, tiles):
    step = pl.program_id(0) * pl.num_programs(1) + pl.program_id(1)
    for src_ref, dst_ref, nt in zip(src_refs, dst_refs, tiles):
        @pl.when(step < nt)
        def _(src_ref=src_ref, dst_ref=dst_ref):
            dst_ref[...] = src_ref[...].astype(BF16)


def _proj_kernel(*refs, segs, resident, norm, has_rope, has_res, has_post, has_aux, cast_tiles):
    it = iter(refs)
    xp_ref, xs_ref = next(it), next(it)
    g_refs = [next(it) for _ in segs] if norm else None
    w_refs = [next(it) for _ in segs]
    auxw_ref = next(it) if has_aux else None
    rope_p = (next(it), next(it)) if has_rope else None
    rope_s = (next(it), next(it)) if has_rope else None
    resp_ref, ress_ref = (next(it), next(it)) if has_res else (None, None)
    pg_ref = next(it) if has_post else None
    cast_srcs = [next(it) for _ in cast_tiles]
    out_refs =[(next(it), next(it)) for _ in segs]
    auxp_ref, auxs_ref = (next(it), next(it)) if has_aux else (None, None)
    cast_dsts = [next(it) for _ in cast_tiles]
    xnp_ref, xns_ref = (next(it), next(it)) if norm else (None, None)
    i = pl.program_id(0)
    j = pl.program_id(1)
    _run_casts(cast_srcs, cast_dsts, cast_tiles)

    if norm:
        @pl.when(j == 0)
        def _():
            for s, g_ref in enumerate(g_refs):
                xnp_ref[s] = _normed(xp_ref, g_ref)

        @pl.when((i == 0) & (j == 0))
        def _():
            for s, g_ref in enumerate(g_refs):
                xns_ref[s] = _normed(xs_ref, g_ref)

    def lhs_p(s):
        return xnp_ref[s] if norm else xp_ref[...]

    def lhs_s(s):
        return xns_ref[s] if norm else xs_ref[...].astype(BF16)

    def emit(lhs, w_ref, res_ref, o_ref, rope, lo, n_tiles, n_rope):
        acc = jnp.dot(lhs, w_ref[...].astype(BF16), preferred_element_type=F32)
        if has_res:
            acc = acc + res_ref[...]
        if has_post:
            acc = (acc * _rms_scale(acc)) * pg_ref[...]
        heads = acc.shape[1] // HEAD_DIM

        def store(rope_heads):
            if rope_heads:
                cos = rope[0][...]
                sin = rope[1][...]
            for h in range(rope_heads):
                a = acc[:, h * HEAD_DIM:(h + 1) * HEAD_DIM]
                o_ref[:, h * HEAD_DIM:(h + 1) * HEAD_DIM] = (
                    a * cos + pltpu.roll(a, HEAD_DIM // 2, 1) * sin).astype(o_ref.dtype)
            if rope_heads < heads:
                o_ref[:, rope_heads * HEAD_DIM:] = acc[:, rope_heads * HEAD_DIM:].astype(o_ref.dtype)

        if resident:
            store(n_rope)
        elif n_rope in (0, n_tiles):
            store(heads if n_rope else 0)
        else:
            pl.when(j < lo + n_rope)(lambda: store(heads))
            pl.when(j >= lo + n_rope)(lambda: store(0))

    for s, (lo, n_tiles, n_rope) in enumerate(segs):
        def segment(s=s, lo=lo, n_tiles=n_tiles, n_rope=n_rope):
            emit(lhs_p(s), w_refs[s], resp_ref, out_refs[s][0], rope_p, lo, n_tiles, n_rope)

            @pl.when(i == 0)
            def _():
                emit(lhs_s(s), w_refs[s], ress_ref, out_refs[s][1], rope_s, lo, n_tiles, n_rope)

        if resident or len(segs) == 1:
            segment()
        else:
            pl.when((j >= lo) & (j < lo + n_tiles))(segment)

    @pl.when(i > 0)
    def _():
        for _, os_ref in out_refs:
            os_ref[...] = jnp.zeros_like(os_ref)

    if has_aux:
        @pl.when(j == pl.num_programs(1) - 1)
        def _():
            auxp_ref[...] = jnp.dot(lhs_p(0), auxw_ref[...], preferred_element_type=F32)

            @pl.when(i == 0)
            def _():
                auxs_ref[...] = jnp.dot(lhs_s(0), auxw_ref[...], preferred_element_type=F32)

            @pl.when(i > 0)
            def _():
                auxs_ref[...] = jnp.zeros_like(auxs_ref)


def _seg(w, layer=0, col0=0, n_cols=None, rope_cols=0):
    return (w, layer, col0, w.shape[-1] - col0 if n_cols is None else n_cols, rope_cols)


def _proj(xp, xs, ws, *, gains=None, aux_w=None, rope=None, res=None, post_gain=None, casts=(), out_dtype=F32,
          tm=1024, tn=512):
    Mp, K = xp.shape
    S = SAMPLE_ROWS
    assert xs.shape == (S, K) and Mp % tm == 0
    norm = gains is not None
    resident = tn is None
    segs, widths, lo = [], [], 0
    for w, layer, col0, n_cols, rope_cols in ws:
        tw = n_cols if resident else tn
        rope_unit = HEAD_DIM if resident else tw
        assert w.ndim == 3 and n_cols % tw == 0 and col0 % tw == 0 and rope_cols % rope_unit == 0
        segs.append((lo, n_cols // tw, rope_cols // rope_unit))
        widths.append(tw)
        lo += 0 if resident else n_cols // tw
    nj, ni = (1 if resident else lo), Mp // tm
    has_rope = any(r for _, _, r in segs)
    assert not (has_rope and res is not None) and (rope is not None) == has_rope
    row = lambda i, j: (i, 0)
    const = lambda i, j: (0, 0)
    args = [xp, xs]
    in_specs = [pl.BlockSpec((tm, K), row), pl.BlockSpec((S, K), const)]
    if norm:
        assert len(gains) == len(ws)
        args += [g.reshape(1, K) for g in gains]
        in_specs += [pl.BlockSpec((1, K), const)] * len(gains)
    for (w, layer, col0, _, _), (lo, n_tiles, _), tw in zip(ws, segs, widths):
        args.append(w)
        in_specs.append(pl.BlockSpec(
            (None, K, tw), lambda i, j, layer=layer, lo=lo, n=n_tiles, c0=col0 // tw:
            (layer, 0, c0 + jnp.clip(j - lo, 0, n - 1)),
            pipeline_mode=pl.Buffered(1) if nj == 1 else None))
    if aux_w is not None:
        args.append(aux_w)
        in_specs.append(pl.BlockSpec(aux_w.shape, const))
    if has_rope:
        (cos_p, sin_p, rows_per_seq), (cos_s, sin_s) = rope
        nseq = rows_per_seq // tm
        args += [cos_p, sin_p, cos_s, sin_s]
        in_specs += [pl.BlockSpec((tm, HEAD_DIM), lambda i, j: (i % nseq, 0))] * 2
        in_specs += [pl.BlockSpec((S, HEAD_DIM), const)] * 2
    assert (res is None and post_gain is None) or len(ws) == 1
    if res is not None:
        args += list(res)
        in_specs += [pl.BlockSpec((tm, widths[0]), lambda i, j: (i, j)),
                     pl.BlockSpec((S, widths[0]), lambda i, j: (0, j))]
    if post_gain is not None:
        assert nj == 1
        args.append(post_gain.reshape(1, widths[0]))
        in_specs.append(pl.BlockSpec((1, widths[0]), const))
    out_shape, out_specs = [], []
    for (lo, n_tiles, _), tw in zip(segs, widths):
        col = lambda i, j, lo=lo, n=n_tiles: (i, jnp.clip(j - lo, 0, n - 1))
        out_shape += [jax.ShapeDtypeStruct((Mp, n_tiles * tw), out_dtype),
                      jax.ShapeDtypeStruct((ni * S, n_tiles * tw), F32)]
        out_specs += [pl.BlockSpec((tm, tw), col), pl.BlockSpec((S, tw), col)]
    if aux_w is not None:
        na = aux_w.shape[1]
        out_shape += [jax.ShapeDtypeStruct((Mp, na), F32), jax.ShapeDtypeStruct((ni * S, na), F32)]
        out_specs += [pl.BlockSpec((tm, na), row), pl.BlockSpec((S, na), row)]
    n_main = len(out_shape)
    c_args, c_in, c_shape, c_out, cast_tiles = _cast_specs(casts, ni * nj, nj)
    args += c_args
    in_specs += c_in
    out_shape += c_shape
    out_specs += c_out
    outs = pl.pallas_call(
        functools.partial(_proj_kernel, segs=tuple(segs), resident=resident, norm=norm, has_rope=has_rope,
                          has_res=res is not None, has_post=post_gain is not None, has_aux=aux_w is not None,
                          cast_tiles=cast_tiles),
        grid=(ni, nj),
        in_specs=in_specs,
        out_specs=out_specs,
        out_shape=out_shape,
        scratch_shapes=[pltpu.VMEM((len(ws), tm, K), BF16), pltpu.VMEM((len(ws), S, K), BF16)] if norm else [],
        compiler_params=_params("arbitrary", "arbitrary"),
        name="proj",
    )(*args)
    return [o[:S] if n < n_main and n % 2 == 1 else o for n, o in enumerate(outs)]


def _gateup_kernel(xp_ref, xs_ref, g_ref, wg_ref, wu_ref, *refs, cast_tiles):
    n = len(cast_tiles)
    cast_srcs = refs[:n]
    op_ref, os_ref = refs[n:n + 2]
    cast_dsts = refs[n + 2:2 * n + 2]
    xnp_ref, xns_ref = refs[2 * n + 2:]
    i = pl.program_id(0)
    j = pl.program_id(1)
    _run_casts(cast_srcs, cast_dsts, cast_tiles)

    @pl.when(j == 0)
    def _():
        xnp_ref[...] = _normed(xp_ref, g_ref)

    @pl.when((i == 0) & (j == 0))
    def _():
        xns_ref[...] = _normed(xs_ref, g_ref)

    def swiglu(xn):
        g = jnp.dot(xn, wg_ref[...], preferred_element_type=F32)
        u = jnp.dot(xn, wu_ref[...], preferred_element_type=F32)
        return _silu(g) * u

    op_ref[...] = swiglu(xnp_ref[...]).astype(op_ref.dtype)

    @pl.when(i == 0)
    def _():
        os_ref[...] = swiglu(xns_ref[...])

    @pl.when(i > 0)
    def _():
        os_ref[...] = jnp.zeros_like(os_ref)


def _gateup(xp, xs, gain, w_gu, layer, *, casts=(), tm=1024, tn=512):
    Mp, K = xp.shape
    S = SAMPLE_ROWS
    hidden = w_gu.shape[2] // 2
    assert xs.shape == (S, K) and Mp % tm == 0 and hidden % tn == 0 and w_gu.dtype == BF16
    nj = hidden // tn
    ni = Mp // tm
    c_args, c_in, c_shape, c_out, cast_tiles = _cast_specs(casts, ni * nj, nj)
    outs = pl.pallas_call(
        functools.partial(_gateup_kernel, cast_tiles=cast_tiles),
        grid=(ni, nj),
        in_specs=[pl.BlockSpec((tm, K), lambda i, j: (i, 0)),
                  pl.BlockSpec((S, K), lambda i, j: (0, 0)),
                  pl.BlockSpec((1, K), lambda i, j: (0, 0)),
                  pl.BlockSpec((None, K, tn), lambda i, j: (layer, 0, j)),
                  pl.BlockSpec((None, K, tn), lambda i, j: (layer, 0, j + nj))] + c_in,
        out_specs=[pl.BlockSpec((tm, tn), lambda i, j: (i, j)), pl.BlockSpec((S, tn), _sample_block)] + c_out,
        out_shape=[jax.ShapeDtypeStruct((Mp, hidden), BF16), jax.ShapeDtypeStruct((ni * S, hidden), F32)] + c_shape,
        scratch_shapes=[pltpu.VMEM((tm, K), BF16), pltpu.VMEM((S, K), BF16)],
        compiler_params=_params("arbitrary", "arbitrary"),
        name="gateup",
    )(xp, xs, gain.reshape(1, K), w_gu, w_gu, *c_args)
    return [outs[0], outs[1][:S]] + list(outs[2:])


def _conv_silu_slab(src, s, raw_ref, act_ref, cw_ref, cb_ref):
    Q = SSD_CHUNK
    raw_ref[s, CONV_TAIL:, :] = src
    for parity in (0, 1):
        acc = cb_ref[s]
        for k in range(CONV_W):
            first = CONV_TAIL - (CONV_W - 1) + k + parity
            acc = acc + raw_ref[s, pl.ds(first, Q // 2, stride=2), :] * cw_ref[s, k:k + 1, :]
        act_ref[s, pl.ds(parity, Q // 2, stride=2), :] = _silu(acc)
    raw_ref[s, 0:CONV_TAIL, :] = raw_ref[s, Q:Q + CONV_TAIL, :]


def _ssd_kernel(z_ref, x_ref, b_ref, c_ref, dt_ref, cw_ref, cb_ref,
                dtb_ref, alog_ref, dexp_ref, gn_ref,
                y_ref, hout_ref,
                h_ref, raw_ref, act_ref):
    Q = SSD_CHUNK
    P = SSM_HEAD_DIM
    N = SSM_STATE
    d_inner = x_ref.shape[1]
    n_heads = d_inner // P
    hpg = n_heads // SSM_GROUPS
    gw = hpg * P
    n_xs = d_inner // LANES
    n_bs = b_ref.shape[1] // LANES
    c = pl.program_id(1)

    @pl.when(c == 0)
    def _():
        h_ref[...] = jnp.zeros_like(h_ref)
        raw_ref[:, 0:CONV_TAIL, :] = jnp.zeros((raw_ref.shape[0], CONV_TAIL, LANES), F32)

    for s in range(n_xs + 2 * n_bs):
        if s < n_xs:
            src = x_ref[:, s * LANES:(s + 1) * LANES]
        elif s < n_xs + n_bs:
            src = b_ref[:, (s - n_xs) * LANES:(s - n_xs + 1) * LANES]
        else:
            src = c_ref[:, (s - n_xs - n_bs) * LANES:(s - n_xs - n_bs + 1) * LANES]
        _conv_silu_slab(src, s, raw_ref, act_ref, cw_ref, cb_ref)

    dt = _softplus(dt_ref[...] + dtb_ref[...])
    dA = dt * (-jnp.exp(alog_ref[...]))
    row = lax.broadcasted_iota(jnp.int32, (Q, Q), 0)
    col = lax.broadcasted_iota(jnp.int32, (Q, Q), 1)
    causal = row >= col
    tri = jnp.where(causal, 1.0, 0.0).astype(F32)
    cs = jnp.dot(tri, dA, precision=HIGHEST, preferred_element_type=F32)
    csT = cs.T
    dtT = dt.T
    dt_decay_end = dt * jnp.exp(cs[Q - 1:Q, :] - cs)
    chunk_decay = jnp.broadcast_to(jnp.exp(csT[0:n_heads, Q - 1:Q]), (n_heads, N))

    lane_lo = lax.broadcasted_iota(jnp.int32, (Q, 2 * P), 1) < P

    def pair_cols(arr, q):
        a0 = jnp.broadcast_to(arr[:, 2 * q:2 * q + 1], (Q, 2 * P))
        a1 = jnp.broadcast_to(arr[:, 2 * q + 1:2 * q + 2], (Q, 2 * P))
        return jnp.where(lane_lo, a0, a1)

    for g in range(SSM_GROUPS):
        Bg = act_ref[n_xs + g].astype(BF16)
        Cg = act_ref[n_xs + n_bs + g].astype(BF16)
        CB = lax.dot_general(Cg, Bg, NT_DIMS, preferred_element_type=F32)
        Hg = h_ref[g * hpg:(g + 1) * hpg].reshape(gw, N)
        y_off = lax.dot_general(Cg, Hg.astype(BF16), NT_DIMS, preferred_element_type=F32)

        ys = []
        xds = []
        for qq in range(hpg // 2):
            q = g * (hpg // 2) + qq
            lo, hi = q * 2 * P, (q + 1) * 2 * P
            xs = act_ref[q]
            xds.append((xs * pair_cols(dt_decay_end, q)).astype(BF16))
            cs_cols = [jnp.broadcast_to(cs[:, hh:hh + 1], (Q, Q)) for hh in (2 * q, 2 * q + 1)]
            atts = []
            for hh, cs_col in zip((2 * q, 2 * q + 1), cs_cols):
                seg = cs_col - csT[hh:hh + 1, :]
                decay = jnp.exp(jnp.where(causal, seg, -jnp.inf))
                atts.append(((CB * decay) * dtT[hh:hh + 1, :]).astype(BF16))
            att = jnp.concatenate(atts, axis=1)
            xbd = jnp.concatenate([jnp.where(lane_lo, xs, 0.0).astype(BF16),
                                   jnp.where(lane_lo, 0.0, xs).astype(BF16)], axis=0)
            y = jnp.dot(att, xbd, preferred_element_type=F32)
            y = y + y_off[:, qq * 2 * P:(qq + 1) * 2 * P] * jnp.exp(jnp.where(lane_lo, *cs_cols))
            y = y + xs * dexp_ref[:, lo:hi]
            ys.append(y * _silu(z_ref[:, lo:hi]))

        ssq = ys[0] * ys[0]
        for y in ys[1:]:
            ssq = ssq + y * y
        scale = lax.rsqrt(jnp.sum(ssq, axis=-1, keepdims=True) / gw + EPS)
        for qq, y in enumerate(ys):
            lo = g * gw + qq * 2 * P
            y_ref[:, lo:lo + 2 * P] = ((y * scale) * gn_ref[:, lo:lo + 2 * P]).astype(y_ref.dtype)

        S = lax.dot_general(jnp.concatenate(xds, axis=1), Bg, TN_DIMS, preferred_element_type=F32)
        for r in range(hpg):
            hh = g * hpg + r
            h_ref[hh] = h_ref[hh] * chunk_decay[hh:hh + 1, :] + S[r * P:(r + 1) * P, :]

    @pl.when(c == pl.num_programs(1) - 1)
    def _():
        hout_ref[0] = h_ref[...]


def _ssd_prompt(zxbc, dt, batch, conv_w, conv_b, dt_bias, a_log, d_skip, gate_norm):
    M = zxbc.shape[0]
    L = M // batch
    H = dt_bias.shape[0]
    d_inner = H * SSM_HEAD_DIM
    bc = SSM_GROUPS * SSM_STATE
    conv_dim = d_inner + 2 * bc
    Q = SSD_CHUNK
    nc = L // Q
    assert L % Q == 0 and d_inner % bc == 0
    xb = d_inner // bc
    row_map = lambda col: (lambda b, c: (b * nc + c, col))
    const = lambda b, c: (0, 0)
    d_exp = jnp.repeat(d_skip.astype(F32), SSM_HEAD_DIM).reshape(1, d_inner)
    assert H <= LANES and SSM_STATE == LANES and 2 * SSM_HEAD_DIM == LANES
    lane_pad = lambda v: jnp.pad(v.reshape(1, H), ((0, 0), (0, LANES - H)))
    n_slabs = conv_dim // LANES
    cw_slabs = conv_w.reshape(CONV_W, n_slabs, LANES).transpose(1, 0, 2)
    cb_slabs = conv_b.reshape(n_slabs, 1, LANES)
    const3 = lambda b, c: (0, 0, 0)
    y, h_final = pl.pallas_call(
        _ssd_kernel,
        grid=(batch, nc),
        in_specs=[pl.BlockSpec((Q, d_inner), row_map(0)),
                  pl.BlockSpec((Q, d_inner), row_map(1)),
                  pl.BlockSpec((Q, bc), row_map(2 * xb)),
                  pl.BlockSpec((Q, bc), row_map(2 * xb + 1)),
                  pl.BlockSpec((Q, LANES), row_map(0)),
                  pl.BlockSpec((n_slabs, CONV_W, LANES), const3),
                  pl.BlockSpec((n_slabs, 1, LANES), const3),
                  pl.BlockSpec((1, LANES), const),
                  pl.BlockSpec((1, LANES), const),
                  pl.BlockSpec((1, d_inner), const),
                  pl.BlockSpec((1, d_inner), const)],
        out_specs=[pl.BlockSpec((Q, d_inner), row_map(0)),
                   pl.BlockSpec((1, H, SSM_HEAD_DIM, SSM_STATE), lambda b, c: (b, 0, 0, 0))],
        out_shape=[jax.ShapeDtypeStruct((M, d_inner), BF16),
                   jax.ShapeDtypeStruct((batch, H, SSM_HEAD_DIM, SSM_STATE), F32)],
        scratch_shapes=[pltpu.VMEM((H, SSM_HEAD_DIM, SSM_STATE), F32),
                        pltpu.VMEM((n_slabs, Q + CONV_TAIL, LANES), F32),
                        pltpu.VMEM((n_slabs, Q, LANES), F32)],
        compiler_params=_params("parallel", "arbitrary"),
        name="ssd_prompt",
    )(zxbc, zxbc, zxbc, zxbc, dt, cw_slabs, cb_slabs,
      lane_pad(dt_bias), lane_pad(a_log), d_exp, gate_norm.reshape(1, d_inner))
    return y, h_final


def _ssd_step_kernel(zxbc_ref, dt_ref, cs_ref, h0_ref, cw_ref, cb_ref, dtb_ref, alog_ref,
                     dexp_ref, gn_ref, y_ref, cso_ref, ho_ref):
    P = SSM_HEAD_DIM
    N = SSM_STATE
    n_heads = h0_ref.shape[2]
    d_inner = n_heads * P
    hpg = n_heads // SSM_GROUPS
    gw = hpg * P
    conv_dim = cw_ref.shape[1]

    raw = zxbc_ref[0, :, d_inner:d_inner + conv_dim]
    prev = cs_ref[0, 0]
    acc = cb_ref[...]
    for k in range(CONV_W - 1):
        acc = acc + prev[k:k + 1, :] * cw_ref[k:k + 1, :]
    acc = acc + raw * cw_ref[CONV_W - 1:CONV_W, :]
    xbc = _silu(acc)
    cso_ref[0, 0, 0:CONV_W - 2, :] = prev[1:CONV_W - 1, :]
    cso_ref[0, 0, CONV_W - 2:CONV_W - 1, :] = raw

    dt = _softplus(dt_ref[0, :, 0:n_heads] + dtb_ref[...])
    decay = jnp.exp(dt * (-jnp.exp(alog_ref[...])))
    z = zxbc_ref[0, :, 0:d_inner]

    eye = (lax.broadcasted_iota(jnp.int32, (P, P), 0) == lax.broadcasted_iota(jnp.int32, (P, P), 1))
    ys = []
    for g in range(SSM_GROUPS):
        Bg = xbc[:, d_inner + g * N:d_inner + (g + 1) * N]
        Cg = xbc[:, d_inner + SSM_GROUPS * N + g * N:d_inner + SSM_GROUPS * N + (g + 1) * N]
        Bb = jnp.broadcast_to(Bg, (P, N))
        for r in range(hpg):
            hh = g * hpg + r
            xs = xbc[:, hh * P:(hh + 1) * P]
            xdt = xs * dt[:, hh:hh + 1]
            xdiag = jnp.where(eye, jnp.broadcast_to(xdt, (P, P)), 0.0)
            outer = jnp.dot(xdiag, Bb, precision=HIGHEST, preferred_element_type=F32)
            ho_ref[0, 0, hh] = h0_ref[0, 0, hh] * decay[:, hh:hh + 1] + outer
        Hg = ho_ref[0, 0, g * hpg:(g + 1) * hpg].reshape(gw, N)
        yg = lax.dot_general(Cg, Hg, NT_DIMS, precision=HIGHEST, preferred_element_type=F32)
        lo, hi = g * gw, (g + 1) * gw
        yg = yg + xbc[:, lo:hi] * dexp_ref[:, lo:hi]
        yg = yg * _silu(z[:, lo:hi])
        yg = yg * lax.rsqrt(jnp.mean(yg * yg, axis=-1, keepdims=True) + EPS)
        ys.append(yg * gn_ref[:, lo:hi])
    y_ref[0] = jnp.concatenate(ys, axis=1).astype(y_ref.dtype)


def _ssd_step(zxbc, dt, conv_state, ssm_state, conv_w, conv_b, dt_bias, a_log, d_skip, gate_norm):
    B = zxbc.shape[0]
    H = dt_bias.shape[0]
    d_inner = H * SSM_HEAD_DIM
    conv_dim = conv_w.shape[1]
    const = lambda b: (0, 0)
    d_exp = jnp.repeat(d_skip.astype(F32), SSM_HEAD_DIM).reshape(1, d_inner)
    cs4 = conv_state.reshape(1, B, CONV_W - 1, conv_dim)
    h5 = ssm_state.reshape(1, B, H, SSM_HEAD_DIM, SSM_STATE)
    return pl.pallas_call(
        _ssd_step_kernel,
        grid=(B,),
        in_specs=[pl.BlockSpec((1, 1, zxbc.shape[1]), lambda b: (b, 0, 0)),
                  pl.BlockSpec((1, 1, dt.shape[1]), lambda b: (b, 0, 0)),
                  pl.BlockSpec((1, 1, CONV_W - 1, conv_dim), lambda b: (0, b, 0, 0)),
                  pl.BlockSpec((1, 1, H, SSM_HEAD_DIM, SSM_STATE), lambda b: (0, b, 0, 0, 0)),
                  pl.BlockSpec((CONV_W, conv_dim), const),
                  pl.BlockSpec((1, conv_dim), const),
                  pl.BlockSpec((1, H), const),
                  pl.BlockSpec((1, H), const),
                  pl.BlockSpec((1, d_inner), const),
                  pl.BlockSpec((1, d_inner), const)],
        out_specs=[pl.BlockSpec((1, 1, d_inner), lambda b: (b, 0, 0)),
                   pl.BlockSpec((1, 1, CONV_W - 1, conv_dim), lambda b: (0, b, 0, 0)),
                   pl.BlockSpec((1, 1, H, SSM_HEAD_DIM, SSM_STATE), lambda b: (0, b, 0, 0, 0))],
        out_shape=[jax.ShapeDtypeStruct((B, 1, d_inner), F32),
                   jax.ShapeDtypeStruct(cs4.shape, F32),
                   jax.ShapeDtypeStruct(h5.shape, F32)],
        compiler_params=_params("arbitrary"),
        name="ssd_step",
    )(zxbc.reshape(B, 1, -1), dt.reshape(B, 1, -1), cs4, h5, conv_w, conv_b.reshape(1, conv_dim),
      dt_bias.reshape(1, H), a_log.reshape(1, H), d_exp, gate_norm.reshape(1, d_inner))


def _attn_kernel(q0_ref, q1_ref, q2_ref, k_ref, v_ref, o_ref, on_ref, lse_ref, qs_ref, ks_ref, vs_ref):
    L = k_ref.shape[0]
    QB = ATT_BLOCK
    KW = QB + DIL_SLOTS
    P = ATT_PRESTRIDE
    p_shift = P.bit_length() - 1
    scale = HEAD_DIM ** -0.5
    e = lax.broadcasted_iota(jnp.int32, (QB, KW), 0) - lax.broadcasted_iota(jnp.int32, (QB, KW), 1)

    staged = {rate: rate > P for rate in DIL_RATES}
    for src_ref, dst_ref, needed in ((q2_ref, qs_ref, staged[DIL_RATES[2]]), (k_ref, ks_ref, any(staged.values())),
                                     (v_ref, vs_ref, any(staged.values()))):
        if needed:
            def stage(i, carry, src_ref=src_ref, dst_ref=dst_ref):
                cp = i & (P - 1)
                t = i >> p_shift
                dst = pl.multiple_of(cp * (L // P) + t * QB, QB)
                dst_ref[pl.ds(dst, QB), :] = src_ref[pl.ds(cp + P * QB * t, QB, stride=P), :]
                return carry
            lax.fori_loop(0, L // QB, stage, 0, unroll=4)

    for g, (q_ref, rate) in enumerate(zip((q0_ref, q1_ref, q2_ref), DIL_RATES)):
        shift = rate.bit_length() - 1

        def unit(n, carry, g=g, q_ref=q_ref, rate=rate, shift=shift):
            c = n & (rate - 1)
            u0 = (n >> shift) * QB
            v0 = jnp.maximum(u0 - DIL_SLOTS, 0)
            q_rows = pl.ds(c + rate * u0, QB, stride=rate)
            if staged[rate]:
                base = (c & (P - 1)) * (L // P) + (c >> p_shift)
                q = qs_ref[pl.ds(base + (rate // P) * u0, QB, stride=rate // P), :].astype(BF16)
                k_rows = pl.ds(base + (rate // P) * v0, KW, stride=rate // P)
                k = ks_ref[k_rows, :].astype(BF16)
                v = vs_ref[k_rows, :].astype(BF16)
            else:
                k_rows = pl.ds(c + rate * v0, KW, stride=rate)
                q = q_ref[q_rows, :].astype(BF16)
                k = k_ref[k_rows, :].astype(BF16)
                v = v_ref[k_rows, :].astype(BF16)
            s = lax.dot_general(q, k, NT_DIMS, preferred_element_type=F32) * scale
            d = e + (u0 - v0)
            s = jnp.where(d >= 0, s, -jnp.inf)
            s = jnp.where(d <= DIL_SLOTS, s, -jnp.inf)
            m = s.max(axis=1, keepdims=True)
            p = jnp.exp(s - m)
            l = p.sum(axis=1, keepdims=True)
            o = jnp.dot(p.astype(BF16), v, preferred_element_type=F32)
            on_ref[g, q_rows, :] = o / l
            lse_ref[g, q_rows, :] = jnp.broadcast_to(m + jnp.log(l), (QB, HEAD_DIM))
            return carry

        lax.fori_loop(0, L // QB, unit, 0, unroll=ATT_UNROLL)

    def mix(i, carry):
        rows = pl.ds(pl.multiple_of(i * QB, QB), QB)
        lses = [lse_ref[g, rows, :] for g in range(len(DIL_RATES))]
        m = functools.reduce(jnp.maximum, lses)
        ws = [jnp.exp(x - m) for x in lses]
        num = functools.reduce(jnp.add, [w * on_ref[g, rows, :] for g, w in enumerate(ws)])
        o_ref[rows, :] = (num / functools.reduce(jnp.add, ws)).astype(o_ref.dtype)
        return carry

    lax.fori_loop(0, L // QB, mix, 0, unroll=2)


def _attn_prompt(q, kv, batch):
    M = q.shape[0]
    L = M // batch
    n_grp = len(DIL_RATES)
    assert all(r & (r - 1) == 0 and L % (ATT_BLOCK * r) == 0 and L >= r * (ATT_BLOCK + DIL_SLOTS)
               for r in DIL_RATES)
    qspec = lambda g: pl.BlockSpec((L, HEAD_DIM), lambda b, h: (b, g * KV_HEADS + h))
    kspec = pl.BlockSpec((L, HEAD_DIM), lambda b, h: (b, h))
    vspec = pl.BlockSpec((L, HEAD_DIM), lambda b, h: (b, KV_HEADS + h))
    return pl.pallas_call(
        _attn_kernel,
        grid=(batch, KV_HEADS),
        in_specs=[qspec(0), qspec(1), qspec(2), kspec, vspec],
        out_specs=pl.BlockSpec((L, HEAD_DIM), lambda b, h: (b, h)),
        out_shape=jax.ShapeDtypeStruct((M, KV_HEADS * HEAD_DIM), BF16),
        scratch_shapes=[pltpu.VMEM((n_grp, L, HEAD_DIM), F32), pltpu.VMEM((n_grp, L, HEAD_DIM), F32)]
        + [pltpu.VMEM((L, HEAD_DIM), F32)] * 3,
        compiler_params=_params("parallel", "arbitrary"),
        name="attn_prompt",
    )(q, q, q, kv, kv)


def _bf16_round(a):
    return a.astype(BF16).astype(F32)


def _attn_step_kernel(q_ref, *refs):
    n = len(DIL_RATES)
    k_refs, v_refs = refs[0:n], refs[n:2 * n]
    kn_ref, vn_ref, o_ref = refs[2 * n:]
    scale = HEAD_DIM ** -0.5
    kn = _bf16_round(kn_ref[0, 0])
    vn = _bf16_round(vn_ref[0, 0])
    scores, new_scores = [], []
    for g, k_ref in enumerate(k_refs):
        qg = _bf16_round(q_ref[0, g])
        kg = _bf16_round(k_ref[0, :, 0])
        scores.append(jnp.sum(kg * qg[None], axis=-1, keepdims=True) * scale)
        new_scores.append(jnp.sum(kn * qg, axis=-1, keepdims=True) * scale)
    m = functools.reduce(jnp.maximum, [s.max(axis=0) for s in scores] + new_scores)
    l = jnp.zeros_like(m)
    o = jnp.zeros((KV_HEADS, HEAD_DIM), F32)
    for s, s_new, v_ref in zip(scores, new_scores, v_refs):
        p = jnp.exp(s - m[None])
        p_new = jnp.exp(s_new - m)
        l = l + p.sum(axis=0) + p_new
        o = o + (_bf16_round(p) * _bf16_round(v_ref[0, :, 0])).sum(axis=0) + _bf16_round(p_new) * vn
    o_ref[0, 0] = (o / l).astype(o_ref.dtype)


def _attn_step(q, cache_k, cache_v, k_new, v_new):
    B, T = cache_k.shape[0], cache_k.shape[1]
    S = DIL_SLOTS
    assert T == S * max(DIL_RATES), "every slot of every dilation group lies inside the cached window"
    qg = q.reshape(B, len(DIL_RATES), KV_HEADS, HEAD_DIM)
    row_shape = (B, 1, KV_HEADS, HEAD_DIM)
    row_spec = pl.BlockSpec((1, 1, KV_HEADS, HEAD_DIM), lambda b: (b, 0, 0, 0))
    args = [qg]
    in_specs = [pl.BlockSpec((1, len(DIL_RATES), KV_HEADS, HEAD_DIM), lambda b: (b, 0, 0, 0))]
    for cache in (cache_k, cache_v):
        for rate in DIL_RATES:
            args.append(cache.reshape(B, T // rate, rate, KV_HEADS, HEAD_DIM))
            in_specs.append(pl.BlockSpec((1, S, 1, KV_HEADS, HEAD_DIM),
                                         lambda b, blk=T // rate // S - 1: (b, blk, 0, 0, 0)))
    args += [k_new.reshape(row_shape), v_new.reshape(row_shape)]
    in_specs += [row_spec, row_spec]
    o = pl.pallas_call(
        _attn_step_kernel,
        grid=(B,),
        in_specs=in_specs,
        out_specs=row_spec,
        out_shape=jax.ShapeDtypeStruct(row_shape, F32),
        compiler_params=_params("parallel"),
        name="attn_step",
    )(*args)
    return o.reshape(B, KV_HEADS * HEAD_DIM)


def _rope_tables(pos):
    half = HEAD_DIM // 2
    inv = jnp.power(jnp.float32(ROPE_THETA), -jnp.arange(half, dtype=jnp.float32) / half)
    ang = pos.astype(jnp.float32)[:, None] * inv[None, :]
    cos = jnp.cos(ang)
    sin = jnp.sin(ang)
    return jnp.concatenate([cos, cos], axis=1), jnp.concatenate([-sin, sin], axis=1)


def kernel(x_prompt, x_sample, state_conv, state_ssm, cache_k, cache_v, a_norm, a_w_in, a_conv_w, a_conv_b,
           a_dt_bias, a_log, a_d, a_gate_norm, a_w_out, kv_norm, w_kv, b_norm, b_w_q, b_w_o, ffn_norm,
           ffn_w_gu, ffn_w_down, final_norm):
    assert a_norm.shape[0] == 1 and b_norm.shape[0] == 1, "one Mamba-2 layer followed by one attention layer"
    Bp, Lp, D = x_prompt.shape
    Bs, Ls, _ = x_sample.shape
    assert Ls == 1 and Bs == SAMPLE_ROWS, "sample group decodes one token for SAMPLE_ROWS sequences"
    n_heads = a_dt_bias.shape[1]
    d_inner = a_w_out.shape[1]
    conv_dim = a_conv_w.shape[2]
    kv_dim = KV_HEADS * HEAD_DIM
    w_dt = jnp.pad(a_w_in[0][:, d_inner + conv_dim:], ((0, 0), (0, LANES - n_heads))).astype(BF16)
    ssd_w = (a_conv_w[0], a_conv_b[0], a_dt_bias[0], a_log[0], a_d[0], a_gate_norm[0])
    w_kv = w_kv[None]

    cos_p, sin_p = _rope_tables(jnp.arange(Lp, dtype=jnp.int32))
    cos_s, sin_s = _rope_tables(PAST_LEN + jnp.arange(Ls, dtype=jnp.int32))
    rope = ((cos_p, sin_p, Lp), (jnp.broadcast_to(cos_s, (Bs, HEAD_DIM)), jnp.broadcast_to(sin_s, (Bs, HEAD_DIM))))

    xp0 = x_prompt.reshape(Bp * Lp, D)
    xs0 = x_sample.reshape(Bs, D)

    zxbc_p, zxbc_s, dt_p, dt_s, w_out_bf, w_gu0_bf, w_down0_bf = _proj(
        xp0, xs0, [_seg(a_w_in.astype(BF16), n_cols=d_inner + conv_dim)], gains=[a_norm[0]], aux_w=w_dt, tn=1024,
        casts=[(a_w_out, 0), (ffn_w_gu, 0), (ffn_w_down, 0)])
    y_p, p_ssm = _ssd_prompt(zxbc_p, dt_p, Bp, *ssd_w)
    p_conv = zxbc_p.reshape(Bp, Lp, -1)[:, Lp - (CONV_W - 1):, d_inner:]
    y_s, s_conv, s_ssm = _ssd_step(zxbc_s, dt_s, state_conv[0], state_ssm[0], *ssd_w)
    *x1, w_gu1_bf = _proj(y_p, y_s.reshape(Bs, d_inner), [_seg(w_out_bf)], res=(xp0, xs0), casts=[(ffn_w_gu, 1)],
                          tm=256, tn=D)
    *h, w_kv_bf, w_q_bf, w_o_bf = _gateup(*x1, ffn_norm[0], w_gu0_bf, 0, casts=[(w_kv, 0), (b_w_q, 0), (b_w_o, 0)])
    x2 = _proj(*h, [_seg(w_down0_bf)], res=x1, tm=256, tn=D)

    kv_p, kv_s, q_p, q_s = _proj(*x2, [_seg(w_kv_bf, rope_cols=kv_dim), _seg(w_q_bf, rope_cols=b_w_q.shape[2])],
                                 gains=[kv_norm, b_norm[0]], rope=rope, tm=256, tn=None)
    o_p = _attn_prompt(q_p, kv_p, Bp)
    o_s = _attn_step(q_s, cache_k, cache_v, kv_s[:, :kv_dim], kv_s[:, kv_dim:])
    x3 = _proj(o_p, o_s, [_seg(w_o_bf)], res=x2, tn=1024)
    *h, w_down1_bf = _gateup(*x3, ffn_norm[1], w_gu1_bf, 0, casts=[(ffn_w_down, 1)])
    y_prompt, y_sample = _proj(*h, [_seg(w_down1_bf)], res=x3, post_gain=final_norm, tm=256, tn=D)
    y_prompt = y_prompt.reshape(Bp, Lp, D)
    y_sample = y_sample.reshape(Bs, Ls, D)
    keep = min(DIL_SLOTS * max(DIL_RATES), Lp)
    p_kv = kv_p.reshape(Bp, Lp, 2 * KV_HEADS, HEAD_DIM)[:, Lp - keep:]
    s_kv = kv_s.reshape(Bs, Ls, 2 * KV_HEADS, HEAD_DIM)
    return (y_prompt, y_sample, p_conv[None], p_ssm[None], p_kv[:, :, :KV_HEADS], p_kv[:, :, KV_HEADS:],
            s_conv, s_ssm, s_kv[:, :, :KV_HEADS], s_kv[:, :, KV_HEADS:])
```

```python
import functools

import jax
import jax.numpy as jnp
from jax import lax
from jax.experimental import pallas as pl
from jax.experimental.pallas import tpu as pltpu

F32 = jnp.float32
BF16 = jnp.bfloat16
HIGHEST = lax.Precision.HIGHEST

EPS = 1e-6
ROPE_THETA = 10000.0
SSD_CHUNK = 128
SSM_HEAD_DIM = 64
SSM_STATE = 128
SSM_GROUPS = 8
CONV_W = 4
HEAD_DIM = 128
KV_HEADS = 8
DIL_RATES = (1, 4, 16)
DIL_SLOTS = 128
ATT_BLOCK = 128
ATT_UNROLL = 8
ATT_PRESTRIDE = 4
PAST_LEN = 16384

V7X_VMEM_BYTES = 64 * 1024 * 1024
VMEM_LIMIT = V7X_VMEM_BYTES - 8 * 1024 * 1024
LANES = 128
CONV_TAIL = 8
SAMPLE_ROWS = 8

NT_DIMS = (((1,), (1,)), ((), ()))
TN_DIMS = (((0,), (0,)), ((), ()))


def _params(*sem):
    return pltpu.CompilerParams(dimension_semantics=sem, vmem_limit_bytes=VMEM_LIMIT)


def _silu(x):
    h = 0.5 * x
    return h + h * jnp.tanh(h)


def _softplus(x):
    return jnp.maximum(x, 0.0) + jnp.log1p(jnp.exp(-jnp.abs(x)))


def _rms_scale(x):
    return lax.rsqrt(jnp.mean(x * x, axis=-1, keepdims=True) + EPS)


def _normed(x_ref, g_ref):
    x = x_ref[...]
    return ((x * _rms_scale(x)) * g_ref[...]).astype(BF16)


def _sample_block(i, j):
    return (i, j)


def _cast_specs(casts, n_steps, nj):
    BF16_ROWS = 16
    args, in_specs, out_shape, out_specs, tiles = [], [], [], [], []
    for src, layer in casts:
        R, C = src.shape[1:]
        rows = next(r for r in range(BF16_ROWS, R + 1, BF16_ROWS) if R % r == 0 and R // r <= n_steps)
        nt = R // rows
        tile = lambda i, j, nt=nt: jnp.minimum(i * nj + j, nt - 1)
        args.append(src)
        in_specs.append(pl.BlockSpec((None, rows, C), lambda i, j, layer=layer, tile=tile: (layer, tile(i, j), 0)))
        out_shape.append(jax.ShapeDtypeStruct((1, R, C), BF16))
        out_specs.append(pl.BlockSpec((None, rows, C), lambda i, j, tile=tile: (0, tile(i, j), 0)))
        tiles.append(nt)
    return args, in_specs, out_shape, out_specs, tuple(tiles)


def _run_casts(src_refs, dst_refs, tiles):
    step = pl.program_id(0) * pl.num_programs(1) + pl.program_id(1)
    for src_ref, dst_ref, nt in zip(src_refs, dst_refs, tiles):
        @pl.when(step < nt)
        def _(src_ref=src_ref, dst_ref=dst_ref):
            dst_ref[...] = src_ref[...].astype(BF16)


def _proj_kernel(*refs, segs, resident, norm, has_rope, has_res, has_post, has_aux, cast_tiles):
    it = iter(refs)
    xp_ref, xs_ref = next(it), next(it)
    g_refs = [next(it) for _ in range(norm)]
    w_refs = [next(it) for _ in segs]
    auxw_ref = next(it) if has_aux else None
    rope_p = (next(it), next(it)) if has_rope else None
    rope_s = (next(it), next(it)) if has_rope else None
    resp_ref, ress_ref = (next(it), next(it)) if has_res else (None, None)
    pg_ref = next(it) if has_post else None
    cast_srcs = [next(it) for _ in cast_tiles]
    out_refs =[(next(it), next(it)) for _ in segs]
    auxp_ref, auxs_ref = (next(it), next(it)) if has_aux else (None, None)
    cast_dsts = [next(it) for _ in cast_tiles]
    xnp_ref, xns_ref = (next(it), next(it)) if norm else (None, None)
    i = pl.program_id(0)
    j = pl.program_id(1)
    _run_casts(cast_srcs, cast_dsts, cast_tiles)

    if norm:
        @pl.when(j == 0)
        def _():
            for s, g_ref in enumerate(g_refs):
                xnp_ref[s] = _normed(xp_ref, g_ref)

        @pl.when((i == 0) & (j == 0))
        def _():
            for s, g_ref in enumerate(g_refs):
                xns_ref[s] = _normed(xs_ref, g_ref)

    def lhs_p(s):
        return xnp_ref[min(s, norm - 1)] if norm else xp_ref[...]

    def lhs_s(s):
        return xns_ref[min(s, norm - 1)] if norm else xs_ref[...].astype(BF16)

    def emit(lhs, w_ref, res_ref, o_ref, rope, lo, n_tiles, n_rope):
        acc = jnp.dot(lhs, w_ref[...].astype(BF16), preferred_element_type=F32)
        if has_res:
            acc = acc + res_ref[...]
        if has_post:
            acc = (acc * _rms_scale(acc)) * pg_ref[...]
        heads = acc.shape[1] // HEAD_DIM

        def store(rope_heads):
            if rope_heads:
                cos = rope[0][...]
                sin = rope[1][...]
            for h in range(rope_heads):
                a = acc[:, h * HEAD_DIM:(h + 1) * HEAD_DIM]
                o_ref[:, h * HEAD_DIM:(h + 1) * HEAD_DIM] = (
                    a * cos + pltpu.roll(a, HEAD_DIM // 2, 1) * sin).astype(o_ref.dtype)
            if rope_heads < heads:
                o_ref[:, rope_heads * HEAD_DIM:] = acc[:, rope_heads * HEAD_DIM:].astype(o_ref.dtype)

        if resident:
            store(n_rope)
        elif n_rope in (0, n_tiles):
            store(heads if n_rope else 0)
        else:
            pl.when(j < lo + n_rope)(lambda: store(heads))
            pl.when(j >= lo + n_rope)(lambda: store(0))

    for s, (lo, n_tiles, n_rope) in enumerate(segs):
        def segment(s=s, lo=lo, n_tiles=n_tiles, n_rope=n_rope):
            emit(lhs_p(s), w_refs[s], resp_ref, out_refs[s][0], rope_p, lo, n_tiles, n_rope)

            @pl.when(i == 0)
            def _():
                emit(lhs_s(s), w_refs[s], ress_ref, out_refs[s][1], rope_s, lo, n_tiles, n_rope)

        if resident or len(segs) == 1:
            segment()
        else:
            pl.when((j >= lo) & (j < lo + n_tiles))(segment)

    @pl.when(i > 0)
    def _():
        for _, os_ref in out_refs:
            os_ref[...] = jnp.zeros_like(os_ref)

    if has_aux:
        @pl.when(j == pl.num_programs(1) - 1)
        def _():
            auxp_ref[...] = jnp.dot(lhs_p(0), auxw_ref[...], preferred_element_type=F32)

            @pl.when(i == 0)
            def _():
                auxs_ref[...] = jnp.dot(lhs_s(0), auxw_ref[...], preferred_element_type=F32)

            @pl.when(i > 0)
            def _():
                auxs_ref[...] = jnp.zeros_like(auxs_ref)


def _seg(w, layer=0, col0=0, n_cols=None, rope_cols=0):
    return (w, layer, col0, w.shape[-1] - col0 if n_cols is None else n_cols, rope_cols)


def _proj(xp, xs, ws, *, gains=None, aux_w=None, rope=None, res=None, post_gain=None, casts=(), out_dtype=F32,
          tm=1024, tn=512):
    Mp, K = xp.shape
    S = SAMPLE_ROWS
    assert xs.shape == (S, K) and Mp % tm == 0
    norm = 0 if gains is None else len(gains)
    resident = tn is None
    segs, widths, lo = [], [], 0
    for w, layer, col0, n_cols, rope_cols in ws:
        tw = n_cols if resident else tn
        rope_unit = HEAD_DIM if resident else tw
        assert w.ndim == 3 and n_cols % tw == 0 and col0 % tw == 0 and rope_cols % rope_unit == 0
        segs.append((lo, n_cols // tw, rope_cols // rope_unit))
        widths.append(tw)
        lo += 0 if resident else n_cols // tw
    nj, ni = (1 if resident else lo), Mp // tm
    has_rope = any(r for _, _, r in segs)
    assert not (has_rope and res is not None) and (rope is not None) == has_rope
    row = lambda i, j: (i, 0)
    const = lambda i, j: (0, 0)
    args = [xp, xs]
    in_specs = [pl.BlockSpec((tm, K), row), pl.BlockSpec((S, K), const)]
    if norm:
        assert len(gains) in (1, len(ws))
        args += [g.reshape(1, K) for g in gains]
        in_specs += [pl.BlockSpec((1, K), const)] * len(gains)
    for (w, layer, col0, _, _), (lo, n_tiles, _), tw in zip(ws, segs, widths):
        args.append(w)
        in_specs.append(pl.BlockSpec(
            (None, K, tw), lambda i, j, layer=layer, lo=lo, n=n_tiles, c0=col0 // tw:
            (layer, 0, c0 + jnp.clip(j - lo, 0, n - 1)),
            pipeline_mode=pl.Buffered(1) if nj == 1 else None))
    if aux_w is not None:
        args.append(aux_w)
        in_specs.append(pl.BlockSpec(aux_w.shape, const))
    if has_rope:
        (cos_p, sin_p, rows_per_seq), (cos_s, sin_s) = rope
        nseq = rows_per_seq // tm
        args += [cos_p, sin_p, cos_s, sin_s]
        in_specs += [pl.BlockSpec((tm, HEAD_DIM), lambda i, j: (i % nseq, 0))] * 2
        in_specs += [pl.BlockSpec((S, HEAD_DIM), const)] * 2
    assert (res is None and post_gain is None) or len(ws) == 1
    if res is not None:
        args += list(res)
        in_specs += [pl.BlockSpec((tm, widths[0]), lambda i, j: (i, j)),
                     pl.BlockSpec((S, widths[0]), lambda i, j: (0, j))]
    if post_gain is not None:
        assert nj == 1
        args.append(post_gain.reshape(1, widths[0]))
        in_specs.append(pl.BlockSpec((1, widths[0]), const))
    out_shape, out_specs = [], []
    for (lo, n_tiles, _), tw in zip(segs, widths):
        col = lambda i, j, lo=lo, n=n_tiles: (i, jnp.clip(j - lo, 0, n - 1))
        out_shape += [jax.ShapeDtypeStruct((Mp, n_tiles * tw), out_dtype),
                      jax.ShapeDtypeStruct((ni * S, n_tiles * tw), F32)]
        out_specs += [pl.BlockSpec((tm, tw), col), pl.BlockSpec((S, tw), col)]
    if aux_w is not None:
        na = aux_w.shape[1]
        out_shape += [jax.ShapeDtypeStruct((Mp, na), F32), jax.ShapeDtypeStruct((ni * S, na), F32)]
        out_specs += [pl.BlockSpec((tm, na), row), pl.BlockSpec((S, na), row)]
    n_main = len(out_shape)
    c_args, c_in, c_shape, c_out, cast_tiles = _cast_specs(casts, ni * nj, nj)
    args += c_args
    in_specs += c_in
    out_shape += c_shape
    out_specs += c_out
    outs = pl.pallas_call(
        functools.partial(_proj_kernel, segs=tuple(segs), resident=resident, norm=norm, has_rope=has_rope,
                          has_res=res is not None, has_post=post_gain is not None, has_aux=aux_w is not None,
                          cast_tiles=cast_tiles),
        grid=(ni, nj),
        in_specs=in_specs,
        out_specs=out_specs,
        out_shape=out_shape,
        scratch_shapes=[pltpu.VMEM((norm, tm, K), BF16), pltpu.VMEM((norm, S, K), BF16)] if norm else [],
        compiler_params=_params("arbitrary", "arbitrary"),
        name="proj",
    )(*args)
    return [o[:S] if n < n_main and n % 2 == 1 else o for n, o in enumerate(outs)]


def _gateup_kernel(xp_ref, xs_ref, g_ref, wg_ref, wu_ref, *refs, cast_tiles):
    n = len(cast_tiles)
    cast_srcs = refs[:n]
    op_ref, os_ref = refs[n:n + 2]
    cast_dsts = refs[n + 2:2 * n + 2]
    xnp_ref, xns_ref = refs[2 * n + 2:]
    i = pl.program_id(0)
    j = pl.program_id(1)
    _run_casts(cast_srcs, cast_dsts, cast_tiles)

    @pl.when(j == 0)
    def _():
        xnp_ref[...] = _normed(xp_ref, g_ref)

    @pl.when((i == 0) & (j == 0))
    def _():
        xns_ref[...] = _normed(xs_ref, g_ref)

    def swiglu(xn):
        g = jnp.dot(xn, wg_ref[...], preferred_element_type=F32)
        u = jnp.dot(xn, wu_ref[...], preferred_element_type=F32)
        return _silu(g) * u

    op_ref[...] = swiglu(xnp_ref[...]).astype(op_ref.dtype)

    @pl.when(i == 0)
    def _():
        os_ref[...] = swiglu(xns_ref[...])

    @pl.when(i > 0)
    def _():
        os_ref[...] = jnp.zeros_like(os_ref)


def _gateup(xp, xs, gain, w_gu, layer, *, casts=(), tm=1024, tn=512):
    Mp, K = xp.shape
    S = SAMPLE_ROWS
    hidden = w_gu.shape[2] // 2
    assert xs.shape == (S, K) and Mp % tm == 0 and hidden % tn == 0 and w_gu.dtype == BF16
    nj = hidden // tn
    ni = Mp // tm
    c_args, c_in, c_shape, c_out, cast_tiles = _cast_specs(casts, ni * nj, nj)
    outs = pl.pallas_call(
        functools.partial(_gateup_kernel, cast_tiles=cast_tiles),
        grid=(ni, nj),
        in_specs=[pl.BlockSpec((tm, K), lambda i, j: (i, 0)),
                  pl.BlockSpec((S, K), lambda i, j: (0, 0)),
                  pl.BlockSpec((1, K), lambda i, j: (0, 0)),
                  pl.BlockSpec((None, K, tn), lambda i, j: (layer, 0, j)),
                  pl.BlockSpec((None, K, tn), lambda i, j: (layer, 0, j + nj))] + c_in,
        out_specs=[pl.BlockSpec((tm, tn), lambda i, j: (i, j)), pl.BlockSpec((S, tn), _sample_block)] + c_out,
        out_shape=[jax.ShapeDtypeStruct((Mp, hidden), BF16), jax.ShapeDtypeStruct((ni * S, hidden), F32)] + c_shape,
        scratch_shapes=[pltpu.VMEM((tm, K), BF16), pltpu.VMEM((S, K), BF16)],
        compiler_params=_params("arbitrary", "arbitrary"),
        name="gateup",
    )(xp, xs, gain.reshape(1, K), w_gu, w_gu, *c_args)
    return [outs[0], outs[1][:S]] + list(outs[2:])


def _conv_silu_slab(src, s, raw_ref, act_ref, cw_ref, cb_ref):
    Q = SSD_CHUNK
    raw_ref[s, CONV_TAIL:, :] = src
    for parity in (0, 1):
        acc = cb_ref[s]
        for k in range(CONV_W):
            first = CONV_TAIL - (CONV_W - 1) + k + parity
            acc = acc + raw_ref[s, pl.ds(first, Q // 2, stride=2), :] * cw_ref[s, k:k + 1, :]
        act_ref[s, pl.ds(parity, Q // 2, stride=2), :] = _silu(acc)
    raw_ref[s, 0:CONV_TAIL, :] = raw_ref[s, Q:Q + CONV_TAIL, :]


def _ssd_kernel(z_ref, x_ref, b_ref, c_ref, dt_ref, cw_ref, cb_ref,
                dtb_ref, alog_ref, dexp_ref, gn_ref,
                y_ref, hout_ref,
                h_ref, raw_ref, act_ref):
    Q = SSD_CHUNK
    P = SSM_HEAD_DIM
    N = SSM_STATE
    d_inner = x_ref.shape[1]
    n_heads = d_inner // P
    hpg = n_heads // SSM_GROUPS
    gw = hpg * P
    n_xs = d_inner // LANES
    n_bs = b_ref.shape[1] // LANES
    c = pl.program_id(1)

    @pl.when(c == 0)
    def _():
        h_ref[...] = jnp.zeros_like(h_ref)
        raw_ref[:, 0:CONV_TAIL, :] = jnp.zeros((raw_ref.shape[0], CONV_TAIL, LANES), F32)

    for s in range(n_xs + 2 * n_bs):
        if s < n_xs:
            src = x_ref[:, s * LANES:(s + 1) * LANES]
        elif s < n_xs + n_bs:
            src = b_ref[:, (s - n_xs) * LANES:(s - n_xs + 1) * LANES]
        else:
            src = c_ref[:, (s - n_xs - n_bs) * LANES:(s - n_xs - n_bs + 1) * LANES]
        _conv_silu_slab(src, s, raw_ref, act_ref, cw_ref, cb_ref)

    dt = _softplus(dt_ref[...] + dtb_ref[...])
    dA = dt * (-jnp.exp(alog_ref[...]))
    row = lax.broadcasted_iota(jnp.int32, (Q, Q), 0)
    col = lax.broadcasted_iota(jnp.int32, (Q, Q), 1)
    causal = row >= col
    tri = jnp.where(causal, 1.0, 0.0).astype(F32)
    cs = jnp.dot(tri, dA, precision=HIGHEST, preferred_element_type=F32)
    csT = cs.T
    dtT = dt.T
    dt_decay_end = dt * jnp.exp(cs[Q - 1:Q, :] - cs)
    chunk_decay = jnp.broadcast_to(jnp.exp(csT[0:n_heads, Q - 1:Q]), (n_heads, N))

    lane_lo = lax.broadcasted_iota(jnp.int32, (Q, 2 * P), 1) < P

    def pair_cols(arr, q):
        a0 = jnp.broadcast_to(arr[:, 2 * q:2 * q + 1], (Q, 2 * P))
        a1 = jnp.broadcast_to(arr[:, 2 * q + 1:2 * q + 2], (Q, 2 * P))
        return jnp.where(lane_lo, a0, a1)

    for g in range(SSM_GROUPS):
        Bg = act_ref[n_xs + g].astype(BF16)
        Cg = act_ref[n_xs + n_bs + g].astype(BF16)
        CB = lax.dot_general(Cg, Bg, NT_DIMS, preferred_element_type=F32)
        Hg = h_ref[g * hpg:(g + 1) * hpg].reshape(gw, N)
        y_off = lax.dot_general(Cg, Hg.astype(BF16), NT_DIMS, preferred_element_type=F32)

        ys = []
        xds = []
        for qq in range(hpg // 2):
            q = g * (hpg // 2) + qq
            lo, hi = q * 2 * P, (q + 1) * 2 * P
            xs = act_ref[q]
            xds.append((xs * pair_cols(dt_decay_end, q)).astype(BF16))
            cs_cols = [jnp.broadcast_to(cs[:, hh:hh + 1], (Q, Q)) for hh in (2 * q, 2 * q + 1)]
            atts = []
            for hh, cs_col in zip((2 * q, 2 * q + 1), cs_cols):
                seg = cs_col - csT[hh:hh + 1, :]
                decay = jnp.exp(jnp.where(causal, seg, -jnp.inf))
                atts.append(((CB * decay) * dtT[hh:hh + 1, :]).astype(BF16))
            att = jnp.concatenate(atts, axis=1)
            xbd = jnp.concatenate([jnp.where(lane_lo, xs, 0.0).astype(BF16),
                                   jnp.where(lane_lo, 0.0, xs).astype(BF16)], axis=0)
            y = jnp.dot(att, xbd, preferred_element_type=F32)
            y = y + y_off[:, qq * 2 * P:(qq + 1) * 2 * P] * jnp.exp(jnp.where(lane_lo, *cs_cols))
            y = y + xs * dexp_ref[:, lo:hi]
            ys.append(y * _silu(z_ref[:, lo:hi]))

        ssq = ys[0] * ys[0]
        for y in ys[1:]:
            ssq = ssq + y * y
        scale = lax.rsqrt(jnp.sum(ssq, axis=-1, keepdims=True) / gw + EPS)
        for qq, y in enumerate(ys):
            lo = g * gw + qq * 2 * P
            y_ref[:, lo:lo + 2 * P] = ((y * scale) * gn_ref[:, lo:lo + 2 * P]).astype(y_ref.dtype)

        S = lax.dot_general(jnp.concatenate(xds, axis=1), Bg, TN_DIMS, preferred_element_type=F32)
        for r in range(hpg):
            hh = g * hpg + r
            h_ref[hh] = h_ref[hh] * chunk_decay[hh:hh + 1, :] + S[r * P:(r + 1) * P, :]

    @pl.when(c == pl.num_programs(1) - 1)
    def _():
        hout_ref[0] = h_ref[...]


def _ssd_prompt(z, x, bc_raw, dt, batch, conv_w, conv_b, dt_bias, a_log, d_skip, gate_norm):
    M = z.shape[0]
    L = M // batch
    H = dt_bias.shape[0]
    d_inner = H * SSM_HEAD_DIM
    bc = SSM_GROUPS * SSM_STATE
    conv_dim = d_inner + 2 * bc
    Q = SSD_CHUNK
    nc = L // Q
    assert L % Q == 0 and d_inner % bc == 0
    xb = d_inner // bc
    row_map = lambda col: (lambda b, c: (b * nc + c, col))
    const = lambda b, c: (0, 0)
    d_exp = jnp.repeat(d_skip.astype(F32), SSM_HEAD_DIM).reshape(1, d_inner)
    assert H <= LANES and SSM_STATE == LANES and 2 * SSM_HEAD_DIM == LANES
    lane_pad = lambda v: jnp.pad(v.reshape(1, H), ((0, 0), (0, LANES - H)))
    n_slabs = conv_dim // LANES
    cw_slabs = conv_w.reshape(CONV_W, n_slabs, LANES).transpose(1, 0, 2)
    cb_slabs = conv_b.reshape(n_slabs, 1, LANES)
    const3 = lambda b, c: (0, 0, 0)
    y, h_final = pl.pallas_call(
        _ssd_kernel,
        grid=(batch, nc),
        in_specs=[pl.BlockSpec((Q, d_inner), row_map(0)),
                  pl.BlockSpec((Q, d_inner), row_map(0)),
                  pl.BlockSpec((Q, bc), row_map(0)),
                  pl.BlockSpec((Q, bc), row_map(1)),
                  pl.BlockSpec((Q, LANES), row_map(0)),
                  pl.BlockSpec((n_slabs, CONV_W, LANES), const3),
                  pl.BlockSpec((n_slabs, 1, LANES), const3),
                  pl.BlockSpec((1, LANES), const),
                  pl.BlockSpec((1, LANES), const),
                  pl.BlockSpec((1, d_inner), const),
                  pl.BlockSpec((1, d_inner), const)],
        out_specs=[pl.BlockSpec((Q, d_inner), row_map(0)),
                   pl.BlockSpec((1, H, SSM_HEAD_DIM, SSM_STATE), lambda b, c: (b, 0, 0, 0))],
        out_shape=[jax.ShapeDtypeStruct((M, d_inner), BF16),
                   jax.ShapeDtypeStruct((batch, H, SSM_HEAD_DIM, SSM_STATE), F32)],
        scratch_shapes=[pltpu.VMEM((H, SSM_HEAD_DIM, SSM_STATE), F32),
                        pltpu.VMEM((n_slabs, Q + CONV_TAIL, LANES), F32),
                        pltpu.VMEM((n_slabs, Q, LANES), F32)],
        compiler_params=_params("parallel", "arbitrary"),
        name="ssd_prompt",
    )(z, x, bc_raw, bc_raw, dt, cw_slabs, cb_slabs,
      lane_pad(dt_bias), lane_pad(a_log), d_exp, gate_norm.reshape(1, d_inner))
    return y, h_final


def _ssd_step_kernel(zxbc_ref, dt_ref, cs_ref, h0_ref, cw_ref, cb_ref, dtb_ref, alog_ref,
                     dexp_ref, gn_ref, y_ref, cso_ref, ho_ref):
    P = SSM_HEAD_DIM
    N = SSM_STATE
    n_heads = h0_ref.shape[2]
    d_inner = n_heads * P
    hpg = n_heads // SSM_GROUPS
    gw = hpg * P
    conv_dim = cw_ref.shape[1]

    raw = zxbc_ref[0, :, d_inner:d_inner + conv_dim]
    prev = cs_ref[0, 0]
    acc = cb_ref[...]
    for k in range(CONV_W - 1):
        acc = acc + prev[k:k + 1, :] * cw_ref[k:k + 1, :]
    acc = acc + raw * cw_ref[CONV_W - 1:CONV_W, :]
    xbc = _silu(acc)
    cso_ref[0, 0, 0:CONV_W - 2, :] = prev[1:CONV_W - 1, :]
    cso_ref[0, 0, CONV_W - 2:CONV_W - 1, :] = raw

    dt = _softplus(dt_ref[0, :, 0:n_heads] + dtb_ref[...])
    decay = jnp.exp(dt * (-jnp.exp(alog_ref[...])))
    z = zxbc_ref[0, :, 0:d_inner]

    eye = (lax.broadcasted_iota(jnp.int32, (P, P), 0) == lax.broadcasted_iota(jnp.int32, (P, P), 1))
    ys = []
    for g in range(SSM_GROUPS):
        Bg = xbc[:, d_inner + g * N:d_inner + (g + 1) * N]
        Cg = xbc[:, d_inner + SSM_GROUPS * N + g * N:d_inner + SSM_GROUPS * N + (g + 1) * N]
        Bb = jnp.broadcast_to(Bg, (P, N))
        for r in range(hpg):
            hh = g * hpg + r
            xs = xbc[:, hh * P:(hh + 1) * P]
            xdt = xs * dt[:, hh:hh + 1]
            xdiag = jnp.where(eye, jnp.broadcast_to(xdt, (P, P)), 0.0)
            outer = jnp.dot(xdiag, Bb, precision=HIGHEST, preferred_element_type=F32)
            ho_ref[0, 0, hh] = h0_ref[0, 0, hh] * decay[:, hh:hh + 1] + outer
        Hg = ho_ref[0, 0, g * hpg:(g + 1) * hpg].reshape(gw, N)
        yg = lax.dot_general(Cg, Hg, NT_DIMS, precision=HIGHEST, preferred_element_type=F32)
        lo, hi = g * gw, (g + 1) * gw
        yg = yg + xbc[:, lo:hi] * dexp_ref[:, lo:hi]
        yg = yg * _silu(z[:, lo:hi])
        yg = yg * lax.rsqrt(jnp.mean(yg * yg, axis=-1, keepdims=True) + EPS)
        ys.append(yg * gn_ref[:, lo:hi])
    y_ref[0] = jnp.concatenate(ys, axis=1).astype(y_ref.dtype)


def _ssd_step(zxbc, dt, conv_state, ssm_state, conv_w, conv_b, dt_bias, a_log, d_skip, gate_norm):
    B = zxbc.shape[0]
    H = dt_bias.shape[0]
    d_inner = H * SSM_HEAD_DIM
    conv_dim = conv_w.shape[1]
    const = lambda b: (0, 0)
    d_exp = jnp.repeat(d_skip.astype(F32), SSM_HEAD_DIM).reshape(1, d_inner)
    cs4 = conv_state.reshape(1, B, CONV_W - 1, conv_dim)
    h5 = ssm_state.reshape(1, B, H, SSM_HEAD_DIM, SSM_STATE)
    return pl.pallas_call(
        _ssd_step_kernel,
        grid=(B,),
        in_specs=[pl.BlockSpec((1, 1, zxbc.shape[1]), lambda b: (b, 0, 0)),
                  pl.BlockSpec((1, 1, dt.shape[1]), lambda b: (b, 0, 0)),
                  pl.BlockSpec((1, 1, CONV_W - 1, conv_dim), lambda b: (0, b, 0, 0)),
                  pl.BlockSpec((1, 1, H, SSM_HEAD_DIM, SSM_STATE), lambda b: (0, b, 0, 0, 0)),
                  pl.BlockSpec((CONV_W, conv_dim), const),
                  pl.BlockSpec((1, conv_dim), const),
                  pl.BlockSpec((1, H), const),
                  pl.BlockSpec((1, H), const),
                  pl.BlockSpec((1, d_inner), const),
                  pl.BlockSpec((1, d_inner), const)],
        out_specs=[pl.BlockSpec((1, 1, d_inner), lambda b: (b, 0, 0)),
                   pl.BlockSpec((1, 1, CONV_W - 1, conv_dim), lambda b: (0, b, 0, 0)),
                   pl.BlockSpec((1, 1, H, SSM_HEAD_DIM, SSM_STATE), lambda b: (0, b, 0, 0, 0))],
        out_shape=[jax.ShapeDtypeStruct((B, 1, d_inner), F32),
                   jax.ShapeDtypeStruct(cs4.shape, F32),
                   jax.ShapeDtypeStruct(h5.shape, F32)],
        compiler_params=_params("arbitrary"),
        name="ssd_step",
    )(zxbc.reshape(B, 1, -1), dt.reshape(B, 1, -1), cs4, h5, conv_w, conv_b.reshape(1, conv_dim),
      dt_bias.reshape(1, H), a_log.reshape(1, H), d_exp, gate_norm.reshape(1, d_inner))


def _attn_kernel(q0_ref, q1_ref, q2_ref, k_ref, v_ref, o_ref, on_ref, lse_ref, qs_ref, ks_ref, vs_ref):
    L = k_ref.shape[0]
    QB = ATT_BLOCK
    KW = QB + DIL_SLOTS
    P = ATT_PRESTRIDE
    p_shift = P.bit_length() - 1
    scale = HEAD_DIM ** -0.5
    e = lax.broadcasted_iota(jnp.int32, (QB, KW), 0) - lax.broadcasted_iota(jnp.int32, (QB, KW), 1)

    staged = {rate: rate > P for rate in DIL_RATES}
    for src_ref, dst_ref, needed in ((q2_ref, qs_ref, staged[DIL_RATES[2]]), (k_ref, ks_ref, any(staged.values())),
                                     (v_ref, vs_ref, any(staged.values()))):
        if needed:
            def stage(i, carry, src_ref=src_ref, dst_ref=dst_ref):
                cp = i & (P - 1)
                t = i >> p_shift
                dst = pl.multiple_of(cp * (L // P) + t * QB, QB)
                dst_ref[pl.ds(dst, QB), :] = src_ref[pl.ds(cp + P * QB * t, QB, stride=P), :]
                return carry
            lax.fori_loop(0, L // QB, stage, 0, unroll=4)

    for g, (q_ref, rate) in enumerate(zip((q0_ref, q1_ref, q2_ref), DIL_RATES)):
        shift = rate.bit_length() - 1

        def unit(n, carry, g=g, q_ref=q_ref, rate=rate, shift=shift):
            c = n & (rate - 1)
            u0 = (n >> shift) * QB
            v0 = jnp.maximum(u0 - DIL_SLOTS, 0)
            q_rows = pl.ds(c + rate * u0, QB, stride=rate)
            if staged[rate]:
                base = (c & (P - 1)) * (L // P) + (c >> p_shift)
                q = qs_ref[pl.ds(base + (rate // P) * u0, QB, stride=rate // P), :].astype(BF16)
                k_rows = pl.ds(base + (rate // P) * v0, KW, stride=rate // P)
                k = ks_ref[k_rows, :].astype(BF16)
                v = vs_ref[k_rows, :].astype(BF16)
            else:
                k_rows = pl.ds(c + rate * v0, KW, stride=rate)
                q = q_ref[q_rows, :].astype(BF16)
                k = k_ref[k_rows, :].astype(BF16)
                v = v_ref[k_rows, :].astype(BF16)
            s = lax.dot_general(q, k, NT_DIMS, preferred_element_type=F32) * scale
            d = e + (u0 - v0)
            s = jnp.where(d >= 0, s, -jnp.inf)
            s = jnp.where(d <= DIL_SLOTS, s, -jnp.inf)
            m = s.max(axis=1, keepdims=True)
            p = jnp.exp(s - m)
            l = p.sum(axis=1, keepdims=True)
            o = jnp.dot(p.astype(BF16), v, preferred_element_type=F32)
            on_ref[g, q_rows, :] = o / l
            lse_ref[g, q_rows, :] = jnp.broadcast_to(m + jnp.log(l), (QB, HEAD_DIM))
            return carry

        lax.fori_loop(0, L // QB, unit, 0, unroll=ATT_UNROLL)

    def mix(i, carry):
        rows = pl.ds(pl.multiple_of(i * QB, QB), QB)
        lses = [lse_ref[g, rows, :] for g in range(len(DIL_RATES))]
        m = functools.reduce(jnp.maximum, lses)
        ws = [jnp.exp(x - m) for x in lses]
        num = functools.reduce(jnp.add, [w * on_ref[g, rows, :] for g, w in enumerate(ws)])
        o_ref[rows, :] = (num / functools.reduce(jnp.add, ws)).astype(o_ref.dtype)
        return carry

    lax.fori_loop(0, L // QB, mix, 0, unroll=2)


def _attn_prompt(q, kv, batch):
    M = q.shape[0]
    L = M // batch
    n_grp = len(DIL_RATES)
    assert all(r & (r - 1) == 0 and L % (ATT_BLOCK * r) == 0 and L >= r * (ATT_BLOCK + DIL_SLOTS)
               for r in DIL_RATES)
    qspec = lambda g: pl.BlockSpec((L, HEAD_DIM), lambda b, h: (b, g * KV_HEADS + h))
    kspec = pl.BlockSpec((L, HEAD_DIM), lambda b, h: (b, h))
    vspec = pl.BlockSpec((L, HEAD_DIM), lambda b, h: (b, KV_HEADS + h))
    return pl.pallas_call(
        _attn_kernel,
        grid=(batch, KV_HEADS),
        in_specs=[qspec(0), qspec(1), qspec(2), kspec, vspec],
        out_specs=pl.BlockSpec((L, HEAD_DIM), lambda b, h: (b, h)),
        out_shape=jax.ShapeDtypeStruct((M, KV_HEADS * HEAD_DIM), BF16),
        scratch_shapes=[pltpu.VMEM((n_grp, L, HEAD_DIM), F32), pltpu.VMEM((n_grp, L, HEAD_DIM), F32)]
        + [pltpu.VMEM((L, HEAD_DIM), F32)] * 3,
        compiler_params=_params("parallel", "arbitrary"),
        name="attn_prompt",
    )(q, q, q, kv, kv)


def _bf16_round(a):
    return a.astype(BF16).astype(F32)


def _attn_step_kernel(q_ref, *refs):
    n = len(DIL_RATES)
    k_refs, v_refs = refs[0:n], refs[n:2 * n]
    kn_ref, vn_ref, o_ref = refs[2 * n:]
    scale = HEAD_DIM ** -0.5
    kn = _bf16_round(kn_ref[0, 0])
    vn = _bf16_round(vn_ref[0, 0])
    scores, new_scores = [], []
    for g, k_ref in enumerate(k_refs):
        qg = _bf16_round(q_ref[0, g])
        kg = _bf16_round(k_ref[0, :, 0])
        scores.append(jnp.sum(kg * qg[None], axis=-1, keepdims=True) * scale)
        new_scores.append(jnp.sum(kn * qg, axis=-1, keepdims=True) * scale)
    m = functools.reduce(jnp.maximum, [s.max(axis=0) for s in scores] + new_scores)
    l = jnp.zeros_like(m)
    o = jnp.zeros((KV_HEADS, HEAD_DIM), F32)
    for s, s_new, v_ref in zip(scores, new_scores, v_refs):
        p = jnp.exp(s - m[None])
        p_new = jnp.exp(s_new - m)
        l = l + p.sum(axis=0) + p_new
        o = o + (_bf16_round(p) * _bf16_round(v_ref[0, :, 0])).sum(axis=0) + _bf16_round(p_new) * vn
    o_ref[0, 0] = (o / l).astype(o_ref.dtype)


def _attn_step(q, cache_k, cache_v, k_new, v_new):
    B, T = cache_k.shape[0], cache_k.shape[1]
    S = DIL_SLOTS
    assert T == S * max(DIL_RATES), "every slot of every dilation group lies inside the cached window"
    qg = q.reshape(B, len(DIL_RATES), KV_HEADS, HEAD_DIM)
    row_shape = (B, 1, KV_HEADS, HEAD_DIM)
    row_spec = pl.BlockSpec((1, 1, KV_HEADS, HEAD_DIM), lambda b: (b, 0, 0, 0))
    args = [qg]
    in_specs = [pl.BlockSpec((1, len(DIL_RATES), KV_HEADS, HEAD_DIM), lambda b: (b, 0, 0, 0))]
    for cache in (cache_k, cache_v):
        for rate in DIL_RATES:
            args.append(cache.reshape(B, T // rate, rate, KV_HEADS, HEAD_DIM))
            in_specs.append(pl.BlockSpec((1, S, 1, KV_HEADS, HEAD_DIM),
                                         lambda b, blk=T // rate // S - 1: (b, blk, 0, 0, 0)))
    args += [k_new.reshape(row_shape), v_new.reshape(row_shape)]
    in_specs += [row_spec, row_spec]
    o = pl.pallas_call(
        _attn_step_kernel,
        grid=(B,),
        in_specs=in_specs,
        out_specs=row_spec,
        out_shape=jax.ShapeDtypeStruct(row_shape, F32),
        compiler_params=_params("parallel"),
        name="attn_step",
    )(*args)
    return o.reshape(B, KV_HEADS * HEAD_DIM)


def _rope_tables(pos):
    half = HEAD_DIM // 2
    inv = jnp.power(jnp.float32(ROPE_THETA), -jnp.arange(half, dtype=jnp.float32) / half)
    ang = pos.astype(jnp.float32)[:, None] * inv[None, :]
    cos = jnp.cos(ang)
    sin = jnp.sin(ang)
    return jnp.concatenate([cos, cos], axis=1), jnp.concatenate([-sin, sin], axis=1)


def kernel(x_prompt, x_sample, state_conv, state_ssm, cache_k, cache_v, a_norm, a_w_in, a_conv_w, a_conv_b,
           a_dt_bias, a_log, a_d, a_gate_norm, a_w_out, kv_norm, w_kv, b_norm, b_w_q, b_w_o, ffn_norm,
           ffn_w_gu, ffn_w_down, final_norm):
    assert a_norm.shape[0] == 1 and b_norm.shape[0] == 1, "one Mamba-2 layer followed by one attention layer"
    Bp, Lp, D = x_prompt.shape
    Bs, Ls, _ = x_sample.shape
    assert Ls == 1 and Bs == SAMPLE_ROWS, "sample group decodes one token for SAMPLE_ROWS sequences"
    n_heads = a_dt_bias.shape[1]
    d_inner = a_w_out.shape[1]
    conv_dim = a_conv_w.shape[2]
    kv_dim = KV_HEADS * HEAD_DIM
    w_dt = jnp.pad(a_w_in[0][:, d_inner + conv_dim:], ((0, 0), (0, LANES - n_heads))).astype(BF16)
    ssd_w = (a_conv_w[0], a_conv_b[0], a_dt_bias[0], a_log[0], a_d[0], a_gate_norm[0])
    w_kv = w_kv[None]

    cos_p, sin_p = _rope_tables(jnp.arange(Lp, dtype=jnp.int32))
    cos_s, sin_s = _rope_tables(PAST_LEN + jnp.arange(Ls, dtype=jnp.int32))
    rope = ((cos_p, sin_p, Lp), (jnp.broadcast_to(cos_s, (Bs, HEAD_DIM)), jnp.broadcast_to(sin_s, (Bs, HEAD_DIM))))

    xp0 = x_prompt.reshape(Bp * Lp, D)
    xs0 = x_sample.reshape(Bs, D)

    bc_dim = conv_dim - d_inner
    w_z_bf = a_w_in[:, :, :d_inner].astype(BF16)
    z_p, z_s, w_in_bf, w_gu0_bf = _proj(xp0, xs0, [_seg(w_z_bf)], gains=[a_norm[0]], tm=256, tn=None,
                                        casts=[(a_w_in, 0), (ffn_w_gu, 0)])
    x_p, x_s, bc_p, bc_s, dt_p, dt_s, w_out_bf = _proj(
        xp0, xs0, [_seg(w_in_bf, col0=d_inner, n_cols=d_inner), _seg(w_in_bf, col0=2 * d_inner, n_cols=bc_dim)],
        gains=[a_norm[0]], aux_w=w_dt, tm=256, tn=None, casts=[(a_w_out, 0)])
    y_p, p_ssm = _ssd_prompt(z_p, x_p, bc_p, dt_p, Bp, *ssd_w)
    last = lambda a: a.reshape(Bp, Lp, -1)[:, Lp - (CONV_W - 1):]
    p_conv = jnp.concatenate([last(x_p), last(bc_p)], axis=-1)
    y_s, s_conv, s_ssm = _ssd_step(jnp.concatenate([z_s, x_s, bc_s], axis=1), dt_s, state_conv[0], state_ssm[0],
                                   *ssd_w)
    *x1, w_gu1_bf, w_down0_bf = _proj(y_p, y_s.reshape(Bs, d_inner), [_seg(w_out_bf)], res=(xp0, xs0),
                                      casts=[(ffn_w_gu, 1), (ffn_w_down, 0)], tm=256, tn=D)
    *h, w_kv_bf, w_q_bf, w_o_bf = _gateup(*x1, ffn_norm[0], w_gu0_bf, 0, casts=[(w_kv, 0), (b_w_q, 0), (b_w_o, 0)])
    x2 = _proj(*h, [_seg(w_down0_bf)], res=x1, tm=256, tn=D)

    kv_p, kv_s, q_p, q_s = _proj(*x2, [_seg(w_kv_bf, rope_cols=kv_dim), _seg(w_q_bf, rope_cols=b_w_q.shape[2])],
                                 gains=[kv_norm, b_norm[0]], rope=rope, tm=256, tn=None)
    o_p = _attn_prompt(q_p, kv_p, Bp)
    o_s = _attn_step(q_s, cache_k, cache_v, kv_s[:, :kv_dim], kv_s[:, kv_dim:])
    x3 = _proj(o_p, o_s, [_seg(w_o_bf)], res=x2, tm=512, tn=D)
    *h, w_down1_bf = _gateup(*x3, ffn_norm[1], w_gu1_bf, 0, casts=[(ffn_w_down, 1)])
    y_prompt, y_sample = _proj(*h, [_seg(w_down1_bf)], res=x3, post_gain=final_norm, tm=256, tn=D)
    y_prompt = y_prompt.reshape(Bp, Lp, D)
    y_sample = y_sample.reshape(Bs, Ls, D)
    keep = min(DIL_SLOTS * max(DIL_RATES), Lp)
    p_kv = kv_p.reshape(Bp, Lp, 2 * KV_HEADS, HEAD_DIM)[:, Lp - keep:]
    s_kv = kv_s.reshape(Bs, Ls, 2 * KV_HEADS, HEAD_DIM)
    return (y_prompt, y_sample, p_conv[None], p_ssm[None], p_kv[:, :, :KV_HEADS], p_kv[:, :, KV_HEADS:],
            s_conv, s_ssm, s_kv[:, :, :KV_HEADS], s_kv[:, :, KV_HEADS:])
```

```python
import functools

import jax
import jax.numpy as jnp
from jax import lax
from jax.experimental import pallas as pl
from jax.experimental.pallas import tpu as pltpu

F32 = jnp.float32
BF16 = jnp.bfloat16
HIGHEST = lax.Precision.HIGHEST

EPS = 1e-6
ROPE_THETA = 10000.0
SSD_CHUNK = 128
SSM_HEAD_DIM = 64
SSM_STATE = 128
SSM_GROUPS = 8
CONV_W = 4
HEAD_DIM = 128
KV_HEADS = 8
DIL_RATES = (1, 4, 16)
DIL_SLOTS = 128
ATT_BLOCK = 128
ATT_UNROLL = 8
ATT_PRESTRIDE = 4
PAST_LEN = 16384

V7X_VMEM_BYTES = 64 * 1024 * 1024
VMEM_LIMIT = V7X_VMEM_BYTES - 8 * 1024 * 1024
LANES = 128
CONV_TAIL = 8
SAMPLE_ROWS = 8

NT_DIMS = (((1,), (1,)), ((), ()))
TN_DIMS = (((0,), (0,)), ((), ()))


def _params(*sem):
    return pltpu.CompilerParams(dimension_semantics=sem, vmem_limit_bytes=VMEM_LIMIT)


def _silu(x):
    h = 0.5 * x
    return h + h * jnp.tanh(h)


def _softplus(x):
    return jnp.maximum(x, 0.0) + jnp.log1p(jnp.exp(-jnp.abs(x)))


def _rms_scale(x):
    return lax.rsqrt(jnp.mean(x * x, axis=-1, keepdims=True) + EPS)


def _normed(x_ref, g_ref):
    x = x_ref[...]
    return ((x * _rms_scale(x)) * g_ref[...]).astype(BF16)


def _sample_block(i, j):
    return (i, j)


def _cast_specs(casts, n_steps, nj):
    BF16_ROWS = 16
    args, in_specs, out_shape, out_specs, tiles = [], [], [], [], []
    for src, layer in casts:
        R, C = src.shape[1:]
        rows = next(r for r in range(BF16_ROWS, R + 1, BF16_ROWS) if R % r == 0 and R // r <= n_steps)
        nt = R // rows
        tile = lambda i, j, nt=nt: jnp.minimum(i * nj + j, nt - 1)
        args.append(src)
        in_specs.append(pl.BlockSpec((None, rows, C), lambda i, j, layer=layer, tile=tile: (layer, tile(i, j), 0)))
        out_shape.append(jax.ShapeDtypeStruct((1, R, C), BF16))
        out_specs.append(pl.BlockSpec((None, rows, C), lambda i, j, tile=tile: (0, tile(i, j), 0)))
        tiles.append(nt)
    return args, in_specs, out_shape, out_specs, tuple(tiles)


def _run_casts(src_refs, dst_refs, tiles):
    step = pl.program_id(0) * pl.num_programs(1) + pl.program_id(1)
    for src_ref, dst_ref, nt in zip(src_refs, dst_refs, tiles):
        @pl.when(step < nt)
        def _(src_ref=src_ref, dst_ref=dst_ref):
            dst_ref[...] = src_ref[...].astype(BF16)


def _proj_kernel(*refs, segs, resident, norm, has_rope, has_res, has_post, has_aux, cast_tiles):
    it = iter(refs)
    xp_ref, xs_ref = next(it), next(it)
    g_refs = [next(it) for _ in segs] if norm else None
    w_refs = [next(it) for _ in segs]
    auxw_ref = next(it) if has_aux else None
    rope_p = (next(it), next(it)) if has_rope else None
    rope_s = (next(it), next(it)) if has_rope else None
    resp_ref, ress_ref = (next(it), next(it)) if has_res else (None, None)
    pg_ref = next(it) if has_post else None
    cast_srcs = [next(it) for _ in cast_tiles]
    out_refs =[(next(it), next(it)) for _ in segs]
    auxp_ref, auxs_ref = (next(it), next(it)) if has_aux else (None, None)
    cast_dsts = [next(it) for _ in cast_tiles]
    xnp_ref, xns_ref = (next(it), next(it)) if norm else (None, None)
    i = pl.program_id(0)
    j = pl.program_id(1)
    _run_casts(cast_srcs, cast_dsts, cast_tiles)

    if norm:
        @pl.when(j == 0)
        def _():
            for s, g_ref in enumerate(g_refs):
                xnp_ref[s] = _normed(xp_ref, g_ref)

        @pl.when((i == 0) & (j == 0))
        def _():
            for s, g_ref in enumerate(g_refs):
                xns_ref[s] = _normed(xs_ref, g_ref)

    def lhs_p(s):
        return xnp_ref[s] if norm else xp_ref[...]

    def lhs_s(s):
        return xns_ref[s] if norm else xs_ref[...].astype(BF16)

    def emit(lhs, w_ref, res_ref, o_ref, rope, lo, n_tiles, n_rope):
        acc = jnp.dot(lhs, w_ref[...].astype(BF16), preferred_element_type=F32)
        if has_res:
            acc = acc + res_ref[...]
        if has_post:
            acc = (acc * _rms_scale(acc)) * pg_ref[...]
        heads = acc.shape[1] // HEAD_DIM

        def store(rope_heads):
            if rope_heads:
                cos = rope[0][...]
                sin = rope[1][...]
            for h in range(rope_heads):
                a = acc[:, h * HEAD_DIM:(h + 1) * HEAD_DIM]
                o_ref[:, h * HEAD_DIM:(h + 1) * HEAD_DIM] = (
                    a * cos + pltpu.roll(a, HEAD_DIM // 2, 1) * sin).astype(o_ref.dtype)
            if rope_heads < heads:
                o_ref[:, rope_heads * HEAD_DIM:] = acc[:, rope_heads * HEAD_DIM:].astype(o_ref.dtype)

        if resident:
            store(n_rope)
        elif n_rope in (0, n_tiles):
            store(heads if n_rope else 0)
        else:
            pl.when(j < lo + n_rope)(lambda: store(heads))
            pl.when(j >= lo + n_rope)(lambda: store(0))

    for s, (lo, n_tiles, n_rope) in enumerate(segs):
        def segment(s=s, lo=lo, n_tiles=n_tiles, n_rope=n_rope):
            emit(lhs_p(s), w_refs[s], resp_ref, out_refs[s][0], rope_p, lo, n_tiles, n_rope)

            @pl.when(i == 0)
            def _():
                emit(lhs_s(s), w_refs[s], ress_ref, out_refs[s][1], rope_s, lo, n_tiles, n_rope)

        if resident or len(segs) == 1:
            segment()
        else:
            pl.when((j >= lo) & (j < lo + n_tiles))(segment)

    @pl.when(i > 0)
    def _():
        for _, os_ref in out_refs:
            os_ref[...] = jnp.zeros_like(os_ref)

    if has_aux:
        @pl.when(j == pl.num_programs(1) - 1)
        def _():
            auxp_ref[...] = jnp.dot(lhs_p(0), auxw_ref[...], preferred_element_type=F32)

            @pl.when(i == 0)
            def _():
                auxs_ref[...] = jnp.dot(lhs_s(0), auxw_ref[...], preferred_element_type=F32)

            @pl.when(i > 0)
            def _():
                auxs_ref[...] = jnp.zeros_like(auxs_ref)


def _seg(w, layer=0, col0=0, n_cols=None, rope_cols=0):
    return (w, layer, col0, w.shape[-1] - col0 if n_cols is None else n_cols, rope_cols)


def _proj(xp, xs, ws, *, gains=None, aux_w=None, rope=None, res=None, post_gain=None, casts=(), out_dtype=F32,
          tm=1024, tn=512):
    Mp, K = xp.shape
    S = SAMPLE_ROWS
    assert xs.shape == (S, K) and Mp % tm == 0
    norm = gains is not None
    resident = tn is None
    segs, widths, lo = [], [], 0
    for w, layer, col0, n_cols, rope_cols in ws:
        tw = n_cols if resident else tn
        rope_unit = HEAD_DIM if resident else tw
        assert w.ndim == 3 and n_cols % tw == 0 and col0 % tw == 0 and rope_cols % rope_unit == 0
        segs.append((lo, n_cols // tw, rope_cols // rope_unit))
        widths.append(tw)
        lo += 0 if resident else n_cols // tw
    nj, ni = (1 if resident else lo), Mp // tm
    has_rope = any(r for _, _, r in segs)
    assert not (has_rope and res is not None) and (rope is not None) == has_rope
    row = lambda i, j: (i, 0)
    const = lambda i, j: (0, 0)
    args = [xp, xs]
    in_specs = [pl.BlockSpec((tm, K), row), pl.BlockSpec((S, K), const)]
    if norm:
        assert len(gains) == len(ws)
        args += [g.reshape(1, K) for g in gains]
        in_specs += [pl.BlockSpec((1, K), const)] * len(gains)
    for (w, layer, col0, _, _), (lo, n_tiles, _), tw in zip(ws, segs, widths):
        args.append(w)
        in_specs.append(pl.BlockSpec(
            (None, K, tw), lambda i, j, layer=layer, lo=lo, n=n_tiles, c0=col0 // tw:
            (layer, 0, c0 + jnp.clip(j - lo, 0, n - 1)),
            pipeline_mode=pl.Buffered(1) if nj == 1 else None))
    if aux_w is not None:
        args.append(aux_w)
        in_specs.append(pl.BlockSpec(aux_w.shape, const))
    if has_rope:
        (cos_p, sin_p, rows_per_seq), (cos_s, sin_s) = rope
        nseq = rows_per_seq // tm
        args += [cos_p, sin_p, cos_s, sin_s]
        in_specs += [pl.BlockSpec((tm, HEAD_DIM), lambda i, j: (i % nseq, 0))] * 2
        in_specs += [pl.BlockSpec((S, HEAD_DIM), const)] * 2
    assert (res is None and post_gain is None) or len(ws) == 1
    if res is not None:
        args += list(res)
        in_specs += [pl.BlockSpec((tm, widths[0]), lambda i, j: (i, j)),
                     pl.BlockSpec((S, widths[0]), lambda i, j: (0, j))]
    if post_gain is not None:
        assert nj == 1
        args.append(post_gain.reshape(1, widths[0]))
        in_specs.append(pl.BlockSpec((1, widths[0]), const))
    out_shape, out_specs = [], []
    for (lo, n_tiles, _), tw in zip(segs, widths):
        col = lambda i, j, lo=lo, n=n_tiles: (i, jnp.clip(j - lo, 0, n - 1))
        out_shape += [jax.ShapeDtypeStruct((Mp, n_tiles * tw), out_dtype),
                      jax.ShapeDtypeStruct((ni * S, n_tiles * tw), F32)]
        out_specs += [pl.BlockSpec((tm, tw), col), pl.BlockSpec((S, tw), col)]
    if aux_w is not None:
        na = aux_w.shape[1]
        out_shape += [jax.ShapeDtypeStruct((Mp, na), F32), jax.ShapeDtypeStruct((ni * S, na), F32)]
        out_specs += [pl.BlockSpec((tm, na), row), pl.BlockSpec((S, na), row)]
    n_main = len(out_shape)
    c_args, c_in, c_shape, c_out, cast_tiles = _cast_specs(casts, ni * nj, nj)
    args += c_args
    in_specs += c_in
    out_shape += c_shape
    out_specs += c_out
    outs = pl.pallas_call(
        functools.partial(_proj_kernel, segs=tuple(segs), resident=resident, norm=norm, has_rope=has_rope,
                          has_res=res is not None, has_post=post_gain is not None, has_aux=aux_w is not None,
                          cast_tiles=cast_tiles),
        grid=(ni, nj),
        in_specs=in_specs,
        out_specs=out_specs,
        out_shape=out_shape,
        scratch_shapes=[pltpu.VMEM((len(ws), tm, K), BF16), pltpu.VMEM((len(ws), S, K), BF16)] if norm else [],
        compiler_params=_params("arbitrary", "arbitrary"),
        name="proj",
    )(*args)
    return [o[:S] if n < n_main and n % 2 == 1 else o for n, o in enumerate(outs)]


def _gateup_kernel(xp_ref, xs_ref, g_ref, wg_ref, wu_ref, *refs, cast_tiles):
    n = len(cast_tiles)
    cast_srcs = refs[:n]
    op_ref, os_ref = refs[n:n + 2]
    cast_dsts = refs[n + 2:2 * n + 2]
    xnp_ref, xns_ref = refs[2 * n + 2:]
    i = pl.program_id(0)
    j = pl.program_id(1)
    _run_casts(cast_srcs, cast_dsts, cast_tiles)

    @pl.when(j == 0)
    def _():
        xnp_ref[...] = _normed(xp_ref, g_ref)

    @pl.when((i == 0) & (j == 0))
    def _():
        xns_ref[...] = _normed(xs_ref, g_ref)

    def swiglu(xn):
        g = jnp.dot(xn, wg_ref[...], preferred_element_type=F32)
        u = jnp.dot(xn, wu_ref[...], preferred_element_type=F32)
        return _silu(g) * u

    op_ref[...] = swiglu(xnp_ref[...]).astype(op_ref.dtype)

    @pl.when(i == 0)
    def _():
        os_ref[...] = swiglu(xns_ref[...])

    @pl.when(i > 0)
    def _():
        os_ref[...] = jnp.zeros_like(os_ref)


def _gateup(xp, xs, gain, w_gu, layer, *, casts=(), tm=1024, tn=512):
    Mp, K = xp.shape
    S = SAMPLE_ROWS
    hidden = w_gu.shape[2] // 2
    assert xs.shape == (S, K) and Mp % tm == 0 and hidden % tn == 0 and w_gu.dtype == BF16
    nj = hidden // tn
    ni = Mp // tm
    c_args, c_in, c_shape, c_out, cast_tiles = _cast_specs(casts, ni * nj, nj)
    outs = pl.pallas_call(
        functools.partial(_gateup_kernel, cast_tiles=cast_tiles),
        grid=(ni, nj),
        in_specs=[pl.BlockSpec((tm, K), lambda i, j: (i, 0)),
                  pl.BlockSpec((S, K), lambda i, j: (0, 0)),
                  pl.BlockSpec((1, K), lambda i, j: (0, 0)),
                  pl.BlockSpec((None, K, tn), lambda i, j: (layer, 0, j)),
                  pl.BlockSpec((None, K, tn), lambda i, j: (layer, 0, j + nj))] + c_in,
        out_specs=[pl.BlockSpec((tm, tn), lambda i, j: (i, j)), pl.BlockSpec((S, tn), _sample_block)] + c_out,
        out_shape=[jax.ShapeDtypeStruct((Mp, hidden), BF16), jax.ShapeDtypeStruct((ni * S, hidden), F32)] + c_shape,
        scratch_shapes=[pltpu.VMEM((tm, K), BF16), pltpu.VMEM((S, K), BF16)],
        compiler_params=_params("arbitrary", "arbitrary"),
        name="gateup",
    )(xp, xs, gain.reshape(1, K), w_gu, w_gu, *c_args)
    return [outs[0], outs[1][:S]] + list(outs[2:])


def _conv_silu_slab(src, s, raw_ref, act_ref, cw_ref, cb_ref):
    Q = SSD_CHUNK
    raw_ref[s, CONV_TAIL:, :] = src
    for parity in (0, 1):
        acc = cb_ref[s]
        for k in range(CONV_W):
            first = CONV_TAIL - (CONV_W - 1) + k + parity
            acc = acc + raw_ref[s, pl.ds(first, Q // 2, stride=2), :] * cw_ref[s, k:k + 1, :]
        act_ref[s, pl.ds(parity, Q // 2, stride=2), :] = _silu(acc)
    raw_ref[s, 0:CONV_TAIL, :] = raw_ref[s, Q:Q + CONV_TAIL, :]


def _ssd_kernel(z_ref, x_ref, b_ref, c_ref, dt_ref, cw_ref, cb_ref,
                dtb_ref, alog_ref, dexp_ref, gn_ref,
                y_ref, hout_ref,
                h_ref, raw_ref, act_ref):
    Q = SSD_CHUNK
    P = SSM_HEAD_DIM
    N = SSM_STATE
    d_inner = x_ref.shape[1]
    n_heads = d_inner // P
    hpg = n_heads // SSM_GROUPS
    gw = hpg * P
    n_xs = d_inner // LANES
    n_bs = b_ref.shape[1] // LANES
    c = pl.program_id(1)

    @pl.when(c == 0)
    def _():
        h_ref[...] = jnp.zeros_like(h_ref)
        raw_ref[:, 0:CONV_TAIL, :] = jnp.zeros((raw_ref.shape[0], CONV_TAIL, LANES), F32)

    for s in range(n_xs + 2 * n_bs):
        if s < n_xs:
            src = x_ref[:, s * LANES:(s + 1) * LANES]
        elif s < n_xs + n_bs:
            src = b_ref[:, (s - n_xs) * LANES:(s - n_xs + 1) * LANES]
        else:
            src = c_ref[:, (s - n_xs - n_bs) * LANES:(s - n_xs - n_bs + 1) * LANES]
        _conv_silu_slab(src, s, raw_ref, act_ref, cw_ref, cb_ref)

    dt = _softplus(dt_ref[...] + dtb_ref[...])
    dA = dt * (-jnp.exp(alog_ref[...]))
    row = lax.broadcasted_iota(jnp.int32, (Q, Q), 0)
    col = lax.broadcasted_iota(jnp.int32, (Q, Q), 1)
    causal = row >= col
    tri = jnp.where(causal, 1.0, 0.0).astype(F32)
    cs = jnp.dot(tri, dA, precision=HIGHEST, preferred_element_type=F32)
    csT = cs.T
    dtT = dt.T
    dt_decay_end = dt * jnp.exp(cs[Q - 1:Q, :] - cs)
    chunk_decay = jnp.broadcast_to(jnp.exp(csT[0:n_heads, Q - 1:Q]), (n_heads, N))

    lane_lo = lax.broadcasted_iota(jnp.int32, (Q, 2 * P), 1) < P

    def pair_cols(arr, q):
        a0 = jnp.broadcast_to(arr[:, 2 * q:2 * q + 1], (Q, 2 * P))
        a1 = jnp.broadcast_to(arr[:, 2 * q + 1:2 * q + 2], (Q, 2 * P))
        return jnp.where(lane_lo, a0, a1)

    for g in range(SSM_GROUPS):
        Bg = act_ref[n_xs + g].astype(BF16)
        Cg = act_ref[n_xs + n_bs + g].astype(BF16)
        CB = lax.dot_general(Cg, Bg, NT_DIMS, preferred_element_type=F32)
        Hg = h_ref[g * hpg:(g + 1) * hpg].reshape(gw, N)
        y_off = lax.dot_general(Cg, Hg.astype(BF16), NT_DIMS, preferred_element_type=F32)

        ys = []
        xds = []
        for qq in range(hpg // 2):
            q = g * (hpg // 2) + qq
            lo, hi = q * 2 * P, (q + 1) * 2 * P
            xs = act_ref[q]
            xds.append((xs * pair_cols(dt_decay_end, q)).astype(BF16))
            cs_cols = [jnp.broadcast_to(cs[:, hh:hh + 1], (Q, Q)) for hh in (2 * q, 2 * q + 1)]
            atts = []
            for hh, cs_col in zip((2 * q, 2 * q + 1), cs_cols):
                seg = cs_col - csT[hh:hh + 1, :]
                decay = jnp.exp(jnp.where(causal, seg, -jnp.inf))
                atts.append(((CB * decay) * dtT[hh:hh + 1, :]).astype(BF16))
            att = jnp.concatenate(atts, axis=1)
            xbd = jnp.concatenate([jnp.where(lane_lo, xs, 0.0).astype(BF16),
                                   jnp.where(lane_lo, 0.0, xs).astype(BF16)], axis=0)
            y = jnp.dot(att, xbd, preferred_element_type=F32)
            y = y + y_off[:, qq * 2 * P:(qq + 1) * 2 * P] * jnp.exp(jnp.where(lane_lo, *cs_cols))
            y = y + xs * dexp_ref[:, lo:hi]
            ys.append(y * _silu(z_ref[:, lo:hi]))

        ssq = ys[0] * ys[0]
        for y in ys[1:]:
            ssq = ssq + y * y
        scale = lax.rsqrt(jnp.sum(ssq, axis=-1, keepdims=True) / gw + EPS)
        for qq, y in enumerate(ys):
            lo = g * gw + qq * 2 * P
            y_ref[:, lo:lo + 2 * P] = ((y * scale) * gn_ref[:, lo:lo + 2 * P]).astype(y_ref.dtype)

        S = lax.dot_general(jnp.concatenate(xds, axis=1), Bg, TN_DIMS, preferred_element_type=F32)
        for r in range(hpg):
            hh = g * hpg + r
            h_ref[hh] = h_ref[hh] * chunk_decay[hh:hh + 1, :] + S[r * P:(r + 1) * P, :]

    @pl.when(c == pl.num_programs(1) - 1)
    def _():
        hout_ref[0] = h_ref[...]


def _ssd_prompt(zxbc, dt, batch, conv_w, conv_b, dt_bias, a_log, d_skip, gate_norm):
    M = zxbc.shape[0]
    L = M // batch
    H = dt_bias.shape[0]
    d_inner = H * SSM_HEAD_DIM
    bc = SSM_GROUPS * SSM_STATE
    conv_dim = d_inner + 2 * bc
    Q = SSD_CHUNK
    nc = L // Q
    assert L % Q == 0 and d_inner % bc == 0
    xb = d_inner // bc
    row_map = lambda col: (lambda b, c: (b * nc + c, col))
    const = lambda b, c: (0, 0)
    d_exp = jnp.repeat(d_skip.astype(F32), SSM_HEAD_DIM).reshape(1, d_inner)
    assert H <= LANES and SSM_STATE == LANES and 2 * SSM_HEAD_DIM == LANES
    lane_pad = lambda v: jnp.pad(v.reshape(1, H), ((0, 0), (0, LANES - H)))
    n_slabs = conv_dim // LANES
    cw_slabs = conv_w.reshape(CONV_W, n_slabs, LANES).transpose(1, 0, 2)
    cb_slabs = conv_b.reshape(n_slabs, 1, LANES)
    const3 = lambda b, c: (0, 0, 0)
    y, h_final = pl.pallas_call(
        _ssd_kernel,
        grid=(batch, nc),
        in_specs=[pl.BlockSpec((Q, d_inner), row_map(0)),
                  pl.BlockSpec((Q, d_inner), row_map(1)),
                  pl.BlockSpec((Q, bc), row_map(2 * xb)),
                  pl.BlockSpec((Q, bc), row_map(2 * xb + 1)),
                  pl.BlockSpec((Q, LANES), row_map(0)),
                  pl.BlockSpec((n_slabs, CONV_W, LANES), const3),
                  pl.BlockSpec((n_slabs, 1, LANES), const3),
                  pl.BlockSpec((1, LANES), const),
                  pl.BlockSpec((1, LANES), const),
                  pl.BlockSpec((1, d_inner), const),
                  pl.BlockSpec((1, d_inner), const)],
        out_specs=[pl.BlockSpec((Q, d_inner), row_map(0)),
                   pl.BlockSpec((1, H, SSM_HEAD_DIM, SSM_STATE), lambda b, c: (b, 0, 0, 0))],
        out_shape=[jax.ShapeDtypeStruct((M, d_inner), BF16),
                   jax.ShapeDtypeStruct((batch, H, SSM_HEAD_DIM, SSM_STATE), F32)],
        scratch_shapes=[pltpu.VMEM((H, SSM_HEAD_DIM, SSM_STATE), F32),
                        pltpu.VMEM((n_slabs, Q + CONV_TAIL, LANES), F32),
                        pltpu.VMEM((n_slabs, Q, LANES), F32)],
        compiler_params=_params("parallel", "arbitrary"),
        name="ssd_prompt",
    )(zxbc, zxbc, zxbc, zxbc, dt, cw_slabs, cb_slabs,
      lane_pad(dt_bias), lane_pad(a_log), d_exp, gate_norm.reshape(1, d_inner))
    return y, h_final


def _ssd_step_kernel(zxbc_ref, dt_ref, cs_ref, h0_ref, cw_ref, cb_ref, dtb_ref, alog_ref,
                     dexp_ref, gn_ref, y_ref, cso_ref, ho_ref):
    P = SSM_HEAD_DIM
    N = SSM_STATE
    n_heads = h0_ref.shape[2]
    d_inner = n_heads * P
    hpg = n_heads // SSM_GROUPS
    gw = hpg * P
    conv_dim = cw_ref.shape[1]

    raw = zxbc_ref[0, :, d_inner:d_inner + conv_dim]
    prev = cs_ref[0, 0]
    acc = cb_ref[...]
    for k in range(CONV_W - 1):
        acc = acc + prev[k:k + 1, :] * cw_ref[k:k + 1, :]
    acc = acc + raw * cw_ref[CONV_W - 1:CONV_W, :]
    xbc = _silu(acc)
    cso_ref[0, 0, 0:CONV_W - 2, :] = prev[1:CONV_W - 1, :]
    cso_ref[0, 0, CONV_W - 2:CONV_W - 1, :] = raw

    dt = _softplus(dt_ref[0, :, 0:n_heads] + dtb_ref[...])
    decay = jnp.exp(dt * (-jnp.exp(alog_ref[...])))
    z = zxbc_ref[0, :, 0:d_inner]

    eye = (lax.broadcasted_iota(jnp.int32, (P, P), 0) == lax.broadcasted_iota(jnp.int32, (P, P), 1))
    ys = []
    for g in range(SSM_GROUPS):
        Bg = xbc[:, d_inner + g * N:d_inner + (g + 1) * N]
        Cg = xbc[:, d_inner + SSM_GROUPS * N + g * N:d_inner + SSM_GROUPS * N + (g + 1) * N]
        Bb = jnp.broadcast_to(Bg, (P, N))
        for r in range(hpg):
            hh = g * hpg + r
            xs = xbc[:, hh * P:(hh + 1) * P]
            xdt = xs * dt[:, hh:hh + 1]
            xdiag = jnp.where(eye, jnp.broadcast_to(xdt, (P, P)), 0.0)
            outer = jnp.dot(xdiag, Bb, precision=HIGHEST, preferred_element_type=F32)
            ho_ref[0, 0, hh] = h0_ref[0, 0, hh] * decay[:, hh:hh + 1] + outer
        Hg = ho_ref[0, 0, g * hpg:(g + 1) * hpg].reshape(gw, N)
        yg = lax.dot_general(Cg.astype(BF16), Hg.astype(BF16), NT_DIMS, preferred_element_type=F32)
        lo, hi = g * gw, (g + 1) * gw
        yg = yg + xbc[:, lo:hi] * dexp_ref[:, lo:hi]
        yg = yg * _silu(z[:, lo:hi])
        yg = yg * lax.rsqrt(jnp.mean(yg * yg, axis=-1, keepdims=True) + EPS)
        ys.append(yg * gn_ref[:, lo:hi])
    y_ref[0] = jnp.concatenate(ys, axis=1).astype(y_ref.dtype)


def _ssd_step(zxbc, dt, conv_state, ssm_state, conv_w, conv_b, dt_bias, a_log, d_skip, gate_norm):
    B = zxbc.shape[0]
    H = dt_bias.shape[0]
    d_inner = H * SSM_HEAD_DIM
    conv_dim = conv_w.shape[1]
    const = lambda b: (0, 0)
    d_exp = jnp.repeat(d_skip.astype(F32), SSM_HEAD_DIM).reshape(1, d_inner)
    cs4 = conv_state.reshape(1, B, CONV_W - 1, conv_dim)
    h5 = ssm_state.reshape(1, B, H, SSM_HEAD_DIM, SSM_STATE)
    return pl.pallas_call(
        _ssd_step_kernel,
        grid=(B,),
        in_specs=[pl.BlockSpec((1, 1, zxbc.shape[1]), lambda b: (b, 0, 0)),
                  pl.BlockSpec((1, 1, dt.shape[1]), lambda b: (b, 0, 0)),
                  pl.BlockSpec((1, 1, CONV_W - 1, conv_dim), lambda b: (0, b, 0, 0)),
                  pl.BlockSpec((1, 1, H, SSM_HEAD_DIM, SSM_STATE), lambda b: (0, b, 0, 0, 0)),
                  pl.BlockSpec((CONV_W, conv_dim), const),
                  pl.BlockSpec((1, conv_dim), const),
                  pl.BlockSpec((1, H), const),
                  pl.BlockSpec((1, H), const),
                  pl.BlockSpec((1, d_inner), const),
                  pl.BlockSpec((1, d_inner), const)],
        out_specs=[pl.BlockSpec((1, 1, d_inner), lambda b: (b, 0, 0)),
                   pl.BlockSpec((1, 1, CONV_W - 1, conv_dim), lambda b: (0, b, 0, 0)),
                   pl.BlockSpec((1, 1, H, SSM_HEAD_DIM, SSM_STATE), lambda b: (0, b, 0, 0, 0))],
        out_shape=[jax.ShapeDtypeStruct((B, 1, d_inner), F32),
                   jax.ShapeDtypeStruct(cs4.shape, F32),
                   jax.ShapeDtypeStruct(h5.shape, F32)],
        compiler_params=_params("arbitrary"),
        name="ssd_step",
    )(zxbc.reshape(B, 1, -1), dt.reshape(B, 1, -1), cs4, h5, conv_w, conv_b.reshape(1, conv_dim),
      dt_bias.reshape(1, H), a_log.reshape(1, H), d_exp, gate_norm.reshape(1, d_inner))


def _attn_kernel(q0_ref, q1_ref, q2_ref, k_ref, v_ref, o_ref, on_ref, lse_ref, qs_ref, ks_ref, vs_ref):
    L = k_ref.shape[0]
    QB = ATT_BLOCK
    KW = QB + DIL_SLOTS
    P = ATT_PRESTRIDE
    p_shift = P.bit_length() - 1
    scale = HEAD_DIM ** -0.5
    e = lax.broadcasted_iota(jnp.int32, (QB, KW), 0) - lax.broadcasted_iota(jnp.int32, (QB, KW), 1)

    staged = {rate: rate > P for rate in DIL_RATES}
    for src_ref, dst_ref, needed in ((q2_ref, qs_ref, staged[DIL_RATES[2]]), (k_ref, ks_ref, any(staged.values())),
                                     (v_ref, vs_ref, any(staged.values()))):
        if needed:
            def stage(i, carry, src_ref=src_ref, dst_ref=dst_ref):
                cp = i & (P - 1)
                t = i >> p_shift
                dst = pl.multiple_of(cp * (L // P) + t * QB, QB)
                dst_ref[pl.ds(dst, QB), :] = src_ref[pl.ds(cp + P * QB * t, QB, stride=P), :]
                return carry
            lax.fori_loop(0, L // QB, stage, 0, unroll=4)

    for g, (q_ref, rate) in enumerate(zip((q0_ref, q1_ref, q2_ref), DIL_RATES)):
        shift = rate.bit_length() - 1

        def unit(n, carry, g=g, q_ref=q_ref, rate=rate, shift=shift):
            c = n & (rate - 1)
            u0 = (n >> shift) * QB
            v0 = jnp.maximum(u0 - DIL_SLOTS, 0)
            q_rows = pl.ds(c + rate * u0, QB, stride=rate)
            if staged[rate]:
                base = (c & (P - 1)) * (L // P) + (c >> p_shift)
                q = qs_ref[pl.ds(base + (rate // P) * u0, QB, stride=rate // P), :].astype(BF16)
                k_rows = pl.ds(base + (rate // P) * v0, KW, stride=rate // P)
                k = ks_ref[k_rows, :].astype(BF16)
                v = vs_ref[k_rows, :].astype(BF16)
            else:
                k_rows = pl.ds(c + rate * v0, KW, stride=rate)
                q = q_ref[q_rows, :].astype(BF16)
                k = k_ref[k_rows, :].astype(BF16)
                v = v_ref[k_rows, :].astype(BF16)
            s = lax.dot_general(q, k, NT_DIMS, preferred_element_type=F32) * scale
            d = e + (u0 - v0)
            s = jnp.where(d >= 0, s, -jnp.inf)
            s = jnp.where(d <= DIL_SLOTS, s, -jnp.inf)
            m = s.max(axis=1, keepdims=True)
            p = jnp.exp(s - m)
            l = p.sum(axis=1, keepdims=True)
            o = jnp.dot(p.astype(BF16), v, preferred_element_type=F32)
            on_ref[g, q_rows, :] = o / l
            lse_ref[g, q_rows, :] = jnp.broadcast_to(m + jnp.log(l), (QB, HEAD_DIM))
            return carry

        lax.fori_loop(0, L // QB, unit, 0, unroll=ATT_UNROLL)

    def mix(i, carry):
        rows = pl.ds(pl.multiple_of(i * QB, QB), QB)
        lses = [lse_ref[g, rows, :] for g in range(len(DIL_RATES))]
        m = functools.reduce(jnp.maximum, lses)
        ws = [jnp.exp(x - m) for x in lses]
        num = functools.reduce(jnp.add, [w * on_ref[g, rows, :] for g, w in enumerate(ws)])
        o_ref[rows, :] = (num / functools.reduce(jnp.add, ws)).astype(o_ref.dtype)
        return carry

    lax.fori_loop(0, L // QB, mix, 0, unroll=2)


def _attn_prompt(q, kv, batch):
    M = q.shape[0]
    L = M // batch
    n_grp = len(DIL_RATES)
    assert all(r & (r - 1) == 0 and L % (ATT_BLOCK * r) == 0 and L >= r * (ATT_BLOCK + DIL_SLOTS)
               for r in DIL_RATES)
    qspec = lambda g: pl.BlockSpec((L, HEAD_DIM), lambda b, h: (b, g * KV_HEADS + h))
    kspec = pl.BlockSpec((L, HEAD_DIM), lambda b, h: (b, h))
    vspec = pl.BlockSpec((L, HEAD_DIM), lambda b, h: (b, KV_HEADS + h))
    return pl.pallas_call(
        _attn_kernel,
        grid=(batch, KV_HEADS),
        in_specs=[qspec(0), qspec(1), qspec(2), kspec, vspec],
        out_specs=pl.BlockSpec((L, HEAD_DIM), lambda b, h: (b, h)),
        out_shape=jax.ShapeDtypeStruct((M, KV_HEADS * HEAD_DIM), BF16),
        scratch_shapes=[pltpu.VMEM((n_grp, L, HEAD_DIM), F32), pltpu.VMEM((n_grp, L, HEAD_DIM), F32)]
        + [pltpu.VMEM((L, HEAD_DIM), F32)] * 3,
        compiler_params=_params("parallel", "arbitrary"),
        name="attn_prompt",
    )(q, q, q, kv, kv)


def _bf16_round(a):
    return a.astype(BF16).astype(F32)


def _attn_step_kernel(q_ref, *refs):
    n = len(DIL_RATES)
    k_refs, v_refs = refs[0:n], refs[n:2 * n]
    kn_ref, vn_ref, o_ref = refs[2 * n:]
    scale = HEAD_DIM ** -0.5
    kn = _bf16_round(kn_ref[0, 0])
    vn = _bf16_round(vn_ref[0, 0])
    scores, new_scores = [], []
    for g, k_ref in enumerate(k_refs):
        qg = _bf16_round(q_ref[0, g])
        kg = _bf16_round(k_ref[0, :, 0])
        scores.append(jnp.sum(kg * qg[None], axis=-1, keepdims=True) * scale)
        new_scores.append(jnp.sum(kn * qg, axis=-1, keepdims=True) * scale)
    m = functools.reduce(jnp.maximum, [s.max(axis=0) for s in scores] + new_scores)
    l = jnp.zeros_like(m)
    o = jnp.zeros((KV_HEADS, HEAD_DIM), F32)
    for s, s_new, v_ref in zip(scores, new_scores, v_refs):
        p = jnp.exp(s - m[None])
        p_new = jnp.exp(s_new - m)
        l = l + p.sum(axis=0) + p_new
        o = o + (_bf16_round(p) * _bf16_round(v_ref[0, :, 0])).sum(axis=0) + _bf16_round(p_new) * vn
    o_ref[0, 0] = (o / l).astype(o_ref.dtype)


def _attn_step(q, cache_k, cache_v, k_new, v_new):
    B, T = cache_k.shape[0], cache_k.shape[1]
    S = DIL_SLOTS
    assert T == S * max(DIL_RATES), "every slot of every dilation group lies inside the cached window"
    qg = q.reshape(B, len(DIL_RATES), KV_HEADS, HEAD_DIM)
    row_shape = (B, 1, KV_HEADS, HEAD_DIM)
    row_spec = pl.BlockSpec((1, 1, KV_HEADS, HEAD_DIM), lambda b: (b, 0, 0, 0))
    args = [qg]
    in_specs = [pl.BlockSpec((1, len(DIL_RATES), KV_HEADS, HEAD_DIM), lambda b: (b, 0, 0, 0))]
    for cache in (cache_k, cache_v):
        for rate in DIL_RATES:
            args.append(cache.reshape(B, T // rate, rate, KV_HEADS, HEAD_DIM))
            in_specs.append(pl.BlockSpec((1, S, 1, KV_HEADS, HEAD_DIM),
                                         lambda b, blk=T // rate // S - 1: (b, blk, 0, 0, 0)))
    args += [k_new.reshape(row_shape), v_new.reshape(row_shape)]
    in_specs += [row_spec, row_spec]
    o = pl.pallas_call(
        _attn_step_kernel,
        grid=(B,),
        in_specs=in_specs,
        out_specs=row_spec,
        out_shape=jax.ShapeDtypeStruct(row_shape, F32),
        compiler_params=_params("parallel"),
        name="attn_step",
    )(*args)
    return o.reshape(B, KV_HEADS * HEAD_DIM)


def _rope_tables(pos):
    half = HEAD_DIM // 2
    inv = jnp.power(jnp.float32(ROPE_THETA), -jnp.arange(half, dtype=jnp.float32) / half)
    ang = pos.astype(jnp.float32)[:, None] * inv[None, :]
    cos = jnp.cos(ang)
    sin = jnp.sin(ang)
    return jnp.concatenate([cos, cos], axis=1), jnp.concatenate([-sin, sin], axis=1)


def kernel(x_prompt, x_sample, state_conv, state_ssm, cache_k, cache_v, a_norm, a_w_in, a_conv_w, a_conv_b,
           a_dt_bias, a_log, a_d, a_gate_norm, a_w_out, kv_norm, w_kv, b_norm, b_w_q, b_w_o, ffn_norm,
           ffn_w_gu, ffn_w_down, final_norm):
    assert a_norm.shape[0] == 1 and b_norm.shape[0] == 1, "one Mamba-2 layer followed by one attention layer"
    Bp, Lp, D = x_prompt.shape
    Bs, Ls, _ = x_sample.shape
    assert Ls == 1 and Bs == SAMPLE_ROWS, "sample group decodes one token for SAMPLE_ROWS sequences"
    n_heads = a_dt_bias.shape[1]
    d_inner = a_w_out.shape[1]
    conv_dim = a_conv_w.shape[2]
    kv_dim = KV_HEADS * HEAD_DIM
    w_dt = jnp.pad(a_w_in[0][:, d_inner + conv_dim:], ((0, 0), (0, LANES - n_heads))).astype(BF16)
    ssd_w = (a_conv_w[0], a_conv_b[0], a_dt_bias[0], a_log[0], a_d[0], a_gate_norm[0])
    w_kv = w_kv[None]

    cos_p, sin_p = _rope_tables(jnp.arange(Lp, dtype=jnp.int32))
    cos_s, sin_s = _rope_tables(PAST_LEN + jnp.arange(Ls, dtype=jnp.int32))
    rope = ((cos_p, sin_p, Lp), (jnp.broadcast_to(cos_s, (Bs, HEAD_DIM)), jnp.broadcast_to(sin_s, (Bs, HEAD_DIM))))

    xp0 = x_prompt.reshape(Bp * Lp, D)
    xs0 = x_sample.reshape(Bs, D)

    zxbc_p, zxbc_s, dt_p, dt_s, w_out_bf, w_gu0_bf, w_down0_bf = _proj(
        xp0, xs0, [_seg(a_w_in.astype(BF16), n_cols=d_inner + conv_dim)], gains=[a_norm[0]], aux_w=w_dt, tn=1024,
        casts=[(a_w_out, 0), (ffn_w_gu, 0), (ffn_w_down, 0)])
    y_p, p_ssm = _ssd_prompt(zxbc_p, dt_p, Bp, *ssd_w)
    p_conv = zxbc_p.reshape(Bp, Lp, -1)[:, Lp - (CONV_W - 1):, d_inner:]
    y_s, s_conv, s_ssm = _ssd_step(zxbc_s, dt_s, state_conv[0], state_ssm[0], *ssd_w)
    *x1, w_gu1_bf = _proj(y_p, y_s.reshape(Bs, d_inner), [_seg(w_out_bf)], res=(xp0, xs0), casts=[(ffn_w_gu, 1)],
                          tm=256, tn=D)
    *h, w_kv_bf, w_q_bf, w_o_bf = _gateup(*x1, ffn_norm[0], w_gu0_bf, 0, casts=[(w_kv, 0), (b_w_q, 0), (b_w_o, 0)])
    x2 = _proj(*h, [_seg(w_down0_bf)], res=x1, tm=256, tn=D)

    kv_p, kv_s, q_p, q_s = _proj(*x2, [_seg(w_kv_bf, rope_cols=kv_dim), _seg(w_q_bf, rope_cols=b_w_q.shape[2])],
                                 gains=[kv_norm, b_norm[0]], rope=rope, tm=256, tn=None)
    o_p = _attn_prompt(q_p, kv_p, Bp)
    o_s = _attn_step(q_s, cache_k, cache_v, kv_s[:, :kv_dim], kv_s[:, kv_dim:])
    x3 = _proj(o_p, o_s, [_seg(w_o_bf)], res=x2, tm=512, tn=D)
    *h, w_down1_bf = _gateup(*x3, ffn_norm[1], w_gu1_bf, 0, casts=[(ffn_w_down, 1)])
    y_prompt, y_sample = _proj(*h, [_seg(w_down1_bf)], res=x3, post_gain=final_norm, tm=256, tn=D)
    y_prompt = y_prompt.reshape(Bp, Lp, D)
    y_sample = y_sample.reshape(Bs, Ls, D)
    keep = min(DIL_SLOTS * max(DIL_RATES), Lp)
    p_kv = kv_p.reshape(Bp, Lp, 2 * KV_HEADS, HEAD_DIM)[:, Lp - keep:]
    s_kv = kv_s.reshape(Bs, Ls, 2 * KV_HEADS, HEAD_DIM)
    return (y_prompt, y_sample, p_conv[None], p_ssm[None], p_kv[:, :, :KV_HEADS], p_kv[:, :, KV_HEADS:],
            s_conv, s_ssm, s_kv[:, :, :KV_HEADS], s_kv[:, :, KV_HEADS:])
```

```python
import functools

import jax
import jax.numpy as jnp
from jax import lax
from jax.experimental import pallas as pl
from jax.experimental.pallas import tpu as pltpu

F32 = jnp.float32
BF16 = jnp.bfloat16
HIGHEST = lax.Precision.HIGHEST

EPS = 1e-6
ROPE_THETA = 10000.0
SSD_CHUNK = 128
SSM_HEAD_DIM = 64
SSM_STATE = 128
SSM_GROUPS = 8
CONV_W = 4
HEAD_DIM = 128
KV_HEADS = 8
DIL_RATES = (1, 4, 16)
DIL_SLOTS = 128
ATT_BLOCK = 128
ATT_UNROLL = 8
ATT_PRESTRIDE = 4
PAST_LEN = 16384

V7X_VMEM_BYTES = 64 * 1024 * 1024
VMEM_LIMIT = V7X_VMEM_BYTES - 8 * 1024 * 1024
LANES = 128
CONV_TAIL = 8
SAMPLE_ROWS = 8

NT_DIMS = (((1,), (1,)), ((), ()))
TN_DIMS = (((0,), (0,)), ((), ()))


def _params(*sem):
    return pltpu.CompilerParams(dimension_semantics=sem, vmem_limit_bytes=VMEM_LIMIT)


def _silu(x):
    h = 0.5 * x
    return h + h * jnp.tanh(h)


def _softplus(x):
    return jnp.maximum(x, 0.0) + jnp.log1p(jnp.exp(-jnp.abs(x)))


def _rms_scale(x):
    return lax.rsqrt(jnp.mean(x * x, axis=-1, keepdims=True) + EPS)


def _normed(x_ref, g_ref):
    x = x_ref[...]
    return ((x * _rms_scale(x)) * g_ref[...]).astype(BF16)


def _sample_block(i, j):
    return (i, j)


def _cast_specs(casts, n_steps, nj):
    BF16_ROWS = 16
    args, in_specs, out_shape, out_specs, tiles = [], [], [], [], []
    for src, layer in casts:
        R, C = src.shape[1:]
        rows = next(r for r in range(BF16_ROWS, R + 1, BF16_ROWS) if R % r == 0 and R // r <= n_steps)
        nt = R // rows
        tile = lambda i, j, nt=nt: jnp.minimum(i * nj + j, nt - 1)
        args.append(src)
        in_specs.append(pl.BlockSpec((None, rows, C), lambda i, j, layer=layer, tile=tile: (layer, tile(i, j), 0)))
        out_shape.append(jax.ShapeDtypeStruct((1, R, C), BF16))
        out_specs.append(pl.BlockSpec((None, rows, C), lambda i, j, tile=tile: (0, tile(i, j), 0)))
        tiles.append(nt)
    return args, in_specs, out_shape, out_specs, tuple(tiles)


def _run_casts(src_refs, dst_refs, tiles):
    step = pl.program_id(0) * pl.num_programs(1) + pl.program_id(1)
    for src_ref, dst_ref, nt in zip(src_refs, dst_refs, tiles):
        @pl.when(step < nt)
        def _(src_ref=src_ref, dst_ref=dst_ref):
            dst_ref[...] = src_ref[...].astype(BF16)


def _proj_kernel(*refs, segs, resident, norm, has_rope, has_res, has_post, has_aux, cast_tiles):
    it = iter(refs)
    xp_ref, xs_ref = next(it), next(it)
    g_refs = [next(it) for _ in segs] if norm else None
    w_refs = [next(it) for _ in segs]
    auxw_ref = next(it) if has_aux else None
    rope_p = (next(it), next(it)) if has_rope else None
    rope_s = (next(it), next(it)) if has_rope else None
    resp_ref, ress_ref = (next(it), next(it)) if has_res else (None, None)
    pg_ref = next(it) if has_post else None
    cast_srcs = [next(it) for _ in cast_tiles]
    out_refs =[(next(it), next(it)) for _ in segs]
    auxp_ref, auxs_ref = (next(it), next(it)) if has_aux else (None, None)
    cast_dsts = [next(it) for _ in cast_tiles]
    xnp_ref, xns_ref = (next(it), next(it)) if norm else (None, None)
    i = pl.program_id(0)
    j = pl.program_id(1)
    _run_casts(cast_srcs, cast_dsts, cast_tiles)

    if norm:
        @pl.when(j == 0)
        def _():
            for s, g_ref in enumerate(g_refs):
                xnp_ref[s] = _normed(xp_ref, g_ref)

        @pl.when((i == 0) & (j == 0))
        def _():
            for s, g_ref in enumerate(g_refs):
                xns_ref[s] = _normed(xs_ref, g_ref)

    def lhs_p(s):
        return xnp_ref[s] if norm else xp_ref[...]

    def lhs_s(s):
        return xns_ref[s] if norm else xs_ref[...].astype(BF16)

    def emit(lhs, w_ref, res_ref, o_ref, rope, lo, n_tiles, n_rope):
        acc = jnp.dot(lhs, w_ref[...].astype(BF16), preferred_element_type=F32)
        if has_res:
            acc = acc + res_ref[...]
        if has_post:
            acc = (acc * _rms_scale(acc)) * pg_ref[...]
        heads = acc.shape[1] // HEAD_DIM

        def store(rope_heads):
            if rope_heads:
                cos = rope[0][...]
                sin = rope[1][...]
            for h in range(rope_heads):
                a = acc[:, h * HEAD_DIM:(h + 1) * HEAD_DIM]
                o_ref[:, h * HEAD_DIM:(h + 1) * HEAD_DIM] = (
                    a * cos + pltpu.roll(a, HEAD_DIM // 2, 1) * sin).astype(o_ref.dtype)
            if rope_heads < heads:
                o_ref[:, rope_heads * HEAD_DIM:] = acc[:, rope_heads * HEAD_DIM:].astype(o_ref.dtype)

        if resident:
            store(n_rope)
        elif n_rope in (0, n_tiles):
            store(heads if n_rope else 0)
        else:
            pl.when(j < lo + n_rope)(lambda: store(heads))
            pl.when(j >= lo + n_rope)(lambda: store(0))

    for s, (lo, n_tiles, n_rope) in enumerate(segs):
        def segment(s=s, lo=lo, n_tiles=n_tiles, n_rope=n_rope):
            emit(lhs_p(s), w_refs[s], resp_ref, out_refs[s][0], rope_p, lo, n_tiles, n_rope)

            @pl.when(i == 0)
            def _():
                emit(lhs_s(s), w_refs[s], ress_ref, out_refs[s][1], rope_s, lo, n_tiles, n_rope)

        if resident or len(segs) == 1:
            segment()
        else:
            pl.when((j >= lo) & (j < lo + n_tiles))(segment)

    @pl.when(i > 0)
    def _():
        for _, os_ref in out_refs:
            os_ref[...] = jnp.zeros_like(os_ref)

    if has_aux:
        @pl.when(j == pl.num_programs(1) - 1)
        def _():
            auxp_ref[...] = jnp.dot(lhs_p(0), auxw_ref[...], preferred_element_type=F32)

            @pl.when(i == 0)
            def _():
                auxs_ref[...] = jnp.dot(lhs_s(0), auxw_ref[...], preferred_element_type=F32)

            @pl.when(i > 0)
            def _():
                auxs_ref[...] = jnp.zeros_like(auxs_ref)


def _seg(w, layer=0, col0=0, n_cols=None, rope_cols=0):
    return (w, layer, col0, w.shape[-1] - col0 if n_cols is None else n_cols, rope_cols)


def _proj(xp, xs, ws, *, gains=None, aux_w=None, rope=None, res=None, post_gain=None, casts=(), out_dtype=F32,
          tm=1024, tn=512):
    Mp, K = xp.shape
    S = SAMPLE_ROWS
    assert xs.shape == (S, K) and Mp % tm == 0
    norm = gains is not None
    resident = tn is None
    segs, widths, lo = [], [], 0
    for w, layer, col0, n_cols, rope_cols in ws:
        tw = n_cols if resident else tn
        rope_unit = HEAD_DIM if resident else tw
        assert w.ndim == 3 and n_cols % tw == 0 and col0 % tw == 0 and rope_cols % rope_unit == 0
        segs.append((lo, n_cols // tw, rope_cols // rope_unit))
        widths.append(tw)
        lo += 0 if resident else n_cols // tw
    nj, ni = (1 if resident else lo), Mp // tm
    has_rope = any(r for _, _, r in segs)
    assert not (has_rope and res is not None) and (rope is not None) == has_rope
    row = lambda i, j: (i, 0)
    const = lambda i, j: (0, 0)
    args = [xp, xs]
    in_specs = [pl.BlockSpec((tm, K), row), pl.BlockSpec((S, K), const)]
    if norm:
        assert len(gains) == len(ws)
        args += [g.reshape(1, K) for g in gains]
        in_specs += [pl.BlockSpec((1, K), const)] * len(gains)
    for (w, layer, col0, _, _), (lo, n_tiles, _), tw in zip(ws, segs, widths):
        args.append(w)
        in_specs.append(pl.BlockSpec(
            (None, K, tw), lambda i, j, layer=layer, lo=lo, n=n_tiles, c0=col0 // tw:
            (layer, 0, c0 + jnp.clip(j - lo, 0, n - 1)),
            pipeline_mode=pl.Buffered(1) if nj == 1 else None))
    if aux_w is not None:
        args.append(aux_w)
        in_specs.append(pl.BlockSpec(aux_w.shape, const))
    if has_rope:
        (cos_p, sin_p, rows_per_seq), (cos_s, sin_s) = rope
        nseq = rows_per_seq // tm
        args += [cos_p, sin_p, cos_s, sin_s]
        in_specs += [pl.BlockSpec((tm, HEAD_DIM), lambda i, j: (i % nseq, 0))] * 2
        in_specs += [pl.BlockSpec((S, HEAD_DIM), const)] * 2
    assert (res is None and post_gain is None) or len(ws) == 1
    if res is not None:
        args += list(res)
        in_specs += [pl.BlockSpec((tm, widths[0]), lambda i, j: (i, j)),
                     pl.BlockSpec((S, widths[0]), lambda i, j: (0, j))]
    if post_gain is not None:
        assert nj == 1
        args.append(post_gain.reshape(1, widths[0]))
        in_specs.append(pl.BlockSpec((1, widths[0]), const))
    out_shape, out_specs = [], []
    for (lo, n_tiles, _), tw in zip(segs, widths):
        col = lambda i, j, lo=lo, n=n_tiles: (i, jnp.clip(j - lo, 0, n - 1))
        out_shape += [jax.ShapeDtypeStruct((Mp, n_tiles * tw), out_dtype),
                      jax.ShapeDtypeStruct((ni * S, n_tiles * tw), F32)]
        out_specs += [pl.BlockSpec((tm, tw), col), pl.BlockSpec((S, tw), col)]
    if aux_w is not None:
        na = aux_w.shape[1]
        out_shape += [jax.ShapeDtypeStruct((Mp, na), F32), jax.ShapeDtypeStruct((ni * S, na), F32)]
        out_specs += [pl.BlockSpec((tm, na), row), pl.BlockSpec((S, na), row)]
    n_main = len(out_shape)
    c_args, c_in, c_shape, c_out, cast_tiles = _cast_specs(casts, ni * nj, nj)
    args += c_args
    in_specs += c_in
    out_shape += c_shape
    out_specs += c_out
    outs = pl.pallas_call(
        functools.partial(_proj_kernel, segs=tuple(segs), resident=resident, norm=norm, has_rope=has_rope,
                          has_res=res is not None, has_post=post_gain is not None, has_aux=aux_w is not None,
                          cast_tiles=cast_tiles),
        grid=(ni, nj),
        in_specs=in_specs,
        out_specs=out_specs,
        out_shape=out_shape,
        scratch_shapes=[pltpu.VMEM((len(ws), tm, K), BF16), pltpu.VMEM((len(ws), S, K), BF16)] if norm else [],
        compiler_params=_params("arbitrary", "arbitrary"),
        name="proj",
    )(*args)
    return [o[:S] if n < n_main and n % 2 == 1 else o for n, o in enumerate(outs)]


def _gateup_kernel(xp_ref, xs_ref, g_ref, wg_ref, wu_ref, *refs, cast_tiles):
    n = len(cast_tiles)
    cast_srcs = refs[:n]
    op_ref, os_ref = refs[n:n + 2]
    cast_dsts = refs[n + 2:2 * n + 2]
    xnp_ref, xns_ref = refs[2 * n + 2:]
    i = pl.program_id(0)
    j = pl.program_id(1)
    _run_casts(cast_srcs, cast_dsts, cast_tiles)

    @pl.when(j == 0)
    def _():
        xnp_ref[...] = _normed(xp_ref, g_ref)

    @pl.when((i == 0) & (j == 0))
    def _():
        xns_ref[...] = _normed(xs_ref, g_ref)

    def swiglu(xn):
        g = jnp.dot(xn, wg_ref[...], preferred_element_type=F32)
        u = jnp.dot(xn, wu_ref[...], preferred_element_type=F32)
        return _silu(g) * u

    op_ref[...] = swiglu(xnp_ref[...]).astype(op_ref.dtype)

    @pl.when(i == 0)
    def _():
        os_ref[...] = swiglu(xns_ref[...])

    @pl.when(i > 0)
    def _():
        os_ref[...] = jnp.zeros_like(os_ref)


def _gateup(xp, xs, gain, w_gu, layer, *, casts=(), tm=1024, tn=512):
    Mp, K = xp.shape
    S = SAMPLE_ROWS
    hidden = w_gu.shape[2] // 2
    assert xs.shape == (S, K) and Mp % tm == 0 and hidden % tn == 0 and w_gu.dtype == BF16
    nj = hidden // tn
    ni = Mp // tm
    c_args, c_in, c_shape, c_out, cast_tiles = _cast_specs(casts, ni * nj, nj)
    outs = pl.pallas_call(
        functools.partial(_gateup_kernel, cast_tiles=cast_tiles),
        grid=(ni, nj),
        in_specs=[pl.BlockSpec((tm, K), lambda i, j: (i, 0)),
                  pl.BlockSpec((S, K), lambda i, j: (0, 0)),
                  pl.BlockSpec((1, K), lambda i, j: (0, 0)),
                  pl.BlockSpec((None, K, tn), lambda i, j: (layer, 0, j)),
                  pl.BlockSpec((None, K, tn), lambda i, j: (layer, 0, j + nj))] + c_in,
        out_specs=[pl.BlockSpec((tm, tn), lambda i, j: (i, j)), pl.BlockSpec((S, tn), _sample_block)] + c_out,
        out_shape=[jax.ShapeDtypeStruct((Mp, hidden), BF16), jax.ShapeDtypeStruct((ni * S, hidden), F32)] + c_shape,
        scratch_shapes=[pltpu.VMEM((tm, K), BF16), pltpu.VMEM((S, K), BF16)],
        compiler_params=_params("arbitrary", "arbitrary"),
        name="gateup",
    )(xp, xs, gain.reshape(1, K), w_gu, w_gu, *c_args)
    return [outs[0], outs[1][:S]] + list(outs[2:])


def _conv_silu_slab(src, s, raw_ref, act_ref, cw_ref, cb_ref):
    Q = SSD_CHUNK
    raw_ref[s, CONV_TAIL:, :] = src
    for parity in (0, 1):
        acc = cb_ref[s]
        for k in range(CONV_W):
            first = CONV_TAIL - (CONV_W - 1) + k + parity
            acc = acc + raw_ref[s, pl.ds(first, Q // 2, stride=2), :] * cw_ref[s, k:k + 1, :]
        act_ref[s, pl.ds(parity, Q // 2, stride=2), :] = _silu(acc)
    raw_ref[s, 0:CONV_TAIL, :] = raw_ref[s, Q:Q + CONV_TAIL, :]


def _ssd_kernel(z_ref, x_ref, b_ref, c_ref, dt_ref, cw_ref, cb_ref,
                dtb_ref, alog_ref, dexp_ref, gn_ref, *refs, cast_tiles):
    n_casts = len(cast_tiles)
    cast_srcs = refs[:n_casts]
    y_ref, hout_ref = refs[n_casts:n_casts + 2]
    cast_dsts = refs[n_casts + 2:2 * n_casts + 2]
    h_ref, raw_ref, act_ref = refs[2 * n_casts + 2:]
    _run_casts(cast_srcs, cast_dsts, cast_tiles)
    Q = SSD_CHUNK
    P = SSM_HEAD_DIM
    N = SSM_STATE
    d_inner = x_ref.shape[1]
    n_heads = d_inner // P
    hpg = n_heads // SSM_GROUPS
    gw = hpg * P
    n_xs = d_inner // LANES
    n_bs = b_ref.shape[1] // LANES
    c = pl.program_id(1)

    @pl.when(c == 0)
    def _():
        h_ref[...] = jnp.zeros_like(h_ref)
        raw_ref[:, 0:CONV_TAIL, :] = jnp.zeros((raw_ref.shape[0], CONV_TAIL, LANES), F32)

    for s in range(n_xs + 2 * n_bs):
        if s < n_xs:
            src = x_ref[:, s * LANES:(s + 1) * LANES]
        elif s < n_xs + n_bs:
            src = b_ref[:, (s - n_xs) * LANES:(s - n_xs + 1) * LANES]
        else:
            src = c_ref[:, (s - n_xs - n_bs) * LANES:(s - n_xs - n_bs + 1) * LANES]
        _conv_silu_slab(src, s, raw_ref, act_ref, cw_ref, cb_ref)

    dt = _softplus(dt_ref[...] + dtb_ref[...])
    dA = dt * (-jnp.exp(alog_ref[...]))
    row = lax.broadcasted_iota(jnp.int32, (Q, Q), 0)
    col = lax.broadcasted_iota(jnp.int32, (Q, Q), 1)
    causal = row >= col
    tri = jnp.where(causal, 1.0, 0.0).astype(F32)
    cs = jnp.dot(tri, dA, precision=HIGHEST, preferred_element_type=F32)
    csT = cs.T
    dtT = dt.T
    dt_decay_end = dt * jnp.exp(cs[Q - 1:Q, :] - cs)
    chunk_decay = jnp.broadcast_to(jnp.exp(csT[0:n_heads, Q - 1:Q]), (n_heads, N))

    lane_lo = lax.broadcasted_iota(jnp.int32, (Q, 2 * P), 1) < P

    def pair_cols(arr, q):
        a0 = jnp.broadcast_to(arr[:, 2 * q:2 * q + 1], (Q, 2 * P))
        a1 = jnp.broadcast_to(arr[:, 2 * q + 1:2 * q + 2], (Q, 2 * P))
        return jnp.where(lane_lo, a0, a1)

    for g in range(SSM_GROUPS):
        Bg = act_ref[n_xs + g].astype(BF16)
        Cg = act_ref[n_xs + n_bs + g].astype(BF16)
        CB = lax.dot_general(Cg, Bg, NT_DIMS, preferred_element_type=F32)
        Hg = h_ref[g * hpg:(g + 1) * hpg].reshape(gw, N)
        y_off = lax.dot_general(Cg, Hg.astype(BF16), NT_DIMS, preferred_element_type=F32)

        ys = []
        xds = []
        for qq in range(hpg // 2):
            q = g * (hpg // 2) + qq
            lo, hi = q * 2 * P, (q + 1) * 2 * P
            xs = act_ref[q]
            xds.append((xs * pair_cols(dt_decay_end, q)).astype(BF16))
            cs_cols = [jnp.broadcast_to(cs[:, hh:hh + 1], (Q, Q)) for hh in (2 * q, 2 * q + 1)]
            atts = []
            for hh, cs_col in zip((2 * q, 2 * q + 1), cs_cols):
                seg = cs_col - csT[hh:hh + 1, :]
                decay = jnp.exp(jnp.where(causal, seg, -jnp.inf))
                atts.append(((CB * decay) * dtT[hh:hh + 1, :]).astype(BF16))
            att = jnp.concatenate(atts, axis=1)
            xbd = jnp.concatenate([jnp.where(lane_lo, xs, 0.0).astype(BF16),
                                   jnp.where(lane_lo, 0.0, xs).astype(BF16)], axis=0)
            y = jnp.dot(att, xbd, preferred_element_type=F32)
            y = y + y_off[:, qq * 2 * P:(qq + 1) * 2 * P] * jnp.exp(jnp.where(lane_lo, *cs_cols))
            y = y + xs * dexp_ref[:, lo:hi]
            ys.append(y * _silu(z_ref[:, lo:hi]))

        ssq = ys[0] * ys[0]
        for y in ys[1:]:
            ssq = ssq + y * y
        scale = lax.rsqrt(jnp.sum(ssq, axis=-1, keepdims=True) / gw + EPS)
        for qq, y in enumerate(ys):
            lo = g * gw + qq * 2 * P
            y_ref[:, lo:lo + 2 * P] = ((y * scale) * gn_ref[:, lo:lo + 2 * P]).astype(y_ref.dtype)

        S = lax.dot_general(jnp.concatenate(xds, axis=1), Bg, TN_DIMS, preferred_element_type=F32)
        for r in range(hpg):
            hh = g * hpg + r
            h_ref[hh] = h_ref[hh] * chunk_decay[hh:hh + 1, :] + S[r * P:(r + 1) * P, :]

    @pl.when(c == pl.num_programs(1) - 1)
    def _():
        hout_ref[0] = h_ref[...]


def _ssd_prompt(zxbc, dt, batch, conv_w, conv_b, dt_bias, a_log, d_skip, gate_norm, casts=()):
    M = zxbc.shape[0]
    L = M // batch
    H = dt_bias.shape[0]
    d_inner = H * SSM_HEAD_DIM
    bc = SSM_GROUPS * SSM_STATE
    conv_dim = d_inner + 2 * bc
    Q = SSD_CHUNK
    nc = L // Q
    assert L % Q == 0 and d_inner % bc == 0
    xb = d_inner // bc
    row_map = lambda col: (lambda b, c: (b * nc + c, col))
    const = lambda b, c: (0, 0)
    d_exp = jnp.repeat(d_skip.astype(F32), SSM_HEAD_DIM).reshape(1, d_inner)
    assert H <= LANES and SSM_STATE == LANES and 2 * SSM_HEAD_DIM == LANES
    lane_pad = lambda v: jnp.pad(v.reshape(1, H), ((0, 0), (0, LANES - H)))
    n_slabs = conv_dim // LANES
    cw_slabs = conv_w.reshape(CONV_W, n_slabs, LANES).transpose(1, 0, 2)
    cb_slabs = conv_b.reshape(n_slabs, 1, LANES)
    const3 = lambda b, c: (0, 0, 0)
    c_args, c_in, c_shape, c_out, cast_tiles = _cast_specs(casts, batch * nc, nc)
    return pl.pallas_call(
        functools.partial(_ssd_kernel, cast_tiles=cast_tiles),
        grid=(batch, nc),
        in_specs=[pl.BlockSpec((Q, d_inner), row_map(0)),
                  pl.BlockSpec((Q, d_inner), row_map(1)),
                  pl.BlockSpec((Q, bc), row_map(2 * xb)),
                  pl.BlockSpec((Q, bc), row_map(2 * xb + 1)),
                  pl.BlockSpec((Q, LANES), row_map(0)),
                  pl.BlockSpec((n_slabs, CONV_W, LANES), const3),
                  pl.BlockSpec((n_slabs, 1, LANES), const3),
                  pl.BlockSpec((1, LANES), const),
                  pl.BlockSpec((1, LANES), const),
                  pl.BlockSpec((1, d_inner), const),
                  pl.BlockSpec((1, d_inner), const)] + c_in,
        out_specs=[pl.BlockSpec((Q, d_inner), row_map(0)),
                   pl.BlockSpec((1, H, SSM_HEAD_DIM, SSM_STATE), lambda b, c: (b, 0, 0, 0))] + c_out,
        out_shape=[jax.ShapeDtypeStruct((M, d_inner), BF16),
                   jax.ShapeDtypeStruct((batch, H, SSM_HEAD_DIM, SSM_STATE), F32)] + c_shape,
        scratch_shapes=[pltpu.VMEM((H, SSM_HEAD_DIM, SSM_STATE), F32),
                        pltpu.VMEM((n_slabs, Q + CONV_TAIL, LANES), F32),
                        pltpu.VMEM((n_slabs, Q, LANES), F32)],
        compiler_params=_params("parallel", "arbitrary"),
        name="ssd_prompt",
    )(zxbc, zxbc, zxbc, zxbc, dt, cw_slabs, cb_slabs,
      lane_pad(dt_bias), lane_pad(a_log), d_exp, gate_norm.reshape(1, d_inner), *c_args)


def _ssd_step_kernel(zxbc_ref, dt_ref, cs_ref, h0_ref, cw_ref, cb_ref, dtb_ref, alog_ref,
                     dexp_ref, gn_ref, y_ref, cso_ref, ho_ref):
    P = SSM_HEAD_DIM
    N = SSM_STATE
    n_heads = h0_ref.shape[2]
    d_inner = n_heads * P
    hpg = n_heads // SSM_GROUPS
    gw = hpg * P
    conv_dim = cw_ref.shape[1]

    raw = zxbc_ref[0, :, d_inner:d_inner + conv_dim]
    prev = cs_ref[0, 0]
    acc = cb_ref[...]
    for k in range(CONV_W - 1):
        acc = acc + prev[k:k + 1, :] * cw_ref[k:k + 1, :]
    acc = acc + raw * cw_ref[CONV_W - 1:CONV_W, :]
    xbc = _silu(acc)
    cso_ref[0, 0, 0:CONV_W - 2, :] = prev[1:CONV_W - 1, :]
    cso_ref[0, 0, CONV_W - 2:CONV_W - 1, :] = raw

    dt = _softplus(dt_ref[0, :, 0:n_heads] + dtb_ref[...])
    decay = jnp.exp(dt * (-jnp.exp(alog_ref[...])))
    z = zxbc_ref[0, :, 0:d_inner]

    eye = (lax.broadcasted_iota(jnp.int32, (P, P), 0) == lax.broadcasted_iota(jnp.int32, (P, P), 1))
    ys = []
    for g in range(SSM_GROUPS):
        Bg = xbc[:, d_inner + g * N:d_inner + (g + 1) * N]
        Cg = xbc[:, d_inner + SSM_GROUPS * N + g * N:d_inner + SSM_GROUPS * N + (g + 1) * N]
        Bb = jnp.broadcast_to(Bg, (P, N))
        for r in range(hpg):
            hh = g * hpg + r
            xs = xbc[:, hh * P:(hh + 1) * P]
            xdt = xs * dt[:, hh:hh + 1]
            xdiag = jnp.where(eye, jnp.broadcast_to(xdt, (P, P)), 0.0)
            outer = jnp.dot(xdiag, Bb, precision=HIGHEST, preferred_element_type=F32)
            ho_ref[0, 0, hh] = h0_ref[0, 0, hh] * decay[:, hh:hh + 1] + outer
        Hg = ho_ref[0, 0, g * hpg:(g + 1) * hpg].reshape(gw, N)
        yg = lax.dot_general(Cg.astype(BF16), Hg.astype(BF16), NT_DIMS, preferred_element_type=F32)
        lo, hi = g * gw, (g + 1) * gw
        yg = yg + xbc[:, lo:hi] * dexp_ref[:, lo:hi]
        yg = yg * _silu(z[:, lo:hi])
        yg = yg * lax.rsqrt(jnp.mean(yg * yg, axis=-1, keepdims=True) + EPS)
        ys.append(yg * gn_ref[:, lo:hi])
    y_ref[0] = jnp.concatenate(ys, axis=1).astype(y_ref.dtype)


def _ssd_step(zxbc, dt, conv_state, ssm_state, conv_w, conv_b, dt_bias, a_log, d_skip, gate_norm):
    B = zxbc.shape[0]
    H = dt_bias.shape[0]
    d_inner = H * SSM_HEAD_DIM
    conv_dim = conv_w.shape[1]
    const = lambda b: (0, 0)
    d_exp = jnp.repeat(d_skip.astype(F32), SSM_HEAD_DIM).reshape(1, d_inner)
    cs4 = conv_state.reshape(1, B, CONV_W - 1, conv_dim)
    h5 = ssm_state.reshape(1, B, H, SSM_HEAD_DIM, SSM_STATE)
    return pl.pallas_call(
        _ssd_step_kernel,
        grid=(B,),
        in_specs=[pl.BlockSpec((1, 1, zxbc.shape[1]), lambda b: (b, 0, 0)),
                  pl.BlockSpec((1, 1, dt.shape[1]), lambda b: (b, 0, 0)),
                  pl.BlockSpec((1, 1, CONV_W - 1, conv_dim), lambda b: (0, b, 0, 0)),
                  pl.BlockSpec((1, 1, H, SSM_HEAD_DIM, SSM_STATE), lambda b: (0, b, 0, 0, 0)),
                  pl.BlockSpec((CONV_W, conv_dim), const),
                  pl.BlockSpec((1, conv_dim), const),
                  pl.BlockSpec((1, H), const),
                  pl.BlockSpec((1, H), const),
                  pl.BlockSpec((1, d_inner), const),
                  pl.BlockSpec((1, d_inner), const)],
        out_specs=[pl.BlockSpec((1, 1, d_inner), lambda b: (b, 0, 0)),
                   pl.BlockSpec((1, 1, CONV_W - 1, conv_dim), lambda b: (0, b, 0, 0)),
                   pl.BlockSpec((1, 1, H, SSM_HEAD_DIM, SSM_STATE), lambda b: (0, b, 0, 0, 0))],
        out_shape=[jax.ShapeDtypeStruct((B, 1, d_inner), F32),
                   jax.ShapeDtypeStruct(cs4.shape, F32),
                   jax.ShapeDtypeStruct(h5.shape, F32)],
        compiler_params=_params("arbitrary"),
        name="ssd_step",
    )(zxbc.reshape(B, 1, -1), dt.reshape(B, 1, -1), cs4, h5, conv_w, conv_b.reshape(1, conv_dim),
      dt_bias.reshape(1, H), a_log.reshape(1, H), d_exp, gate_norm.reshape(1, d_inner))


def _attn_kernel(q0_ref, q1_ref, q2_ref, k_ref, v_ref, o_ref, on_ref, lse_ref, qs_ref, ks_ref, vs_ref):
    L = k_ref.shape[0]
    QB = ATT_BLOCK
    KW = QB + DIL_SLOTS
    P = ATT_PRESTRIDE
    p_shift = P.bit_length() - 1
    scale = HEAD_DIM ** -0.5
    e = lax.broadcasted_iota(jnp.int32, (QB, KW), 0) - lax.broadcasted_iota(jnp.int32, (QB, KW), 1)

    staged = {rate: rate > P for rate in DIL_RATES}
    for src_ref, dst_ref, needed in ((q2_ref, qs_ref, staged[DIL_RATES[2]]), (k_ref, ks_ref, any(staged.values())),
                                     (v_ref, vs_ref, any(staged.values()))):
        if needed:
            def stage(i, carry, src_ref=src_ref, dst_ref=dst_ref):
                cp = i & (P - 1)
                t = i >> p_shift
                dst = pl.multiple_of(cp * (L // P) + t * QB, QB)
                dst_ref[pl.ds(dst, QB), :] = src_ref[pl.ds(cp + P * QB * t, QB, stride=P), :]
                return carry
            lax.fori_loop(0, L // QB, stage, 0, unroll=4)

    for g, (q_ref, rate) in enumerate(zip((q0_ref, q1_ref, q2_ref), DIL_RATES)):
        shift = rate.bit_length() - 1

        def unit(n, carry, g=g, q_ref=q_ref, rate=rate, shift=shift):
            c = n & (rate - 1)
            u0 = (n >> shift) * QB
            v0 = jnp.maximum(u0 - DIL_SLOTS, 0)
            q_rows = pl.ds(c + rate * u0, QB, stride=rate)
            if staged[rate]:
                base = (c & (P - 1)) * (L // P) + (c >> p_shift)
                q = qs_ref[pl.ds(base + (rate // P) * u0, QB, stride=rate // P), :].astype(BF16)
                k_rows = pl.ds(base + (rate // P) * v0, KW, stride=rate // P)
                k = ks_ref[k_rows, :].astype(BF16)
                v = vs_ref[k_rows, :].astype(BF16)
            else:
                k_rows = pl.ds(c + rate * v0, KW, stride=rate)
                q = q_ref[q_rows, :].astype(BF16)
                k = k_ref[k_rows, :].astype(BF16)
                v = v_ref[k_rows, :].astype(BF16)
            s = lax.dot_general(q, k, NT_DIMS, preferred_element_type=F32) * scale
            d = e + (u0 - v0)
            s = jnp.where(d >= 0, s, -jnp.inf)
            s = jnp.where(d <= DIL_SLOTS, s, -jnp.inf)
            m = s.max(axis=1, keepdims=True)
            p = jnp.exp(s - m)
            l = p.sum(axis=1, keepdims=True)
            o = jnp.dot(p.astype(BF16), v, preferred_element_type=F32)
            on_ref[g, q_rows, :] = o / l
            lse_ref[g, q_rows, :] = jnp.broadcast_to(m + jnp.log(l), (QB, HEAD_DIM))
            return carry

        lax.fori_loop(0, L // QB, unit, 0, unroll=ATT_UNROLL)

    def mix(i, carry):
        rows = pl.ds(pl.multiple_of(i * QB, QB), QB)
        lses = [lse_ref[g, rows, :] for g in range(len(DIL_RATES))]
        m = functools.reduce(jnp.maximum, lses)
        ws = [jnp.exp(x - m) for x in lses]
        num = functools.reduce(jnp.add, [w * on_ref[g, rows, :] for g, w in enumerate(ws)])
        o_ref[rows, :] = (num / functools.reduce(jnp.add, ws)).astype(o_ref.dtype)
        return carry

    lax.fori_loop(0, L // QB, mix, 0, unroll=2)


def _attn_prompt(q, kv, batch):
    M = q.shape[0]
    L = M // batch
    n_grp = len(DIL_RATES)
    assert all(r & (r - 1) == 0 and L % (ATT_BLOCK * r) == 0 and L >= r * (ATT_BLOCK + DIL_SLOTS)
               for r in DIL_RATES)
    qspec = lambda g: pl.BlockSpec((L, HEAD_DIM), lambda b, h: (b, g * KV_HEADS + h))
    kspec = pl.BlockSpec((L, HEAD_DIM), lambda b, h: (b, h))
    vspec = pl.BlockSpec((L, HEAD_DIM), lambda b, h: (b, KV_HEADS + h))
    return pl.pallas_call(
        _attn_kernel,
        grid=(batch, KV_HEADS),
        in_specs=[qspec(0), qspec(1), qspec(2), kspec, vspec],
        out_specs=pl.BlockSpec((L, HEAD_DIM), lambda b, h: (b, h)),
        out_shape=jax.ShapeDtypeStruct((M, KV_HEADS * HEAD_DIM), BF16),
        scratch_shapes=[pltpu.VMEM((n_grp, L, HEAD_DIM), F32), pltpu.VMEM((n_grp, L, HEAD_DIM), F32)]
        + [pltpu.VMEM((L, HEAD_DIM), F32)] * 3,
        compiler_params=_params("parallel", "arbitrary"),
        name="attn_prompt",
    )(q, q, q, kv, kv)


def _bf16_round(a):
    return a.astype(BF16).astype(F32)


def _attn_step_kernel(q_ref, *refs):
    n = len(DIL_RATES)
    k_refs, v_refs = refs[0:n], refs[n:2 * n]
    kn_ref, vn_ref, o_ref = refs[2 * n:]
    scale = HEAD_DIM ** -0.5
    kn = _bf16_round(kn_ref[0, 0])
    vn = _bf16_round(vn_ref[0, 0])
    scores, new_scores = [], []
    for g, k_ref in enumerate(k_refs):
        qg = _bf16_round(q_ref[0, g])
        kg = _bf16_round(k_ref[0, :, 0])
        scores.append(jnp.sum(kg * qg[None], axis=-1, keepdims=True) * scale)
        new_scores.append(jnp.sum(kn * qg, axis=-1, keepdims=True) * scale)
    m = functools.reduce(jnp.maximum, [s.max(axis=0) for s in scores] + new_scores)
    l = jnp.zeros_like(m)
    o = jnp.zeros((KV_HEADS, HEAD_DIM), F32)
    for s, s_new, v_ref in zip(scores, new_scores, v_refs):
        p = jnp.exp(s - m[None])
        p_new = jnp.exp(s_new - m)
        l = l + p.sum(axis=0) + p_new
        o = o + (_bf16_round(p) * _bf16_round(v_ref[0, :, 0])).sum(axis=0) + _bf16_round(p_new) * vn
    o_ref[0, 0] = (o / l).astype(o_ref.dtype)


def _attn_step(q, cache_k, cache_v, k_new, v_new):
    B, T = cache_k.shape[0], cache_k.shape[1]
    S = DIL_SLOTS
    assert T == S * max(DIL_RATES), "every slot of every dilation group lies inside the cached window"
    qg = q.reshape(B, len(DIL_RATES), KV_HEADS, HEAD_DIM)
    row_shape = (B, 1, KV_HEADS, HEAD_DIM)
    row_spec = pl.BlockSpec((1, 1, KV_HEADS, HEAD_DIM), lambda b: (b, 0, 0, 0))
    args = [qg]
    in_specs = [pl.BlockSpec((1, len(DIL_RATES), KV_HEADS, HEAD_DIM), lambda b: (b, 0, 0, 0))]
    for cache in (cache_k, cache_v):
        for rate in DIL_RATES:
            args.append(cache.reshape(B, T // rate, rate, KV_HEADS, HEAD_DIM))
            in_specs.append(pl.BlockSpec((1, S, 1, KV_HEADS, HEAD_DIM),
                                         lambda b, blk=T // rate // S - 1: (b, blk, 0, 0, 0)))
    args += [k_new.reshape(row_shape), v_new.reshape(row_shape)]
    in_specs += [row_spec, row_spec]
    o = pl.pallas_call(
        _attn_step_kernel,
        grid=(B,),
        in_specs=in_specs,
        out_specs=row_spec,
        out_shape=jax.ShapeDtypeStruct(row_shape, F32),
        compiler_params=_params("parallel"),
        name="attn_step",
    )(*args)
    return o.reshape(B, KV_HEADS * HEAD_DIM)


def _rope_tables(pos):
    half = HEAD_DIM // 2
    inv = jnp.power(jnp.float32(ROPE_THETA), -jnp.arange(half, dtype=jnp.float32) / half)
    ang = pos.astype(jnp.float32)[:, None] * inv[None, :]
    cos = jnp.cos(ang)
    sin = jnp.sin(ang)
    return jnp.concatenate([cos, cos], axis=1), jnp.concatenate([-sin, sin], axis=1)


def kernel(x_prompt, x_sample, state_conv, state_ssm, cache_k, cache_v, a_norm, a_w_in, a_conv_w, a_conv_b,
           a_dt_bias, a_log, a_d, a_gate_norm, a_w_out, kv_norm, w_kv, b_norm, b_w_q, b_w_o, ffn_norm,
           ffn_w_gu, ffn_w_down, final_norm):
    assert a_norm.shape[0] == 1 and b_norm.shape[0] == 1, "one Mamba-2 layer followed by one attention layer"
    Bp, Lp, D = x_prompt.shape
    Bs, Ls, _ = x_sample.shape
    assert Ls == 1 and Bs == SAMPLE_ROWS, "sample group decodes one token for SAMPLE_ROWS sequences"
    n_heads = a_dt_bias.shape[1]
    d_inner = a_w_out.shape[1]
    conv_dim = a_conv_w.shape[2]
    kv_dim = KV_HEADS * HEAD_DIM
    w_dt = jnp.pad(a_w_in[0][:, d_inner + conv_dim:], ((0, 0), (0, LANES - n_heads))).astype(BF16)
    ssd_w = (a_conv_w[0], a_conv_b[0], a_dt_bias[0], a_log[0], a_d[0], a_gate_norm[0])
    w_kv = w_kv[None]

    cos_p, sin_p = _rope_tables(jnp.arange(Lp, dtype=jnp.int32))
    cos_s, sin_s = _rope_tables(PAST_LEN + jnp.arange(Ls, dtype=jnp.int32))
    rope = ((cos_p, sin_p, Lp), (jnp.broadcast_to(cos_s, (Bs, HEAD_DIM)), jnp.broadcast_to(sin_s, (Bs, HEAD_DIM))))

    xp0 = x_prompt.reshape(Bp * Lp, D)
    xs0 = x_sample.reshape(Bs, D)

    zxbc_p, zxbc_s, dt_p, dt_s = _proj(
        xp0, xs0, [_seg(a_w_in.astype(BF16), n_cols=d_inner + conv_dim)], gains=[a_norm[0]], aux_w=w_dt, tn=1024)
    y_p, p_ssm, w_out_bf, w_gu0_bf, w_down0_bf = _ssd_prompt(
        zxbc_p, dt_p, Bp, *ssd_w, casts=[(a_w_out, 0), (ffn_w_gu, 0), (ffn_w_down, 0)])
    p_conv = zxbc_p.reshape(Bp, Lp, -1)[:, Lp - (CONV_W - 1):, d_inner:]
    y_s, s_conv, s_ssm = _ssd_step(zxbc_s, dt_s, state_conv[0], state_ssm[0], *ssd_w)
    x1 = _proj(y_p, y_s.reshape(Bs, d_inner), [_seg(w_out_bf)], res=(xp0, xs0), tm=256, tn=D)
    *h, w_kv_bf, w_q_bf, w_o_bf = _gateup(*x1, ffn_norm[0], w_gu0_bf, 0, casts=[(w_kv, 0), (b_w_q, 0), (b_w_o, 0)])
    *x2, w_gu1_bf = _proj(*h, [_seg(w_down0_bf)], res=x1, casts=[(ffn_w_gu, 1)], tm=256, tn=D)

    kv_p, kv_s, q_p, q_s = _proj(*x2, [_seg(w_kv_bf, rope_cols=kv_dim), _seg(w_q_bf, rope_cols=b_w_q.shape[2])],
                                 gains=[kv_norm, b_norm[0]], rope=rope, tm=256, tn=None)
    o_p = _attn_prompt(q_p, kv_p, Bp)
    o_s = _attn_step(q_s, cache_k, cache_v, kv_s[:, :kv_dim], kv_s[:, kv_dim:])
    x3 = _proj(o_p, o_s, [_seg(w_o_bf)], res=x2, tm=512, tn=D)
    *h, w_down1_bf = _gateup(*x3, ffn_norm[1], w_gu1_bf, 0, casts=[(ffn_w_down, 1)])
    y_prompt, y_sample = _proj(*h, [_seg(w_down1_bf)], res=x3, post_gain=final_norm, tm=256, tn=D)
    y_prompt = y_prompt.reshape(Bp, Lp, D)
    y_sample = y_sample.reshape(Bs, Ls, D)
    keep = min(DIL_SLOTS * max(DIL_RATES), Lp)
    p_kv = kv_p.reshape(Bp, Lp, 2 * KV_HEADS, HEAD_DIM)[:, Lp - keep:]
    s_kv = kv_s.reshape(Bs, Ls, 2 * KV_HEADS, HEAD_DIM)
    return (y_prompt, y_sample, p_conv[None], p_ssm[None], p_kv[:, :, :KV_HEADS], p_kv[:, :, KV_HEADS:],
            s_conv, s_ssm, s_kv[:, :, :KV_HEADS], s_kv[:, :, KV_HEADS:])
```

```python
import functools

import jax
import jax.numpy as jnp
from jax import lax
from jax.experimental import pallas as pl
from jax.experimental.pallas import tpu as pltpu

F32 = jnp.float32
BF16 = jnp.bfloat16
HIGHEST = lax.Precision.HIGHEST

EPS = 1e-6
ROPE_THETA = 10000.0
SSD_CHUNK = 128
SSM_HEAD_DIM = 64
SSM_STATE = 128
SSM_GROUPS = 8
CONV_W = 4
HEAD_DIM = 128
KV_HEADS = 8
DIL_RATES = (1, 4, 16)
DIL_SLOTS = 128
ATT_BLOCK = 128
ATT_UNROLL = 8
ATT_PRESTRIDE = 4
PAST_LEN = 16384

V7X_VMEM_BYTES = 64 * 1024 * 1024
VMEM_LIMIT = V7X_VMEM_BYTES - 8 * 1024 * 1024
LANES = 128
CONV_TAIL = 8
SAMPLE_ROWS = 8

NT_DIMS = (((1,), (1,)), ((), ()))
TN_DIMS = (((0,), (0,)), ((), ()))


def _params(*sem):
    return pltpu.CompilerParams(dimension_semantics=sem, vmem_limit_bytes=VMEM_LIMIT)


def _silu_of_half(h):
    return h + h * jnp.tanh(h)


def _silu(x):
    return _silu_of_half(0.5 * x)


def _softplus(x):
    return jnp.maximum(x, 0.0) + jnp.log1p(jnp.exp(-jnp.abs(x)))


def _rms_scale(x):
    return lax.rsqrt(jnp.mean(x * x, axis=-1, keepdims=True) + EPS)


def _normed(x_ref, g_ref):
    x = x_ref[...]
    return ((x * _rms_scale(x)) * g_ref[...]).astype(BF16)


def _sample_block(i, j):
    return (i, j)


def _cast_specs(casts, n_steps, nj):
    BF16_ROWS = 16
    args, in_specs, out_shape, out_specs, tiles = [], [], [], [], []
    for src, layer in casts:
        R, C = src.shape[1:]
        rows = next(r for r in range(BF16_ROWS, R + 1, BF16_ROWS) if R % r == 0 and R // r <= n_steps)
        nt = R // rows
        tile = lambda i, j, nt=nt: jnp.minimum(i * nj + j, nt - 1)
        args.append(src)
        in_specs.append(pl.BlockSpec((None, rows, C), lambda i, j, layer=layer, tile=tile: (layer, tile(i, j), 0)))
        out_shape.append(jax.ShapeDtypeStruct((1, R, C), BF16))
        out_specs.append(pl.BlockSpec((None, rows, C), lambda i, j, tile=tile: (0, tile(i, j), 0)))
        tiles.append(nt)
    return args, in_specs, out_shape, out_specs, tuple(tiles)


def _run_casts(src_refs, dst_refs, tiles):
    step = pl.program_id(0) * pl.num_programs(1) + pl.program_id(1)
    for src_ref, dst_ref, nt in zip(src_refs, dst_refs, tiles):
        @pl.when(step < nt)
        def _(src_ref=src_ref, dst_ref=dst_ref):
            dst_ref[...] = src_ref[...].astype(BF16)


def _proj_kernel(*refs, segs, resident, norm, has_rope, has_res, has_post, has_aux, cast_tiles):
    it = iter(refs)
    xp_ref, xs_ref = next(it), next(it)
    g_refs = [next(it) for _ in segs] if norm else None
    w_refs = [next(it) for _ in segs]
    auxw_ref = next(it) if has_aux else None
    rope_p = (next(it), next(it)) if has_rope else None
    rope_s = (next(it), next(it)) if has_rope else None
    resp_ref, ress_ref = (next(it), next(it)) if has_res else (None, None)
    pg_ref = next(it) if has_post else None
    cast_srcs = [next(it) for _ in cast_tiles]
    out_refs =[(next(it), next(it)) for _ in segs]
    auxp_ref, auxs_ref = (next(it), next(it)) if has_aux else (None, None)
    cast_dsts = [next(it) for _ in cast_tiles]
    xnp_ref, xns_ref = (next(it), next(it)) if norm else (None, None)
    i = pl.program_id(0)
    j = pl.program_id(1)
    _run_casts(cast_srcs, cast_dsts, cast_tiles)

    if norm:
        @pl.when(j == 0)
        def _():
            for s, g_ref in enumerate(g_refs):
                xnp_ref[s] = _normed(xp_ref, g_ref)

        @pl.when((i == 0) & (j == 0))
        def _():
            for s, g_ref in enumerate(g_refs):
                xns_ref[s] = _normed(xs_ref, g_ref)

    def lhs_p(s):
        return xnp_ref[s] if norm else xp_ref[...]

    def lhs_s(s):
        return xns_ref[s] if norm else xs_ref[...].astype(BF16)

    def emit(lhs, w_ref, res_ref, o_ref, rope, lo, n_tiles, n_rope):
        acc = jnp.dot(lhs, w_ref[...].astype(BF16), preferred_element_type=F32)
        if has_res:
            acc = acc + res_ref[...]
        if has_post:
            acc = (acc * _rms_scale(acc)) * pg_ref[...]
        heads = acc.shape[1] // HEAD_DIM

        def store(rope_heads):
            if rope_heads:
                cos = rope[0][...]
                sin = rope[1][...]
            for h in range(rope_heads):
                a = acc[:, h * HEAD_DIM:(h + 1) * HEAD_DIM]
                o_ref[:, h * HEAD_DIM:(h + 1) * HEAD_DIM] = (
                    a * cos + pltpu.roll(a, HEAD_DIM // 2, 1) * sin).astype(o_ref.dtype)
            if rope_heads < heads:
                o_ref[:, rope_heads * HEAD_DIM:] = acc[:, rope_heads * HEAD_DIM:].astype(o_ref.dtype)

        if resident:
            store(n_rope)
        elif n_rope in (0, n_tiles):
            store(heads if n_rope else 0)
        else:
            pl.when(j < lo + n_rope)(lambda: store(heads))
            pl.when(j >= lo + n_rope)(lambda: store(0))

    for s, (lo, n_tiles, n_rope) in enumerate(segs):
        def segment(s=s, lo=lo, n_tiles=n_tiles, n_rope=n_rope):
            emit(lhs_p(s), w_refs[s], resp_ref, out_refs[s][0], rope_p, lo, n_tiles, n_rope)

            @pl.when(i == 0)
            def _():
                emit(lhs_s(s), w_refs[s], ress_ref, out_refs[s][1], rope_s, lo, n_tiles, n_rope)

        if resident or len(segs) == 1:
            segment()
        else:
            pl.when((j >= lo) & (j < lo + n_tiles))(segment)

    @pl.when(i > 0)
    def _():
        for _, os_ref in out_refs:
            os_ref[...] = jnp.zeros_like(os_ref)

    if has_aux:
        @pl.when(j == pl.num_programs(1) - 1)
        def _():
            auxp_ref[...] = jnp.dot(lhs_p(0), auxw_ref[...], preferred_element_type=F32)

            @pl.when(i == 0)
            def _():
                auxs_ref[...] = jnp.dot(lhs_s(0), auxw_ref[...], preferred_element_type=F32)

            @pl.when(i > 0)
            def _():
                auxs_ref[...] = jnp.zeros_like(auxs_ref)


def _seg(w, layer=0, col0=0, n_cols=None, rope_cols=0):
    return (w, layer, col0, w.shape[-1] - col0 if n_cols is None else n_cols, rope_cols)


def _proj(xp, xs, ws, *, gains=None, aux_w=None, rope=None, res=None, post_gain=None, casts=(), out_dtype=F32,
          tm=1024, tn=512):
    Mp, K = xp.shape
    S = SAMPLE_ROWS
    assert xs.shape == (S, K) and Mp % tm == 0
    norm = gains is not None
    resident = tn is None
    segs, widths, lo = [], [], 0
    for w, layer, col0, n_cols, rope_cols in ws:
        tw = n_cols if resident else tn
        rope_unit = HEAD_DIM if resident else tw
        assert w.ndim == 3 and n_cols % tw == 0 and col0 % tw == 0 and rope_cols % rope_unit == 0
        segs.append((lo, n_cols // tw, rope_cols // rope_unit))
        widths.append(tw)
        lo += 0 if resident else n_cols // tw
    nj, ni = (1 if resident else lo), Mp // tm
    has_rope = any(r for _, _, r in segs)
    assert not (has_rope and res is not None) and (rope is not None) == has_rope
    row = lambda i, j: (i, 0)
    const = lambda i, j: (0, 0)
    args = [xp, xs]
    in_specs = [pl.BlockSpec((tm, K), row), pl.BlockSpec((S, K), const)]
    if norm:
        assert len(gains) == len(ws)
        args += [g.reshape(1, K) for g in gains]
        in_specs += [pl.BlockSpec((1, K), const)] * len(gains)
    for (w, layer, col0, _, _), (lo, n_tiles, _), tw in zip(ws, segs, widths):
        args.append(w)
        in_specs.append(pl.BlockSpec(
            (None, K, tw), lambda i, j, layer=layer, lo=lo, n=n_tiles, c0=col0 // tw:
            (layer, 0, c0 + jnp.clip(j - lo, 0, n - 1)),
            pipeline_mode=pl.Buffered(1) if nj == 1 else None))
    if aux_w is not None:
        args.append(aux_w)
        in_specs.append(pl.BlockSpec(aux_w.shape, const))
    if has_rope:
        (cos_p, sin_p, rows_per_seq), (cos_s, sin_s) = rope
        nseq = rows_per_seq // tm
        args += [cos_p, sin_p, cos_s, sin_s]
        in_specs += [pl.BlockSpec((tm, HEAD_DIM), lambda i, j: (i % nseq, 0))] * 2
        in_specs += [pl.BlockSpec((S, HEAD_DIM), const)] * 2
    assert (res is None and post_gain is None) or len(ws) == 1
    if res is not None:
        args += list(res)
        in_specs += [pl.BlockSpec((tm, widths[0]), lambda i, j: (i, j)),
                     pl.BlockSpec((S, widths[0]), lambda i, j: (0, j))]
    if post_gain is not None:
        assert nj == 1
        args.append(post_gain.reshape(1, widths[0]))
        in_specs.append(pl.BlockSpec((1, widths[0]), const))
    out_shape, out_specs = [], []
    for (lo, n_tiles, _), tw in zip(segs, widths):
        col = lambda i, j, lo=lo, n=n_tiles: (i, jnp.clip(j - lo, 0, n - 1))
        out_shape += [jax.ShapeDtypeStruct((Mp, n_tiles * tw), out_dtype),
                      jax.ShapeDtypeStruct((ni * S, n_tiles * tw), F32)]
        out_specs += [pl.BlockSpec((tm, tw), col), pl.BlockSpec((S, tw), col)]
    if aux_w is not None:
        na = aux_w.shape[1]
        out_shape += [jax.ShapeDtypeStruct((Mp, na), F32), jax.ShapeDtypeStruct((ni * S, na), F32)]
        out_specs += [pl.BlockSpec((tm, na), row), pl.BlockSpec((S, na), row)]
    n_main = len(out_shape)
    c_args, c_in, c_shape, c_out, cast_tiles = _cast_specs(casts, ni * nj, nj)
    args += c_args
    in_specs += c_in
    out_shape += c_shape
    out_specs += c_out
    outs = pl.pallas_call(
        functools.partial(_proj_kernel, segs=tuple(segs), resident=resident, norm=norm, has_rope=has_rope,
                          has_res=res is not None, has_post=post_gain is not None, has_aux=aux_w is not None,
                          cast_tiles=cast_tiles),
        grid=(ni, nj),
        in_specs=in_specs,
        out_specs=out_specs,
        out_shape=out_shape,
        scratch_shapes=[pltpu.VMEM((len(ws), tm, K), BF16), pltpu.VMEM((len(ws), S, K), BF16)] if norm else [],
        compiler_params=_params("arbitrary", "arbitrary"),
        name="proj",
    )(*args)
    return [o[:S] if n < n_main and n % 2 == 1 else o for n, o in enumerate(outs)]


def _gateup_kernel(xp_ref, xs_ref, g_ref, wg_ref, wu_ref, *refs, cast_tiles):
    n = len(cast_tiles)
    cast_srcs = refs[:n]
    op_ref, os_ref = refs[n:n + 2]
    cast_dsts = refs[n + 2:2 * n + 2]
    xnp_ref, xns_ref = refs[2 * n + 2:]
    i = pl.program_id(0)
    j = pl.program_id(1)
    _run_casts(cast_srcs, cast_dsts, cast_tiles)

    @pl.when(j == 0)
    def _():
        xnp_ref[...] = _normed(xp_ref, g_ref)

    @pl.when((i == 0) & (j == 0))
    def _():
        xns_ref[...] = _normed(xs_ref, g_ref)

    def swiglu(xn):
        g = jnp.dot(xn, wg_ref[...], preferred_element_type=F32)
        u = jnp.dot(xn, wu_ref[...], preferred_element_type=F32)
        return _silu(g) * u

    op_ref[...] = swiglu(xnp_ref[...]).astype(op_ref.dtype)

    @pl.when(i == 0)
    def _():
        os_ref[...] = swiglu(xns_ref[...])

    @pl.when(i > 0)
    def _():
        os_ref[...] = jnp.zeros_like(os_ref)


def _gateup(xp, xs, gain, w_gu, layer, *, casts=(), tm=1024, tn=512):
    Mp, K = xp.shape
    S = SAMPLE_ROWS
    hidden = w_gu.shape[2] // 2
    assert xs.shape == (S, K) and Mp % tm == 0 and hidden % tn == 0 and w_gu.dtype == BF16
    nj = hidden // tn
    ni = Mp // tm
    c_args, c_in, c_shape, c_out, cast_tiles = _cast_specs(casts, ni * nj, nj)
    outs = pl.pallas_call(
        functools.partial(_gateup_kernel, cast_tiles=cast_tiles),
        grid=(ni, nj),
        in_specs=[pl.BlockSpec((tm, K), lambda i, j: (i, 0)),
                  pl.BlockSpec((S, K), lambda i, j: (0, 0)),
                  pl.BlockSpec((1, K), lambda i, j: (0, 0)),
                  pl.BlockSpec((None, K, tn), lambda i, j: (layer, 0, j)),
                  pl.BlockSpec((None, K, tn), lambda i, j: (layer, 0, j + nj))] + c_in,
        out_specs=[pl.BlockSpec((tm, tn), lambda i, j: (i, j)), pl.BlockSpec((S, tn), _sample_block)] + c_out,
        out_shape=[jax.ShapeDtypeStruct((Mp, hidden), BF16), jax.ShapeDtypeStruct((ni * S, hidden), F32)] + c_shape,
        scratch_shapes=[pltpu.VMEM((tm, K), BF16), pltpu.VMEM((S, K), BF16)],
        compiler_params=_params("arbitrary", "arbitrary"),
        name="gateup",
    )(xp, xs, gain.reshape(1, K), w_gu, w_gu, *c_args)
    return [outs[0], outs[1][:S]] + list(outs[2:])


def _conv_silu_slab(src, s, raw_ref, act_ref, cw_ref, cb_ref):
    Q = SSD_CHUNK
    raw_ref[s, CONV_TAIL:, :] = src
    for parity in (0, 1):
        acc = cb_ref[s]
        for k in range(CONV_W):
            first = CONV_TAIL - (CONV_W - 1) + k + parity
            acc = acc + raw_ref[s, pl.ds(first, Q // 2, stride=2), :] * cw_ref[s, k:k + 1, :]
        act_ref[s, pl.ds(parity, Q // 2, stride=2), :] = _silu_of_half(acc)
    raw_ref[s, 0:CONV_TAIL, :] = raw_ref[s, Q:Q + CONV_TAIL, :]


def _ssd_kernel(z_ref, x_ref, b_ref, c_ref, dt_ref, cw_ref, cb_ref,
                dtb_ref, alog_ref, dexp_ref, gn_ref,
                y_ref, hout_ref,
                h_ref, raw_ref, act_ref):
    Q = SSD_CHUNK
    P = SSM_HEAD_DIM
    N = SSM_STATE
    d_inner = x_ref.shape[1]
    n_heads = d_inner // P
    hpg = n_heads // SSM_GROUPS
    gw = hpg * P
    n_xs = d_inner // LANES
    n_bs = b_ref.shape[1] // LANES
    c = pl.program_id(1)

    @pl.when(c == 0)
    def _():
        h_ref[...] = jnp.zeros_like(h_ref)
        raw_ref[:, 0:CONV_TAIL, :] = jnp.zeros((raw_ref.shape[0], CONV_TAIL, LANES), F32)

    for s in range(n_xs + 2 * n_bs):
        if s < n_xs:
            src = x_ref[:, s * LANES:(s + 1) * LANES]
        elif s < n_xs + n_bs:
            src = b_ref[:, (s - n_xs) * LANES:(s - n_xs + 1) * LANES]
        else:
            src = c_ref[:, (s - n_xs - n_bs) * LANES:(s - n_xs - n_bs + 1) * LANES]
        _conv_silu_slab(src, s, raw_ref, act_ref, cw_ref, cb_ref)

    dt = _softplus(dt_ref[...] + dtb_ref[...])
    dA = dt * (-jnp.exp(alog_ref[...]))
    row = lax.broadcasted_iota(jnp.int32, (Q, Q), 0)
    col = lax.broadcasted_iota(jnp.int32, (Q, Q), 1)
    causal = row >= col
    tri = jnp.where(causal, 1.0, 0.0).astype(F32)
    cs = jnp.dot(tri, dA, precision=HIGHEST, preferred_element_type=F32)
    csT = cs.T
    dtT = dt.T
    dt_decay_end = dt * jnp.exp(cs[Q - 1:Q, :] - cs)
    chunk_decay = jnp.broadcast_to(jnp.exp(csT[0:n_heads, Q - 1:Q]), (n_heads, N))

    lane_lo = lax.broadcasted_iota(jnp.int32, (Q, 2 * P), 1) < P

    def pair_cols(arr, q):
        a0 = jnp.broadcast_to(arr[:, 2 * q:2 * q + 1], (Q, 2 * P))
        a1 = jnp.broadcast_to(arr[:, 2 * q + 1:2 * q + 2], (Q, 2 * P))
        return jnp.where(lane_lo, a0, a1)

    for g in range(SSM_GROUPS):
        Bg = act_ref[n_xs + g].astype(BF16)
        Cg = act_ref[n_xs + n_bs + g].astype(BF16)
        CB = lax.dot_general(Cg, Bg, NT_DIMS, preferred_element_type=F32)
        Hg = h_ref[g * hpg:(g + 1) * hpg].reshape(gw, N)
        y_off = lax.dot_general(Cg, Hg.astype(BF16), NT_DIMS, preferred_element_type=F32)

        ys = []
        xds = []
        for qq in range(hpg // 2):
            q = g * (hpg // 2) + qq
            lo, hi = q * 2 * P, (q + 1) * 2 * P
            xs = act_ref[q]
            xds.append((xs * pair_cols(dt_decay_end, q)).astype(BF16))
            cs_cols = [jnp.broadcast_to(cs[:, hh:hh + 1], (Q, Q)) for hh in (2 * q, 2 * q + 1)]
            atts = []
            for hh, cs_col in zip((2 * q, 2 * q + 1), cs_cols):
                seg = cs_col - csT[hh:hh + 1, :]
                decay = jnp.exp(jnp.where(causal, seg, -jnp.inf))
                atts.append(((CB * decay) * dtT[hh:hh + 1, :]).astype(BF16))
            att = jnp.concatenate(atts, axis=1)
            xbd = jnp.concatenate([jnp.where(lane_lo, xs, 0.0).astype(BF16),
                                   jnp.where(lane_lo, 0.0, xs).astype(BF16)], axis=0)
            y = jnp.dot(att, xbd, preferred_element_type=F32)
            y = y + y_off[:, qq * 2 * P:(qq + 1) * 2 * P] * jnp.exp(jnp.where(lane_lo, *cs_cols))
            y = y + xs * dexp_ref[:, lo:hi]
            ys.append(y * _silu(z_ref[:, lo:hi]))

        ssq = ys[0] * ys[0]
        for y in ys[1:]:
            ssq = ssq + y * y
        scale = lax.rsqrt(jnp.sum(ssq, axis=-1, keepdims=True) / gw + EPS)
        for qq, y in enumerate(ys):
            lo = g * gw + qq * 2 * P
            y_ref[:, lo:lo + 2 * P] = ((y * scale) * gn_ref[:, lo:lo + 2 * P]).astype(y_ref.dtype)

        S = lax.dot_general(jnp.concatenate(xds, axis=1), Bg, TN_DIMS, preferred_element_type=F32)
        for r in range(hpg):
            hh = g * hpg + r
            h_ref[hh] = h_ref[hh] * chunk_decay[hh:hh + 1, :] + S[r * P:(r + 1) * P, :]

    @pl.when(c == pl.num_programs(1) - 1)
    def _():
        hout_ref[0] = h_ref[...]


def _ssd_prompt(zxbc, dt, batch, conv_w, conv_b, dt_bias, a_log, d_skip, gate_norm):
    M = zxbc.shape[0]
    L = M // batch
    H = dt_bias.shape[0]
    d_inner = H * SSM_HEAD_DIM
    bc = SSM_GROUPS * SSM_STATE
    conv_dim = d_inner + 2 * bc
    Q = SSD_CHUNK
    nc = L // Q
    assert L % Q == 0 and d_inner % bc == 0
    xb = d_inner // bc
    row_map = lambda col: (lambda b, c: (b * nc + c, col))
    const = lambda b, c: (0, 0)
    d_exp = jnp.repeat(d_skip.astype(F32), SSM_HEAD_DIM).reshape(1, d_inner)
    assert H <= LANES and SSM_STATE == LANES and 2 * SSM_HEAD_DIM == LANES
    lane_pad = lambda v: jnp.pad(v.reshape(1, H), ((0, 0), (0, LANES - H)))
    n_slabs = conv_dim // LANES
    cw_slabs = (0.5 * conv_w).reshape(CONV_W, n_slabs, LANES).transpose(1, 0, 2)
    cb_slabs = (0.5 * conv_b).reshape(n_slabs, 1, LANES)
    const3 = lambda b, c: (0, 0, 0)
    y, h_final = pl.pallas_call(
        _ssd_kernel,
        grid=(batch, nc),
        in_specs=[pl.BlockSpec((Q, d_inner), row_map(0)),
                  pl.BlockSpec((Q, d_inner), row_map(1)),
                  pl.BlockSpec((Q, bc), row_map(2 * xb)),
                  pl.BlockSpec((Q, bc), row_map(2 * xb + 1)),
                  pl.BlockSpec((Q, LANES), row_map(0)),
                  pl.BlockSpec((n_slabs, CONV_W, LANES), const3),
                  pl.BlockSpec((n_slabs, 1, LANES), const3),
                  pl.BlockSpec((1, LANES), const),
                  pl.BlockSpec((1, LANES), const),
                  pl.BlockSpec((1, d_inner), const),
                  pl.BlockSpec((1, d_inner), const)],
        out_specs=[pl.BlockSpec((Q, d_inner), row_map(0)),
                   pl.BlockSpec((1, H, SSM_HEAD_DIM, SSM_STATE), lambda b, c: (b, 0, 0, 0))],
        out_shape=[jax.ShapeDtypeStruct((M, d_inner), BF16),
                   jax.ShapeDtypeStruct((batch, H, SSM_HEAD_DIM, SSM_STATE), F32)],
        scratch_shapes=[pltpu.VMEM((H, SSM_HEAD_DIM, SSM_STATE), F32),
                        pltpu.VMEM((n_slabs, Q + CONV_TAIL, LANES), F32),
                        pltpu.VMEM((n_slabs, Q, LANES), F32)],
        compiler_params=_params("parallel", "arbitrary"),
        name="ssd_prompt",
    )(zxbc, zxbc, zxbc, zxbc, dt, cw_slabs, cb_slabs,
      lane_pad(dt_bias), lane_pad(a_log), d_exp, gate_norm.reshape(1, d_inner))
    return y, h_final


def _ssd_step_kernel(zxbc_ref, dt_ref, cs_ref, h0_ref, cw_ref, cb_ref, dtb_ref, alog_ref,
                     dexp_ref, gn_ref, y_ref, cso_ref, ho_ref):
    P = SSM_HEAD_DIM
    N = SSM_STATE
    n_heads = h0_ref.shape[2]
    d_inner = n_heads * P
    hpg = n_heads // SSM_GROUPS
    gw = hpg * P
    conv_dim = cw_ref.shape[1]

    raw = zxbc_ref[0, :, d_inner:d_inner + conv_dim]
    prev = cs_ref[0, 0]
    acc = cb_ref[...]
    for k in range(CONV_W - 1):
        acc = acc + prev[k:k + 1, :] * cw_ref[k:k + 1, :]
    acc = acc + raw * cw_ref[CONV_W - 1:CONV_W, :]
    xbc = _silu(acc)
    cso_ref[0, 0, 0:CONV_W - 2, :] = prev[1:CONV_W - 1, :]
    cso_ref[0, 0, CONV_W - 2:CONV_W - 1, :] = raw

    dt = _softplus(dt_ref[0, :, 0:n_heads] + dtb_ref[...])
    decay = jnp.exp(dt * (-jnp.exp(alog_ref[...])))
    z = zxbc_ref[0, :, 0:d_inner]

    eye = (lax.broadcasted_iota(jnp.int32, (P, P), 0) == lax.broadcasted_iota(jnp.int32, (P, P), 1))
    ys = []
    for g in range(SSM_GROUPS):
        Bg = xbc[:, d_inner + g * N:d_inner + (g + 1) * N]
        Cg = xbc[:, d_inner + SSM_GROUPS * N + g * N:d_inner + SSM_GROUPS * N + (g + 1) * N]
        Bb = jnp.broadcast_to(Bg, (P, N))
        for r in range(hpg):
            hh = g * hpg + r
            xs = xbc[:, hh * P:(hh + 1) * P]
            xdt = xs * dt[:, hh:hh + 1]
            xdiag = jnp.where(eye, jnp.broadcast_to(xdt, (P, P)), 0.0)
            outer = jnp.dot(xdiag, Bb, precision=HIGHEST, preferred_element_type=F32)
            ho_ref[0, 0, hh] = h0_ref[0, 0, hh] * decay[:, hh:hh + 1] + outer
        Hg = ho_ref[0, 0, g * hpg:(g + 1) * hpg].reshape(gw, N)
        yg = lax.dot_general(Cg.astype(BF16), Hg.astype(BF16), NT_DIMS, preferred_element_type=F32)
        lo, hi = g * gw, (g + 1) * gw
        yg = yg + xbc[:, lo:hi] * dexp_ref[:, lo:hi]
        yg = yg * _silu(z[:, lo:hi])
        yg = yg * lax.rsqrt(jnp.mean(yg * yg, axis=-1, keepdims=True) + EPS)
        ys.append(yg * gn_ref[:, lo:hi])
    y_ref[0] = jnp.concatenate(ys, axis=1).astype(y_ref.dtype)


def _ssd_step(zxbc, dt, conv_state, ssm_state, conv_w, conv_b, dt_bias, a_log, d_skip, gate_norm):
    B = zxbc.shape[0]
    H = dt_bias.shape[0]
    d_inner = H * SSM_HEAD_DIM
    conv_dim = conv_w.shape[1]
    const = lambda b: (0, 0)
    d_exp = jnp.repeat(d_skip.astype(F32), SSM_HEAD_DIM).reshape(1, d_inner)
    cs4 = conv_state.reshape(1, B, CONV_W - 1, conv_dim)
    h5 = ssm_state.reshape(1, B, H, SSM_HEAD_DIM, SSM_STATE)
    return pl.pallas_call(
        _ssd_step_kernel,
        grid=(B,),
        in_specs=[pl.BlockSpec((1, 1, zxbc.shape[1]), lambda b: (b, 0, 0)),
                  pl.BlockSpec((1, 1, dt.shape[1]), lambda b: (b, 0, 0)),
                  pl.BlockSpec((1, 1, CONV_W - 1, conv_dim), lambda b: (0, b, 0, 0)),
                  pl.BlockSpec((1, 1, H, SSM_HEAD_DIM, SSM_STATE), lambda b: (0, b, 0, 0, 0)),
                  pl.BlockSpec((CONV_W, conv_dim), const),
                  pl.BlockSpec((1, conv_dim), const),
                  pl.BlockSpec((1, H), const),
                  pl.BlockSpec((1, H), const),
                  pl.BlockSpec((1, d_inner), const),
                  pl.BlockSpec((1, d_inner), const)],
        out_specs=[pl.BlockSpec((1, 1, d_inner), lambda b: (b, 0, 0)),
                   pl.BlockSpec((1, 1, CONV_W - 1, conv_dim), lambda b: (0, b, 0, 0)),
                   pl.BlockSpec((1, 1, H, SSM_HEAD_DIM, SSM_STATE), lambda b: (0, b, 0, 0, 0))],
        out_shape=[jax.ShapeDtypeStruct((B, 1, d_inner), F32),
                   jax.ShapeDtypeStruct(cs4.shape, F32),
                   jax.ShapeDtypeStruct(h5.shape, F32)],
        compiler_params=_params("arbitrary"),
        name="ssd_step",
    )(zxbc.reshape(B, 1, -1), dt.reshape(B, 1, -1), cs4, h5, conv_w, conv_b.reshape(1, conv_dim),
      dt_bias.reshape(1, H), a_log.reshape(1, H), d_exp, gate_norm.reshape(1, d_inner))


def _attn_kernel(q0_ref, q1_ref, q2_ref, k_ref, v_ref, o_ref, on_ref, lse_ref, qs_ref, ks_ref, vs_ref):
    L = k_ref.shape[0]
    QB = ATT_BLOCK
    KW = QB + DIL_SLOTS
    P = ATT_PRESTRIDE
    p_shift = P.bit_length() - 1
    scale = HEAD_DIM ** -0.5
    e = lax.broadcasted_iota(jnp.int32, (QB, KW), 0) - lax.broadcasted_iota(jnp.int32, (QB, KW), 1)

    staged = {rate: rate > P for rate in DIL_RATES}
    for src_ref, dst_ref, needed in ((q2_ref, qs_ref, staged[DIL_RATES[2]]), (k_ref, ks_ref, any(staged.values())),
                                     (v_ref, vs_ref, any(staged.values()))):
        if needed:
            def stage(i, carry, src_ref=src_ref, dst_ref=dst_ref):
                cp = i & (P - 1)
                t = i >> p_shift
                dst = pl.multiple_of(cp * (L // P) + t * QB, QB)
                dst_ref[pl.ds(dst, QB), :] = src_ref[pl.ds(cp + P * QB * t, QB, stride=P), :]
                return carry
            lax.fori_loop(0, L // QB, stage, 0, unroll=4)

    for g, (q_ref, rate) in enumerate(zip((q0_ref, q1_ref, q2_ref), DIL_RATES)):
        shift = rate.bit_length() - 1

        def unit(n, carry, g=g, q_ref=q_ref, rate=rate, shift=shift):
            c = n & (rate - 1)
            u0 = (n >> shift) * QB
            v0 = jnp.maximum(u0 - DIL_SLOTS, 0)
            q_rows = pl.ds(c + rate * u0, QB, stride=rate)
            if staged[rate]:
                base = (c & (P - 1)) * (L // P) + (c >> p_shift)
                q = qs_ref[pl.ds(base + (rate // P) * u0, QB, stride=rate // P), :].astype(BF16)
                k_rows = pl.ds(base + (rate // P) * v0, KW, stride=rate // P)
                k = ks_ref[k_rows, :].astype(BF16)
                v = vs_ref[k_rows, :].astype(BF16)
            else:
                k_rows = pl.ds(c + rate * v0, KW, stride=rate)
                q = q_ref[q_rows, :].astype(BF16)
                k = k_ref[k_rows, :].astype(BF16)
                v = v_ref[k_rows, :].astype(BF16)
            s = lax.dot_general(q, k, NT_DIMS, preferred_element_type=F32) * scale
            d = e + (u0 - v0)
            s = jnp.where(d >= 0, s, -jnp.inf)
            s = jnp.where(d <= DIL_SLOTS, s, -jnp.inf)
            m = s.max(axis=1, keepdims=True)
            p = jnp.exp(s - m)
            l = p.sum(axis=1, keepdims=True)
            o = jnp.dot(p.astype(BF16), v, preferred_element_type=F32)
            on_ref[g, q_rows, :] = o / l
            lse_ref[g, q_rows, :] = jnp.broadcast_to(m + jnp.log(l), (QB, HEAD_DIM))
            return carry

        lax.fori_loop(0, L // QB, unit, 0, unroll=ATT_UNROLL)

    def mix(i, carry):
        rows = pl.ds(pl.multiple_of(i * QB, QB), QB)
        lses = [lse_ref[g, rows, :] for g in range(len(DIL_RATES))]
        m = functools.reduce(jnp.maximum, lses)
        ws = [jnp.exp(x - m) for x in lses]
        num = functools.reduce(jnp.add, [w * on_ref[g, rows, :] for g, w in enumerate(ws)])
        o_ref[rows, :] = (num / functools.reduce(jnp.add, ws)).astype(o_ref.dtype)
        return carry

    lax.fori_loop(0, L // QB, mix, 0, unroll=2)


def _attn_prompt(q, kv, batch):
    M = q.shape[0]
    L = M // batch
    n_grp = len(DIL_RATES)
    assert all(r & (r - 1) == 0 and L % (ATT_BLOCK * r) == 0 and L >= r * (ATT_BLOCK + DIL_SLOTS)
               for r in DIL_RATES)
    qspec = lambda g: pl.BlockSpec((L, HEAD_DIM), lambda b, h: (b, g * KV_HEADS + h))
    kspec = pl.BlockSpec((L, HEAD_DIM), lambda b, h: (b, h))
    vspec = pl.BlockSpec((L, HEAD_DIM), lambda b, h: (b, KV_HEADS + h))
    return pl.pallas_call(
        _attn_kernel,
        grid=(batch, KV_HEADS),
        in_specs=[qspec(0), qspec(1), qspec(2), kspec, vspec],
        out_specs=pl.BlockSpec((L, HEAD_DIM), lambda b, h: (b, h)),
        out_shape=jax.ShapeDtypeStruct((M, KV_HEADS * HEAD_DIM), BF16),
        scratch_shapes=[pltpu.VMEM((n_grp, L, HEAD_DIM), F32), pltpu.VMEM((n_grp, L, HEAD_DIM), F32)]
        + [pltpu.VMEM((L, HEAD_DIM), F32)] * 3,
        compiler_params=_params("parallel", "arbitrary"),
        name="attn_prompt",
    )(q, q, q, kv, kv)


def _bf16_round(a):
    return a.astype(BF16).astype(F32)


def _attn_step_kernel(q_ref, *refs):
    n = len(DIL_RATES)
    k_refs, v_refs = refs[0:n], refs[n:2 * n]
    kn_ref, vn_ref, o_ref = refs[2 * n:]
    scale = HEAD_DIM ** -0.5
    kn = _bf16_round(kn_ref[0, 0])
    vn = _bf16_round(vn_ref[0, 0])
    scores, new_scores = [], []
    for g, k_ref in enumerate(k_refs):
        qg = _bf16_round(q_ref[0, g])
        kg = _bf16_round(k_ref[0, :, 0])
        scores.append(jnp.sum(kg * qg[None], axis=-1, keepdims=True) * scale)
        new_scores.append(jnp.sum(kn * qg, axis=-1, keepdims=True) * scale)
    m = functools.reduce(jnp.maximum, [s.max(axis=0) for s in scores] + new_scores)
    l = jnp.zeros_like(m)
    o = jnp.zeros((KV_HEADS, HEAD_DIM), F32)
    for s, s_new, v_ref in zip(scores, new_scores, v_refs):
        p = jnp.exp(s - m[None])
        p_new = jnp.exp(s_new - m)
        l = l + p.sum(axis=0) + p_new
        o = o + (_bf16_round(p) * _bf16_round(v_ref[0, :, 0])).sum(axis=0) + _bf16_round(p_new) * vn
    o_ref[0, 0] = (o / l).astype(o_ref.dtype)


def _attn_step(q, cache_k, cache_v, k_new, v_new):
    B, T = cache_k.shape[0], cache_k.shape[1]
    S = DIL_SLOTS
    assert T == S * max(DIL_RATES), "every slot of every dilation group lies inside the cached window"
    qg = q.reshape(B, len(DIL_RATES), KV_HEADS, HEAD_DIM)
    row_shape = (B, 1, KV_HEADS, HEAD_DIM)
    row_spec = pl.BlockSpec((1, 1, KV_HEADS, HEAD_DIM), lambda b: (b, 0, 0, 0))
    args = [qg]
    in_specs = [pl.BlockSpec((1, len(DIL_RATES), KV_HEADS, HEAD_DIM), lambda b: (b, 0, 0, 0))]
    for cache in (cache_k, cache_v):
        for rate in DIL_RATES:
            args.append(cache.reshape(B, T // rate, rate, KV_HEADS, HEAD_DIM))
            in_specs.append(pl.BlockSpec((1, S, 1, KV_HEADS, HEAD_DIM),
                                         lambda b, blk=T // rate // S - 1: (b, blk, 0, 0, 0)))
    args += [k_new.reshape(row_shape), v_new.reshape(row_shape)]
    in_specs += [row_spec, row_spec]
    o = pl.pallas_call(
        _attn_step_kernel,
        grid=(B,),
        in_specs=in_specs,
        out_specs=row_spec,
        out_shape=jax.ShapeDtypeStruct(row_shape, F32),
        compiler_params=_params("parallel"),
        name="attn_step",
    )(*args)
    return o.reshape(B, KV_HEADS * HEAD_DIM)


def _rope_tables(pos):
    half = HEAD_DIM // 2
    inv = jnp.power(jnp.float32(ROPE_THETA), -jnp.arange(half, dtype=jnp.float32) / half)
    ang = pos.astype(jnp.float32)[:, None] * inv[None, :]
    cos = jnp.cos(ang)
    sin = jnp.sin(ang)
    return jnp.concatenate([cos, cos], axis=1), jnp.concatenate([-sin, sin], axis=1)


def kernel(x_prompt, x_sample, state_conv, state_ssm, cache_k, cache_v, a_norm, a_w_in, a_conv_w, a_conv_b,
           a_dt_bias, a_log, a_d, a_gate_norm, a_w_out, kv_norm, w_kv, b_norm, b_w_q, b_w_o, ffn_norm,
           ffn_w_gu, ffn_w_down, final_norm):
    assert a_norm.shape[0] == 1 and b_norm.shape[0] == 1, "one Mamba-2 layer followed by one attention layer"
    Bp, Lp, D = x_prompt.shape
    Bs, Ls, _ = x_sample.shape
    assert Ls == 1 and Bs == SAMPLE_ROWS, "sample group decodes one token for SAMPLE_ROWS sequences"
    n_heads = a_dt_bias.shape[1]
    d_inner = a_w_out.shape[1]
    conv_dim = a_conv_w.shape[2]
    kv_dim = KV_HEADS * HEAD_DIM
    w_dt = jnp.pad(a_w_in[0][:, d_inner + conv_dim:], ((0, 0), (0, LANES - n_heads))).astype(BF16)
    ssd_w = (a_conv_w[0], a_conv_b[0], a_dt_bias[0], a_log[0], a_d[0], a_gate_norm[0])
    w_kv = w_kv[None]

    cos_p, sin_p = _rope_tables(jnp.arange(Lp, dtype=jnp.int32))
    cos_s, sin_s = _rope_tables(PAST_LEN + jnp.arange(Ls, dtype=jnp.int32))
    rope = ((cos_p, sin_p, Lp), (jnp.broadcast_to(cos_s, (Bs, HEAD_DIM)), jnp.broadcast_to(sin_s, (Bs, HEAD_DIM))))

    xp0 = x_prompt.reshape(Bp * Lp, D)
    xs0 = x_sample.reshape(Bs, D)

    zxbc_p, zxbc_s, dt_p, dt_s, w_out_bf, w_gu0_bf, w_down0_bf = _proj(
        xp0, xs0, [_seg(a_w_in[:, :, :d_inner + conv_dim].astype(BF16))], gains=[a_norm[0]], aux_w=w_dt, tn=1024,
        casts=[(a_w_out, 0), (ffn_w_gu, 0), (ffn_w_down, 0)])
    y_p, p_ssm = _ssd_prompt(zxbc_p, dt_p, Bp, *ssd_w)
    p_conv = zxbc_p.reshape(Bp, Lp, -1)[:, Lp - (CONV_W - 1):, d_inner:]
    y_s, s_conv, s_ssm = _ssd_step(zxbc_s, dt_s, state_conv[0], state_ssm[0], *ssd_w)
    *x1, w_gu1_bf = _proj(y_p, y_s.reshape(Bs, d_inner), [_seg(w_out_bf)], res=(xp0, xs0), casts=[(ffn_w_gu, 1)],
                          tm=256, tn=D)
    *h, w_kv_bf, w_q_bf, w_o_bf = _gateup(*x1, ffn_norm[0], w_gu0_bf, 0, casts=[(w_kv, 0), (b_w_q, 0), (b_w_o, 0)])
    x2 = _proj(*h, [_seg(w_down0_bf)], res=x1, tm=256, tn=D)

    kv_p, kv_s, q_p, q_s = _proj(*x2, [_seg(w_kv_bf, rope_cols=kv_dim), _seg(w_q_bf, rope_cols=b_w_q.shape[2])],
                                 gains=[kv_norm, b_norm[0]], rope=rope, tm=256, tn=None)
    o_p = _attn_prompt(q_p, kv_p, Bp)
    o_s = _attn_step(q_s, cache_k, cache_v, kv_s[:, :kv_dim], kv_s[:, kv_dim:])
    x3 = _proj(o_p, o_s, [_seg(w_o_bf)], res=x2, tm=512, tn=D)
    *h, w_down1_bf = _gateup(*x3, ffn_norm[1], w_gu1_bf, 0, casts=[(ffn_w_down, 1)])
    y_prompt, y_sample = _proj(*h, [_seg(w_down1_bf)], res=x3, post_gain=final_norm, tm=256, tn=D)
    y_prompt = y_prompt.reshape(Bp, Lp, D)
    y_sample = y_sample.reshape(Bs, Ls, D)
    keep = min(DIL_SLOTS * max(DIL_RATES), Lp)
    p_kv = kv_p.reshape(Bp, Lp, 2 * KV_HEADS, HEAD_DIM)[:, Lp - keep:]
    s_kv = kv_s.reshape(Bs, Ls, 2 * KV_HEADS, HEAD_DIM)
    return (y_prompt, y_sample, p_conv[None], p_ssm[None], p_kv[:, :, :KV_HEADS], p_kv[:, :, KV_HEADS:],
            s_conv, s_ssm, s_kv[:, :, :KV_HEADS], s_kv[:, :, KV_HEADS:])
```

```python
import functools

import jax
import jax.numpy as jnp
from jax import lax
from jax.experimental import pallas as pl
from jax.experimental.pallas import tpu as pltpu

F32 = jnp.float32
BF16 = jnp.bfloat16
HIGHEST = lax.Precision.HIGHEST

EPS = 1e-6
ROPE_THETA = 10000.0
SSD_CHUNK = 128
SSM_HEAD_DIM = 64
SSM_STATE = 128
SSM_GROUPS = 8
CONV_W = 4
HEAD_DIM = 128
KV_HEADS = 8
DIL_RATES = (1, 4, 16)
DIL_SLOTS = 128
ATT_BLOCK = 128
ATT_UNROLL = 8
ATT_PRESTRIDE = 4
PAST_LEN = 16384

V7X_VMEM_BYTES = 64 * 1024 * 1024
VMEM_LIMIT = V7X_VMEM_BYTES - 8 * 1024 * 1024
LANES = 128
CONV_TAIL = 8
SAMPLE_ROWS = 8

NT_DIMS = (((1,), (1,)), ((), ()))
TN_DIMS = (((0,), (0,)), ((), ()))


def _params(*sem):
    return pltpu.CompilerParams(dimension_semantics=sem, vmem_limit_bytes=VMEM_LIMIT)


def _silu(x):
    h = 0.5 * x
    return h + h * jnp.tanh(h)


def _softplus(x):
    return jnp.maximum(x, 0.0) + jnp.log1p(jnp.exp(-jnp.abs(x)))


def _rms_scale(x):
    return lax.rsqrt(jnp.mean(x * x, axis=-1, keepdims=True) + EPS)


def _normed(x_ref, g_ref):
    x = x_ref[...]
    return ((x * _rms_scale(x)) * g_ref[...]).astype(BF16)


def _sample_block(i, j):
    return (i, j)


def _cast_specs(casts, n_steps, nj):
    BF16_ROWS = 16
    args, in_specs, out_shape, out_specs, tiles = [], [], [], [], []
    for src, layer in casts:
        R, C = src.shape[1:]
        rows = next(r for r in range(BF16_ROWS, R + 1, BF16_ROWS) if R % r == 0 and R // r <= n_steps)
        nt = R // rows
        tile = lambda i, j, nt=nt: jnp.minimum(i * nj + j, nt - 1)
        args.append(src)
        in_specs.append(pl.BlockSpec((None, rows, C), lambda i, j, layer=layer, tile=tile: (layer, tile(i, j), 0)))
        out_shape.append(jax.ShapeDtypeStruct((1, R, C), BF16))
        out_specs.append(pl.BlockSpec((None, rows, C), lambda i, j, tile=tile: (0, tile(i, j), 0)))
        tiles.append(nt)
    return args, in_specs, out_shape, out_specs, tuple(tiles)


def _run_casts(src_refs, dst_refs, tiles):
    step = pl.program_id(0) * pl.num_programs(1) + pl.program_id(1)
    for src_ref, dst_ref, nt in zip(src_refs, dst_refs, tiles):
        @pl.when(step < nt)
        def _(src_ref=src_ref, dst_ref=dst_ref):
            dst_ref[...] = src_ref[...].astype(BF16)


def _round_weight_kernel(src_ref, dst_ref):
    dst_ref[...] = src_ref[...].astype(BF16)


def _round_weight(w, layer=0, n_steps=32):
    c_args, c_in, c_shape, c_out, (nt,) = _cast_specs([(w, layer)], n_steps, 1)
    return pl.pallas_call(
        _round_weight_kernel, grid=(nt, 1), in_specs=c_in, out_specs=c_out[0], out_shape=c_shape[0],
        compiler_params=_params("arbitrary", "arbitrary"), name="round_weight",
    )(*c_args)


def _proj_kernel(*refs, segs, resident, norm, has_rope, has_res, has_post, has_aux, cast_tiles):
    it = iter(refs)
    xp_ref, xs_ref = next(it), next(it)
    g_refs = [next(it) for _ in segs] if norm else None
    w_refs = [next(it) for _ in segs]
    auxw_ref = next(it) if has_aux else None
    rope_p = (next(it), next(it)) if has_rope else None
    rope_s = (next(it), next(it)) if has_rope else None
    resp_ref, ress_ref = (next(it), next(it)) if has_res else (None, None)
    pg_ref = next(it) if has_post else None
    cast_srcs = [next(it) for _ in cast_tiles]
    out_refs =[(next(it), next(it)) for _ in segs]
    auxp_ref, auxs_ref = (next(it), next(it)) if has_aux else (None, None)
    cast_dsts = [next(it) for _ in cast_tiles]
    xnp_ref, xns_ref = (next(it), next(it)) if norm else (None, None)
    i = pl.program_id(0)
    j = pl.program_id(1)
    _run_casts(cast_srcs, cast_dsts, cast_tiles)

    if norm:
        @pl.when(j == 0)
        def _():
            for s, g_ref in enumerate(g_refs):
                xnp_ref[s] = _normed(xp_ref, g_ref)

        @pl.when((i == 0) & (j == 0))
        def _():
            for s, g_ref in enumerate(g_refs):
                xns_ref[s] = _normed(xs_ref, g_ref)

    def lhs_p(s):
        return xnp_ref[s] if norm else xp_ref[...]

    def lhs_s(s):
        return xns_ref[s] if norm else xs_ref[...].astype(BF16)

    def emit(lhs, w_ref, res_ref, o_ref, rope, lo, n_tiles, n_rope):
        acc = jnp.dot(lhs, w_ref[...].astype(BF16), preferred_element_type=F32)
        if has_res:
            acc = acc + res_ref[...]
        if has_post:
            acc = (acc * _rms_scale(acc)) * pg_ref[...]
        heads = acc.shape[1] // HEAD_DIM

        def store(rope_heads):
            if rope_heads:
                cos = rope[0][...]
                sin = rope[1][...]
            for h in range(rope_heads):
                a = acc[:, h * HEAD_DIM:(h + 1) * HEAD_DIM]
                o_ref[:, h * HEAD_DIM:(h + 1) * HEAD_DIM] = (
                    a * cos + pltpu.roll(a, HEAD_DIM // 2, 1) * sin).astype(o_ref.dtype)
            if rope_heads < heads:
                o_ref[:, rope_heads * HEAD_DIM:] = acc[:, rope_heads * HEAD_DIM:].astype(o_ref.dtype)

        if resident:
            store(n_rope)
        elif n_rope in (0, n_tiles):
            store(heads if n_rope else 0)
        else:
            pl.when(j < lo + n_rope)(lambda: store(heads))
            pl.when(j >= lo + n_rope)(lambda: store(0))

    for s, (lo, n_tiles, n_rope) in enumerate(segs):
        def segment(s=s, lo=lo, n_tiles=n_tiles, n_rope=n_rope):
            emit(lhs_p(s), w_refs[s], resp_ref, out_refs[s][0], rope_p, lo, n_tiles, n_rope)

            @pl.when(i == 0)
            def _():
                emit(lhs_s(s), w_refs[s], ress_ref, out_refs[s][1], rope_s, lo, n_tiles, n_rope)

        if resident or len(segs) == 1:
            segment()
        else:
            pl.when((j >= lo) & (j < lo + n_tiles))(segment)

    @pl.when(i > 0)
    def _():
        for _, os_ref in out_refs:
            os_ref[...] = jnp.zeros_like(os_ref)

    if has_aux:
        @pl.when(j == pl.num_programs(1) - 1)
        def _():
            auxp_ref[...] = jnp.dot(lhs_p(0), auxw_ref[...], preferred_element_type=F32)

            @pl.when(i == 0)
            def _():
                auxs_ref[...] = jnp.dot(lhs_s(0), auxw_ref[...], preferred_element_type=F32)

            @pl.when(i > 0)
            def _():
                auxs_ref[...] = jnp.zeros_like(auxs_ref)


def _seg(w, layer=0, col0=0, n_cols=None, rope_cols=0):
    return (w, layer, col0, w.shape[-1] - col0 if n_cols is None else n_cols, rope_cols)


def _proj(xp, xs, ws, *, gains=None, aux_w=None, rope=None, res=None, post_gain=None, casts=(), out_dtype=F32,
          tm=1024, tn=512):
    Mp, K = xp.shape
    S = SAMPLE_ROWS
    assert xs.shape == (S, K) and Mp % tm == 0
    norm = gains is not None
    resident = tn is None
    segs, widths, lo = [], [], 0
    for w, layer, col0, n_cols, rope_cols in ws:
        tw = n_cols if resident else tn
        rope_unit = HEAD_DIM if resident else tw
        assert w.ndim == 3 and n_cols % tw == 0 and col0 % tw == 0 and rope_cols % rope_unit == 0
        segs.append((lo, n_cols // tw, rope_cols // rope_unit))
        widths.append(tw)
        lo += 0 if resident else n_cols // tw
    nj, ni = (1 if resident else lo), Mp // tm
    has_rope = any(r for _, _, r in segs)
    assert not (has_rope and res is not None) and (rope is not None) == has_rope
    row = lambda i, j: (i, 0)
    const = lambda i, j: (0, 0)
    args = [xp, xs]
    in_specs = [pl.BlockSpec((tm, K), row), pl.BlockSpec((S, K), const)]
    if norm:
        assert len(gains) == len(ws)
        args += [g.reshape(1, K) for g in gains]
        in_specs += [pl.BlockSpec((1, K), const)] * len(gains)
    for (w, layer, col0, _, _), (lo, n_tiles, _), tw in zip(ws, segs, widths):
        args.append(w)
        in_specs.append(pl.BlockSpec(
            (None, K, tw), lambda i, j, layer=layer, lo=lo, n=n_tiles, c0=col0 // tw:
            (layer, 0, c0 + jnp.clip(j - lo, 0, n - 1)),
            pipeline_mode=pl.Buffered(1) if nj == 1 else None))
    if aux_w is not None:
        args.append(aux_w)
        in_specs.append(pl.BlockSpec(aux_w.shape, const))
    if has_rope:
        (cos_p, sin_p, rows_per_seq), (cos_s, sin_s) = rope
        nseq = rows_per_seq // tm
        args += [cos_p, sin_p, cos_s, sin_s]
        in_specs += [pl.BlockSpec((tm, HEAD_DIM), lambda i, j: (i % nseq, 0))] * 2
        in_specs += [pl.BlockSpec((S, HEAD_DIM), const)] * 2
    assert (res is None and post_gain is None) or len(ws) == 1
    if res is not None:
        args += list(res)
        in_specs += [pl.BlockSpec((tm, widths[0]), lambda i, j: (i, j)),
                     pl.BlockSpec((S, widths[0]), lambda i, j: (0, j))]
    if post_gain is not None:
        assert nj == 1
        args.append(post_gain.reshape(1, widths[0]))
        in_specs.append(pl.BlockSpec((1, widths[0]), const))
    out_shape, out_specs = [], []
    for (lo, n_tiles, _), tw in zip(segs, widths):
        col = lambda i, j, lo=lo, n=n_tiles: (i, jnp.clip(j - lo, 0, n - 1))
        out_shape += [jax.ShapeDtypeStruct((Mp, n_tiles * tw), out_dtype),
                      jax.ShapeDtypeStruct((ni * S, n_tiles * tw), F32)]
        out_specs += [pl.BlockSpec((tm, tw), col), pl.BlockSpec((S, tw), col)]
    if aux_w is not None:
        na = aux_w.shape[1]
        out_shape += [jax.ShapeDtypeStruct((Mp, na), F32), jax.ShapeDtypeStruct((ni * S, na), F32)]
        out_specs += [pl.BlockSpec((tm, na), row), pl.BlockSpec((S, na), row)]
    n_main = len(out_shape)
    c_args, c_in, c_shape, c_out, cast_tiles = _cast_specs(casts, ni * nj, nj)
    args += c_args
    in_specs += c_in
    out_shape += c_shape
    out_specs += c_out
    outs = pl.pallas_call(
        functools.partial(_proj_kernel, segs=tuple(segs), resident=resident, norm=norm, has_rope=has_rope,
                          has_res=res is not None, has_post=post_gain is not None, has_aux=aux_w is not None,
                          cast_tiles=cast_tiles),
        grid=(ni, nj),
        in_specs=in_specs,
        out_specs=out_specs,
        out_shape=out_shape,
        scratch_shapes=[pltpu.VMEM((len(ws), tm, K), BF16), pltpu.VMEM((len(ws), S, K), BF16)] if norm else [],
        compiler_params=_params("arbitrary", "arbitrary"),
        name="proj",
    )(*args)
    return [o[:S] if n < n_main and n % 2 == 1 else o for n, o in enumerate(outs)]


def _gateup_kernel(xp_ref, xs_ref, g_ref, wg_ref, wu_ref, *refs, cast_tiles):
    n = len(cast_tiles)
    cast_srcs = refs[:n]
    op_ref, os_ref = refs[n:n + 2]
    cast_dsts = refs[n + 2:2 * n + 2]
    xnp_ref, xns_ref = refs[2 * n + 2:]
    i = pl.program_id(0)
    j = pl.program_id(1)
    _run_casts(cast_srcs, cast_dsts, cast_tiles)

    @pl.when(j == 0)
    def _():
        xnp_ref[...] = _normed(xp_ref, g_ref)

    @pl.when((i == 0) & (j == 0))
    def _():
        xns_ref[...] = _normed(xs_ref, g_ref)

    def swiglu(xn):
        g = jnp.dot(xn, wg_ref[...], preferred_element_type=F32)
        u = jnp.dot(xn, wu_ref[...], preferred_element_type=F32)
        return _silu(g) * u

    op_ref[...] = swiglu(xnp_ref[...]).astype(op_ref.dtype)

    @pl.when(i == 0)
    def _():
        os_ref[...] = swiglu(xns_ref[...])

    @pl.when(i > 0)
    def _():
        os_ref[...] = jnp.zeros_like(os_ref)


def _gateup(xp, xs, gain, w_gu, layer, *, casts=(), tm=1024, tn=512):
    Mp, K = xp.shape
    S = SAMPLE_ROWS
    hidden = w_gu.shape[2] // 2
    assert xs.shape == (S, K) and Mp % tm == 0 and hidden % tn == 0 and w_gu.dtype == BF16
    nj = hidden // tn
    ni = Mp // tm
    c_args, c_in, c_shape, c_out, cast_tiles = _cast_specs(casts, ni * nj, nj)
    outs = pl.pallas_call(
        functools.partial(_gateup_kernel, cast_tiles=cast_tiles),
        grid=(ni, nj),
        in_specs=[pl.BlockSpec((tm, K), lambda i, j: (i, 0)),
                  pl.BlockSpec((S, K), lambda i, j: (0, 0)),
                  pl.BlockSpec((1, K), lambda i, j: (0, 0)),
                  pl.BlockSpec((None, K, tn), lambda i, j: (layer, 0, j)),
                  pl.BlockSpec((None, K, tn), lambda i, j: (layer, 0, j + nj))] + c_in,
        out_specs=[pl.BlockSpec((tm, tn), lambda i, j: (i, j)), pl.BlockSpec((S, tn), _sample_block)] + c_out,
        out_shape=[jax.ShapeDtypeStruct((Mp, hidden), BF16), jax.ShapeDtypeStruct((ni * S, hidden), F32)] + c_shape,
        scratch_shapes=[pltpu.VMEM((tm, K), BF16), pltpu.VMEM((S, K), BF16)],
        compiler_params=_params("arbitrary", "arbitrary"),
        name="gateup",
    )(xp, xs, gain.reshape(1, K), w_gu, w_gu, *c_args)
    return [outs[0], outs[1][:S]] + list(outs[2:])


def _conv_silu_slab(src, s, raw_ref, act_ref, cw_ref, cb_ref):
    Q = SSD_CHUNK
    raw_ref[s, CONV_TAIL:, :] = src
    for parity in (0, 1):
        acc = cb_ref[s]
        for k in range(CONV_W):
            first = CONV_TAIL - (CONV_W - 1) + k + parity
            acc = acc + raw_ref[s, pl.ds(first, Q // 2, stride=2), :] * cw_ref[s, k:k + 1, :]
        act_ref[s, pl.ds(parity, Q // 2, stride=2), :] = _silu(acc)
    raw_ref[s, 0:CONV_TAIL, :] = raw_ref[s, Q:Q + CONV_TAIL, :]


def _ssd_kernel(z_ref, x_ref, b_ref, c_ref, dt_ref, cw_ref, cb_ref,
                dtb_ref, alog_ref, dexp_ref, gn_ref,
                y_ref, hout_ref,
                h_ref, raw_ref, act_ref):
    Q = SSD_CHUNK
    P = SSM_HEAD_DIM
    N = SSM_STATE
    d_inner = x_ref.shape[1]
    n_heads = d_inner // P
    hpg = n_heads // SSM_GROUPS
    gw = hpg * P
    n_xs = d_inner // LANES
    n_bs = b_ref.shape[1] // LANES
    c = pl.program_id(1)

    @pl.when(c == 0)
    def _():
        h_ref[...] = jnp.zeros_like(h_ref)
        raw_ref[:, 0:CONV_TAIL, :] = jnp.zeros((raw_ref.shape[0], CONV_TAIL, LANES), F32)

    for s in range(n_xs + 2 * n_bs):
        if s < n_xs:
            src = x_ref[:, s * LANES:(s + 1) * LANES]
        elif s < n_xs + n_bs:
            src = b_ref[:, (s - n_xs) * LANES:(s - n_xs + 1) * LANES]
        else:
            src = c_ref[:, (s - n_xs - n_bs) * LANES:(s - n_xs - n_bs + 1) * LANES]
        _conv_silu_slab(src, s, raw_ref, act_ref, cw_ref, cb_ref)

    dt = _softplus(dt_ref[...] + dtb_ref[...])
    dA = dt * (-jnp.exp(alog_ref[...]))
    row = lax.broadcasted_iota(jnp.int32, (Q, Q), 0)
    col = lax.broadcasted_iota(jnp.int32, (Q, Q), 1)
    causal = row >= col
    tri = jnp.where(causal, 1.0, 0.0).astype(F32)
    cs = jnp.dot(tri, dA, precision=HIGHEST, preferred_element_type=F32)
    csT = cs.T
    dtT = dt.T
    dt_decay_end = dt * jnp.exp(cs[Q - 1:Q, :] - cs)
    chunk_decay = jnp.broadcast_to(jnp.exp(csT[0:n_heads, Q - 1:Q]), (n_heads, N))

    lane_lo = lax.broadcasted_iota(jnp.int32, (Q, 2 * P), 1) < P

    def pair_cols(arr, q):
        a0 = jnp.broadcast_to(arr[:, 2 * q:2 * q + 1], (Q, 2 * P))
        a1 = jnp.broadcast_to(arr[:, 2 * q + 1:2 * q + 2], (Q, 2 * P))
        return jnp.where(lane_lo, a0, a1)

    for g in range(SSM_GROUPS):
        Bg = act_ref[n_xs + g].astype(BF16)
        Cg = act_ref[n_xs + n_bs + g].astype(BF16)
        CB = lax.dot_general(Cg, Bg, NT_DIMS, preferred_element_type=F32)
        Hg = h_ref[g * hpg:(g + 1) * hpg].reshape(gw, N)
        y_off = lax.dot_general(Cg, Hg.astype(BF16), NT_DIMS, preferred_element_type=F32)

        ys = []
        xds = []
        for qq in range(hpg // 2):
            q = g * (hpg // 2) + qq
            lo, hi = q * 2 * P, (q + 1) * 2 * P
            xs = act_ref[q]
            xds.append((xs * pair_cols(dt_decay_end, q)).astype(BF16))
            cs_cols = [jnp.broadcast_to(cs[:, hh:hh + 1], (Q, Q)) for hh in (2 * q, 2 * q + 1)]
            atts = []
            for hh, cs_col in zip((2 * q, 2 * q + 1), cs_cols):
                seg = cs_col - csT[hh:hh + 1, :]
                decay = jnp.exp(jnp.where(causal, seg, -jnp.inf))
                atts.append(((CB * decay) * dtT[hh:hh + 1, :]).astype(BF16))
            att = jnp.concatenate(atts, axis=1)
            xbd = jnp.concatenate([jnp.where(lane_lo, xs, 0.0).astype(BF16),
                                   jnp.where(lane_lo, 0.0, xs).astype(BF16)], axis=0)
            y = jnp.dot(att, xbd, preferred_element_type=F32)
            y = y + y_off[:, qq * 2 * P:(qq + 1) * 2 * P] * jnp.exp(jnp.where(lane_lo, *cs_cols))
            y = y + xs * dexp_ref[:, lo:hi]
            ys.append(y * _silu(z_ref[:, lo:hi]))

        ssq = ys[0] * ys[0]
        for y in ys[1:]:
            ssq = ssq + y * y
        scale = lax.rsqrt(jnp.sum(ssq, axis=-1, keepdims=True) / gw + EPS)
        for qq, y in enumerate(ys):
            lo = g * gw + qq * 2 * P
            y_ref[:, lo:lo + 2 * P] = ((y * scale) * gn_ref[:, lo:lo + 2 * P]).astype(y_ref.dtype)

        S = lax.dot_general(jnp.concatenate(xds, axis=1), Bg, TN_DIMS, preferred_element_type=F32)
        for r in range(hpg):
            hh = g * hpg + r
            h_ref[hh] = h_ref[hh] * chunk_decay[hh:hh + 1, :] + S[r * P:(r + 1) * P, :]

    @pl.when(c == pl.num_programs(1) - 1)
    def _():
        hout_ref[0] = h_ref[...]


def _ssd_prompt(zxbc, dt, batch, conv_w, conv_b, dt_bias, a_log, d_skip, gate_norm):
    M = zxbc.shape[0]
    L = M // batch
    H = dt_bias.shape[0]
    d_inner = H * SSM_HEAD_DIM
    bc = SSM_GROUPS * SSM_STATE
    conv_dim = d_inner + 2 * bc
    Q = SSD_CHUNK
    nc = L // Q
    assert L % Q == 0 and d_inner % bc == 0
    xb = d_inner // bc
    row_map = lambda col: (lambda b, c: (b * nc + c, col))
    const = lambda b, c: (0, 0)
    d_exp = jnp.repeat(d_skip.astype(F32), SSM_HEAD_DIM).reshape(1, d_inner)
    assert H <= LANES and SSM_STATE == LANES and 2 * SSM_HEAD_DIM == LANES
    lane_pad = lambda v: jnp.pad(v.reshape(1, H), ((0, 0), (0, LANES - H)))
    n_slabs = conv_dim // LANES
    cw_slabs = conv_w.reshape(CONV_W, n_slabs, LANES).transpose(1, 0, 2)
    cb_slabs = conv_b.reshape(n_slabs, 1, LANES)
    const3 = lambda b, c: (0, 0, 0)
    y, h_final = pl.pallas_call(
        _ssd_kernel,
        grid=(batch, nc),
        in_specs=[pl.BlockSpec((Q, d_inner), row_map(0)),
                  pl.BlockSpec((Q, d_inner), row_map(1)),
                  pl.BlockSpec((Q, bc), row_map(2 * xb)),
                  pl.BlockSpec((Q, bc), row_map(2 * xb + 1)),
                  pl.BlockSpec((Q, LANES), row_map(0)),
                  pl.BlockSpec((n_slabs, CONV_W, LANES), const3),
                  pl.BlockSpec((n_slabs, 1, LANES), const3),
                  pl.BlockSpec((1, LANES), const),
                  pl.BlockSpec((1, LANES), const),
                  pl.BlockSpec((1, d_inner), const),
                  pl.BlockSpec((1, d_inner), const)],
        out_specs=[pl.BlockSpec((Q, d_inner), row_map(0)),
                   pl.BlockSpec((1, H, SSM_HEAD_DIM, SSM_STATE), lambda b, c: (b, 0, 0, 0))],
        out_shape=[jax.ShapeDtypeStruct((M, d_inner), BF16),
                   jax.ShapeDtypeStruct((batch, H, SSM_HEAD_DIM, SSM_STATE), F32)],
        scratch_shapes=[pltpu.VMEM((H, SSM_HEAD_DIM, SSM_STATE), F32),
                        pltpu.VMEM((n_slabs, Q + CONV_TAIL, LANES), F32),
                        pltpu.VMEM((n_slabs, Q, LANES), F32)],
        compiler_params=_params("parallel", "arbitrary"),
        name="ssd_prompt",
    )(zxbc, zxbc, zxbc, zxbc, dt, cw_slabs, cb_slabs,
      lane_pad(dt_bias), lane_pad(a_log), d_exp, gate_norm.reshape(1, d_inner))
    return y, h_final


def _ssd_step_kernel(zxbc_ref, dt_ref, cs_ref, h0_ref, cw_ref, cb_ref, dtb_ref, alog_ref,
                     dexp_ref, gn_ref, y_ref, cso_ref, ho_ref):
    P = SSM_HEAD_DIM
    N = SSM_STATE
    n_heads = h0_ref.shape[2]
    d_inner = n_heads * P
    hpg = n_heads // SSM_GROUPS
    gw = hpg * P
    conv_dim = cw_ref.shape[1]

    raw = zxbc_ref[0, :, d_inner:d_inner + conv_dim]
    prev = cs_ref[0, 0]
    acc = cb_ref[...]
    for k in range(CONV_W - 1):
        acc = acc + prev[k:k + 1, :] * cw_ref[k:k + 1, :]
    acc = acc + raw * cw_ref[CONV_W - 1:CONV_W, :]
    xbc = _silu(acc)
    cso_ref[0, 0, 0:CONV_W - 2, :] = prev[1:CONV_W - 1, :]
    cso_ref[0, 0, CONV_W - 2:CONV_W - 1, :] = raw

    dt = _softplus(dt_ref[0, :, 0:n_heads] + dtb_ref[...])
    decay = jnp.exp(dt * (-jnp.exp(alog_ref[...])))
    z = zxbc_ref[0, :, 0:d_inner]

    eye = (lax.broadcasted_iota(jnp.int32, (P, P), 0) == lax.broadcasted_iota(jnp.int32, (P, P), 1))
    ys = []
    for g in range(SSM_GROUPS):
        Bg = xbc[:, d_inner + g * N:d_inner + (g + 1) * N]
        Cg = xbc[:, d_inner + SSM_GROUPS * N + g * N:d_inner + SSM_GROUPS * N + (g + 1) * N]
        Bb = jnp.broadcast_to(Bg, (P, N))
        for r in range(hpg):
            hh = g * hpg + r
            xs = xbc[:, hh * P:(hh + 1) * P]
            xdt = xs * dt[:, hh:hh + 1]
            xdiag = jnp.where(eye, jnp.broadcast_to(xdt, (P, P)), 0.0)
            outer = jnp.dot(xdiag, Bb, precision=HIGHEST, preferred_element_type=F32)
            ho_ref[0, 0, hh] = h0_ref[0, 0, hh] * decay[:, hh:hh + 1] + outer
        Hg = ho_ref[0, 0, g * hpg:(g + 1) * hpg].reshape(gw, N)
        yg = lax.dot_general(Cg.astype(BF16), Hg.astype(BF16), NT_DIMS, preferred_element_type=F32)
        lo, hi = g * gw, (g + 1) * gw
        yg = yg + xbc[:, lo:hi] * dexp_ref[:, lo:hi]
        yg = yg * _silu(z[:, lo:hi])
        yg = yg * lax.rsqrt(jnp.mean(yg * yg, axis=-1, keepdims=True) + EPS)
        ys.append(yg * gn_ref[:, lo:hi])
    y_ref[0] = jnp.concatenate(ys, axis=1).astype(y_ref.dtype)


def _ssd_step(zxbc, dt, conv_state, ssm_state, conv_w, conv_b, dt_bias, a_log, d_skip, gate_norm):
    B = zxbc.shape[0]
    H = dt_bias.shape[0]
    d_inner = H * SSM_HEAD_DIM
    conv_dim = conv_w.shape[1]
    const = lambda b: (0, 0)
    d_exp = jnp.repeat(d_skip.astype(F32), SSM_HEAD_DIM).reshape(1, d_inner)
    cs4 = conv_state.reshape(1, B, CONV_W - 1, conv_dim)
    h5 = ssm_state.reshape(1, B, H, SSM_HEAD_DIM, SSM_STATE)
    return pl.pallas_call(
        _ssd_step_kernel,
        grid=(B,),
        in_specs=[pl.BlockSpec((1, 1, zxbc.shape[1]), lambda b: (b, 0, 0)),
                  pl.BlockSpec((1, 1, dt.shape[1]), lambda b: (b, 0, 0)),
                  pl.BlockSpec((1, 1, CONV_W - 1, conv_dim), lambda b: (0, b, 0, 0)),
                  pl.BlockSpec((1, 1, H, SSM_HEAD_DIM, SSM_STATE), lambda b: (0, b, 0, 0, 0)),
                  pl.BlockSpec((CONV_W, conv_dim), const),
                  pl.BlockSpec((1, conv_dim), const),
                  pl.BlockSpec((1, H), const),
                  pl.BlockSpec((1, H), const),
                  pl.BlockSpec((1, d_inner), const),
                  pl.BlockSpec((1, d_inner), const)],
        out_specs=[pl.BlockSpec((1, 1, d_inner), lambda b: (b, 0, 0)),
                   pl.BlockSpec((1, 1, CONV_W - 1, conv_dim), lambda b: (0, b, 0, 0)),
                   pl.BlockSpec((1, 1, H, SSM_HEAD_DIM, SSM_STATE), lambda b: (0, b, 0, 0, 0))],
        out_shape=[jax.ShapeDtypeStruct((B, 1, d_inner), F32),
                   jax.ShapeDtypeStruct(cs4.shape, F32),
                   jax.ShapeDtypeStruct(h5.shape, F32)],
        compiler_params=_params("arbitrary"),
        name="ssd_step",
    )(zxbc.reshape(B, 1, -1), dt.reshape(B, 1, -1), cs4, h5, conv_w, conv_b.reshape(1, conv_dim),
      dt_bias.reshape(1, H), a_log.reshape(1, H), d_exp, gate_norm.reshape(1, d_inner))


def _attn_kernel(q0_ref, q1_ref, q2_ref, k_ref, v_ref, o_ref, on_ref, lse_ref, qs_ref, ks_ref, vs_ref):
    L = k_ref.shape[0]
    QB = ATT_BLOCK
    KW = QB + DIL_SLOTS
    P = ATT_PRESTRIDE
    p_shift = P.bit_length() - 1
    scale = HEAD_DIM ** -0.5
    e = lax.broadcasted_iota(jnp.int32, (QB, KW), 0) - lax.broadcasted_iota(jnp.int32, (QB, KW), 1)

    staged = {rate: rate > P for rate in DIL_RATES}
    for src_ref, dst_ref, needed in ((q2_ref, qs_ref, staged[DIL_RATES[2]]), (k_ref, ks_ref, any(staged.values())),
                                     (v_ref, vs_ref, any(staged.values()))):
        if needed:
            def stage(i, carry, src_ref=src_ref, dst_ref=dst_ref):
                cp = i & (P - 1)
                t = i >> p_shift
                dst = pl.multiple_of(cp * (L // P) + t * QB, QB)
                dst_ref[pl.ds(dst, QB), :] = src_ref[pl.ds(cp + P * QB * t, QB, stride=P), :]
                return carry
            lax.fori_loop(0, L // QB, stage, 0, unroll=4)

    for g, (q_ref, rate) in enumerate(zip((q0_ref, q1_ref, q2_ref), DIL_RATES)):
        shift = rate.bit_length() - 1

        def unit(n, carry, g=g, q_ref=q_ref, rate=rate, shift=shift):
            c = n & (rate - 1)
            u0 = (n >> shift) * QB
            v0 = jnp.maximum(u0 - DIL_SLOTS, 0)
            q_rows = pl.ds(c + rate * u0, QB, stride=rate)
            if staged[rate]:
                base = (c & (P - 1)) * (L // P) + (c >> p_shift)
                q = qs_ref[pl.ds(base + (rate // P) * u0, QB, stride=rate // P), :].astype(BF16)
                k_rows = pl.ds(base + (rate // P) * v0, KW, stride=rate // P)
                k = ks_ref[k_rows, :].astype(BF16)
                v = vs_ref[k_rows, :].astype(BF16)
            else:
                k_rows = pl.ds(c + rate * v0, KW, stride=rate)
                q = q_ref[q_rows, :].astype(BF16)
                k = k_ref[k_rows, :].astype(BF16)
                v = v_ref[k_rows, :].astype(BF16)
            s = lax.dot_general(q, k, NT_DIMS, preferred_element_type=F32) * scale
            d = e + (u0 - v0)
            s = jnp.where(d >= 0, s, -jnp.inf)
            s = jnp.where(d <= DIL_SLOTS, s, -jnp.inf)
            m = s.max(axis=1, keepdims=True)
            p = jnp.exp(s - m)
            l = p.sum(axis=1, keepdims=True)
            o = jnp.dot(p.astype(BF16), v, preferred_element_type=F32)
            on_ref[g, q_rows, :] = o / l
            lse_ref[g, q_rows, :] = jnp.broadcast_to(m + jnp.log(l), (QB, HEAD_DIM))
            return carry

        lax.fori_loop(0, L // QB, unit, 0, unroll=ATT_UNROLL)

    def mix(i, carry):
        rows = pl.ds(pl.multiple_of(i * QB, QB), QB)
        lses = [lse_ref[g, rows, :] for g in range(len(DIL_RATES))]
        m = functools.reduce(jnp.maximum, lses)
        ws = [jnp.exp(x - m) for x in lses]
        num = functools.reduce(jnp.add, [w * on_ref[g, rows, :] for g, w in enumerate(ws)])
        o_ref[rows, :] = (num / functools.reduce(jnp.add, ws)).astype(o_ref.dtype)
        return carry

    lax.fori_loop(0, L // QB, mix, 0, unroll=2)


def _attn_prompt(q, kv, batch):
    M = q.shape[0]
    L = M // batch
    n_grp = len(DIL_RATES)
    assert all(r & (r - 1) == 0 and L % (ATT_BLOCK * r) == 0 and L >= r * (ATT_BLOCK + DIL_SLOTS)
               for r in DIL_RATES)
    qspec = lambda g: pl.BlockSpec((L, HEAD_DIM), lambda b, h: (b, g * KV_HEADS + h))
    kspec = pl.BlockSpec((L, HEAD_DIM), lambda b, h: (b, h))
    vspec = pl.BlockSpec((L, HEAD_DIM), lambda b, h: (b, KV_HEADS + h))
    return pl.pallas_call(
        _attn_kernel,
        grid=(batch, KV_HEADS),
        in_specs=[qspec(0), qspec(1), qspec(2), kspec, vspec],
        out_specs=pl.BlockSpec((L, HEAD_DIM), lambda b, h: (b, h)),
        out_shape=jax.ShapeDtypeStruct((M, KV_HEADS * HEAD_DIM), BF16),
        scratch_shapes=[pltpu.VMEM((n_grp, L, HEAD_DIM), F32), pltpu.VMEM((n_grp, L, HEAD_DIM), F32)]
        + [pltpu.VMEM((L, HEAD_DIM), F32)] * 3,
        compiler_params=_params("parallel", "arbitrary"),
        name="attn_prompt",
    )(q, q, q, kv, kv)


def _bf16_round(a):
    return a.astype(BF16).astype(F32)


def _attn_step_kernel(q_ref, *refs):
    n = len(DIL_RATES)
    k_refs, v_refs = refs[0:n], refs[n:2 * n]
    kn_ref, vn_ref, o_ref = refs[2 * n:]
    scale = HEAD_DIM ** -0.5
    kn = _bf16_round(kn_ref[0, 0])
    vn = _bf16_round(vn_ref[0, 0])
    scores, new_scores = [], []
    for g, k_ref in enumerate(k_refs):
        qg = _bf16_round(q_ref[0, g])
        kg = _bf16_round(k_ref[0, :, 0])
        scores.append(jnp.sum(kg * qg[None], axis=-1, keepdims=True) * scale)
        new_scores.append(jnp.sum(kn * qg, axis=-1, keepdims=True) * scale)
    m = functools.reduce(jnp.maximum, [s.max(axis=0) for s in scores] + new_scores)
    l = jnp.zeros_like(m)
    o = jnp.zeros((KV_HEADS, HEAD_DIM), F32)
    for s, s_new, v_ref in zip(scores, new_scores, v_refs):
        p = jnp.exp(s - m[None])
        p_new = jnp.exp(s_new - m)
        l = l + p.sum(axis=0) + p_new
        o = o + (_bf16_round(p) * _bf16_round(v_ref[0, :, 0])).sum(axis=0) + _bf16_round(p_new) * vn
    o_ref[0, 0] = (o / l).astype(o_ref.dtype)


def _attn_step(q, cache_k, cache_v, k_new, v_new):
    B, T = cache_k.shape[0], cache_k.shape[1]
    S = DIL_SLOTS
    assert T == S * max(DIL_RATES), "every slot of every dilation group lies inside the cached window"
    qg = q.reshape(B, len(DIL_RATES), KV_HEADS, HEAD_DIM)
    row_shape = (B, 1, KV_HEADS, HEAD_DIM)
    row_spec = pl.BlockSpec((1, 1, KV_HEADS, HEAD_DIM), lambda b: (b, 0, 0, 0))
    args = [qg]
    in_specs = [pl.BlockSpec((1, len(DIL_RATES), KV_HEADS, HEAD_DIM), lambda b: (b, 0, 0, 0))]
    for cache in (cache_k, cache_v):
        for rate in DIL_RATES:
            args.append(cache.reshape(B, T // rate, rate, KV_HEADS, HEAD_DIM))
            in_specs.append(pl.BlockSpec((1, S, 1, KV_HEADS, HEAD_DIM),
                                         lambda b, blk=T // rate // S - 1: (b, blk, 0, 0, 0)))
    args += [k_new.reshape(row_shape), v_new.reshape(row_shape)]
    in_specs += [row_spec, row_spec]
    o = pl.pallas_call(
        _attn_step_kernel,
        grid=(B,),
        in_specs=in_specs,
        out_specs=row_spec,
        out_shape=jax.ShapeDtypeStruct(row_shape, F32),
        compiler_params=_params("parallel"),
        name="attn_step",
    )(*args)
    return o.reshape(B, KV_HEADS * HEAD_DIM)


def _rope_tables(pos):
    half = HEAD_DIM // 2
    inv = jnp.power(jnp.float32(ROPE_THETA), -jnp.arange(half, dtype=jnp.float32) / half)
    ang = pos.astype(jnp.float32)[:, None] * inv[None, :]
    cos = jnp.cos(ang)
    sin = jnp.sin(ang)
    return jnp.concatenate([cos, cos], axis=1), jnp.concatenate([-sin, sin], axis=1)


def kernel(x_prompt, x_sample, state_conv, state_ssm, cache_k, cache_v, a_norm, a_w_in, a_conv_w, a_conv_b,
           a_dt_bias, a_log, a_d, a_gate_norm, a_w_out, kv_norm, w_kv, b_norm, b_w_q, b_w_o, ffn_norm,
           ffn_w_gu, ffn_w_down, final_norm):
    assert a_norm.shape[0] == 1 and b_norm.shape[0] == 1, "one Mamba-2 layer followed by one attention layer"
    Bp, Lp, D = x_prompt.shape
    Bs, Ls, _ = x_sample.shape
    assert Ls == 1 and Bs == SAMPLE_ROWS, "sample group decodes one token for SAMPLE_ROWS sequences"
    n_heads = a_dt_bias.shape[1]
    d_inner = a_w_out.shape[1]
    conv_dim = a_conv_w.shape[2]
    kv_dim = KV_HEADS * HEAD_DIM
    w_dt = jnp.pad(a_w_in[0][:, d_inner + conv_dim:], ((0, 0), (0, LANES - n_heads))).astype(BF16)
    ssd_w = (a_conv_w[0], a_conv_b[0], a_dt_bias[0], a_log[0], a_d[0], a_gate_norm[0])
    w_kv = w_kv[None]

    cos_p, sin_p = _rope_tables(jnp.arange(Lp, dtype=jnp.int32))
    cos_s, sin_s = _rope_tables(PAST_LEN + jnp.arange(Ls, dtype=jnp.int32))
    rope = ((cos_p, sin_p, Lp), (jnp.broadcast_to(cos_s, (Bs, HEAD_DIM)), jnp.broadcast_to(sin_s, (Bs, HEAD_DIM))))

    xp0 = x_prompt.reshape(Bp * Lp, D)
    xs0 = x_sample.reshape(Bs, D)

    zxbc_p, zxbc_s, dt_p, dt_s, w_out_bf, w_gu0_bf, w_down0_bf = _proj(
        xp0, xs0, [_seg(_round_weight(a_w_in), n_cols=d_inner + conv_dim)], gains=[a_norm[0]], aux_w=w_dt, tn=1024,
        casts=[(a_w_out, 0), (ffn_w_gu, 0), (ffn_w_down, 0)])
    y_p, p_ssm = _ssd_prompt(zxbc_p, dt_p, Bp, *ssd_w)
    p_conv = zxbc_p.reshape(Bp, Lp, -1)[:, Lp - (CONV_W - 1):, d_inner:]
    y_s, s_conv, s_ssm = _ssd_step(zxbc_s, dt_s, state_conv[0], state_ssm[0], *ssd_w)
    *x1, w_gu1_bf = _proj(y_p, y_s.reshape(Bs, d_inner), [_seg(w_out_bf)], res=(xp0, xs0), casts=[(ffn_w_gu, 1)],
                          tm=256, tn=D)
    *h, w_kv_bf, w_q_bf, w_o_bf = _gateup(*x1, ffn_norm[0], w_gu0_bf, 0, casts=[(w_kv, 0), (b_w_q, 0), (b_w_o, 0)])
    x2 = _proj(*h, [_seg(w_down0_bf)], res=x1, tm=256, tn=D)

    kv_p, kv_s, q_p, q_s = _proj(*x2, [_seg(w_kv_bf, rope_cols=kv_dim), _seg(w_q_bf, rope_cols=b_w_q.shape[2])],
                                 gains=[kv_norm, b_norm[0]], rope=rope, tm=256, tn=None)
    o_p = _attn_prompt(q_p, kv_p, Bp)
    o_s = _attn_step(q_s, cache_k, cache_v, kv_s[:, :kv_dim], kv_s[:, kv_dim:])
    x3 = _proj(o_p, o_s, [_seg(w_o_bf)], res=x2, tm=512, tn=D)
    *h, w_down1_bf = _gateup(*x3, ffn_norm[1], w_gu1_bf, 0, casts=[(ffn_w_down, 1)])
    y_prompt, y_sample = _proj(*h, [_seg(w_down1_bf)], res=x3, post_gain=final_norm, tm=256, tn=D)
    y_prompt = y_prompt.reshape(Bp, Lp, D)
    y_sample = y_sample.reshape(Bs, Ls, D)
    keep = min(DIL_SLOTS * max(DIL_RATES), Lp)
    p_kv = kv_p.reshape(Bp, Lp, 2 * KV_HEADS, HEAD_DIM)[:, Lp - keep:]
    s_kv = kv_s.reshape(Bs, Ls, 2 * KV_HEADS, HEAD_DIM)
    return (y_prompt, y_sample, p_conv[None], p_ssm[None], p_kv[:, :, :KV_HEADS], p_kv[:, :, KV_HEADS:],
            s_conv, s_ssm, s_kv[:, :, :KV_HEADS], s_kv[:, :, KV_HEADS:])
```

```python
import functools

import jax
import jax.numpy as jnp
from jax import lax
from jax.experimental import pallas as pl
from jax.experimental.pallas import tpu as pltpu

F32 = jnp.float32
BF16 = jnp.bfloat16
HIGHEST = lax.Precision.HIGHEST

EPS = 1e-6
ROPE_THETA = 10000.0
SSD_CHUNK = 128
SSM_HEAD_DIM = 64
SSM_STATE = 128
SSM_GROUPS = 8
CONV_W = 4
HEAD_DIM = 128
KV_HEADS = 8
DIL_RATES = (1, 4, 16)
DIL_SLOTS = 128
ATT_BLOCK = 128
ATT_UNROLL = 16
ATT_PRESTRIDE = 4
PAST_LEN = 16384

V7X_VMEM_BYTES = 64 * 1024 * 1024
VMEM_LIMIT = V7X_VMEM_BYTES - 8 * 1024 * 1024
LANES = 128
CONV_TAIL = 8
SAMPLE_ROWS = 8

NT_DIMS = (((1,), (1,)), ((), ()))
TN_DIMS = (((0,), (0,)), ((), ()))


def _params(*sem):
    return pltpu.CompilerParams(dimension_semantics=sem, vmem_limit_bytes=VMEM_LIMIT)


def _silu(x):
    h = 0.5 * x
    return h + h * jnp.tanh(h)


def _softplus(x):
    return jnp.maximum(x, 0.0) + jnp.log1p(jnp.exp(-jnp.abs(x)))


def _rms_scale(x):
    return lax.rsqrt(jnp.mean(x * x, axis=-1, keepdims=True) + EPS)


def _normed(x_ref, g_ref):
    x = x_ref[...]
    return ((x * _rms_scale(x)) * g_ref[...]).astype(BF16)


def _sample_block(i, j):
    return (i, j)


def _cast_specs(casts, n_steps, nj):
    BF16_ROWS = 16
    args, in_specs, out_shape, out_specs, tiles = [], [], [], [], []
    for src, layer in casts:
        R, C = src.shape[1:]
        rows = next(r for r in range(BF16_ROWS, R + 1, BF16_ROWS) if R % r == 0 and R // r <= n_steps)
        nt = R // rows
        tile = lambda i, j, nt=nt: jnp.minimum(i * nj + j, nt - 1)
        args.append(src)
        in_specs.append(pl.BlockSpec((None, rows, C), lambda i, j, layer=layer, tile=tile: (layer, tile(i, j), 0)))
        out_shape.append(jax.ShapeDtypeStruct((1, R, C), BF16))
        out_specs.append(pl.BlockSpec((None, rows, C), lambda i, j, tile=tile: (0, tile(i, j), 0)))
        tiles.append(nt)
    return args, in_specs, out_shape, out_specs, tuple(tiles)


def _run_casts(src_refs, dst_refs, tiles):
    step = pl.program_id(0) * pl.num_programs(1) + pl.program_id(1)
    for src_ref, dst_ref, nt in zip(src_refs, dst_refs, tiles):
        @pl.when(step < nt)
        def _(src_ref=src_ref, dst_ref=dst_ref):
            dst_ref[...] = src_ref[...].astype(BF16)


def _proj_kernel(*refs, segs, resident, norm, has_rope, has_res, has_post, has_aux, cast_tiles):
    it = iter(refs)
    xp_ref, xs_ref = next(it), next(it)
    g_refs = [next(it) for _ in segs] if norm else None
    w_refs = [next(it) for _ in segs]
    auxw_ref = next(it) if has_aux else None
    rope_p = (next(it), next(it)) if has_rope else None
    rope_s = (next(it), next(it)) if has_rope else None
    resp_ref, ress_ref = (next(it), next(it)) if has_res else (None, None)
    pg_ref = next(it) if has_post else None
    cast_srcs = [next(it) for _ in cast_tiles]
    out_refs =[(next(it), next(it)) for _ in segs]
    auxp_ref, auxs_ref = (next(it), next(it)) if has_aux else (None, None)
    cast_dsts = [next(it) for _ in cast_tiles]
    xnp_ref, xns_ref = (next(it), next(it)) if norm else (None, None)
    i = pl.program_id(0)
    j = pl.program_id(1)
    _run_casts(cast_srcs, cast_dsts, cast_tiles)

    if norm:
        @pl.when(j == 0)
        def _():
            for s, g_ref in enumerate(g_refs):
                xnp_ref[s] = _normed(xp_ref, g_ref)

        @pl.when((i == 0) & (j == 0))
        def _():
            for s, g_ref in enumerate(g_refs):
                xns_ref[s] = _normed(xs_ref, g_ref)

    def lhs_p(s):
        return xnp_ref[s] if norm else xp_ref[...]

    def lhs_s(s):
        return xns_ref[s] if norm else xs_ref[...].astype(BF16)

    def emit(lhs, w_ref, res_ref, o_ref, rope, lo, n_tiles, n_rope):
        acc = jnp.dot(lhs, w_ref[...].astype(BF16), preferred_element_type=F32)
        if has_res:
            acc = acc + res_ref[...]
        if has_post:
            acc = (acc * _rms_scale(acc)) * pg_ref[...]
        heads = acc.shape[1] // HEAD_DIM

        def store(rope_heads):
            if rope_heads:
                cos = rope[0][...]
                sin = rope[1][...]
            for h in range(rope_heads):
                a = acc[:, h * HEAD_DIM:(h + 1) * HEAD_DIM]
                o_ref[:, h * HEAD_DIM:(h + 1) * HEAD_DIM] = (
                    a * cos + pltpu.roll(a, HEAD_DIM // 2, 1) * sin).astype(o_ref.dtype)
            if rope_heads < heads:
                o_ref[:, rope_heads * HEAD_DIM:] = acc[:, rope_heads * HEAD_DIM:].astype(o_ref.dtype)

        if resident:
            store(n_rope)
        elif n_rope in (0, n_tiles):
            store(heads if n_rope else 0)
        else:
            pl.when(j < lo + n_rope)(lambda: store(heads))
            pl.when(j >= lo + n_rope)(lambda: store(0))

    for s, (lo, n_tiles, n_rope) in enumerate(segs):
        def segment(s=s, lo=lo, n_tiles=n_tiles, n_rope=n_rope):
            emit(lhs_p(s), w_refs[s], resp_ref, out_refs[s][0], rope_p, lo, n_tiles, n_rope)

            @pl.when(i == 0)
            def _():
                emit(lhs_s(s), w_refs[s], ress_ref, out_refs[s][1], rope_s, lo, n_tiles, n_rope)

        if resident or len(segs) == 1:
            segment()
        else:
            pl.when((j >= lo) & (j < lo + n_tiles))(segment)

    @pl.when(i > 0)
    def _():
        for _, os_ref in out_refs:
            os_ref[...] = jnp.zeros_like(os_ref)

    if has_aux:
        @pl.when(j == pl.num_programs(1) - 1)
        def _():
            auxp_ref[...] = jnp.dot(lhs_p(0), auxw_ref[...], preferred_element_type=F32)

            @pl.when(i == 0)
            def _():
                auxs_ref[...] = jnp.dot(lhs_s(0), auxw_ref[...], preferred_element_type=F32)

            @pl.when(i > 0)
            def _():
                auxs_ref[...] = jnp.zeros_like(auxs_ref)


def _seg(w, layer=0, col0=0, n_cols=None, rope_cols=0):
    return (w, layer, col0, w.shape[-1] - col0 if n_cols is None else n_cols, rope_cols)


def _proj(xp, xs, ws, *, gains=None, aux_w=None, rope=None, res=None, post_gain=None, casts=(), out_dtype=F32,
          tm=1024, tn=512):
    Mp, K = xp.shape
    S = SAMPLE_ROWS
    assert xs.shape == (S, K) and Mp % tm == 0
    norm = gains is not None
    resident = tn is None
    segs, widths, lo = [], [], 0
    for w, layer, col0, n_cols, rope_cols in ws:
        tw = n_cols if resident else tn
        rope_unit = HEAD_DIM if resident else tw
        assert w.ndim == 3 and n_cols % tw == 0 and col0 % tw == 0 and rope_cols % rope_unit == 0
        segs.append((lo, n_cols // tw, rope_cols // rope_unit))
        widths.append(tw)
        lo += 0 if resident else n_cols // tw
    nj, ni = (1 if resident else lo), Mp // tm
    has_rope = any(r for _, _, r in segs)
    assert not (has_rope and res is not None) and (rope is not None) == has_rope
    row = lambda i, j: (i, 0)
    const = lambda i, j: (0, 0)
    args = [xp, xs]
    in_specs = [pl.BlockSpec((tm, K), row), pl.BlockSpec((S, K), const)]
    if norm:
        assert len(gains) == len(ws)
        args += [g.reshape(1, K) for g in gains]
        in_specs += [pl.BlockSpec((1, K), const)] * len(gains)
    for (w, layer, col0, _, _), (lo, n_tiles, _), tw in zip(ws, segs, widths):
        args.append(w)
        in_specs.append(pl.BlockSpec(
            (None, K, tw), lambda i, j, layer=layer, lo=lo, n=n_tiles, c0=col0 // tw:
            (layer, 0, c0 + jnp.clip(j - lo, 0, n - 1)),
            pipeline_mode=pl.Buffered(1) if nj == 1 else None))
    if aux_w is not None:
        args.append(aux_w)
        in_specs.append(pl.BlockSpec(aux_w.shape, const))
    if has_rope:
        (cos_p, sin_p, rows_per_seq), (cos_s, sin_s) = rope
        nseq = rows_per_seq // tm
        args += [cos_p, sin_p, cos_s, sin_s]
        in_specs += [pl.BlockSpec((tm, HEAD_DIM), lambda i, j: (i % nseq, 0))] * 2
        in_specs += [pl.BlockSpec((S, HEAD_DIM), const)] * 2
    assert (res is None and post_gain is None) or len(ws) == 1
    if res is not None:
        args += list(res)
        in_specs += [pl.BlockSpec((tm, widths[0]), lambda i, j: (i, j)),
                     pl.BlockSpec((S, widths[0]), lambda i, j: (0, j))]
    if post_gain is not None:
        assert nj == 1
        args.append(post_gain.reshape(1, widths[0]))
        in_specs.append(pl.BlockSpec((1, widths[0]), const))
    out_shape, out_specs = [], []
    for (lo, n_tiles, _), tw in zip(segs, widths):
        col = lambda i, j, lo=lo, n=n_tiles: (i, jnp.clip(j - lo, 0, n - 1))
        out_shape += [jax.ShapeDtypeStruct((Mp, n_tiles * tw), out_dtype),
                      jax.ShapeDtypeStruct((ni * S, n_tiles * tw), F32)]
        out_specs += [pl.BlockSpec((tm, tw), col), pl.BlockSpec((S, tw), col)]
    if aux_w is not None:
        na = aux_w.shape[1]
        out_shape += [jax.ShapeDtypeStruct((Mp, na), F32), jax.ShapeDtypeStruct((ni * S, na), F32)]
        out_specs += [pl.BlockSpec((tm, na), row), pl.BlockSpec((S, na), row)]
    n_main = len(out_shape)
    c_args, c_in, c_shape, c_out, cast_tiles = _cast_specs(casts, ni * nj, nj)
    args += c_args
    in_specs += c_in
    out_shape += c_shape
    out_specs += c_out
    outs = pl.pallas_call(
        functools.partial(_proj_kernel, segs=tuple(segs), resident=resident, norm=norm, has_rope=has_rope,
                          has_res=res is not None, has_post=post_gain is not None, has_aux=aux_w is not None,
                          cast_tiles=cast_tiles),
        grid=(ni, nj),
        in_specs=in_specs,
        out_specs=out_specs,
        out_shape=out_shape,
        scratch_shapes=[pltpu.VMEM((len(ws), tm, K), BF16), pltpu.VMEM((len(ws), S, K), BF16)] if norm else [],
        compiler_params=_params("arbitrary", "arbitrary"),
        name="proj",
    )(*args)
    return [o[:S] if n < n_main and n % 2 == 1 else o for n, o in enumerate(outs)]


def _gateup_kernel(xp_ref, xs_ref, g_ref, wg_ref, wu_ref, *refs, cast_tiles):
    n = len(cast_tiles)
    cast_srcs = refs[:n]
    op_ref, os_ref = refs[n:n + 2]
    cast_dsts = refs[n + 2:2 * n + 2]
    xnp_ref, xns_ref = refs[2 * n + 2:]
    i = pl.program_id(0)
    j = pl.program_id(1)
    _run_casts(cast_srcs, cast_dsts, cast_tiles)

    @pl.when(j == 0)
    def _():
        xnp_ref[...] = _normed(xp_ref, g_ref)

    @pl.when((i == 0) & (j == 0))
    def _():
        xns_ref[...] = _normed(xs_ref, g_ref)

    def swiglu(xn):
        g = jnp.dot(xn, wg_ref[...], preferred_element_type=F32)
        u = jnp.dot(xn, wu_ref[...], preferred_element_type=F32)
        return _silu(g) * u

    op_ref[...] = swiglu(xnp_ref[...]).astype(op_ref.dtype)

    @pl.when(i == 0)
    def _():
        os_ref[...] = swiglu(xns_ref[...])

    @pl.when(i > 0)
    def _():
        os_ref[...] = jnp.zeros_like(os_ref)


def _gateup(xp, xs, gain, w_gu, layer, *, casts=(), tm=1024, tn=512):
    Mp, K = xp.shape
    S = SAMPLE_ROWS
    hidden = w_gu.shape[2] // 2
    assert xs.shape == (S, K) and Mp % tm == 0 and hidden % tn == 0 and w_gu.dtype == BF16
    nj = hidden // tn
    ni = Mp // tm
    c_args, c_in, c_shape, c_out, cast_tiles = _cast_specs(casts, ni * nj, nj)
    outs = pl.pallas_call(
        functools.partial(_gateup_kernel, cast_tiles=cast_tiles),
        grid=(ni, nj),
        in_specs=[pl.BlockSpec((tm, K), lambda i, j: (i, 0)),
                  pl.BlockSpec((S, K), lambda i, j: (0, 0)),
                  pl.BlockSpec((1, K), lambda i, j: (0, 0)),
                  pl.BlockSpec((None, K, tn), lambda i, j: (layer, 0, j)),
                  pl.BlockSpec((None, K, tn), lambda i, j: (layer, 0, j + nj))] + c_in,
        out_specs=[pl.BlockSpec((tm, tn), lambda i, j: (i, j)), pl.BlockSpec((S, tn), _sample_block)] + c_out,
        out_shape=[jax.ShapeDtypeStruct((Mp, hidden), BF16), jax.ShapeDtypeStruct((ni * S, hidden), F32)] + c_shape,
        scratch_shapes=[pltpu.VMEM((tm, K), BF16), pltpu.VMEM((S, K), BF16)],
        compiler_params=_params("arbitrary", "arbitrary"),
        name="gateup",
    )(xp, xs, gain.reshape(1, K), w_gu, w_gu, *c_args)
    return [outs[0], outs[1][:S]] + list(outs[2:])


def _conv_silu_slab(src, s, raw_ref, act_ref, cw_ref, cb_ref):
    Q = SSD_CHUNK
    raw_ref[s, CONV_TAIL:, :] = src
    for parity in (0, 1):
        acc = cb_ref[s]
        for k in range(CONV_W):
            first = CONV_TAIL - (CONV_W - 1) + k + parity
            acc = acc + raw_ref[s, pl.ds(first, Q // 2, stride=2), :] * cw_ref[s, k:k + 1, :]
        act_ref[s, pl.ds(parity, Q // 2, stride=2), :] = _silu(acc)
    raw_ref[s, 0:CONV_TAIL, :] = raw_ref[s, Q:Q + CONV_TAIL, :]


def _ssd_kernel(z_ref, x_ref, b_ref, c_ref, dt_ref, cw_ref, cb_ref,
                dtb_ref, alog_ref, dexp_ref, gn_ref,
                y_ref, hout_ref,
                h_ref, raw_ref, act_ref):
    Q = SSD_CHUNK
    P = SSM_HEAD_DIM
    N = SSM_STATE
    d_inner = x_ref.shape[1]
    n_heads = d_inner // P
    hpg = n_heads // SSM_GROUPS
    gw = hpg * P
    n_xs = d_inner // LANES
    n_bs = b_ref.shape[1] // LANES
    c = pl.program_id(1)

    @pl.when(c == 0)
    def _():
        h_ref[...] = jnp.zeros_like(h_ref)
        raw_ref[:, 0:CONV_TAIL, :] = jnp.zeros((raw_ref.shape[0], CONV_TAIL, LANES), F32)

    for s in range(n_xs + 2 * n_bs):
        if s < n_xs:
            src = x_ref[:, s * LANES:(s + 1) * LANES]
        elif s < n_xs + n_bs:
            src = b_ref[:, (s - n_xs) * LANES:(s - n_xs + 1) * LANES]
        else:
            src = c_ref[:, (s - n_xs - n_bs) * LANES:(s - n_xs - n_bs + 1) * LANES]
        _conv_silu_slab(src, s, raw_ref, act_ref, cw_ref, cb_ref)

    dt = _softplus(dt_ref[...] + dtb_ref[...])
    dA = dt * (-jnp.exp(alog_ref[...]))
    row = lax.broadcasted_iota(jnp.int32, (Q, Q), 0)
    col = lax.broadcasted_iota(jnp.int32, (Q, Q), 1)
    causal = row >= col
    tri = jnp.where(causal, 1.0, 0.0).astype(F32)
    cs = jnp.dot(tri, dA, precision=HIGHEST, preferred_element_type=F32)
    csT = cs.T
    dtT = dt.T
    dt_decay_end = dt * jnp.exp(cs[Q - 1:Q, :] - cs)
    chunk_decay = jnp.broadcast_to(jnp.exp(csT[0:n_heads, Q - 1:Q]), (n_heads, N))

    lane_lo = lax.broadcasted_iota(jnp.int32, (Q, 2 * P), 1) < P

    def pair_cols(arr, q):
        a0 = jnp.broadcast_to(arr[:, 2 * q:2 * q + 1], (Q, 2 * P))
        a1 = jnp.broadcast_to(arr[:, 2 * q + 1:2 * q + 2], (Q, 2 * P))
        return jnp.where(lane_lo, a0, a1)

    for g in range(SSM_GROUPS):
        Bg = act_ref[n_xs + g].astype(BF16)
        Cg = act_ref[n_xs + n_bs + g].astype(BF16)
        CB = lax.dot_general(Cg, Bg, NT_DIMS, preferred_element_type=F32)
        Hg = h_ref[g * hpg:(g + 1) * hpg].reshape(gw, N)
        y_off = lax.dot_general(Cg, Hg.astype(BF16), NT_DIMS, preferred_element_type=F32)

        ys = []
        xds = []
        for qq in range(hpg // 2):
            q = g * (hpg // 2) + qq
            lo, hi = q * 2 * P, (q + 1) * 2 * P
            xs = act_ref[q]
            xds.append((xs * pair_cols(dt_decay_end, q)).astype(BF16))
            cs_cols = [jnp.broadcast_to(cs[:, hh:hh + 1], (Q, Q)) for hh in (2 * q, 2 * q + 1)]
            atts = []
            for hh, cs_col in zip((2 * q, 2 * q + 1), cs_cols):
                seg = cs_col - csT[hh:hh + 1, :]
                decay = jnp.exp(jnp.where(causal, seg, -jnp.inf))
                atts.append(((CB * decay) * dtT[hh:hh + 1, :]).astype(BF16))
            att = jnp.concatenate(atts, axis=1)
            xbd = jnp.concatenate([jnp.where(lane_lo, xs, 0.0).astype(BF16),
                                   jnp.where(lane_lo, 0.0, xs).astype(BF16)], axis=0)
            y = jnp.dot(att, xbd, preferred_element_type=F32)
            y = y + y_off[:, qq * 2 * P:(qq + 1) * 2 * P] * jnp.exp(jnp.where(lane_lo, *cs_cols))
            y = y + xs * dexp_ref[:, lo:hi]
            ys.append(y * _silu(z_ref[:, lo:hi]))

        ssq = ys[0] * ys[0]
        for y in ys[1:]:
            ssq = ssq + y * y
        scale = lax.rsqrt(jnp.sum(ssq, axis=-1, keepdims=True) / gw + EPS)
        for qq, y in enumerate(ys):
            lo = g * gw + qq * 2 * P
            y_ref[:, lo:lo + 2 * P] = ((y * scale) * gn_ref[:, lo:lo + 2 * P]).astype(y_ref.dtype)

        S = lax.dot_general(jnp.concatenate(xds, axis=1), Bg, TN_DIMS, preferred_element_type=F32)
        for r in range(hpg):
            hh = g * hpg + r
            h_ref[hh] = h_ref[hh] * chunk_decay[hh:hh + 1, :] + S[r * P:(r + 1) * P, :]

    @pl.when(c == pl.num_programs(1) - 1)
    def _():
        hout_ref[0] = h_ref[...]


def _ssd_prompt(zxbc, dt, batch, conv_w, conv_b, dt_bias, a_log, d_skip, gate_norm):
    M = zxbc.shape[0]
    L = M // batch
    H = dt_bias.shape[0]
    d_inner = H * SSM_HEAD_DIM
    bc = SSM_GROUPS * SSM_STATE
    conv_dim = d_inner + 2 * bc
    Q = SSD_CHUNK
    nc = L // Q
    assert L % Q == 0 and d_inner % bc == 0
    xb = d_inner // bc
    row_map = lambda col: (lambda b, c: (b * nc + c, col))
    const = lambda b, c: (0, 0)
    d_exp = jnp.repeat(d_skip.astype(F32), SSM_HEAD_DIM).reshape(1, d_inner)
    assert H <= LANES and SSM_STATE == LANES and 2 * SSM_HEAD_DIM == LANES
    lane_pad = lambda v: jnp.pad(v.reshape(1, H), ((0, 0), (0, LANES - H)))
    n_slabs = conv_dim // LANES
    cw_slabs = conv_w.reshape(CONV_W, n_slabs, LANES).transpose(1, 0, 2)
    cb_slabs = conv_b.reshape(n_slabs, 1, LANES)
    const3 = lambda b, c: (0, 0, 0)
    y, h_final = pl.pallas_call(
        _ssd_kernel,
        grid=(batch, nc),
        in_specs=[pl.BlockSpec((Q, d_inner), row_map(0)),
                  pl.BlockSpec((Q, d_inner), row_map(1)),
                  pl.BlockSpec((Q, bc), row_map(2 * xb)),
                  pl.BlockSpec((Q, bc), row_map(2 * xb + 1)),
                  pl.BlockSpec((Q, LANES), row_map(0)),
                  pl.BlockSpec((n_slabs, CONV_W, LANES), const3),
                  pl.BlockSpec((n_slabs, 1, LANES), const3),
                  pl.BlockSpec((1, LANES), const),
                  pl.BlockSpec((1, LANES), const),
                  pl.BlockSpec((1, d_inner), const),
                  pl.BlockSpec((1, d_inner), const)],
        out_specs=[pl.BlockSpec((Q, d_inner), row_map(0)),
                   pl.BlockSpec((1, H, SSM_HEAD_DIM, SSM_STATE), lambda b, c: (b, 0, 0, 0))],
        out_shape=[jax.ShapeDtypeStruct((M, d_inner), BF16),
                   jax.ShapeDtypeStruct((batch, H, SSM_HEAD_DIM, SSM_STATE), F32)],
        scratch_shapes=[pltpu.VMEM((H, SSM_HEAD_DIM, SSM_STATE), F32),
                        pltpu.VMEM((n_slabs, Q + CONV_TAIL, LANES), F32),
                        pltpu.VMEM((n_slabs, Q, LANES), F32)],
        compiler_params=_params("parallel", "arbitrary"),
        name="ssd_prompt",
    )(zxbc, zxbc, zxbc, zxbc, dt, cw_slabs, cb_slabs,
      lane_pad(dt_bias), lane_pad(a_log), d_exp, gate_norm.reshape(1, d_inner))
    return y, h_final


def _ssd_step_kernel(zxbc_ref, dt_ref, cs_ref, h0_ref, cw_ref, cb_ref, dtb_ref, alog_ref,
                     dexp_ref, gn_ref, y_ref, cso_ref, ho_ref):
    P = SSM_HEAD_DIM
    N = SSM_STATE
    n_heads = h0_ref.shape[2]
    d_inner = n_heads * P
    hpg = n_heads // SSM_GROUPS
    gw = hpg * P
    conv_dim = cw_ref.shape[1]

    raw = zxbc_ref[0, :, d_inner:d_inner + conv_dim]
    prev = cs_ref[0, 0]
    acc = cb_ref[...]
    for k in range(CONV_W - 1):
        acc = acc + prev[k:k + 1, :] * cw_ref[k:k + 1, :]
    acc = acc + raw * cw_ref[CONV_W - 1:CONV_W, :]
    xbc = _silu(acc)
    cso_ref[0, 0, 0:CONV_W - 2, :] = prev[1:CONV_W - 1, :]
    cso_ref[0, 0, CONV_W - 2:CONV_W - 1, :] = raw

    dt = _softplus(dt_ref[0, :, 0:n_heads] + dtb_ref[...])
    decay = jnp.exp(dt * (-jnp.exp(alog_ref[...])))
    z = zxbc_ref[0, :, 0:d_inner]

    eye = (lax.broadcasted_iota(jnp.int32, (P, P), 0) == lax.broadcasted_iota(jnp.int32, (P, P), 1))
    ys = []
    for g in range(SSM_GROUPS):
        Bg = xbc[:, d_inner + g * N:d_inner + (g + 1) * N]
        Cg = xbc[:, d_inner + SSM_GROUPS * N + g * N:d_inner + SSM_GROUPS * N + (g + 1) * N]
        Bb = jnp.broadcast_to(Bg, (P, N))
        for r in range(hpg):
            hh = g * hpg + r
            xs = xbc[:, hh * P:(hh + 1) * P]
            xdt = xs * dt[:, hh:hh + 1]
            xdiag = jnp.where(eye, jnp.broadcast_to(xdt, (P, P)), 0.0)
            outer = jnp.dot(xdiag, Bb, precision=HIGHEST, preferred_element_type=F32)
            ho_ref[0, 0, hh] = h0_ref[0, 0, hh] * decay[:, hh:hh + 1] + outer
        Hg = ho_ref[0, 0, g * hpg:(g + 1) * hpg].reshape(gw, N)
        yg = lax.dot_general(Cg.astype(BF16), Hg.astype(BF16), NT_DIMS, preferred_element_type=F32)
        lo, hi = g * gw, (g + 1) * gw
        yg = yg + xbc[:, lo:hi] * dexp_ref[:, lo:hi]
        yg = yg * _silu(z[:, lo:hi])
        yg = yg * lax.rsqrt(jnp.mean(yg * yg, axis=-1, keepdims=True) + EPS)
        ys.append(yg * gn_ref[:, lo:hi])
    y_ref[0] = jnp.concatenate(ys, axis=1).astype(y_ref.dtype)


def _ssd_step(zxbc, dt, conv_state, ssm_state, conv_w, conv_b, dt_bias, a_log, d_skip, gate_norm):
    B = zxbc.shape[0]
    H = dt_bias.shape[0]
    d_inner = H * SSM_HEAD_DIM
    conv_dim = conv_w.shape[1]
    const = lambda b: (0, 0)
    d_exp = jnp.repeat(d_skip.astype(F32), SSM_HEAD_DIM).reshape(1, d_inner)
    cs4 = conv_state.reshape(1, B, CONV_W - 1, conv_dim)
    h5 = ssm_state.reshape(1, B, H, SSM_HEAD_DIM, SSM_STATE)
    return pl.pallas_call(
        _ssd_step_kernel,
        grid=(B,),
        in_specs=[pl.BlockSpec((1, 1, zxbc.shape[1]), lambda b: (b, 0, 0)),
                  pl.BlockSpec((1, 1, dt.shape[1]), lambda b: (b, 0, 0)),
                  pl.BlockSpec((1, 1, CONV_W - 1, conv_dim), lambda b: (0, b, 0, 0)),
                  pl.BlockSpec((1, 1, H, SSM_HEAD_DIM, SSM_STATE), lambda b: (0, b, 0, 0, 0)),
                  pl.BlockSpec((CONV_W, conv_dim), const),
                  pl.BlockSpec((1, conv_dim), const),
                  pl.BlockSpec((1, H), const),
                  pl.BlockSpec((1, H), const),
                  pl.BlockSpec((1, d_inner), const),
                  pl.BlockSpec((1, d_inner), const)],
        out_specs=[pl.BlockSpec((1, 1, d_inner), lambda b: (b, 0, 0)),
                   pl.BlockSpec((1, 1, CONV_W - 1, conv_dim), lambda b: (0, b, 0, 0)),
                   pl.BlockSpec((1, 1, H, SSM_HEAD_DIM, SSM_STATE), lambda b: (0, b, 0, 0, 0))],
        out_shape=[jax.ShapeDtypeStruct((B, 1, d_inner), F32),
                   jax.ShapeDtypeStruct(cs4.shape, F32),
                   jax.ShapeDtypeStruct(h5.shape, F32)],
        compiler_params=_params("arbitrary"),
        name="ssd_step",
    )(zxbc.reshape(B, 1, -1), dt.reshape(B, 1, -1), cs4, h5, conv_w, conv_b.reshape(1, conv_dim),
      dt_bias.reshape(1, H), a_log.reshape(1, H), d_exp, gate_norm.reshape(1, d_inner))


def _attn_kernel(q0_ref, q1_ref, q2_ref, k_ref, v_ref, o_ref, on_ref, lse_ref, qs_ref, ks_ref, vs_ref):
    L = k_ref.shape[0]
    QB = ATT_BLOCK
    KW = QB + DIL_SLOTS
    P = ATT_PRESTRIDE
    p_shift = P.bit_length() - 1
    scale = HEAD_DIM ** -0.5
    e = lax.broadcasted_iota(jnp.int32, (QB, KW), 0) - lax.broadcasted_iota(jnp.int32, (QB, KW), 1)

    staged = {rate: rate > P for rate in DIL_RATES}
    for src_ref, dst_ref, needed in ((q2_ref, qs_ref, staged[DIL_RATES[2]]), (k_ref, ks_ref, any(staged.values())),
                                     (v_ref, vs_ref, any(staged.values()))):
        if needed:
            def stage(i, carry, src_ref=src_ref, dst_ref=dst_ref):
                cp = i & (P - 1)
                t = i >> p_shift
                dst = pl.multiple_of(cp * (L // P) + t * QB, QB)
                dst_ref[pl.ds(dst, QB), :] = src_ref[pl.ds(cp + P * QB * t, QB, stride=P), :]
                return carry
            lax.fori_loop(0, L // QB, stage, 0, unroll=4)

    for g, (q_ref, rate) in enumerate(zip((q0_ref, q1_ref, q2_ref), DIL_RATES)):
        shift = rate.bit_length() - 1

        def unit(n, carry, g=g, q_ref=q_ref, rate=rate, shift=shift):
            c = n & (rate - 1)
            u0 = (n >> shift) * QB
            v0 = jnp.maximum(u0 - DIL_SLOTS, 0)
            q_rows = pl.ds(c + rate * u0, QB, stride=rate)
            if staged[rate]:
                base = (c & (P - 1)) * (L // P) + (c >> p_shift)
                q = qs_ref[pl.ds(base + (rate // P) * u0, QB, stride=rate // P), :].astype(BF16)
                k_rows = pl.ds(base + (rate // P) * v0, KW, stride=rate // P)
                k = ks_ref[k_rows, :].astype(BF16)
                v = vs_ref[k_rows, :].astype(BF16)
            else:
                k_rows = pl.ds(c + rate * v0, KW, stride=rate)
                q = q_ref[q_rows, :].astype(BF16)
                k = k_ref[k_rows, :].astype(BF16)
                v = v_ref[k_rows, :].astype(BF16)
            s = lax.dot_general(q, k, NT_DIMS, preferred_element_type=F32) * scale
            d = e + (u0 - v0)
            s = jnp.where(d >= 0, s, -jnp.inf)
            s = jnp.where(d <= DIL_SLOTS, s, -jnp.inf)
            m = s.max(axis=1, keepdims=True)
            p = jnp.exp(s - m)
            l = p.sum(axis=1, keepdims=True)
            o = jnp.dot(p.astype(BF16), v, preferred_element_type=F32)
            on_ref[g, q_rows, :] = o / l
            lse_ref[g, q_rows, :] = jnp.broadcast_to(m + jnp.log(l), (QB, HEAD_DIM))
            return carry

        lax.fori_loop(0, L // QB, unit, 0, unroll=ATT_UNROLL)

    def mix(i, carry):
        rows = pl.ds(pl.multiple_of(i * QB, QB), QB)
        lses = [lse_ref[g, rows, :] for g in range(len(DIL_RATES))]
        m = functools.reduce(jnp.maximum, lses)
        ws = [jnp.exp(x - m) for x in lses]
        num = functools.reduce(jnp.add, [w * on_ref[g, rows, :] for g, w in enumerate(ws)])
        o_ref[rows, :] = (num / functools.reduce(jnp.add, ws)).astype(o_ref.dtype)
        return carry

    lax.fori_loop(0, L // QB, mix, 0, unroll=2)


def _attn_prompt(q, kv, batch):
    M = q.shape[0]
    L = M // batch
    n_grp = len(DIL_RATES)
    assert all(r & (r - 1) == 0 and L % (ATT_BLOCK * r) == 0 and L >= r * (ATT_BLOCK + DIL_SLOTS)
               for r in DIL_RATES)
    qspec = lambda g: pl.BlockSpec((L, HEAD_DIM), lambda b, h: (b, g * KV_HEADS + h))
    kspec = pl.BlockSpec((L, HEAD_DIM), lambda b, h: (b, h))
    vspec = pl.BlockSpec((L, HEAD_DIM), lambda b, h: (b, KV_HEADS + h))
    return pl.pallas_call(
        _attn_kernel,
        grid=(batch, KV_HEADS),
        in_specs=[qspec(0), qspec(1), qspec(2), kspec, vspec],
        out_specs=pl.BlockSpec((L, HEAD_DIM), lambda b, h: (b, h)),
        out_shape=jax.ShapeDtypeStruct((M, KV_HEADS * HEAD_DIM), BF16),
        scratch_shapes=[pltpu.VMEM((n_grp, L, HEAD_DIM), F32), pltpu.VMEM((n_grp, L, HEAD_DIM), F32)]
        + [pltpu.VMEM((L, HEAD_DIM), F32)] * 3,
        compiler_params=_params("parallel", "arbitrary"),
        name="attn_prompt",
    )(q, q, q, kv, kv)


def _bf16_round(a):
    return a.astype(BF16).astype(F32)


def _attn_step_kernel(q_ref, *refs):
    n = len(DIL_RATES)
    k_refs, v_refs = refs[0:n], refs[n:2 * n]
    kn_ref, vn_ref, o_ref = refs[2 * n:]
    scale = HEAD_DIM ** -0.5
    kn = _bf16_round(kn_ref[0, 0])
    vn = _bf16_round(vn_ref[0, 0])
    scores, new_scores = [], []
    for g, k_ref in enumerate(k_refs):
        qg = _bf16_round(q_ref[0, g])
        kg = _bf16_round(k_ref[0, :, 0])
        scores.append(jnp.sum(kg * qg[None], axis=-1, keepdims=True) * scale)
        new_scores.append(jnp.sum(kn * qg, axis=-1, keepdims=True) * scale)
    m = functools.reduce(jnp.maximum, [s.max(axis=0) for s in scores] + new_scores)
    l = jnp.zeros_like(m)
    o = jnp.zeros((KV_HEADS, HEAD_DIM), F32)
    for s, s_new, v_ref in zip(scores, new_scores, v_refs):
        p = jnp.exp(s - m[None])
        p_new = jnp.exp(s_new - m)
        l = l + p.sum(axis=0) + p_new
        o = o + (_bf16_round(p) * _bf16_round(v_ref[0, :, 0])).sum(axis=0) + _bf16_round(p_new) * vn
    o_ref[0, 0] = (o / l).astype(o_ref.dtype)


def _attn_step(q, cache_k, cache_v, k_new, v_new):
    B, T = cache_k.shape[0], cache_k.shape[1]
    S = DIL_SLOTS
    assert T == S * max(DIL_RATES), "every slot of every dilation group lies inside the cached window"
    qg = q.reshape(B, len(DIL_RATES), KV_HEADS, HEAD_DIM)
    row_shape = (B, 1, KV_HEADS, HEAD_DIM)
    row_spec = pl.BlockSpec((1, 1, KV_HEADS, HEAD_DIM), lambda b: (b, 0, 0, 0))
    args = [qg]
    in_specs = [pl.BlockSpec((1, len(DIL_RATES), KV_HEADS, HEAD_DIM), lambda b: (b, 0, 0, 0))]
    for cache in (cache_k, cache_v):
        for rate in DIL_RATES:
            args.append(cache.reshape(B, T // rate, rate, KV_HEADS, HEAD_DIM))
            in_specs.append(pl.BlockSpec((1, S, 1, KV_HEADS, HEAD_DIM),
                                         lambda b, blk=T // rate // S - 1: (b, blk, 0, 0, 0)))
    args += [k_new.reshape(row_shape), v_new.reshape(row_shape)]
    in_specs += [row_spec, row_spec]
    o = pl.pallas_call(
        _attn_step_kernel,
        grid=(B,),
        in_specs=in_specs,
        out_specs=row_spec,
        out_shape=jax.ShapeDtypeStruct(row_shape, F32),
        compiler_params=_params("parallel"),
        name="attn_step",
    )(*args)
    return o.reshape(B, KV_HEADS * HEAD_DIM)


def _rope_tables(pos):
    half = HEAD_DIM // 2
    inv = jnp.power(jnp.float32(ROPE_THETA), -jnp.arange(half, dtype=jnp.float32) / half)
    ang = pos.astype(jnp.float32)[:, None] * inv[None, :]
    cos = jnp.cos(ang)
    sin = jnp.sin(ang)
    return jnp.concatenate([cos, cos], axis=1), jnp.concatenate([-sin, sin], axis=1)


def kernel(x_prompt, x_sample, state_conv, state_ssm, cache_k, cache_v, a_norm, a_w_in, a_conv_w, a_conv_b,
           a_dt_bias, a_log, a_d, a_gate_norm, a_w_out, kv_norm, w_kv, b_norm, b_w_q, b_w_o, ffn_norm,
           ffn_w_gu, ffn_w_down, final_norm):
    assert a_norm.shape[0] == 1 and b_norm.shape[0] == 1, "one Mamba-2 layer followed by one attention layer"
    Bp, Lp, D = x_prompt.shape
    Bs, Ls, _ = x_sample.shape
    assert Ls == 1 and Bs == SAMPLE_ROWS, "sample group decodes one token for SAMPLE_ROWS sequences"
    n_heads = a_dt_bias.shape[1]
    d_inner = a_w_out.shape[1]
    conv_dim = a_conv_w.shape[2]
    kv_dim = KV_HEADS * HEAD_DIM
    w_dt = jnp.pad(a_w_in[0][:, d_inner + conv_dim:], ((0, 0), (0, LANES - n_heads))).astype(BF16)
    ssd_w = (a_conv_w[0], a_conv_b[0], a_dt_bias[0], a_log[0], a_d[0], a_gate_norm[0])
    w_kv = w_kv[None]

    cos_p, sin_p = _rope_tables(jnp.arange(Lp, dtype=jnp.int32))
    cos_s, sin_s = _rope_tables(PAST_LEN + jnp.arange(Ls, dtype=jnp.int32))
    rope = ((cos_p, sin_p, Lp), (jnp.broadcast_to(cos_s, (Bs, HEAD_DIM)), jnp.broadcast_to(sin_s, (Bs, HEAD_DIM))))

    xp0 = x_prompt.reshape(Bp * Lp, D)
    xs0 = x_sample.reshape(Bs, D)

    zxbc_p, zxbc_s, dt_p, dt_s, w_out_bf, w_gu0_bf, w_down0_bf = _proj(
        xp0, xs0, [_seg(a_w_in.astype(BF16), n_cols=d_inner + conv_dim)], gains=[a_norm[0]], aux_w=w_dt, tn=1024,
        casts=[(a_w_out, 0), (ffn_w_gu, 0), (ffn_w_down, 0)])
    y_p, p_ssm = _ssd_prompt(zxbc_p, dt_p, Bp, *ssd_w)
    p_conv = zxbc_p.reshape(Bp, Lp, -1)[:, Lp - (CONV_W - 1):, d_inner:]
    y_s, s_conv, s_ssm = _ssd_step(zxbc_s, dt_s, state_conv[0], state_ssm[0], *ssd_w)
    *x1, w_gu1_bf = _proj(y_p, y_s.reshape(Bs, d_inner), [_seg(w_out_bf)], res=(xp0, xs0), casts=[(ffn_w_gu, 1)],
                          tm=256, tn=D)
    *h, w_kv_bf, w_q_bf, w_o_bf = _gateup(*x1, ffn_norm[0], w_gu0_bf, 0, casts=[(w_kv, 0), (b_w_q, 0), (b_w_o, 0)])
    x2 = _proj(*h, [_seg(w_down0_bf)], res=x1, tm=256, tn=D)

    kv_p, kv_s, q_p, q_s = _proj(*x2, [_seg(w_kv_bf, rope_cols=kv_dim), _seg(w_q_bf, rope_cols=b_w_q.shape[2])],
                                 gains=[kv_norm, b_norm[0]], rope=rope, tm=256, tn=None)
    o_p = _attn_prompt(q_p, kv_p, Bp)
    o_s = _attn_step(q_s, cache_k, cache_v, kv_s[:, :kv_dim], kv_s[:, kv_dim:])
    x3 = _proj(o_p, o_s, [_seg(w_o_bf)], res=x2, tm=512, tn=D)
    *h, w_down1_bf = _gateup(*x3, ffn_norm[1], w_gu1_bf, 0, casts=[(ffn_w_down, 1)])
    y_prompt, y_sample = _proj(*h, [_seg(w_down1_bf)], res=x3, post_gain=final_norm, tm=256, tn=D)
    y_prompt = y_prompt.reshape(Bp, Lp, D)
    y_sample = y_sample.reshape(Bs, Ls, D)
    keep = min(DIL_SLOTS * max(DIL_RATES), Lp)
    p_kv = kv_p.reshape(Bp, Lp, 2 * KV_HEADS, HEAD_DIM)[:, Lp - keep:]
    s_kv = kv_s.reshape(Bs, Ls, 2 * KV_HEADS, HEAD_DIM)
    return (y_prompt, y_sample, p_conv[None], p_ssm[None], p_kv[:, :, :KV_HEADS], p_kv[:, :, KV_HEADS:],
            s_conv, s_ssm, s_kv[:, :, :KV_HEADS], s_kv[:, :, KV_HEADS:])
```

```python
import functools

import jax
import jax.numpy as jnp
from jax import lax
from jax.experimental import pallas as pl
from jax.experimental.pallas import tpu as pltpu

F32 = jnp.float32
BF16 = jnp.bfloat16
HIGHEST = lax.Precision.HIGHEST

EPS = 1e-6
ROPE_THETA = 10000.0
SSD_CHUNK = 128
SSM_HEAD_DIM = 64
SSM_STATE = 128
SSM_GROUPS = 8
CONV_W = 4
HEAD_DIM = 128
KV_HEADS = 8
DIL_RATES = (1, 4, 16)
DIL_SLOTS = 128
ATT_BLOCK = 128
ATT_UNROLL = 32
ATT_PRESTRIDE = 4
PAST_LEN = 16384

V7X_VMEM_BYTES = 64 * 1024 * 1024
VMEM_LIMIT = V7X_VMEM_BYTES - 8 * 1024 * 1024
LANES = 128
CONV_TAIL = 8
SAMPLE_ROWS = 8

NT_DIMS = (((1,), (1,)), ((), ()))
TN_DIMS = (((0,), (0,)), ((), ()))


def _params(*sem):
    return pltpu.CompilerParams(dimension_semantics=sem, vmem_limit_bytes=VMEM_LIMIT)


def _silu(x):
    h = 0.5 * x
    return h + h * jnp.tanh(h)


def _softplus(x):
    return jnp.maximum(x, 0.0) + jnp.log1p(jnp.exp(-jnp.abs(x)))


def _rms_scale(x):
    return lax.rsqrt(jnp.mean(x * x, axis=-1, keepdims=True) + EPS)


def _normed(x_ref, g_ref):
    x = x_ref[...]
    return ((x * _rms_scale(x)) * g_ref[...]).astype(BF16)


def _sample_block(i, j):
    return (i, j)


def _cast_specs(casts, n_steps, nj):
    BF16_ROWS = 16
    args, in_specs, out_shape, out_specs, tiles = [], [], [], [], []
    for src, layer in casts:
        R, C = src.shape[1:]
        rows = next(r for r in range(BF16_ROWS, R + 1, BF16_ROWS) if R % r == 0 and R // r <= n_steps)
        nt = R // rows
        tile = lambda i, j, nt=nt: jnp.minimum(i * nj + j, nt - 1)
        args.append(src)
        in_specs.append(pl.BlockSpec((None, rows, C), lambda i, j, layer=layer, tile=tile: (layer, tile(i, j), 0)))
        out_shape.append(jax.ShapeDtypeStruct((1, R, C), BF16))
        out_specs.append(pl.BlockSpec((None, rows, C), lambda i, j, tile=tile: (0, tile(i, j), 0)))
        tiles.append(nt)
    return args, in_specs, out_shape, out_specs, tuple(tiles)


def _run_casts(src_refs, dst_refs, tiles):
    step = pl.program_id(0) * pl.num_programs(1) + pl.program_id(1)
    for src_ref, dst_ref, nt in zip(src_refs, dst_refs, tiles):
        @pl.when(step < nt)
        def _(src_ref=src_ref, dst_ref=dst_ref):
            dst_ref[...] = src_ref[...].astype(BF16)


def _proj_kernel(*refs, segs, resident, norm, has_rope, has_res, has_post, has_aux, cast_tiles):
    it = iter(refs)
    xp_ref, xs_ref = next(it), next(it)
    g_refs = [next(it) for _ in segs] if norm else None
    w_refs = [next(it) for _ in segs]
    auxw_ref = next(it) if has_aux else None
    rope_p = (next(it), next(it)) if has_rope else None
    rope_s = (next(it), next(it)) if has_rope else None
    resp_ref, ress_ref = (next(it), next(it)) if has_res else (None, None)
    pg_ref = next(it) if has_post else None
    cast_srcs = [next(it) for _ in cast_tiles]
    out_refs =[(next(it), next(it)) for _ in segs]
    auxp_ref, auxs_ref = (next(it), next(it)) if has_aux else (None, None)
    cast_dsts = [next(it) for _ in cast_tiles]
    xnp_ref, xns_ref = (next(it), next(it)) if norm else (None, None)
    i = pl.program_id(0)
    j = pl.program_id(1)
    _run_casts(cast_srcs, cast_dsts, cast_tiles)

    if norm:
        @pl.when(j == 0)
        def _():
            for s, g_ref in enumerate(g_refs):
                xnp_ref[s] = _normed(xp_ref, g_ref)

        @pl.when((i == 0) & (j == 0))
        def _():
            for s, g_ref in enumerate(g_refs):
                xns_ref[s] = _normed(xs_ref, g_ref)

    def lhs_p(s):
        return xnp_ref[s] if norm else xp_ref[...]

    def lhs_s(s):
        return xns_ref[s] if norm else xs_ref[...].astype(BF16)

    def emit(lhs, w_ref, res_ref, o_ref, rope, lo, n_tiles, n_rope):
        acc = jnp.dot(lhs, w_ref[...].astype(BF16), preferred_element_type=F32)
        if has_res:
            acc = acc + res_ref[...]
        if has_post:
            acc = (acc * _rms_scale(acc)) * pg_ref[...]
        heads = acc.shape[1] // HEAD_DIM

        def store(rope_heads):
            if rope_heads:
                cos = rope[0][...]
                sin = rope[1][...]
            for h in range(rope_heads):
                a = acc[:, h * HEAD_DIM:(h + 1) * HEAD_DIM]
                o_ref[:, h * HEAD_DIM:(h + 1) * HEAD_DIM] = (
                    a * cos + pltpu.roll(a, HEAD_DIM // 2, 1) * sin).astype(o_ref.dtype)
            if rope_heads < heads:
                o_ref[:, rope_heads * HEAD_DIM:] = acc[:, rope_heads * HEAD_DIM:].astype(o_ref.dtype)

        if resident:
            store(n_rope)
        elif n_rope in (0, n_tiles):
            store(heads if n_rope else 0)
        else:
            pl.when(j < lo + n_rope)(lambda: store(heads))
            pl.when(j >= lo + n_rope)(lambda: store(0))

    for s, (lo, n_tiles, n_rope) in enumerate(segs):
        def segment(s=s, lo=lo, n_tiles=n_tiles, n_rope=n_rope):
            emit(lhs_p(s), w_refs[s], resp_ref, out_refs[s][0], rope_p, lo, n_tiles, n_rope)

            @pl.when(i == 0)
            def _():
                emit(lhs_s(s), w_refs[s], ress_ref, out_refs[s][1], rope_s, lo, n_tiles, n_rope)

        if resident or len(segs) == 1:
            segment()
        else:
            pl.when((j >= lo) & (j < lo + n_tiles))(segment)

    @pl.when(i > 0)
    def _():
        for _, os_ref in out_refs:
            os_ref[...] = jnp.zeros_like(os_ref)

    if has_aux:
        @pl.when(j == pl.num_programs(1) - 1)
        def _():
            auxp_ref[...] = jnp.dot(lhs_p(0), auxw_ref[...], preferred_element_type=F32)

            @pl.when(i == 0)
            def _():
                auxs_ref[...] = jnp.dot(lhs_s(0), auxw_ref[...], preferred_element_type=F32)

            @pl.when(i > 0)
            def _():
                auxs_ref[...] = jnp.zeros_like(auxs_ref)


def _seg(w, layer=0, col0=0, n_cols=None, rope_cols=0):
    return (w, layer, col0, w.shape[-1] - col0 if n_cols is None else n_cols, rope_cols)


def _proj(xp, xs, ws, *, gains=None, aux_w=None, rope=None, res=None, post_gain=None, casts=(), out_dtype=F32,
          tm=1024, tn=512):
    Mp, K = xp.shape
    S = SAMPLE_ROWS
    assert xs.shape == (S, K) and Mp % tm == 0
    norm = gains is not None
    resident = tn is None
    segs, widths, lo = [], [], 0
    for w, layer, col0, n_cols, rope_cols in ws:
        tw = n_cols if resident else tn
        rope_unit = HEAD_DIM if resident else tw
        assert w.ndim == 3 and n_cols % tw == 0 and col0 % tw == 0 and rope_cols % rope_unit == 0
        segs.append((lo, n_cols // tw, rope_cols // rope_unit))
        widths.append(tw)
        lo += 0 if resident else n_cols // tw
    nj, ni = (1 if resident else lo), Mp // tm
    has_rope = any(r for _, _, r in segs)
    assert not (has_rope and res is not None) and (rope is not None) == has_rope
    row = lambda i, j: (i, 0)
    const = lambda i, j: (0, 0)
    args = [xp, xs]
    in_specs = [pl.BlockSpec((tm, K), row), pl.BlockSpec((S, K), const)]
    if norm:
        assert len(gains) == len(ws)
        args += [g.reshape(1, K) for g in gains]
        in_specs += [pl.BlockSpec((1, K), const)] * len(gains)
    for (w, layer, col0, _, _), (lo, n_tiles, _), tw in zip(ws, segs, widths):
        args.append(w)
        in_specs.append(pl.BlockSpec(
            (None, K, tw), lambda i, j, layer=layer, lo=lo, n=n_tiles, c0=col0 // tw:
            (layer, 0, c0 + jnp.clip(j - lo, 0, n - 1)),
            pipeline_mode=pl.Buffered(1) if nj == 1 else None))
    if aux_w is not None:
        args.append(aux_w)
        in_specs.append(pl.BlockSpec(aux_w.shape, const))
    if has_rope:
        (cos_p, sin_p, rows_per_seq), (cos_s, sin_s) = rope
        nseq = rows_per_seq // tm
        args += [cos_p, sin_p, cos_s, sin_s]
        in_specs += [pl.BlockSpec((tm, HEAD_DIM), lambda i, j: (i % nseq, 0))] * 2
        in_specs += [pl.BlockSpec((S, HEAD_DIM), const)] * 2
    assert (res is None and post_gain is None) or len(ws) == 1
    if res is not None:
        args += list(res)
        in_specs += [pl.BlockSpec((tm, widths[0]), lambda i, j: (i, j)),
                     pl.BlockSpec((S, widths[0]), lambda i, j: (0, j))]
    if post_gain is not None:
        assert nj == 1
        args.append(post_gain.reshape(1, widths[0]))
        in_specs.append(pl.BlockSpec((1, widths[0]), const))
    out_shape, out_specs = [], []
    for (lo, n_tiles, _), tw in zip(segs, widths):
        col = lambda i, j, lo=lo, n=n_tiles: (i, jnp.clip(j - lo, 0, n - 1))
        out_shape += [jax.ShapeDtypeStruct((Mp, n_tiles * tw), out_dtype),
                      jax.ShapeDtypeStruct((ni * S, n_tiles * tw), F32)]
        out_specs += [pl.BlockSpec((tm, tw), col), pl.BlockSpec((S, tw), col)]
    if aux_w is not None:
        na = aux_w.shape[1]
        out_shape += [jax.ShapeDtypeStruct((Mp, na), F32), jax.ShapeDtypeStruct((ni * S, na), F32)]
        out_specs += [pl.BlockSpec((tm, na), row), pl.BlockSpec((S, na), row)]
    n_main = len(out_shape)
    c_args, c_in, c_shape, c_out, cast_tiles = _cast_specs(casts, ni * nj, nj)
    args += c_args
    in_specs += c_in
    out_shape += c_shape
    out_specs += c_out
    outs = pl.pallas_call(
        functools.partial(_proj_kernel, segs=tuple(segs), resident=resident, norm=norm, has_rope=has_rope,
                          has_res=res is not None, has_post=post_gain is not None, has_aux=aux_w is not None,
                          cast_tiles=cast_tiles),
        grid=(ni, nj),
        in_specs=in_specs,
        out_specs=out_specs,
        out_shape=out_shape,
        scratch_shapes=[pltpu.VMEM((len(ws), tm, K), BF16), pltpu.VMEM((len(ws), S, K), BF16)] if norm else [],
        compiler_params=_params("arbitrary", "arbitrary"),
        name="proj",
    )(*args)
    return [o[:S] if n < n_main and n % 2 == 1 else o for n, o in enumerate(outs)]


def _gateup_kernel(xp_ref, xs_ref, g_ref, wg_ref, wu_ref, *refs, cast_tiles):
    n = len(cast_tiles)
    cast_srcs = refs[:n]
    op_ref, os_ref = refs[n:n + 2]
    cast_dsts = refs[n + 2:2 * n + 2]
    xnp_ref, xns_ref = refs[2 * n + 2:]
    i = pl.program_id(0)
    j = pl.program_id(1)
    _run_casts(cast_srcs, cast_dsts, cast_tiles)

    @pl.when(j == 0)
    def _():
        xnp_ref[...] = _normed(xp_ref, g_ref)

    @pl.when((i == 0) & (j == 0))
    def _():
        xns_ref[...] = _normed(xs_ref, g_ref)

    def swiglu(xn):
        g = jnp.dot(xn, wg_ref[...], preferred_element_type=F32)
        u = jnp.dot(xn, wu_ref[...], preferred_element_type=F32)
        return _silu(g) * u

    op_ref[...] = swiglu(xnp_ref[...]).astype(op_ref.dtype)

    @pl.when(i == 0)
    def _():
        os_ref[...] = swiglu(xns_ref[...])

    @pl.when(i > 0)
    def _():
        os_ref[...] = jnp.zeros_like(os_ref)


def _gateup(xp, xs, gain, w_gu, layer, *, casts=(), tm=1024, tn=512):
    Mp, K = xp.shape
    S = SAMPLE_ROWS
    hidden = w_gu.shape[2] // 2
    assert xs.shape == (S, K) and Mp % tm == 0 and hidden % tn == 0 and w_gu.dtype == BF16
    nj = hidden // tn
    ni = Mp // tm
    c_args, c_in, c_shape, c_out, cast_tiles = _cast_specs(casts, ni * nj, nj)
    outs = pl.pallas_call(
        functools.partial(_gateup_kernel, cast_tiles=cast_tiles),
        grid=(ni, nj),
        in_specs=[pl.BlockSpec((tm, K), lambda i, j: (i, 0)),
                  pl.BlockSpec((S, K), lambda i, j: (0, 0)),
                  pl.BlockSpec((1, K), lambda i, j: (0, 0)),
                  pl.BlockSpec((None, K, tn), lambda i, j: (layer, 0, j)),
                  pl.BlockSpec((None, K, tn), lambda i, j: (layer, 0, j + nj))] + c_in,
        out_specs=[pl.BlockSpec((tm, tn), lambda i, j: (i, j)), pl.BlockSpec((S, tn), _sample_block)] + c_out,
        out_shape=[jax.ShapeDtypeStruct((Mp, hidden), BF16), jax.ShapeDtypeStruct((ni * S, hidden), F32)] + c_shape,
        scratch_shapes=[pltpu.VMEM((tm, K), BF16), pltpu.VMEM((S, K), BF16)],
        compiler_params=_params("arbitrary", "arbitrary"),
        name="gateup",
    )(xp, xs, gain.reshape(1, K), w_gu, w_gu, *c_args)
    return [outs[0], outs[1][:S]] + list(outs[2:])


def _conv_silu_slab(src, s, raw_ref, act_ref, cw_ref, cb_ref):
    Q = SSD_CHUNK
    raw_ref[s, CONV_TAIL:, :] = src
    for parity in (0, 1):
        acc = cb_ref[s]
        for k in range(CONV_W):
            first = CONV_TAIL - (CONV_W - 1) + k + parity
            acc = acc + raw_ref[s, pl.ds(first, Q // 2, stride=2), :] * cw_ref[s, k:k + 1, :]
        act_ref[s, pl.ds(parity, Q // 2, stride=2), :] = _silu(acc)
    raw_ref[s, 0:CONV_TAIL, :] = raw_ref[s, Q:Q + CONV_TAIL, :]


def _ssd_kernel(z_ref, x_ref, b_ref, c_ref, dt_ref, cw_ref, cb_ref,
                dtb_ref, alog_ref, dexp_ref, gn_ref,
                y_ref, hout_ref,
                h_ref, raw_ref, act_ref):
    Q = SSD_CHUNK
    P = SSM_HEAD_DIM
    N = SSM_STATE
    d_inner = x_ref.shape[1]
    n_heads = d_inner // P
    hpg = n_heads // SSM_GROUPS
    gw = hpg * P
    n_xs = d_inner // LANES
    n_bs = b_ref.shape[1] // LANES
    c = pl.program_id(1)

    @pl.when(c == 0)
    def _():
        h_ref[...] = jnp.zeros_like(h_ref)
        raw_ref[:, 0:CONV_TAIL, :] = jnp.zeros((raw_ref.shape[0], CONV_TAIL, LANES), F32)

    for s in range(n_xs + 2 * n_bs):
        if s < n_xs:
            src = x_ref[:, s * LANES:(s + 1) * LANES]
        elif s < n_xs + n_bs:
            src = b_ref[:, (s - n_xs) * LANES:(s - n_xs + 1) * LANES]
        else:
            src = c_ref[:, (s - n_xs - n_bs) * LANES:(s - n_xs - n_bs + 1) * LANES]
        _conv_silu_slab(src, s, raw_ref, act_ref, cw_ref, cb_ref)

    dt = _softplus(dt_ref[...] + dtb_ref[...])
    dA = dt * (-jnp.exp(alog_ref[...]))
    row = lax.broadcasted_iota(jnp.int32, (Q, Q), 0)
    col = lax.broadcasted_iota(jnp.int32, (Q, Q), 1)
    causal = row >= col
    tri = jnp.where(causal, 1.0, 0.0).astype(F32)
    cs = jnp.dot(tri, dA, precision=HIGHEST, preferred_element_type=F32)
    csT = cs.T
    dtT = dt.T
    dt_decay_end = dt * jnp.exp(cs[Q - 1:Q, :] - cs)
    chunk_decay = jnp.broadcast_to(jnp.exp(csT[0:n_heads, Q - 1:Q]), (n_heads, N))

    lane_lo = lax.broadcasted_iota(jnp.int32, (Q, 2 * P), 1) < P

    def pair_cols(arr, q):
        a0 = jnp.broadcast_to(arr[:, 2 * q:2 * q + 1], (Q, 2 * P))
        a1 = jnp.broadcast_to(arr[:, 2 * q + 1:2 * q + 2], (Q, 2 * P))
        return jnp.where(lane_lo, a0, a1)

    for g in range(SSM_GROUPS):
        Bg = act_ref[n_xs + g].astype(BF16)
        Cg = act_ref[n_xs + n_bs + g].astype(BF16)
        CB = lax.dot_general(Cg, Bg, NT_DIMS, preferred_element_type=F32)
        Hg = h_ref[g * hpg:(g + 1) * hpg].reshape(gw, N)
        y_off = lax.dot_general(Cg, Hg.astype(BF16), NT_DIMS, preferred_element_type=F32)

        ys = []
        xds = []
        for qq in range(hpg // 2):
            q = g * (hpg // 2) + qq
            lo, hi = q * 2 * P, (q + 1) * 2 * P
            xs = act_ref[q]
            xds.append((xs * pair_cols(dt_decay_end, q)).astype(BF16))
            cs_cols = [jnp.broadcast_to(cs[:, hh:hh + 1], (Q, Q)) for hh in (2 * q, 2 * q + 1)]
            atts = []
            for hh, cs_col in zip((2 * q, 2 * q + 1), cs_cols):
                seg = cs_col - csT[hh:hh + 1, :]
                decay = jnp.exp(jnp.where(causal, seg, -jnp.inf))
                atts.append(((CB * decay) * dtT[hh:hh + 1, :]).astype(BF16))
            att = jnp.concatenate(atts, axis=1)
            xbd = jnp.concatenate([jnp.where(lane_lo, xs, 0.0).astype(BF16),
                                   jnp.where(lane_lo, 0.0, xs).astype(BF16)], axis=0)
            y = jnp.dot(att, xbd, preferred_element_type=F32)
            y = y + y_off[:, qq * 2 * P:(qq + 1) * 2 * P] * jnp.exp(jnp.where(lane_lo, *cs_cols))
            y = y + xs * dexp_ref[:, lo:hi]
            ys.append(y * _silu(z_ref[:, lo:hi]))

        ssq = ys[0] * ys[0]
        for y in ys[1:]:
            ssq = ssq + y * y
        scale = lax.rsqrt(jnp.sum(ssq, axis=-1, keepdims=True) / gw + EPS)
        for qq, y in enumerate(ys):
            lo = g * gw + qq * 2 * P
            y_ref[:, lo:lo + 2 * P] = ((y * scale) * gn_ref[:, lo:lo + 2 * P]).astype(y_ref.dtype)

        S = lax.dot_general(jnp.concatenate(xds, axis=1), Bg, TN_DIMS, preferred_element_type=F32)
        for r in range(hpg):
            hh = g * hpg + r
            h_ref[hh] = h_ref[hh] * chunk_decay[hh:hh + 1, :] + S[r * P:(r + 1) * P, :]

    @pl.when(c == pl.num_programs(1) - 1)
    def _():
        hout_ref[0] = h_ref[...]


def _ssd_prompt(zxbc, dt, batch, conv_w, conv_b, dt_bias, a_log, d_skip, gate_norm):
    M = zxbc.shape[0]
    L = M // batch
    H = dt_bias.shape[0]
    d_inner = H * SSM_HEAD_DIM
    bc = SSM_GROUPS * SSM_STATE
    conv_dim = d_inner + 2 * bc
    Q = SSD_CHUNK
    nc = L // Q
    assert L % Q == 0 and d_inner % bc == 0
    xb = d_inner // bc
    row_map = lambda col: (lambda b, c: (b * nc + c, col))
    const = lambda b, c: (0, 0)
    d_exp = jnp.repeat(d_skip.astype(F32), SSM_HEAD_DIM).reshape(1, d_inner)
    assert H <= LANES and SSM_STATE == LANES and 2 * SSM_HEAD_DIM == LANES
    lane_pad = lambda v: jnp.pad(v.reshape(1, H), ((0, 0), (0, LANES - H)))
    n_slabs = conv_dim // LANES
    cw_slabs = conv_w.reshape(CONV_W, n_slabs, LANES).transpose(1, 0, 2)
    cb_slabs = conv_b.reshape(n_slabs, 1, LANES)
    const3 = lambda b, c: (0, 0, 0)
    y, h_final = pl.pallas_call(
        _ssd_kernel,
        grid=(batch, nc),
        in_specs=[pl.BlockSpec((Q, d_inner), row_map(0)),
                  pl.BlockSpec((Q, d_inner), row_map(1)),
                  pl.BlockSpec((Q, bc), row_map(2 * xb)),
                  pl.BlockSpec((Q, bc), row_map(2 * xb + 1)),
                  pl.BlockSpec((Q, LANES), row_map(0)),
                  pl.BlockSpec((n_slabs, CONV_W, LANES), const3),
                  pl.BlockSpec((n_slabs, 1, LANES), const3),
                  pl.BlockSpec((1, LANES), const),
                  pl.BlockSpec((1, LANES), const),
                  pl.BlockSpec((1, d_inner), const),
                  pl.BlockSpec((1, d_inner), const)],
        out_specs=[pl.BlockSpec((Q, d_inner), row_map(0)),
                   pl.BlockSpec((1, H, SSM_HEAD_DIM, SSM_STATE), lambda b, c: (b, 0, 0, 0))],
        out_shape=[jax.ShapeDtypeStruct((M, d_inner), BF16),
                   jax.ShapeDtypeStruct((batch, H, SSM_HEAD_DIM, SSM_STATE), F32)],
        scratch_shapes=[pltpu.VMEM((H, SSM_HEAD_DIM, SSM_STATE), F32),
                        pltpu.VMEM((n_slabs, Q + CONV_TAIL, LANES), F32),
                        pltpu.VMEM((n_slabs, Q, LANES), F32)],
        compiler_params=_params("parallel", "arbitrary"),
        name="ssd_prompt",
    )(zxbc, zxbc, zxbc, zxbc, dt, cw_slabs, cb_slabs,
      lane_pad(dt_bias), lane_pad(a_log), d_exp, gate_norm.reshape(1, d_inner))
    return y, h_final


def _ssd_step_kernel(zxbc_ref, dt_ref, cs_ref, h0_ref, cw_ref, cb_ref, dtb_ref, alog_ref,
                     dexp_ref, gn_ref, y_ref, cso_ref, ho_ref):
    P = SSM_HEAD_DIM
    N = SSM_STATE
    n_heads = h0_ref.shape[2]
    d_inner = n_heads * P
    hpg = n_heads // SSM_GROUPS
    gw = hpg * P
    conv_dim = cw_ref.shape[1]

    raw = zxbc_ref[0, :, d_inner:d_inner + conv_dim]
    prev = cs_ref[0, 0]
    acc = cb_ref[...]
    for k in range(CONV_W - 1):
        acc = acc + prev[k:k + 1, :] * cw_ref[k:k + 1, :]
    acc = acc + raw * cw_ref[CONV_W - 1:CONV_W, :]
    xbc = _silu(acc)
    cso_ref[0, 0, 0:CONV_W - 2, :] = prev[1:CONV_W - 1, :]
    cso_ref[0, 0, CONV_W - 2:CONV_W - 1, :] = raw

    dt = _softplus(dt_ref[0, :, 0:n_heads] + dtb_ref[...])
    decay = jnp.exp(dt * (-jnp.exp(alog_ref[...])))
    z = zxbc_ref[0, :, 0:d_inner]

    eye = (lax.broadcasted_iota(jnp.int32, (P, P), 0) == lax.broadcasted_iota(jnp.int32, (P, P), 1))
    ys = []
    for g in range(SSM_GROUPS):
        Bg = xbc[:, d_inner + g * N:d_inner + (g + 1) * N]
        Cg = xbc[:, d_inner + SSM_GROUPS * N + g * N:d_inner + SSM_GROUPS * N + (g + 1) * N]
        Bb = jnp.broadcast_to(Bg, (P, N))
        for r in range(hpg):
            hh = g * hpg + r
            xs = xbc[:, hh * P:(hh + 1) * P]
            xdt = xs * dt[:, hh:hh + 1]
            xdiag = jnp.where(eye, jnp.broadcast_to(xdt, (P, P)), 0.0)
            outer = jnp.dot(xdiag, Bb, precision=HIGHEST, preferred_element_type=F32)
            ho_ref[0, 0, hh] = h0_ref[0, 0, hh] * decay[:, hh:hh + 1] + outer
        Hg = ho_ref[0, 0, g * hpg:(g + 1) * hpg].reshape(gw, N)
        yg = lax.dot_general(Cg.astype(BF16), Hg.astype(BF16), NT_DIMS, preferred_element_type=F32)
        lo, hi = g * gw, (g + 1) * gw
        yg = yg + xbc[:, lo:hi] * dexp_ref[:, lo:hi]
        yg = yg * _silu(z[:, lo:hi])
        yg = yg * lax.rsqrt(jnp.mean(yg * yg, axis=-1, keepdims=True) + EPS)
        ys.append(yg * gn_ref[:, lo:hi])
    y_ref[0] = jnp.concatenate(ys, axis=1).astype(y_ref.dtype)


def _ssd_step(zxbc, dt, conv_state, ssm_state, conv_w, conv_b, dt_bias, a_log, d_skip, gate_norm):
    B = zxbc.shape[0]
    H = dt_bias.shape[0]
    d_inner = H * SSM_HEAD_DIM
    conv_dim = conv_w.shape[1]
    const = lambda b: (0, 0)
    d_exp = jnp.repeat(d_skip.astype(F32), SSM_HEAD_DIM).reshape(1, d_inner)
    cs4 = conv_state.reshape(1, B, CONV_W - 1, conv_dim)
    h5 = ssm_state.reshape(1, B, H, SSM_HEAD_DIM, SSM_STATE)
    return pl.pallas_call(
        _ssd_step_kernel,
        grid=(B,),
        in_specs=[pl.BlockSpec((1, 1, zxbc.shape[1]), lambda b: (b, 0, 0)),
                  pl.BlockSpec((1, 1, dt.shape[1]), lambda b: (b, 0, 0)),
                  pl.BlockSpec((1, 1, CONV_W - 1, conv_dim), lambda b: (0, b, 0, 0)),
                  pl.BlockSpec((1, 1, H, SSM_HEAD_DIM, SSM_STATE), lambda b: (0, b, 0, 0, 0)),
                  pl.BlockSpec((CONV_W, conv_dim), const),
                  pl.BlockSpec((1, conv_dim), const),
                  pl.BlockSpec((1, H), const),
                  pl.BlockSpec((1, H), const),
                  pl.BlockSpec((1, d_inner), const),
                  pl.BlockSpec((1, d_inner), const)],
        out_specs=[pl.BlockSpec((1, 1, d_inner), lambda b: (b, 0, 0)),
                   pl.BlockSpec((1, 1, CONV_W - 1, conv_dim), lambda b: (0, b, 0, 0)),
                   pl.BlockSpec((1, 1, H, SSM_HEAD_DIM, SSM_STATE), lambda b: (0, b, 0, 0, 0))],
        out_shape=[jax.ShapeDtypeStruct((B, 1, d_inner), F32),
                   jax.ShapeDtypeStruct(cs4.shape, F32),
                   jax.ShapeDtypeStruct(h5.shape, F32)],
        compiler_params=_params("arbitrary"),
        name="ssd_step",
    )(zxbc.reshape(B, 1, -1), dt.reshape(B, 1, -1), cs4, h5, conv_w, conv_b.reshape(1, conv_dim),
      dt_bias.reshape(1, H), a_log.reshape(1, H), d_exp, gate_norm.reshape(1, d_inner))


def _attn_kernel(q0_ref, q1_ref, q2_ref, k_ref, v_ref, o_ref, on_ref, lse_ref, qs_ref, ks_ref, vs_ref):
    L = k_ref.shape[0]
    QB = ATT_BLOCK
    KW = QB + DIL_SLOTS
    P = ATT_PRESTRIDE
    p_shift = P.bit_length() - 1
    scale = HEAD_DIM ** -0.5
    e = lax.broadcasted_iota(jnp.int32, (QB, KW), 0) - lax.broadcasted_iota(jnp.int32, (QB, KW), 1)

    staged = {rate: rate > P for rate in DIL_RATES}
    for src_ref, dst_ref, needed in ((q2_ref, qs_ref, staged[DIL_RATES[2]]), (k_ref, ks_ref, any(staged.values())),
                                     (v_ref, vs_ref, any(staged.values()))):
        if needed:
            def stage(i, carry, src_ref=src_ref, dst_ref=dst_ref):
                cp = i & (P - 1)
                t = i >> p_shift
                dst = pl.multiple_of(cp * (L // P) + t * QB, QB)
                dst_ref[pl.ds(dst, QB), :] = src_ref[pl.ds(cp + P * QB * t, QB, stride=P), :]
                return carry
            lax.fori_loop(0, L // QB, stage, 0, unroll=4)

    for g, (q_ref, rate) in enumerate(zip((q0_ref, q1_ref, q2_ref), DIL_RATES)):
        shift = rate.bit_length() - 1

        def unit(n, carry, g=g, q_ref=q_ref, rate=rate, shift=shift):
            c = n & (rate - 1)
            u0 = (n >> shift) * QB
            v0 = jnp.maximum(u0 - DIL_SLOTS, 0)
            q_rows = pl.ds(c + rate * u0, QB, stride=rate)
            if staged[rate]:
                base = (c & (P - 1)) * (L // P) + (c >> p_shift)
                q = qs_ref[pl.ds(base + (rate // P) * u0, QB, stride=rate // P), :].astype(BF16)
                k_rows = pl.ds(base + (rate // P) * v0, KW, stride=rate // P)
                k = ks_ref[k_rows, :].astype(BF16)
                v = vs_ref[k_rows, :].astype(BF16)
            else:
                k_rows = pl.ds(c + rate * v0, KW, stride=rate)
                q = q_ref[q_rows, :].astype(BF16)
                k = k_ref[k_rows, :].astype(BF16)
                v = v_ref[k_rows, :].astype(BF16)
            s = lax.dot_general(q, k, NT_DIMS, preferred_element_type=F32) * scale
            d = e + (u0 - v0)
            s = jnp.where(d >= 0, s, -jnp.inf)
            s = jnp.where(d <= DIL_SLOTS, s, -jnp.inf)
            m = s.max(axis=1, keepdims=True)
            p = jnp.exp(s - m)
            l = p.sum(axis=1, keepdims=True)
            o = jnp.dot(p.astype(BF16), v, preferred_element_type=F32)
            on_ref[g, q_rows, :] = o / l
            lse_ref[g, q_rows, :] = jnp.broadcast_to(m + jnp.log(l), (QB, HEAD_DIM))
            return carry

        lax.fori_loop(0, L // QB, unit, 0, unroll=ATT_UNROLL)

    def mix(i, carry):
        rows = pl.ds(pl.multiple_of(i * QB, QB), QB)
        lses = [lse_ref[g, rows, :] for g in range(len(DIL_RATES))]
        m = functools.reduce(jnp.maximum, lses)
        ws = [jnp.exp(x - m) for x in lses]
        num = functools.reduce(jnp.add, [w * on_ref[g, rows, :] for g, w in enumerate(ws)])
        o_ref[rows, :] = (num / functools.reduce(jnp.add, ws)).astype(o_ref.dtype)
        return carry

    lax.fori_loop(0, L // QB, mix, 0, unroll=2)


def _attn_prompt(q, kv, batch):
    M = q.shape[0]
    L = M // batch
    n_grp = len(DIL_RATES)
    assert all(r & (r - 1) == 0 and L % (ATT_BLOCK * r) == 0 and L >= r * (ATT_BLOCK + DIL_SLOTS)
               for r in DIL_RATES)
    qspec = lambda g: pl.BlockSpec((L, HEAD_DIM), lambda b, h: (b, g * KV_HEADS + h))
    kspec = pl.BlockSpec((L, HEAD_DIM), lambda b, h: (b, h))
    vspec = pl.BlockSpec((L, HEAD_DIM), lambda b, h: (b, KV_HEADS + h))
    return pl.pallas_call(
        _attn_kernel,
        grid=(batch, KV_HEADS),
        in_specs=[qspec(0), qspec(1), qspec(2), kspec, vspec],
        out_specs=pl.BlockSpec((L, HEAD_DIM), lambda b, h: (b, h)),
        out_shape=jax.ShapeDtypeStruct((M, KV_HEADS * HEAD_DIM), BF16),
        scratch_shapes=[pltpu.VMEM((n_grp, L, HEAD_DIM), F32), pltpu.VMEM((n_grp, L, HEAD_DIM), F32)]
        + [pltpu.VMEM((L, HEAD_DIM), F32)] * 3,
        compiler_params=_params("parallel", "arbitrary"),
        name="attn_prompt",
    )(q, q, q, kv, kv)


def _bf16_round(a):
    return a.astype(BF16).astype(F32)


def _attn_step_kernel(q_ref, *refs):
    n = len(DIL_RATES)
    k_refs, v_refs = refs[0:n], refs[n:2 * n]
    kn_ref, vn_ref, o_ref = refs[2 * n:]
    scale = HEAD_DIM ** -0.5
    kn = _bf16_round(kn_ref[0, 0])
    vn = _bf16_round(vn_ref[0, 0])
    scores, new_scores = [], []
    for g, k_ref in enumerate(k_refs):
        qg = _bf16_round(q_ref[0, g])
        kg = _bf16_round(k_ref[0, :, 0])
        scores.append(jnp.sum(kg * qg[None], axis=-1, keepdims=True) * scale)
        new_scores.append(jnp.sum(kn * qg, axis=-1, keepdims=True) * scale)
    m = functools.reduce(jnp.maximum, [s.max(axis=0) for s in scores] + new_scores)
    l = jnp.zeros_like(m)
    o = jnp.zeros((KV_HEADS, HEAD_DIM), F32)
    for s, s_new, v_ref in zip(scores, new_scores, v_refs):
        p = jnp.exp(s - m[None])
        p_new = jnp.exp(s_new - m)
        l = l + p.sum(axis=0) + p_new
        o = o + (_bf16_round(p) * _bf16_round(v_ref[0, :, 0])).sum(axis=0) + _bf16_round(p_new) * vn
    o_ref[0, 0] = (o / l).astype(o_ref.dtype)


def _attn_step(q, cache_k, cache_v, k_new, v_new):
    B, T = cache_k.shape[0], cache_k.shape[1]
    S = DIL_SLOTS
    assert T == S * max(DIL_RATES), "every slot of every dilation group lies inside the cached window"
    qg = q.reshape(B, len(DIL_RATES), KV_HEADS, HEAD_DIM)
    row_shape = (B, 1, KV_HEADS, HEAD_DIM)
    row_spec = pl.BlockSpec((1, 1, KV_HEADS, HEAD_DIM), lambda b: (b, 0, 0, 0))
    args = [qg]
    in_specs = [pl.BlockSpec((1, len(DIL_RATES), KV_HEADS, HEAD_DIM), lambda b: (b, 0, 0, 0))]
    for cache in (cache_k, cache_v):
        for rate in DIL_RATES:
            args.append(cache.reshape(B, T // rate, rate, KV_HEADS, HEAD_DIM))
            in_specs.append(pl.BlockSpec((1, S, 1, KV_HEADS, HEAD_DIM),
                                         lambda b, blk=T // rate // S - 1: (b, blk, 0, 0, 0)))
    args += [k_new.reshape(row_shape), v_new.reshape(row_shape)]
    in_specs += [row_spec, row_spec]
    o = pl.pallas_call(
        _attn_step_kernel,
        grid=(B,),
        in_specs=in_specs,
        out_specs=row_spec,
        out_shape=jax.ShapeDtypeStruct(row_shape, F32),
        compiler_params=_params("parallel"),
        name="attn_step",
    )(*args)
    return o.reshape(B, KV_HEADS * HEAD_DIM)


def _rope_tables(pos):
    half = HEAD_DIM // 2
    inv = jnp.power(jnp.float32(ROPE_THETA), -jnp.arange(half, dtype=jnp.float32) / half)
    ang = pos.astype(jnp.float32)[:, None] * inv[None, :]
    cos = jnp.cos(ang)
    sin = jnp.sin(ang)
    return jnp.concatenate([cos, cos], axis=1), jnp.concatenate([-sin, sin], axis=1)


def kernel(x_prompt, x_sample, state_conv, state_ssm, cache_k, cache_v, a_norm, a_w_in, a_conv_w, a_conv_b,
           a_dt_bias, a_log, a_d, a_gate_norm, a_w_out, kv_norm, w_kv, b_norm, b_w_q, b_w_o, ffn_norm,
           ffn_w_gu, ffn_w_down, final_norm):
    assert a_norm.shape[0] == 1 and b_norm.shape[0] == 1, "one Mamba-2 layer followed by one attention layer"
    Bp, Lp, D = x_prompt.shape
    Bs, Ls, _ = x_sample.shape
    assert Ls == 1 and Bs == SAMPLE_ROWS, "sample group decodes one token for SAMPLE_ROWS sequences"
    n_heads = a_dt_bias.shape[1]
    d_inner = a_w_out.shape[1]
    conv_dim = a_conv_w.shape[2]
    kv_dim = KV_HEADS * HEAD_DIM
    w_dt = jnp.pad(a_w_in[0][:, d_inner + conv_dim:], ((0, 0), (0, LANES - n_heads))).astype(BF16)
    ssd_w = (a_conv_w[0], a_conv_b[0], a_dt_bias[0], a_log[0], a_d[0], a_gate_norm[0])
    w_kv = w_kv[None]

    cos_p, sin_p = _rope_tables(jnp.arange(Lp, dtype=jnp.int32))
    cos_s, sin_s = _rope_tables(PAST_LEN + jnp.arange(Ls, dtype=jnp.int32))
    rope = ((cos_p, sin_p, Lp), (jnp.broadcast_to(cos_s, (Bs, HEAD_DIM)), jnp.broadcast_to(sin_s, (Bs, HEAD_DIM))))

    xp0 = x_prompt.reshape(Bp * Lp, D)
    xs0 = x_sample.reshape(Bs, D)

    zxbc_p, zxbc_s, dt_p, dt_s, w_out_bf, w_gu0_bf, w_down0_bf = _proj(
        xp0, xs0, [_seg(a_w_in.astype(BF16), n_cols=d_inner + conv_dim)], gains=[a_norm[0]], aux_w=w_dt, tn=1024,
        casts=[(a_w_out, 0), (ffn_w_gu, 0), (ffn_w_down, 0)])
    y_p, p_ssm = _ssd_prompt(zxbc_p, dt_p, Bp, *ssd_w)
    p_conv = zxbc_p.reshape(Bp, Lp, -1)[:, Lp - (CONV_W - 1):, d_inner:]
    y_s, s_conv, s_ssm = _ssd_step(zxbc_s, dt_s, state_conv[0], state_ssm[0], *ssd_w)
    *x1, w_gu1_bf = _proj(y_p, y_s.reshape(Bs, d_inner), [_seg(w_out_bf)], res=(xp0, xs0), casts=[(ffn_w_gu, 1)],
                          tm=256, tn=D)
    *h, w_kv_bf, w_q_bf, w_o_bf = _gateup(*x1, ffn_norm[0], w_gu0_bf, 0, casts=[(w_kv, 0), (b_w_q, 0), (b_w_o, 0)])
    x2 = _proj(*h, [_seg(w_down0_bf)], res=x1, tm=256, tn=D)

    kv_p, kv_s, q_p, q_s = _proj(*x2, [_seg(w_kv_bf, rope_cols=kv_dim), _seg(w_q_bf, rope_cols=b_w_q.shape[2])],
                                 gains=[kv_norm, b_norm[0]], rope=rope, tm=256, tn=None)
    o_p = _attn_prompt(q_p, kv_p, Bp)
    o_s = _attn_step(q_s, cache_k, cache_v, kv_s[:, :kv_dim], kv_s[:, kv_dim:])
    x3 = _proj(o_p, o_s, [_seg(w_o_bf)], res=x2, tm=512, tn=D)
    *h, w_down1_bf = _gateup(*x3, ffn_norm[1], w_gu1_bf, 0, casts=[(ffn_w_down, 1)])
    y_prompt, y_sample = _proj(*h, [_seg(w_down1_bf)], res=x3, post_gain=final_norm, tm=256, tn=D)
    y_prompt = y_prompt.reshape(Bp, Lp, D)
    y_sample = y_sample.reshape(Bs, Ls, D)
    keep = min(DIL_SLOTS * max(DIL_RATES), Lp)
    p_kv = kv_p.reshape(Bp, Lp, 2 * KV_HEADS, HEAD_DIM)[:, Lp - keep:]
    s_kv = kv_s.reshape(Bs, Ls, 2 * KV_HEADS, HEAD_DIM)
    return (y_prompt, y_sample, p_conv[None], p_ssm[None], p_kv[:, :, :KV_HEADS], p_kv[:, :, KV_HEADS:],
            s_conv, s_ssm, s_kv[:, :, :KV_HEADS], s_kv[:, :, KV_HEADS:])
```

```python
import functools

import jax
import jax.numpy as jnp
from jax import lax
from jax.experimental import pallas as pl
from jax.experimental.pallas import tpu as pltpu

F32 = jnp.float32
BF16 = jnp.bfloat16
HIGHEST = lax.Precision.HIGHEST

EPS = 1e-6
ROPE_THETA = 10000.0
SSD_CHUNK = 128
SSM_HEAD_DIM = 64
SSM_STATE = 128
SSM_GROUPS = 8
CONV_W = 4
HEAD_DIM = 128
KV_HEADS = 8
DIL_RATES = (1, 4, 16)
DIL_SLOTS = 128
ATT_BLOCK = 128
ATT_UNROLL = 32
ATT_PRESTRIDE = 4
PAST_LEN = 16384

V7X_VMEM_BYTES = 64 * 1024 * 1024
VMEM_LIMIT = V7X_VMEM_BYTES - 8 * 1024 * 1024
LANES = 128
CONV_TAIL = 8
SAMPLE_ROWS = 8

NT_DIMS = (((1,), (1,)), ((), ()))
TN_DIMS = (((0,), (0,)), ((), ()))


def _params(*sem):
    return pltpu.CompilerParams(dimension_semantics=sem, vmem_limit_bytes=VMEM_LIMIT)


def _silu(x):
    h = 0.5 * x
    return h + h * jnp.tanh(h)


def _softplus(x):
    return jnp.maximum(x, 0.0) + jnp.log1p(jnp.exp(-jnp.abs(x)))


def _rms_scale(x):
    return lax.rsqrt(jnp.mean(x * x, axis=-1, keepdims=True) + EPS)


def _normed(x_ref, g_ref):
    x = x_ref[...]
    return ((x * _rms_scale(x)) * g_ref[...]).astype(BF16)


def _sample_block(i, j):
    return (i, j)


def _cast_specs(casts, n_steps, nj):
    BF16_ROWS = 16
    args, in_specs, out_shape, out_specs, tiles = [], [], [], [], []
    for src, layer in casts:
        R, C = src.shape[1:]
        rows = next(r for r in range(BF16_ROWS, R + 1, BF16_ROWS) if R % r == 0 and R // r <= n_steps)
        nt = R // rows
        tile = lambda i, j, nt=nt: jnp.minimum(i * nj + j, nt - 1)
        args.append(src)
        in_specs.append(pl.BlockSpec((None, rows, C), lambda i, j, layer=layer, tile=tile: (layer, tile(i, j), 0)))
        out_shape.append(jax.ShapeDtypeStruct((1, R, C), BF16))
        out_specs.append(pl.BlockSpec((None, rows, C), lambda i, j, tile=tile: (0, tile(i, j), 0)))
        tiles.append(nt)
    return args, in_specs, out_shape, out_specs, tuple(tiles)


def _run_casts(src_refs, dst_refs, tiles):
    step = pl.program_id(0) * pl.num_programs(1) + pl.program_id(1)
    for src_ref, dst_ref, nt in zip(src_refs, dst_refs, tiles):
        @pl.when(step < nt)
        def _(src_ref=src_ref, dst_ref=dst_ref):
            dst_ref[...] = src_ref[...].astype(BF16)


def _proj_kernel(*refs, segs, resident, norm, has_rope, has_res, has_post, has_aux, cast_tiles):
    it = iter(refs)
    xp_ref, xs_ref = next(it), next(it)
    g_refs = [next(it) for _ in segs] if norm else None
    w_refs = [next(it) for _ in segs]
    auxw_ref = next(it) if has_aux else None
    rope_p = (next(it), next(it)) if has_rope else None
    rope_s = (next(it), next(it)) if has_rope else None
    resp_ref, ress_ref = (next(it), next(it)) if has_res else (None, None)
    pg_ref = next(it) if has_post else None
    cast_srcs = [next(it) for _ in cast_tiles]
    out_refs =[(next(it), next(it)) for _ in segs]
    auxp_ref, auxs_ref = (next(it), next(it)) if has_aux else (None, None)
    cast_dsts = [next(it) for _ in cast_tiles]
    xnp_ref, xns_ref = (next(it), next(it)) if norm else (None, None)
    i = pl.program_id(0)
    j = pl.program_id(1)
    _run_casts(cast_srcs, cast_dsts, cast_tiles)

    if norm:
        @pl.when(j == 0)
        def _():
            for s, g_ref in enumerate(g_refs):
                xnp_ref[s] = _normed(xp_ref, g_ref)

        @pl.when((i == 0) & (j == 0))
        def _():
            for s, g_ref in enumerate(g_refs):
                xns_ref[s] = _normed(xs_ref, g_ref)

    def lhs_p(s):
        return xnp_ref[s] if norm else xp_ref[...]

    def lhs_s(s):
        return xns_ref[s] if norm else xs_ref[...].astype(BF16)

    def emit(lhs, w_ref, res_ref, o_ref, rope, lo, n_tiles, n_rope):
        acc = jnp.dot(lhs, w_ref[...].astype(BF16), preferred_element_type=F32)
        if has_res:
            acc = acc + res_ref[...]
        if has_post:
            acc = (acc * _rms_scale(acc)) * pg_ref[...]
        heads = acc.shape[1] // HEAD_DIM

        def store(rope_heads):
            if rope_heads:
                cos = rope[0][...]
                sin = rope[1][...]
            for h in range(rope_heads):
                a = acc[:, h * HEAD_DIM:(h + 1) * HEAD_DIM]
                o_ref[:, h * HEAD_DIM:(h + 1) * HEAD_DIM] = (
                    a * cos + pltpu.roll(a, HEAD_DIM // 2, 1) * sin).astype(o_ref.dtype)
            if rope_heads < heads:
                o_ref[:, rope_heads * HEAD_DIM:] = acc[:, rope_heads * HEAD_DIM:].astype(o_ref.dtype)

        if resident:
            store(n_rope)
        elif n_rope in (0, n_tiles):
            store(heads if n_rope else 0)
        else:
            pl.when(j < lo + n_rope)(lambda: store(heads))
            pl.when(j >= lo + n_rope)(lambda: store(0))

    for s, (lo, n_tiles, n_rope) in enumerate(segs):
        def segment(s=s, lo=lo, n_tiles=n_tiles, n_rope=n_rope):
            emit(lhs_p(s), w_refs[s], resp_ref, out_refs[s][0], rope_p, lo, n_tiles, n_rope)

            @pl.when(i == 0)
            def _():
                emit(lhs_s(s), w_refs[s], ress_ref, out_refs[s][1], rope_s, lo, n_tiles, n_rope)

        if resident or len(segs) == 1:
            segment()
        else:
            pl.when((j >= lo) & (j < lo + n_tiles))(segment)

    @pl.when(i > 0)
    def _():
        for _, os_ref in out_refs:
            os_ref[...] = jnp.zeros_like(os_ref)

    if has_aux:
        @pl.when(j == pl.num_programs(1) - 1)
        def _():
            auxp_ref[...] = jnp.dot(lhs_p(0), auxw_ref[...], preferred_element_type=F32)

            @pl.when(i == 0)
            def _():
                auxs_ref[...] = jnp.dot(lhs_s(0), auxw_ref[...], preferred_element_type=F32)

            @pl.when(i > 0)
            def _():
                auxs_ref[...] = jnp.zeros_like(auxs_ref)


def _seg(w, layer=0, col0=0, n_cols=None, rope_cols=0):
    return (w, layer, col0, w.shape[-1] - col0 if n_cols is None else n_cols, rope_cols)


def _proj(xp, xs, ws, *, gains=None, aux_w=None, rope=None, res=None, post_gain=None, casts=(), out_dtype=F32,
          tm=1024, tn=512):
    Mp, K = xp.shape
    S = SAMPLE_ROWS
    assert xs.shape == (S, K) and Mp % tm == 0
    norm = gains is not None
    resident = tn is None
    segs, widths, lo = [], [], 0
    for w, layer, col0, n_cols, rope_cols in ws:
        tw = n_cols if resident else tn
        rope_unit = HEAD_DIM if resident else tw
        assert w.ndim == 3 and n_cols % tw == 0 and col0 % tw == 0 and rope_cols % rope_unit == 0
        segs.append((lo, n_cols // tw, rope_cols // rope_unit))
        widths.append(tw)
        lo += 0 if resident else n_cols // tw
    nj, ni = (1 if resident else lo), Mp // tm
    has_rope = any(r for _, _, r in segs)
    assert not (has_rope and res is not None) and (rope is not None) == has_rope
    row = lambda i, j: (i, 0)
    const = lambda i, j: (0, 0)
    args = [xp, xs]
    in_specs = [pl.BlockSpec((tm, K), row), pl.BlockSpec((S, K), const)]
    if norm:
        assert len(gains) == len(ws)
        args += [g.reshape(1, K) for g in gains]
        in_specs += [pl.BlockSpec((1, K), const)] * len(gains)
    for (w, layer, col0, _, _), (lo, n_tiles, _), tw in zip(ws, segs, widths):
        args.append(w)
        in_specs.append(pl.BlockSpec(
            (None, K, tw), lambda i, j, layer=layer, lo=lo, n=n_tiles, c0=col0 // tw:
            (layer, 0, c0 + jnp.clip(j - lo, 0, n - 1)),
            pipeline_mode=pl.Buffered(1) if nj == 1 else None))
    if aux_w is not None:
        args.append(aux_w)
        in_specs.append(pl.BlockSpec(aux_w.shape, const))
    if has_rope:
        (cos_p, sin_p, rows_per_seq), (cos_s, sin_s) = rope
        nseq = rows_per_seq // tm
        args += [cos_p, sin_p, cos_s, sin_s]
        in_specs += [pl.BlockSpec((tm, HEAD_DIM), lambda i, j: (i % nseq, 0))] * 2
        in_specs += [pl.BlockSpec((S, HEAD_DIM), const)] * 2
    assert (res is None and post_gain is None) or len(ws) == 1
    if res is not None:
        args += list(res)
        in_specs += [pl.BlockSpec((tm, widths[0]), lambda i, j: (i, j)),
                     pl.BlockSpec((S, widths[0]), lambda i, j: (0, j))]
    if post_gain is not None:
        assert nj == 1
        args.append(post_gain.reshape(1, widths[0]))
        in_specs.append(pl.BlockSpec((1, widths[0]), const))
    out_shape, out_specs = [], []
    for (lo, n_tiles, _), tw in zip(segs, widths):
        col = lambda i, j, lo=lo, n=n_tiles: (i, jnp.clip(j - lo, 0, n - 1))
        out_shape += [jax.ShapeDtypeStruct((Mp, n_tiles * tw), out_dtype),
                      jax.ShapeDtypeStruct((ni * S, n_tiles * tw), F32)]
        out_specs += [pl.BlockSpec((tm, tw), col), pl.BlockSpec((S, tw), col)]
    if aux_w is not None:
        na = aux_w.shape[1]
        out_shape += [jax.ShapeDtypeStruct((Mp, na), F32), jax.ShapeDtypeStruct((ni * S, na), F32)]
        out_specs += [pl.BlockSpec((tm, na), row), pl.BlockSpec((S, na), row)]
    n_main = len(out_shape)
    c_args, c_in, c_shape, c_out, cast_tiles = _cast_specs(casts, ni * nj, nj)
    args += c_args
    in_specs += c_in
    out_shape += c_shape
    out_specs += c_out
    outs = pl.pallas_call(
        functools.partial(_proj_kernel, segs=tuple(segs), resident=resident, norm=norm, has_rope=has_rope,
                          has_res=res is not None, has_post=post_gain is not None, has_aux=aux_w is not None,
                          cast_tiles=cast_tiles),
        grid=(ni, nj),
        in_specs=in_specs,
        out_specs=out_specs,
        out_shape=out_shape,
        scratch_shapes=[pltpu.VMEM((len(ws), tm, K), BF16), pltpu.VMEM((len(ws), S, K), BF16)] if norm else [],
        compiler_params=_params("arbitrary", "arbitrary"),
        name="proj",
    )(*args)
    return [o[:S] if n < n_main and n % 2 == 1 else o for n, o in enumerate(outs)]


def _gateup_kernel(xp_ref, xs_ref, g_ref, wg_ref, wu_ref, *refs, cast_tiles):
    n = len(cast_tiles)
    cast_srcs = refs[:n]
    op_ref, os_ref = refs[n:n + 2]
    cast_dsts = refs[n + 2:2 * n + 2]
    xnp_ref, xns_ref = refs[2 * n + 2:]
    i = pl.program_id(0)
    j = pl.program_id(1)
    _run_casts(cast_srcs, cast_dsts, cast_tiles)

    @pl.when(j == 0)
    def _():
        xnp_ref[...] = _normed(xp_ref, g_ref)

    @pl.when((i == 0) & (j == 0))
    def _():
        xns_ref[...] = _normed(xs_ref, g_ref)

    def swiglu(xn):
        g = jnp.dot(xn, wg_ref[...], preferred_element_type=F32)
        u = jnp.dot(xn, wu_ref[...], preferred_element_type=F32)
        return _silu(g) * u

    op_ref[...] = swiglu(xnp_ref[...]).astype(op_ref.dtype)

    @pl.when(i == 0)
    def _():
        os_ref[...] = swiglu(xns_ref[...])

    @pl.when(i > 0)
    def _():
        os_ref[...] = jnp.zeros_like(os_ref)


def _gateup(xp, xs, gain, w_gu, layer, *, casts=(), tm=1024, tn=512):
    Mp, K = xp.shape
    S = SAMPLE_ROWS
    hidden = w_gu.shape[2] // 2
    assert xs.shape == (S, K) and Mp % tm == 0 and hidden % tn == 0 and w_gu.dtype == BF16
    nj = hidden // tn
    ni = Mp // tm
    c_args, c_in, c_shape, c_out, cast_tiles = _cast_specs(casts, ni * nj, nj)
    outs = pl.pallas_call(
        functools.partial(_gateup_kernel, cast_tiles=cast_tiles),
        grid=(ni, nj),
        in_specs=[pl.BlockSpec((tm, K), lambda i, j: (i, 0)),
                  pl.BlockSpec((S, K), lambda i, j: (0, 0)),
                  pl.BlockSpec((1, K), lambda i, j: (0, 0)),
                  pl.BlockSpec((None, K, tn), lambda i, j: (layer, 0, j)),
                  pl.BlockSpec((None, K, tn), lambda i, j: (layer, 0, j + nj))] + c_in,
        out_specs=[pl.BlockSpec((tm, tn), lambda i, j: (i, j)), pl.BlockSpec((S, tn), _sample_block)] + c_out,
        out_shape=[jax.ShapeDtypeStruct((Mp, hidden), BF16), jax.ShapeDtypeStruct((ni * S, hidden), F32)] + c_shape,
        scratch_shapes=[pltpu.VMEM((tm, K), BF16), pltpu.VMEM((S, K), BF16)],
        compiler_params=_params("arbitrary", "arbitrary"),
        name="gateup",
    )(xp, xs, gain.reshape(1, K), w_gu, w_gu, *c_args)
    return [outs[0], outs[1][:S]] + list(outs[2:])


def _conv_silu_slab(src, s, raw_ref, act_ref, cw_ref, cb_ref):
    Q = SSD_CHUNK
    raw_ref[s, CONV_TAIL:, :] = src
    for parity in (0, 1):
        acc = cb_ref[s]
        for k in range(CONV_W):
            first = CONV_TAIL - (CONV_W - 1) + k + parity
            acc = acc + raw_ref[s, pl.ds(first, Q // 2, stride=2), :] * cw_ref[s, k:k + 1, :]
        act_ref[s, pl.ds(parity, Q // 2, stride=2), :] = _silu(acc)
    raw_ref[s, 0:CONV_TAIL, :] = raw_ref[s, Q:Q + CONV_TAIL, :]


def _ssd_kernel(z_ref, x_ref, b_ref, c_ref, dt_ref, cw_ref, cb_ref,
                dtb_ref, alog_ref, dexp_ref, gn_ref,
                y_ref, hout_ref,
                h_ref, raw_ref, act_ref):
    Q = SSD_CHUNK
    P = SSM_HEAD_DIM
    N = SSM_STATE
    d_inner = x_ref.shape[1]
    n_heads = d_inner // P
    hpg = n_heads // SSM_GROUPS
    gw = hpg * P
    n_xs = d_inner // LANES
    n_bs = b_ref.shape[1] // LANES
    c = pl.program_id(1)

    @pl.when(c == 0)
    def _():
        h_ref[...] = jnp.zeros_like(h_ref)
        raw_ref[:, 0:CONV_TAIL, :] = jnp.zeros((raw_ref.shape[0], CONV_TAIL, LANES), F32)

    for s in range(n_xs + 2 * n_bs):
        if s < n_xs:
            src = x_ref[:, s * LANES:(s + 1) * LANES]
        elif s < n_xs + n_bs:
            src = b_ref[:, (s - n_xs) * LANES:(s - n_xs + 1) * LANES]
        else:
            src = c_ref[:, (s - n_xs - n_bs) * LANES:(s - n_xs - n_bs + 1) * LANES]
        _conv_silu_slab(src, s, raw_ref, act_ref, cw_ref, cb_ref)

    dt = _softplus(dt_ref[...] + dtb_ref[...])
    dA = dt * (-jnp.exp(alog_ref[...]))
    row = lax.broadcasted_iota(jnp.int32, (Q, Q), 0)
    col = lax.broadcasted_iota(jnp.int32, (Q, Q), 1)
    causal = row >= col
    tri = jnp.where(causal, 1.0, 0.0).astype(F32)
    cs = jnp.dot(tri, dA, precision=HIGHEST, preferred_element_type=F32)
    csT = cs.T
    dtT = dt.T
    dt_decay_end = dt * jnp.exp(cs[Q - 1:Q, :] - cs)
    chunk_decay = jnp.broadcast_to(jnp.exp(csT[0:n_heads, Q - 1:Q]), (n_heads, N))

    lane_lo = lax.broadcasted_iota(jnp.int32, (Q, 2 * P), 1) < P

    def pair_cols(arr, q):
        a0 = jnp.broadcast_to(arr[:, 2 * q:2 * q + 1], (Q, 2 * P))
        a1 = jnp.broadcast_to(arr[:, 2 * q + 1:2 * q + 2], (Q, 2 * P))
        return jnp.where(lane_lo, a0, a1)

    for g in range(SSM_GROUPS):
        Bg = act_ref[n_xs + g].astype(BF16)
        Cg = act_ref[n_xs + n_bs + g].astype(BF16)
        CB = lax.dot_general(Cg, Bg, NT_DIMS, preferred_element_type=F32)
        Hg = h_ref[g * hpg:(g + 1) * hpg].reshape(gw, N)
        y_off = lax.dot_general(Cg, Hg.astype(BF16), NT_DIMS, preferred_element_type=F32)

        ys = []
        xds = []
        for qq in range(hpg // 2):
            q = g * (hpg // 2) + qq
            lo, hi = q * 2 * P, (q + 1) * 2 * P
            xs = act_ref[q]
            xds.append((xs * pair_cols(dt_decay_end, q)).astype(BF16))
            cs_cols = [jnp.broadcast_to(cs[:, hh:hh + 1], (Q, Q)) for hh in (2 * q, 2 * q + 1)]
            atts = []
            for hh, cs_col in zip((2 * q, 2 * q + 1), cs_cols):
                seg = cs_col - csT[hh:hh + 1, :]
                decay = jnp.exp(jnp.where(causal, seg, -jnp.inf))
                atts.append(((CB * decay) * dtT[hh:hh + 1, :]).astype(BF16))
            att = jnp.concatenate(atts, axis=1)
            xbd = jnp.concatenate([jnp.where(lane_lo, xs, 0.0).astype(BF16),
                                   jnp.where(lane_lo, 0.0, xs).astype(BF16)], axis=0)
            y = jnp.dot(att, xbd, preferred_element_type=F32)
            y = y + y_off[:, qq * 2 * P:(qq + 1) * 2 * P] * jnp.exp(jnp.where(lane_lo, *cs_cols))
            y = y + xs * dexp_ref[:, lo:hi]
            ys.append(y * _silu(z_ref[:, lo:hi]))

        ssq = ys[0] * ys[0]
        for y in ys[1:]:
            ssq = ssq + y * y
        scale = lax.rsqrt(jnp.sum(ssq, axis=-1, keepdims=True) / gw + EPS)
        for qq, y in enumerate(ys):
            lo = g * gw + qq * 2 * P
            y_ref[:, lo:lo + 2 * P] = ((y * scale) * gn_ref[:, lo:lo + 2 * P]).astype(y_ref.dtype)

        S = lax.dot_general(jnp.concatenate(xds, axis=1), Bg, TN_DIMS, preferred_element_type=F32)
        for r in range(hpg):
            hh = g * hpg + r
            h_ref[hh] = h_ref[hh] * chunk_decay[hh:hh + 1, :] + S[r * P:(r + 1) * P, :]

    @pl.when(c == pl.num_programs(1) - 1)
    def _():
        hout_ref[0] = h_ref[...]


def _ssd_prompt(zxbc, dt, batch, conv_w, conv_b, dt_bias, a_log, d_skip, gate_norm):
    M = zxbc.shape[0]
    L = M // batch
    H = dt_bias.shape[0]
    d_inner = H * SSM_HEAD_DIM
    bc = SSM_GROUPS * SSM_STATE
    conv_dim = d_inner + 2 * bc
    Q = SSD_CHUNK
    nc = L // Q
    assert L % Q == 0 and d_inner % bc == 0
    xb = d_inner // bc
    row_map = lambda col: (lambda b, c: (b * nc + c, col))
    const = lambda b, c: (0, 0)
    d_exp = jnp.repeat(d_skip.astype(F32), SSM_HEAD_DIM).reshape(1, d_inner)
    assert H <= LANES and SSM_STATE == LANES and 2 * SSM_HEAD_DIM == LANES
    lane_pad = lambda v: jnp.pad(v.reshape(1, H), ((0, 0), (0, LANES - H)))
    n_slabs = conv_dim // LANES
    cw_slabs = conv_w.reshape(CONV_W, n_slabs, LANES).transpose(1, 0, 2)
    cb_slabs = conv_b.reshape(n_slabs, 1, LANES)
    const3 = lambda b, c: (0, 0, 0)
    y, h_final = pl.pallas_call(
        _ssd_kernel,
        grid=(batch, nc),
        in_specs=[pl.BlockSpec((Q, d_inner), row_map(0)),
                  pl.BlockSpec((Q, d_inner), row_map(1)),
                  pl.BlockSpec((Q, bc), row_map(2 * xb)),
                  pl.BlockSpec((Q, bc), row_map(2 * xb + 1)),
                  pl.BlockSpec((Q, LANES), row_map(0)),
                  pl.BlockSpec((n_slabs, CONV_W, LANES), const3),
                  pl.BlockSpec((n_slabs, 1, LANES), const3),
                  pl.BlockSpec((1, LANES), const),
                  pl.BlockSpec((1, LANES), const),
                  pl.BlockSpec((1, d_inner), const),
                  pl.BlockSpec((1, d_inner), const)],
        out_specs=[pl.BlockSpec((Q, d_inner), row_map(0)),
                   pl.BlockSpec((1, H, SSM_HEAD_DIM, SSM_STATE), lambda b, c: (b, 0, 0, 0))],
        out_shape=[jax.ShapeDtypeStruct((M, d_inner), BF16),
                   jax.ShapeDtypeStruct((batch, H, SSM_HEAD_DIM, SSM_STATE), F32)],
        scratch_shapes=[pltpu.VMEM((H, SSM_HEAD_DIM, SSM_STATE), F32),
                        pltpu.VMEM((n_slabs, Q + CONV_TAIL, LANES), F32),
                        pltpu.VMEM((n_slabs, Q, LANES), F32)],
        compiler_params=_params("parallel", "arbitrary"),
        name="ssd_prompt",
    )(zxbc, zxbc, zxbc, zxbc, dt, cw_slabs, cb_slabs,
      lane_pad(dt_bias), lane_pad(a_log), d_exp, gate_norm.reshape(1, d_inner))
    return y, h_final


def _ssd_step_kernel(zxbc_ref, dt_ref, cs_ref, h0_ref, cw_ref, cb_ref, dtb_ref, alog_ref,
                     dexp_ref, gn_ref, y_ref, cso_ref, ho_ref):
    P = SSM_HEAD_DIM
    N = SSM_STATE
    n_heads = h0_ref.shape[2]
    d_inner = n_heads * P
    hpg = n_heads // SSM_GROUPS
    gw = hpg * P
    conv_dim = cw_ref.shape[1]

    raw = zxbc_ref[0, :, d_inner:d_inner + conv_dim]
    prev = cs_ref[0, 0]
    acc = cb_ref[...]
    for k in range(CONV_W - 1):
        acc = acc + prev[k:k + 1, :] * cw_ref[k:k + 1, :]
    acc = acc + raw * cw_ref[CONV_W - 1:CONV_W, :]
    xbc = _silu(acc)
    cso_ref[0, 0, 0:CONV_W - 2, :] = prev[1:CONV_W - 1, :]
    cso_ref[0, 0, CONV_W - 2:CONV_W - 1, :] = raw

    dt = _softplus(dt_ref[0, :, 0:n_heads] + dtb_ref[...])
    decay = jnp.exp(dt * (-jnp.exp(alog_ref[...])))
    z = zxbc_ref[0, :, 0:d_inner]

    eye = (lax.broadcasted_iota(jnp.int32, (P, P), 0) == lax.broadcasted_iota(jnp.int32, (P, P), 1))
    ys = []
    for g in range(SSM_GROUPS):
        Bg = xbc[:, d_inner + g * N:d_inner + (g + 1) * N]
        Cg = xbc[:, d_inner + SSM_GROUPS * N + g * N:d_inner + SSM_GROUPS * N + (g + 1) * N]
        Bb = jnp.broadcast_to(Bg, (P, N))
        for r in range(hpg):
            hh = g * hpg + r
            xs = xbc[:, hh * P:(hh + 1) * P]
            xdt = xs * dt[:, hh:hh + 1]
            xdiag = jnp.where(eye, jnp.broadcast_to(xdt, (P, P)), 0.0)
            outer = jnp.dot(xdiag, Bb, precision=HIGHEST, preferred_element_type=F32)
            ho_ref[0, 0, hh] = h0_ref[0, 0, hh] * decay[:, hh:hh + 1] + outer
        Hg = ho_ref[0, 0, g * hpg:(g + 1) * hpg].reshape(gw, N)
        yg = lax.dot_general(Cg.astype(BF16), Hg.astype(BF16), NT_DIMS, preferred_element_type=F32)
        lo, hi = g * gw, (g + 1) * gw
        yg = yg + xbc[:, lo:hi] * dexp_ref[:, lo:hi]
        yg = yg * _silu(z[:, lo:hi])
        yg = yg * lax.rsqrt(jnp.mean(yg * yg, axis=-1, keepdims=True) + EPS)
        ys.append(yg * gn_ref[:, lo:hi])
    y_ref[0] = jnp.concatenate(ys, axis=1).astype(y_ref.dtype)


def _ssd_step(zxbc, dt, conv_state, ssm_state, conv_w, conv_b, dt_bias, a_log, d_skip, gate_norm):
    B = zxbc.shape[0]
    H = dt_bias.shape[0]
    d_inner = H * SSM_HEAD_DIM
    conv_dim = conv_w.shape[1]
    const = lambda b: (0, 0)
    d_exp = jnp.repeat(d_skip.astype(F32), SSM_HEAD_DIM).reshape(1, d_inner)
    cs4 = conv_state.reshape(1, B, CONV_W - 1, conv_dim)
    h5 = ssm_state.reshape(1, B, H, SSM_HEAD_DIM, SSM_STATE)
    return pl.pallas_call(
        _ssd_step_kernel,
        grid=(B,),
        in_specs=[pl.BlockSpec((1, 1, zxbc.shape[1]), lambda b: (b, 0, 0)),
                  pl.BlockSpec((1, 1, dt.shape[1]), lambda b: (b, 0, 0)),
                  pl.BlockSpec((1, 1, CONV_W - 1, conv_dim), lambda b: (0, b, 0, 0)),
                  pl.BlockSpec((1, 1, H, SSM_HEAD_DIM, SSM_STATE), lambda b: (0, b, 0, 0, 0)),
                  pl.BlockSpec((CONV_W, conv_dim), const),
                  pl.BlockSpec((1, conv_dim), const),
                  pl.BlockSpec((1, H), const),
                  pl.BlockSpec((1, H), const),
                  pl.BlockSpec((1, d_inner), const),
                  pl.BlockSpec((1, d_inner), const)],
        out_specs=[pl.BlockSpec((1, 1, d_inner), lambda b: (b, 0, 0)),
                   pl.BlockSpec((1, 1, CONV_W - 1, conv_dim), lambda b: (0, b, 0, 0)),
                   pl.BlockSpec((1, 1, H, SSM_HEAD_DIM, SSM_STATE), lambda b: (0, b, 0, 0, 0))],
        out_shape=[jax.ShapeDtypeStruct((B, 1, d_inner), F32),
                   jax.ShapeDtypeStruct(cs4.shape, F32),
                   jax.ShapeDtypeStruct(h5.shape, F32)],
        compiler_params=_params("arbitrary"),
        name="ssd_step",
    )(zxbc.reshape(B, 1, -1), dt.reshape(B, 1, -1), cs4, h5, conv_w, conv_b.reshape(1, conv_dim),
      dt_bias.reshape(1, H), a_log.reshape(1, H), d_exp, gate_norm.reshape(1, d_inner))


def _attn_kernel(q0_ref, q1_ref, q2_ref, k_ref, v_ref, o_ref, on_ref, lse_ref, qs_ref, ks_ref, vs_ref):
    L = k_ref.shape[0]
    QB = ATT_BLOCK
    KW = QB + DIL_SLOTS
    P = ATT_PRESTRIDE
    p_shift = P.bit_length() - 1
    scale = HEAD_DIM ** -0.5
    e = lax.broadcasted_iota(jnp.int32, (QB, KW), 0) - lax.broadcasted_iota(jnp.int32, (QB, KW), 1)

    staged = {rate: rate > P for rate in DIL_RATES}
    for src_ref, dst_ref, needed in ((q2_ref, qs_ref, staged[DIL_RATES[2]]), (k_ref, ks_ref, any(staged.values())),
                                     (v_ref, vs_ref, any(staged.values()))):
        if needed:
            def stage(i, carry, src_ref=src_ref, dst_ref=dst_ref):
                cp = i & (P - 1)
                t = i >> p_shift
                dst = pl.multiple_of(cp * (L // P) + t * QB, QB)
                dst_ref[pl.ds(dst, QB), :] = src_ref[pl.ds(cp + P * QB * t, QB, stride=P), :]
                return carry
            lax.fori_loop(0, L // QB, stage, 0, unroll=ATT_UNROLL)

    for g, (q_ref, rate) in enumerate(zip((q0_ref, q1_ref, q2_ref), DIL_RATES)):
        shift = rate.bit_length() - 1

        def unit(n, carry, g=g, q_ref=q_ref, rate=rate, shift=shift):
            c = n & (rate - 1)
            u0 = (n >> shift) * QB
            v0 = jnp.maximum(u0 - DIL_SLOTS, 0)
            q_rows = pl.ds(c + rate * u0, QB, stride=rate)
            if staged[rate]:
                base = (c & (P - 1)) * (L // P) + (c >> p_shift)
                q = qs_ref[pl.ds(base + (rate // P) * u0, QB, stride=rate // P), :].astype(BF16)
                k_rows = pl.ds(base + (rate // P) * v0, KW, stride=rate // P)
                k = ks_ref[k_rows, :].astype(BF16)
                v = vs_ref[k_rows, :].astype(BF16)
            else:
                k_rows = pl.ds(c + rate * v0, KW, stride=rate)
                q = q_ref[q_rows, :].astype(BF16)
                k = k_ref[k_rows, :].astype(BF16)
                v = v_ref[k_rows, :].astype(BF16)
            s = lax.dot_general(q, k, NT_DIMS, preferred_element_type=F32) * scale
            d = e + (u0 - v0)
            s = jnp.where(d >= 0, s, -jnp.inf)
            s = jnp.where(d <= DIL_SLOTS, s, -jnp.inf)
            m = s.max(axis=1, keepdims=True)
            p = jnp.exp(s - m)
            l = p.sum(axis=1, keepdims=True)
            o = jnp.dot(p.astype(BF16), v, preferred_element_type=F32)
            on_ref[g, q_rows, :] = o / l
            lse_ref[g, q_rows, :] = jnp.broadcast_to(m + jnp.log(l), (QB, HEAD_DIM))
            return carry

        lax.fori_loop(0, L // QB, unit, 0, unroll=ATT_UNROLL)

    def mix(i, carry):
        rows = pl.ds(pl.multiple_of(i * QB, QB), QB)
        lses = [lse_ref[g, rows, :] for g in range(len(DIL_RATES))]
        m = functools.reduce(jnp.maximum, lses)
        ws = [jnp.exp(x - m) for x in lses]
        num = functools.reduce(jnp.add, [w * on_ref[g, rows, :] for g, w in enumerate(ws)])
        o_ref[rows, :] = (num / functools.reduce(jnp.add, ws)).astype(o_ref.dtype)
        return carry

    lax.fori_loop(0, L // QB, mix, 0, unroll=2)


def _attn_prompt(q, kv, batch):
    M = q.shape[0]
    L = M // batch
    n_grp = len(DIL_RATES)
    assert all(r & (r - 1) == 0 and L % (ATT_BLOCK * r) == 0 and L >= r * (ATT_BLOCK + DIL_SLOTS)
               for r in DIL_RATES)
    qspec = lambda g: pl.BlockSpec((L, HEAD_DIM), lambda b, h: (b, g * KV_HEADS + h))
    kspec = pl.BlockSpec((L, HEAD_DIM), lambda b, h: (b, h))
    vspec = pl.BlockSpec((L, HEAD_DIM), lambda b, h: (b, KV_HEADS + h))
    return pl.pallas_call(
        _attn_kernel,
        grid=(batch, KV_HEADS),
        in_specs=[qspec(0), qspec(1), qspec(2), kspec, vspec],
        out_specs=pl.BlockSpec((L, HEAD_DIM), lambda b, h: (b, h)),
        out_shape=jax.ShapeDtypeStruct((M, KV_HEADS * HEAD_DIM), BF16),
        scratch_shapes=[pltpu.VMEM((n_grp, L, HEAD_DIM), F32), pltpu.VMEM((n_grp, L, HEAD_DIM), F32)]
        + [pltpu.VMEM((L, HEAD_DIM), F32)] * 3,
        compiler_params=_params("parallel", "arbitrary"),
        name="attn_prompt",
    )(q, q, q, kv, kv)


def _bf16_round(a):
    return a.astype(BF16).astype(F32)


def _attn_step_kernel(q_ref, *refs):
    n = len(DIL_RATES)
    k_refs, v_refs = refs[0:n], refs[n:2 * n]
    kn_ref, vn_ref, o_ref = refs[2 * n:]
    scale = HEAD_DIM ** -0.5
    kn = _bf16_round(kn_ref[0, 0])
    vn = _bf16_round(vn_ref[0, 0])
    scores, new_scores = [], []
    for g, k_ref in enumerate(k_refs):
        qg = _bf16_round(q_ref[0, g])
        kg = _bf16_round(k_ref[0, :, 0])
        scores.append(jnp.sum(kg * qg[None], axis=-1, keepdims=True) * scale)
        new_scores.append(jnp.sum(kn * qg, axis=-1, keepdims=True) * scale)
    m = functools.reduce(jnp.maximum, [s.max(axis=0) for s in scores] + new_scores)
    l = jnp.zeros_like(m)
    o = jnp.zeros((KV_HEADS, HEAD_DIM), F32)
    for s, s_new, v_ref in zip(scores, new_scores, v_refs):
        p = jnp.exp(s - m[None])
        p_new = jnp.exp(s_new - m)
        l = l + p.sum(axis=0) + p_new
        o = o + (_bf16_round(p) * _bf16_round(v_ref[0, :, 0])).sum(axis=0) + _bf16_round(p_new) * vn
    o_ref[0, 0] = (o / l).astype(o_ref.dtype)


def _attn_step(q, cache_k, cache_v, k_new, v_new):
    B, T = cache_k.shape[0], cache_k.shape[1]
    S = DIL_SLOTS
    assert T == S * max(DIL_RATES), "every slot of every dilation group lies inside the cached window"
    qg = q.reshape(B, len(DIL_RATES), KV_HEADS, HEAD_DIM)
    row_shape = (B, 1, KV_HEADS, HEAD_DIM)
    row_spec = pl.BlockSpec((1, 1, KV_HEADS, HEAD_DIM), lambda b: (b, 0, 0, 0))
    args = [qg]
    in_specs = [pl.BlockSpec((1, len(DIL_RATES), KV_HEADS, HEAD_DIM), lambda b: (b, 0, 0, 0))]
    for cache in (cache_k, cache_v):
        for rate in DIL_RATES:
            args.append(cache.reshape(B, T // rate, rate, KV_HEADS, HEAD_DIM))
            in_specs.append(pl.BlockSpec((1, S, 1, KV_HEADS, HEAD_DIM),
                                         lambda b, blk=T // rate // S - 1: (b, blk, 0, 0, 0)))
    args += [k_new.reshape(row_shape), v_new.reshape(row_shape)]
    in_specs += [row_spec, row_spec]
    o = pl.pallas_call(
        _attn_step_kernel,
        grid=(B,),
        in_specs=in_specs,
        out_specs=row_spec,
        out_shape=jax.ShapeDtypeStruct(row_shape, F32),
        compiler_params=_params("parallel"),
        name="attn_step",
    )(*args)
    return o.reshape(B, KV_HEADS * HEAD_DIM)


def _rope_tables(pos):
    half = HEAD_DIM // 2
    inv = jnp.power(jnp.float32(ROPE_THETA), -jnp.arange(half, dtype=jnp.float32) / half)
    ang = pos.astype(jnp.float32)[:, None] * inv[None, :]
    cos = jnp.cos(ang)
    sin = jnp.sin(ang)
    return jnp.concatenate([cos, cos], axis=1), jnp.concatenate([-sin, sin], axis=1)


def kernel(x_prompt, x_sample, state_conv, state_ssm, cache_k, cache_v, a_norm, a_w_in, a_conv_w, a_conv_b,
           a_dt_bias, a_log, a_d, a_gate_norm, a_w_out, kv_norm, w_kv, b_norm, b_w_q, b_w_o, ffn_norm,
           ffn_w_gu, ffn_w_down, final_norm):
    assert a_norm.shape[0] == 1 and b_norm.shape[0] == 1, "one Mamba-2 layer followed by one attention layer"
    Bp, Lp, D = x_prompt.shape
    Bs, Ls, _ = x_sample.shape
    assert Ls == 1 and Bs == SAMPLE_ROWS, "sample group decodes one token for SAMPLE_ROWS sequences"
    n_heads = a_dt_bias.shape[1]
    d_inner = a_w_out.shape[1]
    conv_dim = a_conv_w.shape[2]
    kv_dim = KV_HEADS * HEAD_DIM
    w_dt = jnp.pad(a_w_in[0][:, d_inner + conv_dim:], ((0, 0), (0, LANES - n_heads))).astype(BF16)
    ssd_w = (a_conv_w[0], a_conv_b[0], a_dt_bias[0], a_log[0], a_d[0], a_gate_norm[0])
    w_kv = w_kv[None]

    cos_p, sin_p = _rope_tables(jnp.arange(Lp, dtype=jnp.int32))
    cos_s, sin_s = _rope_tables(PAST_LEN + jnp.arange(Ls, dtype=jnp.int32))
    rope = ((cos_p, sin_p, Lp), (jnp.broadcast_to(cos_s, (Bs, HEAD_DIM)), jnp.broadcast_to(sin_s, (Bs, HEAD_DIM))))

    xp0 = x_prompt.reshape(Bp * Lp, D)
    xs0 = x_sample.reshape(Bs, D)

    zxbc_p, zxbc_s, dt_p, dt_s, w_out_bf, w_gu0_bf, w_down0_bf = _proj(
        xp0, xs0, [_seg(a_w_in.astype(BF16), n_cols=d_inner + conv_dim)], gains=[a_norm[0]], aux_w=w_dt, tn=1024,
        casts=[(a_w_out, 0), (ffn_w_gu, 0), (ffn_w_down, 0)])
    y_p, p_ssm = _ssd_prompt(zxbc_p, dt_p, Bp, *ssd_w)
    p_conv = zxbc_p.reshape(Bp, Lp, -1)[:, Lp - (CONV_W - 1):, d_inner:]
    y_s, s_conv, s_ssm = _ssd_step(zxbc_s, dt_s, state_conv[0], state_ssm[0], *ssd_w)
    *x1, w_gu1_bf = _proj(y_p, y_s.reshape(Bs, d_inner), [_seg(w_out_bf)], res=(xp0, xs0), casts=[(ffn_w_gu, 1)],
                          tm=256, tn=D)
    *h, w_kv_bf, w_q_bf, w_o_bf = _gateup(*x1, ffn_norm[0], w_gu0_bf, 0, casts=[(w_kv, 0), (b_w_q, 0), (b_w_o, 0)])
    x2 = _proj(*h, [_seg(w_down0_bf)], res=x1, tm=256, tn=D)

    kv_p, kv_s, q_p, q_s = _proj(*x2, [_seg(w_kv_bf, rope_cols=kv_dim), _seg(w_q_bf, rope_cols=b_w_q.shape[2])],
                                 gains=[kv_norm, b_norm[0]], rope=rope, tm=256, tn=None)
    o_p = _attn_prompt(q_p, kv_p, Bp)
    o_s = _attn_step(q_s, cache_k, cache_v, kv_s[:, :kv_dim], kv_s[:, kv_dim:])
    x3 = _proj(o_p, o_s, [_seg(w_o_bf)], res=x2, tm=512, tn=D)
    *h, w_down1_bf = _gateup(*x3, ffn_norm[1], w_gu1_bf, 0, casts=[(ffn_w_down, 1)])
    y_prompt, y_sample = _proj(*h, [_seg(w_down1_bf)], res=x3, post_gain=final_norm, tm=256, tn=D)
    y_prompt = y_prompt.reshape(Bp, Lp, D)
    y_sample = y_sample.reshape(Bs, Ls, D)
    keep = min(DIL_SLOTS * max(DIL_RATES), Lp)
    p_kv = kv_p.reshape(Bp, Lp, 2 * KV_HEADS, HEAD_DIM)[:, Lp - keep:]
    s_kv = kv_s.reshape(Bs, Ls, 2 * KV_HEADS, HEAD_DIM)
    return (y_prompt, y_sample, p_conv[None], p_ssm[None], p_kv[:, :, :KV_HEADS], p_kv[:, :, KV_HEADS:],
            s_conv, s_ssm, s_kv[:, :, :KV_HEADS], s_kv[:, :, KV_HEADS:])
```
